```python
import math
import jax, jax.numpy as jnp
from jax import lax
import numpy as np

D_MODEL = 1024
BATCH = 4
SEQ = 4096
DEPTH = 4
DEC_BATCH = 128
DEC_SEQ = 8
PAST_LEN = 2048
PAGE_SIZE = 128

N_MIXERS = 3
N_HEADS = 16
HEAD_DIM = D_MODEL // N_HEADS
NSA_KV_GROUPS = 4
NSA_HPG = N_HEADS // NSA_KV_GROUPS
CMP_LEN = 32
CMP_STRIDE = 16
CMP_HIDDEN = 2 * HEAD_DIM
SEL_LEN = 64
N_SELECT = 16
WINDOW = 512
N_BUCKETS = 32
MAX_DISTANCE = 128
GLA_HEADS = 4
GLA_DK = D_MODEL // 2 // GLA_HEADS
GLA_DV = D_MODEL // GLA_HEADS
GLA_QK = GLA_HEADS * GLA_DK
GLA_RANK = 16
GLA_TAU = 16.0
GLA_CHUNK = 64
D_FF = 4 * D_MODEL
Q_BLOCK = 128
EPS = 1e-6
FOX_FORGET_BIAS = 3.0
NEG = -1e30
BIG = 1e9
N_NSA = (DEPTH + 2) // 3
N_FOX = (DEPTH + 1) // 3
N_GLA = DEPTH // 3
NSA_KV_COLS = 4 * NSA_KV_GROUPS * HEAD_DIM
NSA_WIN_COLS = 2 * NSA_KV_GROUPS * HEAD_DIM
NSA_IN = D_MODEL + NSA_KV_COLS + NSA_WIN_COLS + 3 * N_HEADS
FOX_IN = 3 * D_MODEL + N_HEADS
GLA_IN = 2 * GLA_QK + 2 * D_MODEL + GLA_RANK

kernel_name = 'hybrid_nsa_fox_gla_decode_step'


def rms_norm(x, g):
    xf = x.astype(jnp.float32)
    y = xf * lax.rsqrt(jnp.mean(xf * xf, axis=-1, keepdims=True) + EPS)
    return (y * g.astype(jnp.float32)).astype(x.dtype)


def masked_softmax(logits, mask, axis):
    logits = jnp.where(mask, logits.astype(jnp.float32), NEG)
    m = jnp.max(logits, axis=axis, keepdims=True)
    e = jnp.exp(logits - m) * mask
    return e / jnp.maximum(jnp.sum(e, axis=axis, keepdims=True), 1e-30)


def t5_bucket(dist):
    n = jnp.maximum(dist, 0)
    max_exact = N_BUCKETS // 2
    nf = jnp.maximum(n, 1).astype(jnp.float32)
    log_part = max_exact + (jnp.log(nf / max_exact) / math.log(MAX_DISTANCE / max_exact)
                            * (N_BUCKETS - max_exact)).astype(jnp.int32)
    return jnp.where(n < max_exact, n, jnp.minimum(log_part, N_BUCKETS - 1))


def head_bias(dist, table):
    b = table.astype(jnp.float32)[t5_bucket(dist)]
    return b.reshape(*dist.shape, NSA_KV_GROUPS, NSA_HPG).transpose(2, 3, 0, 1)


def gather_pages(pool, page_table):
    rows = pool[page_table]
    return rows.reshape(page_table.shape[0], -1, *pool.shape[2:])


def query_blocks(fn, args, axes):
    T = args[0].shape[axes[0]]
    if T % Q_BLOCK != 0 or T <= Q_BLOCK:
        return fn(*args)
    nb = T // Q_BLOCK

    def split(a, ax):
        a = a.reshape(a.shape[:ax] + (nb, Q_BLOCK) + a.shape[ax + 1:])
        return jnp.moveaxis(a, ax, 0)

    out = lax.map(lambda blk: fn(*blk), tuple(split(a, ax) for a, ax in zip(args, axes)))
    out = jnp.moveaxis(out, 0, 1)
    return out.reshape(out.shape[:1] + (T,) + out.shape[3:])


def sq_relu_mlp(h, w1, w2):
    return jnp.square(jax.nn.relu(h @ w1)) @ w2


def compress_tokens(kv, pe, w1, w2):
    B, S = kv.shape[:2]
    n_cmp = (S - CMP_LEN) // CMP_STRIDE + 1
    idx = jnp.arange(n_cmp)[:, None] * CMP_STRIDE + jnp.arange(CMP_LEN)[None, :]
    blocks = kv[:, idx] + pe[None, None, :, None, :]
    z = blocks.transpose(0, 1, 3, 2, 4).reshape(B, n_cmp, NSA_KV_GROUPS, CMP_LEN * HEAD_DIM)
    return jax.nn.gelu(z @ w1) @ w2


def gqa_window_attend(q, k, v, q_pos, k_pos, table):
    B, Tq = q.shape[:2]
    qg = q.reshape(B, Tq, NSA_KV_GROUPS, NSA_HPG, HEAD_DIM)
    logits = jnp.einsum('bqghd,bkgd->bghqk', qg, k).astype(jnp.float32) * HEAD_DIM ** -0.5
    dist = q_pos[:, None] - k_pos[None, :]
    mask = (dist >= 0) & (dist < WINDOW) & (k_pos[None, :] >= 0)
    p = masked_softmax(logits + head_bias(dist, table), mask, -1)
    o = jnp.einsum('bghqk,bkgd->bqghd', p, v)
    return o.reshape(B, Tq, N_HEADS, HEAD_DIM)


def window_prompt(q, kw, vw, table):
    B, T = q.shape[:2]
    kp = jnp.pad(kw, ((0, 0), (WINDOW, 0), (0, 0), (0, 0)))
    vp = jnp.pad(vw, ((0, 0), (WINDOW, 0), (0, 0), (0, 0)))

    def blk(i):
        q0 = i * Q_BLOCK
        qb = lax.dynamic_slice_in_dim(q, q0, Q_BLOCK, axis=1)
        kb = lax.dynamic_slice_in_dim(kp, q0, WINDOW + Q_BLOCK, axis=1)
        vb = lax.dynamic_slice_in_dim(vp, q0, WINDOW + Q_BLOCK, axis=1)
        q_pos = q0 + jnp.arange(Q_BLOCK)
        k_pos = q0 - WINDOW + jnp.arange(WINDOW + Q_BLOCK)
        return gqa_window_attend(qb, kb, vb, q_pos, k_pos, table)

    o = lax.map(blk, jnp.arange(T // Q_BLOCK))
    return jnp.moveaxis(o, 0, 1).reshape(B, T, N_HEADS, HEAD_DIM)


def selected_attend(q, ks, vs, sel, q_pos, table):
    B, Qb = q.shape[:2]
    G, HPG = NSA_KV_GROUPS, NSA_HPG
    b_idx = jnp.arange(B)[:, None, None, None]
    g_idx = jnp.arange(G)[None, :, None, None]
    kg = ks[b_idx, g_idx, sel]
    vg = vs[b_idx, g_idx, sel]
    qg = q.reshape(B, Qb, G, HPG, HEAD_DIM)
    logits = jnp.einsum('bqghd,bgqnld->bghqnl', qg, kg).astype(jnp.float32) * HEAD_DIM ** -0.5
    k_pos = sel[..., None] * SEL_LEN + jnp.arange(SEL_LEN)
    dist = q_pos[:, None, None] - k_pos
    table_g = table.astype(jnp.float32).reshape(N_BUCKETS, G, HPG).transpose(1, 0, 2)
    bias = table_g[jnp.arange(G)[None, :, None, None, None], t5_bucket(dist)]
    logits = logits + bias.transpose(0, 1, 5, 2, 3, 4)
    p = masked_softmax(logits, (dist >= 0)[:, :, None], (-2, -1))
    o = jnp.einsum('bghqnl,bgqnld->bqghd', p, vg)
    return o.reshape(B, Qb, N_HEADS, HEAD_DIM)


def nsa_mixer(h, w_in, phi_pe, phi_w1, phi_w2, w_out, table, past_kv, past_win, pos0):
    B, T, _ = h.shape
    G = NSA_KV_GROUPS
    c1 = D_MODEL + NSA_KV_COLS
    c2 = c1 + NSA_WIN_COLS
    z = h @ w_in
    q = z[..., :D_MODEL].reshape(B, T, N_HEADS, HEAD_DIM)
    kv_new = z[..., D_MODEL:c1].reshape(B, T, 4, G, HEAD_DIM)
    win_new = z[..., c1:c2].reshape(B, T, 2, G, HEAD_DIM)
    gates = jax.nn.sigmoid(z[..., c2:].astype(jnp.float32)).reshape(B, T, 3, N_HEADS)
    kv = kv_new if past_kv is None else jnp.concatenate([past_kv, kv_new], axis=1)
    S = kv.shape[1]
    q_pos = pos0 + jnp.arange(T)
    qg = q.reshape(B, T, G, NSA_HPG, HEAD_DIM)

    k_cmp = compress_tokens(kv[:, :, 0], phi_pe[0], phi_w1[0], phi_w2[0])
    v_cmp = compress_tokens(kv[:, :, 1], phi_pe[1], phi_w1[1], phi_w2[1])
    n_cmp = k_cmp.shape[1]
    c_start = jnp.arange(n_cmp) * CMP_STRIDE
    dist_c = q_pos[:, None] - (c_start + CMP_LEN - 1)[None, :]
    logits_c = jnp.einsum('bqghd,bcgd->bghqc', qg, k_cmp).astype(jnp.float32) * HEAD_DIM ** -0.5
    p_cmp = masked_softmax(logits_c + head_bias(dist_c, table), dist_c >= 0, -1)
    o_cmp = jnp.einsum('bghqc,bcgd->bqghd', p_cmp, v_cmp).reshape(B, T, N_HEADS, HEAD_DIM)

    n_sel = -(-S // SEL_LEN)
    s_start = jnp.arange(n_sel) * SEL_LEN
    incidence = ((c_start[:, None] < s_start[None, :] + SEL_LEN)
                 & (c_start[:, None] + CMP_LEN > s_start[None, :])).astype(jnp.float32)
    imp = jnp.einsum('bghqc,cj->bgqj', p_cmp, incidence)
    cur = q_pos // SEL_LEN
    j = jnp.arange(n_sel)
    forced = (j[None, :] == 0) | (j[None, :] == cur[:, None]) | (j[None, :] == cur[:, None] - 1)
    avail = j[None, :] <= cur[:, None]
    score = jnp.where(forced, BIG, jnp.where(avail, imp, -BIG))
    _, sel = lax.top_k(score, min(N_SELECT, n_sel))

    pad = n_sel * SEL_LEN - S
    ks = jnp.pad(kv[:, :, 2], ((0, 0), (0, pad), (0, 0), (0, 0))).reshape(B, n_sel, SEL_LEN, G, HEAD_DIM).transpose(0, 3, 1, 2, 4)
    vs = jnp.pad(kv[:, :, 3], ((0, 0), (0, pad), (0, 0), (0, 0))).reshape(B, n_sel, SEL_LEN, G, HEAD_DIM).transpose(0, 3, 1, 2, 4)
    o_slc = query_blocks(lambda q_, s_, p_: selected_attend(q_, ks, vs, s_, p_, table), [q, sel, q_pos], [1, 2, 0])

    if past_win is None:
        win_all = win_new
        o_win = window_prompt(q, win_new[:, :, 0], win_new[:, :, 1], table)
    else:
        win_all = jnp.concatenate([past_win, win_new], axis=1)
        lw = win_all.shape[1]
        k_pos = pos0 + T - lw + jnp.arange(lw)
        o_win = gqa_window_attend(q, win_all[:, :, 0], win_all[:, :, 1], q_pos, k_pos, table)
    win_state = win_all[:, win_all.shape[1] - min(WINDOW, win_all.shape[1]):]

    o = (gates[:, :, 0, :, None] * o_cmp + gates[:, :, 1, :, None] * o_slc + gates[:, :, 2, :, None] * o_win)
    out = o.reshape(B, T, D_MODEL).astype(h.dtype) @ w_out
    return out, kv_new, win_state


def fox_attend(q, dq, q_pos, k, v, dk, k_pos):
    logits = jnp.einsum('bqhd,bkhd->bhqk', q, k).astype(jnp.float32) * HEAD_DIM ** -0.5
    logits = logits + dq.transpose(0, 2, 1)[..., None] - dk.transpose(0, 2, 1)[:, :, None, :]
    p = masked_softmax(logits, q_pos[:, None] >= k_pos[None, :], -1)
    return jnp.einsum('bhqk,bkhd->bqhd', p, v)


def fox_mixer(h, w_in, b_f, w_out, past_kv, past_logf, pos0):
    B, T, _ = h.shape
    z = h @ w_in
    q = z[..., :D_MODEL].reshape(B, T, N_HEADS, HEAD_DIM)
    kv_new = z[..., D_MODEL:3 * D_MODEL].reshape(B, T, 2, N_HEADS, HEAD_DIM)
    logf_new = jax.nn.log_sigmoid(z[..., 3 * D_MODEL:].astype(jnp.float32) + b_f.astype(jnp.float32))
    if past_kv is None:
        kv, logf = kv_new, logf_new
    else:
        kv = jnp.concatenate([past_kv, kv_new], axis=1)
        logf = jnp.concatenate([past_logf.astype(jnp.float32), logf_new], axis=1)
    S = kv.shape[1]
    cum = jnp.cumsum(logf, axis=1)
    k_pos = jnp.arange(S)
    q_pos = pos0 + jnp.arange(T)
    k, v = kv[:, :, 0], kv[:, :, 1]
    o = query_blocks(lambda q_, dq_, p_: fox_attend(q_, dq_, p_, k, v, cum, k_pos), [q, cum[:, S - T:], q_pos], [1, 1, 0])
    out = o.reshape(B, T, D_MODEL).astype(h.dtype) @ w_out
    return out, kv_new, logf_new


def gla_chunk(S, q, k, v, g):
    C = q.shape[1]
    b = jnp.cumsum(g, axis=1)
    causal = jnp.tril(jnp.ones((C, C), dtype=bool))
    diff = b[:, :, None] - b[:, None, :]
    decay = jnp.exp(jnp.where(causal[None, :, :, None, None], diff, NEG))
    A = jnp.einsum('bthd,bshd,btshd->bhts', q, k, decay)
    o = jnp.einsum('bhts,bshv->bthv', A, v) + jnp.einsum('bthd,bhdv->bthv', q * jnp.exp(b), S)
    b_last = b[:, -1]
    S_new = jnp.exp(b_last)[..., None] * S + jnp.einsum('bshd,bshv->bhdv', k * jnp.exp(b_last[:, None] - b), v)
    return o, S_new


def gla_scan(S0, q, k, v, g):
    B, T = q.shape[:2]
    c = GLA_CHUNK if T % GLA_CHUNK == 0 else T
    nc = T // c

    def to_chunks(a):
        return jnp.moveaxis(a.reshape(B, nc, c, *a.shape[2:]), 1, 0)

    def step(S, blk):
        o, S = gla_chunk(S, *blk)
        return S, o

    S_fin, o = lax.scan(step, S0, tuple(to_chunks(a) for a in (q, k, v, g)))
    return jnp.moveaxis(o, 0, 1).reshape(B, T, GLA_HEADS, GLA_DV), S_fin


def gla_mixer(h, w_in, w_a2, b_a, norm_g, w_out, S0):
    B, T, _ = h.shape
    z = (h @ w_in).astype(jnp.float32)
    q = z[..., :GLA_QK].reshape(B, T, GLA_HEADS, GLA_DK) * GLA_DK ** -0.5
    k = z[..., GLA_QK:2 * GLA_QK].reshape(B, T, GLA_HEADS, GLA_DK)
    v = z[..., 2 * GLA_QK:2 * GLA_QK + D_MODEL].reshape(B, T, GLA_HEADS, GLA_DV)
    r = z[..., 2 * GLA_QK + D_MODEL:2 * GLA_QK + 2 * D_MODEL]
    a_lr = z[..., 2 * GLA_QK + 2 * D_MODEL:]
    g = jax.nn.log_sigmoid(a_lr @ w_a2.astype(jnp.float32) + b_a.astype(jnp.float32)) / GLA_TAU
    g = g.reshape(B, T, GLA_HEADS, GLA_DK)
    if S0 is None:
        S0 = jnp.zeros((B, GLA_HEADS, GLA_DK, GLA_DV), jnp.float32)
    o, S_fin = gla_scan(S0.astype(jnp.float32), q, k, v, g)
    o = rms_norm(o, norm_g).reshape(B, T, D_MODEL) * jax.nn.silu(r)
    return o.astype(h.dtype) @ w_out, S_fin


def trunk(x, pos0, W, past):
    nsa_kv, nsa_win, fox_kv, fox_lf, gla_s = [], [], [], [], []
    for i in range(DEPTH):
        kind, li = i % N_MIXERS, i // N_MIXERS
        h = rms_norm(x, W['norm_g'][i, 0])
        if kind == 0:
            p_kv = None if past is None else gather_pages(past['cache_nsa_kv'][li], past['page_table'])
            p_win = None if past is None else past['state_nsa_win'][li]
            out, kv_new, win_state = nsa_mixer(h, W['nsa_w_in'][li], W['nsa_phi_pe'][li], W['nsa_phi_w1'][li],
                                               W['nsa_phi_w2'][li], W['nsa_w_out'][li], W['rel_bias'], p_kv, p_win, pos0)
            nsa_kv.append(kv_new)
            nsa_win.append(win_state)
        elif kind == 1:
            p_kv = None if past is None else gather_pages(past['cache_fox_kv'][li], past['page_table'])
            p_lf = None if past is None else gather_pages(past['cache_fox_logf'][li], past['page_table'])
            out, kv_new, lf_new = fox_mixer(h, W['fox_w_in'][li], W['fox_b_f'][li], W['fox_w_out'][li], p_kv, p_lf, pos0)
            fox_kv.append(kv_new)
            fox_lf.append(lf_new)
        else:
            s0 = None if past is None else past['state_gla'][li]
            out, s_new = gla_mixer(h, W['gla_w_in'][li], W['gla_w_a2'][li], W['gla_b_a'][li], W['gla_norm_g'][li],
                                   W['gla_w_out'][li], s0)
            gla_s.append(s_new)
        x = x + out
        h = rms_norm(x, W['norm_g'][i, 1])
        x = x + sq_relu_mlp(h, W['mlp_w1'][i], W['mlp_w2'][i])
    return (rms_norm(x, W['final_g']), jnp.stack(nsa_kv), jnp.stack(nsa_win), jnp.stack(fox_kv),
            jnp.stack(fox_lf), jnp.stack(gla_s))


def setup_inputs(seed: int = 0) -> dict:
    key = jax.random.key(seed)
    ks = jax.random.split(key, 26)
    f32 = jnp.float32

    def nrm(k, shape, scale):
        return jax.random.normal(k, shape, f32) * scale

    n_pages = PAST_LEN // PAGE_SIZE
    n_used = DEC_BATCH * n_pages
    n_pool = n_used + max(1, n_used // 4)
    G = NSA_KV_GROUPS
    page_table = jax.random.permutation(ks[0], n_pool)[:n_used].reshape(DEC_BATCH, n_pages).astype(jnp.int32)
    win_buf = min(WINDOW, PAST_LEN)
    return {
        'x_prompt': nrm(ks[1], (BATCH, SEQ, D_MODEL), 1.0),
        'x_sample': nrm(ks[2], (DEC_BATCH, DEC_SEQ, D_MODEL), 1.0),
        'cache_nsa_kv': nrm(ks[3], (N_NSA, n_pool, PAGE_SIZE, 4, G, HEAD_DIM), 1.0),
        'state_nsa_win': nrm(ks[4], (N_NSA, DEC_BATCH, win_buf, 2, G, HEAD_DIM), 1.0),
        'cache_fox_kv': nrm(ks[5], (N_FOX, n_pool, PAGE_SIZE, 2, N_HEADS, HEAD_DIM), 1.0),
        'cache_fox_logf': jax.nn.log_sigmoid(FOX_FORGET_BIAS + nrm(ks[6], (N_FOX, n_pool, PAGE_SIZE, N_HEADS), 1.0)),
        'state_gla': nrm(ks[7], (N_GLA, DEC_BATCH, GLA_HEADS, GLA_DK, GLA_DV), 0.5),
        'page_table': page_table,
        'rel_bias': nrm(ks[8], (N_BUCKETS, N_HEADS), 0.5),
        'norm_g': 1.0 + nrm(ks[9], (DEPTH, 2, D_MODEL), 0.05),
        'final_g': 1.0 + nrm(ks[10], (D_MODEL,), 0.05),
        'nsa_w_in': nrm(ks[11], (N_NSA, D_MODEL, NSA_IN), D_MODEL ** -0.5),
        'nsa_phi_pe': nrm(ks[12], (N_NSA, 2, CMP_LEN, HEAD_DIM), 0.1),
        'nsa_phi_w1': nrm(ks[13], (N_NSA, 2, CMP_LEN * HEAD_DIM, CMP_HIDDEN), (CMP_LEN * HEAD_DIM) ** -0.5),
        'nsa_phi_w2': nrm(ks[14], (N_NSA, 2, CMP_HIDDEN, HEAD_DIM), CMP_HIDDEN ** -0.5),
        'nsa_w_out': nrm(ks[15], (N_NSA, D_MODEL, D_MODEL), D_MODEL ** -0.5),
        'fox_w_in': nrm(ks[16], (N_FOX, D_MODEL, FOX_IN), D_MODEL ** -0.5),
        'fox_b_f': FOX_FORGET_BIAS + nrm(ks[17], (N_FOX, N_HEADS), 0.1),
        'fox_w_out': nrm(ks[18], (N_FOX, D_MODEL, D_MODEL), D_MODEL ** -0.5),
        'gla_w_in': nrm(ks[19], (N_GLA, D_MODEL, GLA_IN), D_MODEL ** -0.5),
        'gla_w_a2': nrm(ks[20], (N_GLA, GLA_RANK, GLA_QK), GLA_RANK ** -0.5),
        'gla_b_a': nrm(ks[21], (N_GLA, GLA_QK), 0.1),
        'gla_norm_g': 1.0 + nrm(ks[22], (N_GLA, GLA_DV), 0.05),
        'gla_w_out': nrm(ks[23], (N_GLA, D_MODEL, D_MODEL), D_MODEL ** -0.5),
        'mlp_w1': nrm(ks[24], (DEPTH, D_MODEL, D_FF), D_MODEL ** -0.5),
        'mlp_w2': nrm(ks[25], (DEPTH, D_FF, D_MODEL), D_FF ** -0.5),
    }


def reference(x_prompt, x_sample, cache_nsa_kv, state_nsa_win, cache_fox_kv, cache_fox_logf, state_gla, page_table,
              rel_bias, norm_g, final_g, nsa_w_in, nsa_phi_pe, nsa_phi_w1, nsa_phi_w2, nsa_w_out,
              fox_w_in, fox_b_f, fox_w_out, gla_w_in, gla_w_a2, gla_b_a, gla_norm_g, gla_w_out, mlp_w1, mlp_w2):
    W = dict(rel_bias=rel_bias, norm_g=norm_g, final_g=final_g, nsa_w_in=nsa_w_in, nsa_phi_pe=nsa_phi_pe,
             nsa_phi_w1=nsa_phi_w1, nsa_phi_w2=nsa_phi_w2, nsa_w_out=nsa_w_out, fox_w_in=fox_w_in, fox_b_f=fox_b_f,
             fox_w_out=fox_w_out, gla_w_in=gla_w_in, gla_w_a2=gla_w_a2, gla_b_a=gla_b_a, gla_norm_g=gla_norm_g,
             gla_w_out=gla_w_out, mlp_w1=mlp_w1, mlp_w2=mlp_w2)
    past = dict(cache_nsa_kv=cache_nsa_kv, state_nsa_win=state_nsa_win, cache_fox_kv=cache_fox_kv,
                cache_fox_logf=cache_fox_logf, state_gla=state_gla, page_table=page_table)
    past_len = page_table.shape[1] * PAGE_SIZE
    y_prompt, nsa_kv_p, nsa_win_p, fox_kv_p, fox_lf_p, gla_p = trunk(x_prompt, 0, W, None)
    y_sample, nsa_kv_s, nsa_win_s, fox_kv_s, fox_lf_s, gla_s = trunk(x_sample, past_len, W, past)
    return (y_prompt, y_sample, nsa_kv_p, nsa_kv_s, nsa_win_p, nsa_win_s, fox_kv_p, fox_kv_s, fox_lf_p, fox_lf_s, gla_p, gla_s)
```

```python
import functools
import math

import jax
import jax.numpy as jnp
from jax import lax
from jax.experimental import pallas as pl
from jax.experimental.pallas import tpu as pltpu

D_MODEL = 1024
DEPTH = 4
PAGE_SIZE = 128
N_MIXERS = 3
N_HEADS = 16
HEAD_DIM = D_MODEL // N_HEADS
NSA_KV_GROUPS = 4
NSA_HPG = N_HEADS // NSA_KV_GROUPS
CMP_LEN = 32
CMP_STRIDE = 16
SEL_LEN = 64
N_SELECT = 16
WINDOW = 512
N_BUCKETS = 32
MAX_DISTANCE = 128
GLA_HEADS = 4
GLA_DK = D_MODEL // 2 // GLA_HEADS
GLA_DV = D_MODEL // GLA_HEADS
GLA_QK = GLA_HEADS * GLA_DK
GLA_RANK = 16
GLA_TAU = 16.0
GLA_CHUNK = 64
D_FF = 4 * D_MODEL
Q_BLOCK = 128
EPS = 1e-6
NEG = -1e30
BIG = 1e9
NSA_KV_COLS = 4 * NSA_KV_GROUPS * HEAD_DIM
NSA_WIN_COLS = 2 * NSA_KV_GROUPS * HEAD_DIM

F32 = jnp.float32
BF16 = jnp.bfloat16

V7X_VMEM_LIMIT_BYTES = 48 * 1024 * 1024
TOKEN_TILE = 512
FF_TILE = 1024


def _cparams(*sem):
    return pltpu.CompilerParams(dimension_semantics=sem, vmem_limit_bytes=V7X_VMEM_LIMIT_BYTES)


def _rms(x, g):
    return x * lax.rsqrt(jnp.mean(x * x, axis=-1, keepdims=True) + EPS) * g


def _norm_proj_kernel(x_ref, g_ref, w_ref, *out_refs, splits):
    h = _rms(x_ref[...], g_ref[...]).astype(BF16)
    for (lo, hi), o_ref in zip(splits, out_refs):
        o_ref[...] = jnp.dot(h, w_ref[:, lo:hi], preferred_element_type=F32)


def norm_proj(x, g, w, splits):
    m = x.shape[0]
    n = w.shape[1]
    kern = functools.partial(_norm_proj_kernel, splits=splits)
    return pl.pallas_call(
        kern,
        grid=(m // TOKEN_TILE,),
        in_specs=[
            pl.BlockSpec((TOKEN_TILE, D_MODEL), lambda i: (i, 0)),
            pl.BlockSpec((1, D_MODEL), lambda i: (0, 0)),
            pl.BlockSpec((D_MODEL, n), lambda i: (0, 0)),
        ],
        out_specs=[pl.BlockSpec((TOKEN_TILE, hi - lo), lambda i: (i, 0)) for lo, hi in splits],
        out_shape=[jax.ShapeDtypeStruct((m, hi - lo), F32) for lo, hi in splits],
        compiler_params=_cparams("parallel"),
        name="norm_proj",
    )(x, g.reshape(1, D_MODEL), w)


def _out_mlp_kernel(x_ref, o_ref, wo_ref, g_ref, w1_ref, w2_ref, y_ref, x1_ref, h_ref, acc_ref):
    j = pl.program_id(1)

    @pl.when(j == 0)
    def _():
        x1 = x_ref[...] + jnp.dot(o_ref[...].astype(BF16), wo_ref[...], preferred_element_type=F32)
        x1_ref[...] = x1
        h_ref[...] = _rms(x1, g_ref[...]).astype(BF16)
        acc_ref[...] = jnp.zeros_like(acc_ref)

    a = jnp.dot(h_ref[...], w1_ref[...], preferred_element_type=F32)
    a = jnp.square(jnp.maximum(a, 0.0)).astype(BF16)
    acc_ref[...] += jnp.dot(a, w2_ref[...], preferred_element_type=F32)

    @pl.when(j == pl.num_programs(1) - 1)
    def _():
        y_ref[...] = x1_ref[...] + acc_ref[...]


def out_mlp(x, o, w_out, g, w1, w2):
    m = x.shape[0]
    return pl.pallas_call(
        _out_mlp_kernel,
        grid=(m // TOKEN_TILE, D_FF // FF_TILE),
        in_specs=[
            pl.BlockSpec((TOKEN_TILE, D_MODEL), lambda i, j: (i, 0)),
            pl.BlockSpec((TOKEN_TILE, D_MODEL), lambda i, j: (i, 0)),
            pl.BlockSpec((D_MODEL, D_MODEL), lambda i, j: (0, 0)),
            pl.BlockSpec((1, D_MODEL), lambda i, j: (0, 0)),
            pl.BlockSpec((D_MODEL, FF_TILE), lambda i, j: (0, j)),
            pl.BlockSpec((FF_TILE, D_MODEL), lambda i, j: (j, 0)),
        ],
        out_specs=pl.BlockSpec((TOKEN_TILE, D_MODEL), lambda i, j: (i, 0)),
        out_shape=jax.ShapeDtypeStruct((m, D_MODEL), F32),
        scratch_shapes=[
            pltpu.VMEM((TOKEN_TILE, D_MODEL), F32),
            pltpu.VMEM((TOKEN_TILE, D_MODEL), BF16),
            pltpu.VMEM((TOKEN_TILE, D_MODEL), F32),
        ],
        compiler_params=_cparams("parallel", "arbitrary"),
        name="out_mlp",
    )(x, o, w_out, g.reshape(1, D_MODEL), w1, w2)


def _final_norm_kernel(x_ref, g_ref, y_ref):
    y_ref[...] = _rms(x_ref[...], g_ref[...])


def final_norm(x, g):
    m = x.shape[0]
    return pl.pallas_call(
        _final_norm_kernel,
        grid=(m // TOKEN_TILE,),
        in_specs=[pl.BlockSpec((TOKEN_TILE, D_MODEL), lambda i: (i, 0)),
                  pl.BlockSpec((1, D_MODEL), lambda i: (0, 0))],
        out_specs=pl.BlockSpec((TOKEN_TILE, D_MODEL), lambda i: (i, 0)),
        out_shape=jax.ShapeDtypeStruct((m, D_MODEL), F32),
        compiler_params=_cparams("parallel"),
        name="final_norm",
    )(x, g.reshape(1, D_MODEL))


def masked_softmax(logits, mask, axis):
    logits = jnp.where(mask, logits.astype(jnp.float32), NEG)
    m = jnp.max(logits, axis=axis, keepdims=True)
    e = jnp.exp(logits - m) * mask
    return e / jnp.maximum(jnp.sum(e, axis=axis, keepdims=True), 1e-30)


def t5_bucket(dist):
    n = jnp.maximum(dist, 0)
    max_exact = N_BUCKETS // 2
    nf = jnp.maximum(n, 1).astype(jnp.float32)
    log_part = max_exact + (jnp.log(nf / max_exact) / math.log(MAX_DISTANCE / max_exact)
                            * (N_BUCKETS - max_exact)).astype(jnp.int32)
    return jnp.where(n < max_exact, n, jnp.minimum(log_part, N_BUCKETS - 1))


def head_bias(dist, table):
    b = table.astype(jnp.float32)[t5_bucket(dist)]
    return b.reshape(*dist.shape, NSA_KV_GROUPS, NSA_HPG).transpose(2, 3, 0, 1)


def gather_pages(pool, page_table):
    rows = pool[page_table]
    return rows.reshape(page_table.shape[0], -1, *pool.shape[2:])


def query_blocks(fn, args, axes):
    T = args[0].shape[axes[0]]
    if T % Q_BLOCK != 0 or T <= Q_BLOCK:
        return fn(*args)
    nb = T // Q_BLOCK

    def split(a, ax):
        a = a.reshape(a.shape[:ax] + (nb, Q_BLOCK) + a.shape[ax + 1:])
        return jnp.moveaxis(a, ax, 0)

    out = lax.map(lambda blk: fn(*blk), tuple(split(a, ax) for a, ax in zip(args, axes)))
    out = jnp.moveaxis(out, 0, 1)
    return out.reshape(out.shape[:1] + (T,) + out.shape[3:])


def compress_tokens(kv, pe, w1, w2):
    B, S = kv.shape[:2]
    n_cmp = (S - CMP_LEN) // CMP_STRIDE + 1
    idx = jnp.arange(n_cmp)[:, None] * CMP_STRIDE + jnp.arange(CMP_LEN)[None, :]
    blocks = kv[:, idx] + pe[None, None, :, None, :]
    z = blocks.transpose(0, 1, 3, 2, 4).reshape(B, n_cmp, NSA_KV_GROUPS, CMP_LEN * HEAD_DIM)
    return jax.nn.gelu(z @ w1) @ w2


def gqa_window_attend(q, k, v, q_pos, k_pos, table):
    B, Tq = q.shape[:2]
    qg = q.reshape(B, Tq, NSA_KV_GROUPS, NSA_HPG, HEAD_DIM)
    logits = jnp.einsum('bqghd,bkgd->bghqk', qg, k).astype(jnp.float32) * HEAD_DIM ** -0.5
    dist = q_pos[:, None] - k_pos[None, :]
    mask = (dist >= 0) & (dist < WINDOW) & (k_pos[None, :] >= 0)
    p = masked_softmax(logits + head_bias(dist, table), mask, -1)
    o = jnp.einsum('bghqk,bkgd->bqghd', p, v)
    return o.reshape(B, Tq, N_HEADS, HEAD_DIM)


def window_prompt(q, kw, vw, table):
    B, T = q.shape[:2]
    kp = jnp.pad(kw, ((0, 0), (WINDOW, 0), (0, 0), (0, 0)))
    vp = jnp.pad(vw, ((0, 0), (WINDOW, 0), (0, 0), (0, 0)))

    def blk(i):
        q0 = i * Q_BLOCK
        qb = lax.dynamic_slice_in_dim(q, q0, Q_BLOCK, axis=1)
        kb = lax.dynamic_slice_in_dim(kp, q0, WINDOW + Q_BLOCK, axis=1)
        vb = lax.dynamic_slice_in_dim(vp, q0, WINDOW + Q_BLOCK, axis=1)
        q_pos = q0 + jnp.arange(Q_BLOCK)
        k_pos = q0 - WINDOW + jnp.arange(WINDOW + Q_BLOCK)
        return gqa_window_attend(qb, kb, vb, q_pos, k_pos, table)

    o = lax.map(blk, jnp.arange(T // Q_BLOCK))
    return jnp.moveaxis(o, 0, 1).reshape(B, T, N_HEADS, HEAD_DIM)


def selected_attend(q, ks, vs, sel, q_pos, table):
    B, Qb = q.shape[:2]
    G, HPG = NSA_KV_GROUPS, NSA_HPG
    b_idx = jnp.arange(B)[:, None, None, None]
    g_idx = jnp.arange(G)[None, :, None, None]
    kg = ks[b_idx, g_idx, sel]
    vg = vs[b_idx, g_idx, sel]
    qg = q.reshape(B, Qb, G, HPG, HEAD_DIM)
    logits = jnp.einsum('bqghd,bgqnld->bghqnl', qg, kg).astype(jnp.float32) * HEAD_DIM ** -0.5
    k_pos = sel[..., None] * SEL_LEN + jnp.arange(SEL_LEN)
    dist = q_pos[:, None, None] - k_pos
    table_g = table.astype(jnp.float32).reshape(N_BUCKETS, G, HPG).transpose(1, 0, 2)
    bias = table_g[jnp.arange(G)[None, :, None, None, None], t5_bucket(dist)]
    logits = logits + bias.transpose(0, 1, 5, 2, 3, 4)
    p = masked_softmax(logits, (dist >= 0)[:, :, None], (-2, -1))
    o = jnp.einsum('bghqnl,bgqnld->bqghd', p, vg)
    return o.reshape(B, Qb, N_HEADS, HEAD_DIM)


def nsa_core(q, kv_new, win_new, gates, phi_pe, phi_w1, phi_w2, table, past_kv, past_win, pos0):
    B, T = q.shape[:2]
    G = NSA_KV_GROUPS
    gates = jax.nn.sigmoid(gates).reshape(B, T, 3, N_HEADS)
    kv = kv_new if past_kv is None else jnp.concatenate([past_kv, kv_new], axis=1)
    S = kv.shape[1]
    q_pos = pos0 + jnp.arange(T)
    qg = q.reshape(B, T, G, NSA_HPG, HEAD_DIM)
    k_cmp = compress_tokens(kv[:, :, 0], phi_pe[0], phi_w1[0], phi_w2[0])
    v_cmp = compress_tokens(kv[:, :, 1], phi_pe[1], phi_w1[1], phi_w2[1])
    n_cmp = k_cmp.shape[1]
    c_start = jnp.arange(n_cmp) * CMP_STRIDE
    dist_c = q_pos[:, None] - (c_start + CMP_LEN - 1)[None, :]
    logits_c = jnp.einsum('bqghd,bcgd->bghqc', qg, k_cmp).astype(jnp.float32) * HEAD_DIM ** -0.5
    p_cmp = masked_softmax(logits_c + head_bias(dist_c, table), dist_c >= 0, -1)
    o_cmp = jnp.einsum('bghqc,bcgd->bqghd', p_cmp, v_cmp).reshape(B, T, N_HEADS, HEAD_DIM)
    n_sel = -(-S // SEL_LEN)
    s_start = jnp.arange(n_sel) * SEL_LEN
    incidence = ((c_start[:, None] < s_start[None, :] + SEL_LEN)
                 & (c_start[:, None] + CMP_LEN > s_start[None, :])).astype(jnp.float32)
    imp = jnp.einsum('bghqc,cj->bgqj', p_cmp, incidence)
    cur = q_pos // SEL_LEN
    j = jnp.arange(n_sel)
    forced = (j[None, :] == 0) | (j[None, :] == cur[:, None]) | (j[None, :] == cur[:, None] - 1)
    avail = j[None, :] <= cur[:, None]
    score = jnp.where(forced, BIG, jnp.where(avail, imp, -BIG))
    _, sel = lax.top_k(score, min(N_SELECT, n_sel))
    pad = n_sel * SEL_LEN - S
    ks = jnp.pad(kv[:, :, 2], ((0, 0), (0, pad), (0, 0), (0, 0))).reshape(B, n_sel, SEL_LEN, G, HEAD_DIM).transpose(0, 3, 1, 2, 4)
    vs = jnp.pad(kv[:, :, 3], ((0, 0), (0, pad), (0, 0), (0, 0))).reshape(B, n_sel, SEL_LEN, G, HEAD_DIM).transpose(0, 3, 1, 2, 4)
    o_slc = query_blocks(lambda q_, s_, p_: selected_attend(q_, ks, vs, s_, p_, table), [q, sel, q_pos], [1, 2, 0])
    if past_win is None:
        win_all = win_new
        o_win = window_prompt(q, win_new[:, :, 0], win_new[:, :, 1], table)
    else:
        win_all = jnp.concatenate([past_win, win_new], axis=1)
        lw = win_all.shape[1]
        k_pos = pos0 + T - lw + jnp.arange(lw)
        o_win = gqa_window_attend(q, win_all[:, :, 0], win_all[:, :, 1], q_pos, k_pos, table)
    win_state = win_all[:, win_all.shape[1] - min(WINDOW, win_all.shape[1]):]
    o = (gates[:, :, 0, :, None] * o_cmp + gates[:, :, 1, :, None] * o_slc + gates[:, :, 2, :, None] * o_win)
    return o.reshape(B * T, D_MODEL), win_state


def fox_attend(q, dq, q_pos, k, v, dk, k_pos):
    logits = jnp.einsum('bqhd,bkhd->bhqk', q, k).astype(jnp.float32) * HEAD_DIM ** -0.5
    logits = logits + dq.transpose(0, 2, 1)[..., None] - dk.transpose(0, 2, 1)[:, :, None, :]
    p = masked_softmax(logits, q_pos[:, None] >= k_pos[None, :], -1)
    return jnp.einsum('bhqk,bkhd->bqhd', p, v)


def fox_core(q, kv_new, f_raw, b_f, past_kv, past_logf, pos0):
    B, T = q.shape[:2]
    logf_new = jax.nn.log_sigmoid(f_raw + b_f.astype(jnp.float32))
    if past_kv is None:
        kv, logf = kv_new, logf_new
    else:
        kv = jnp.concatenate([past_kv, kv_new], axis=1)
        logf = jnp.concatenate([past_logf.astype(jnp.float32), logf_new], axis=1)
    S = kv.shape[1]
    cum = jnp.cumsum(logf, axis=1)
    k_pos = jnp.arange(S)
    q_pos = pos0 + jnp.arange(T)
    k, v = kv[:, :, 0], kv[:, :, 1]
    o = query_blocks(lambda q_, dq_, p_: fox_attend(q_, dq_, p_, k, v, cum, k_pos), [q, cum[:, S - T:], q_pos], [1, 1, 0])
    return o.reshape(B * T, D_MODEL), logf_new


def gla_chunk(S, q, k, v, g):
    C = q.shape[1]
    b = jnp.cumsum(g, axis=1)
    causal = jnp.tril(jnp.ones((C, C), dtype=bool))
    diff = b[:, :, None] - b[:, None, :]
    decay = jnp.exp(jnp.where(causal[None, :, :, None, None], diff, NEG))
    A = jnp.einsum('bthd,bshd,btshd->bhts', q, k, decay)
    o = jnp.einsum('bhts,bshv->bthv', A, v) + jnp.einsum('bthd,bhdv->bthv', q * jnp.exp(b), S)
    b_last = b[:, -1]
    S_new = jnp.exp(b_last)[..., None] * S + jnp.einsum('bshd,bshv->bhdv', k * jnp.exp(b_last[:, None] - b), v)
    return o, S_new


def gla_scan(S0, q, k, v, g):
    B, T = q.shape[:2]
    c = GLA_CHUNK if T % GLA_CHUNK == 0 else T
    nc = T // c

    def to_chunks(a):
        return jnp.moveaxis(a.reshape(B, nc, c, *a.shape[2:]), 1, 0)

    def step(S, blk):
        o, S = gla_chunk(S, *blk)
        return S, o

    S_fin, o = lax.scan(step, S0, tuple(to_chunks(a) for a in (q, k, v, g)))
    return jnp.moveaxis(o, 0, 1).reshape(B, T, GLA_HEADS, GLA_DV), S_fin


def gla_core(q, k, v, r, a_lr, w_a2, b_a, norm_g, S0):
    B, T = q.shape[:2]
    q = q.reshape(B, T, GLA_HEADS, GLA_DK) * GLA_DK ** -0.5
    k = k.reshape(B, T, GLA_HEADS, GLA_DK)
    v = v.reshape(B, T, GLA_HEADS, GLA_DV)
    g = jax.nn.log_sigmoid(a_lr @ w_a2.astype(jnp.float32) + b_a.astype(jnp.float32)) / GLA_TAU
    g = g.reshape(B, T, GLA_HEADS, GLA_DK)
    if S0 is None:
        S0 = jnp.zeros((B, GLA_HEADS, GLA_DK, GLA_DV), jnp.float32)
    o, S_fin = gla_scan(S0.astype(jnp.float32), q, k, v, g)
    xf = o
    y = xf * lax.rsqrt(jnp.mean(xf * xf, axis=-1, keepdims=True) + EPS) * norm_g
    o = y.reshape(B, T, D_MODEL) * jax.nn.silu(r)
    return o.reshape(B * T, D_MODEL), S_fin


NSA_SPLITS = ((0, D_MODEL), (D_MODEL, D_MODEL + NSA_KV_COLS),
              (D_MODEL + NSA_KV_COLS, D_MODEL + NSA_KV_COLS + NSA_WIN_COLS),
              (D_MODEL + NSA_KV_COLS + NSA_WIN_COLS, D_MODEL + NSA_KV_COLS + NSA_WIN_COLS + 3 * N_HEADS))
FOX_SPLITS = ((0, D_MODEL), (D_MODEL, 3 * D_MODEL), (3 * D_MODEL, 3 * D_MODEL + N_HEADS))
GLA_SPLITS = ((0, GLA_QK), (GLA_QK, 2 * GLA_QK), (2 * GLA_QK, 2 * GLA_QK + D_MODEL),
              (2 * GLA_QK + D_MODEL, 2 * GLA_QK + 2 * D_MODEL),
              (2 * GLA_QK + 2 * D_MODEL, 2 * GLA_QK + 2 * D_MODEL + GLA_RANK))


def kernel(x_prompt, x_sample, cache_nsa_kv, state_nsa_win, cache_fox_kv, cache_fox_logf, state_gla, page_table, rel_bias, norm_g, final_g, nsa_w_in, nsa_phi_pe, nsa_phi_w1, nsa_phi_w2, nsa_w_out, fox_w_in, fox_b_f, fox_w_out, gla_w_in, gla_w_a2, gla_b_a, gla_norm_g, gla_w_out, mlp_w1, mlp_w2):
    bp, tp, _ = x_prompt.shape
    bs, ts, _ = x_sample.shape
    mp, ms = bp * tp, bs * ts
    pos_s = page_table.shape[1] * PAGE_SIZE
    x = jnp.concatenate([x_prompt.reshape(mp, D_MODEL), x_sample.reshape(ms, D_MODEL)], axis=0)

    def both(a, shape_p, shape_s):
        return a[:mp].reshape(shape_p), a[mp:].reshape(shape_s)

    nsa_kv_p, nsa_kv_s, nsa_win_p, nsa_win_s = [], [], [], []
    fox_kv_p, fox_kv_s, fox_lf_p, fox_lf_s, gla_p, gla_s = [], [], [], [], [], []
    G = NSA_KV_GROUPS
    for i in range(DEPTH):
        kind, li = i % N_MIXERS, i // N_MIXERS
        if kind == 0:
            q, kvn, winn, gt = norm_proj(x, norm_g[i, 0], nsa_w_in[li].astype(BF16), NSA_SPLITS)
            q_p, q_s = both(q, (bp, tp, N_HEADS, HEAD_DIM), (bs, ts, N_HEADS, HEAD_DIM))
            kv_p, kv_s = both(kvn, (bp, tp, 4, G, HEAD_DIM), (bs, ts, 4, G, HEAD_DIM))
            w_p, w_s = both(winn, (bp, tp, 2, G, HEAD_DIM), (bs, ts, 2, G, HEAD_DIM))
            g_p, g_s = both(gt, (bp, tp, 3 * N_HEADS), (bs, ts, 3 * N_HEADS))
            o_p, ws_p = nsa_core(q_p, kv_p, w_p, g_p, nsa_phi_pe[li], nsa_phi_w1[li], nsa_phi_w2[li], rel_bias,
                                 None, None, 0)
            o_s, ws_s = nsa_core(q_s, kv_s, w_s, g_s, nsa_phi_pe[li], nsa_phi_w1[li], nsa_phi_w2[li], rel_bias,
                                 gather_pages(cache_nsa_kv[li], page_table), state_nsa_win[li], pos_s)
            nsa_kv_p.append(kv_p); nsa_kv_s.append(kv_s); nsa_win_p.append(ws_p); nsa_win_s.append(ws_s)
            w_out = nsa_w_out[li]
        elif kind == 1:
            q, kvn, fr = norm_proj(x, norm_g[i, 0], fox_w_in[li].astype(BF16), FOX_SPLITS)
            q_p, q_s = both(q, (bp, tp, N_HEADS, HEAD_DIM), (bs, ts, N_HEADS, HEAD_DIM))
            kv_p, kv_s = both(kvn, (bp, tp, 2, N_HEADS, HEAD_DIM), (bs, ts, 2, N_HEADS, HEAD_DIM))
            f_p, f_s = both(fr, (bp, tp, N_HEADS), (bs, ts, N_HEADS))
            o_p, lf_p = fox_core(q_p, kv_p, f_p, fox_b_f[li], None, None, 0)
            o_s, lf_s = fox_core(q_s, kv_s, f_s, fox_b_f[li], gather_pages(cache_fox_kv[li], page_table),
                                 gather_pages(cache_fox_logf[li], page_table), pos_s)
            fox_kv_p.append(kv_p); fox_kv_s.append(kv_s); fox_lf_p.append(lf_p); fox_lf_s.append(lf_s)
            w_out = fox_w_out[li]
        else:
            q, k, v, r, a = norm_proj(x, norm_g[i, 0], gla_w_in[li].astype(BF16), GLA_SPLITS)
            q_p, q_s = both(q, (bp, tp, GLA_QK), (bs, ts, GLA_QK))
            k_p, k_s = both(k, (bp, tp, GLA_QK), (bs, ts, GLA_QK))
            v_p, v_s = both(v, (bp, tp, D_MODEL), (bs, ts, D_MODEL))
            r_p, r_s = both(r, (bp, tp, D_MODEL), (bs, ts, D_MODEL))
            a_p, a_s = both(a, (bp, tp, GLA_RANK), (bs, ts, GLA_RANK))
            o_p, s_p = gla_core(q_p, k_p, v_p, r_p, a_p, gla_w_a2[li], gla_b_a[li], gla_norm_g[li], None)
            o_s, s_s = gla_core(q_s, k_s, v_s, r_s, a_s, gla_w_a2[li], gla_b_a[li], gla_norm_g[li], state_gla[li])
            gla_p.append(s_p); gla_s.append(s_s)
            w_out = gla_w_out[li]
        o = jnp.concatenate([o_p, o_s], axis=0)
        x = out_mlp(x, o, w_out.astype(BF16), norm_g[i, 1], mlp_w1[i].astype(BF16), mlp_w2[i].astype(BF16))
    y = final_norm(x, final_g)
    y_p, y_s = both(y, (bp, tp, D_MODEL), (bs, ts, D_MODEL))
    return (y_p, y_s, jnp.stack(nsa_kv_p), jnp.stack(nsa_kv_s), jnp.stack(nsa_win_p), jnp.stack(nsa_win_s),
            jnp.stack(fox_kv_p), jnp.stack(fox_kv_s), jnp.stack(fox_lf_p), jnp.stack(fox_lf_s),
            jnp.stack(gla_p), jnp.stack(gla_s))
```

```python
import functools
import math

import jax
import jax.numpy as jnp
from jax import lax
from jax.experimental import pallas as pl
from jax.experimental.pallas import tpu as pltpu

D_MODEL = 1024
DEPTH = 4
PAGE_SIZE = 128
N_MIXERS = 3
N_HEADS = 16
HEAD_DIM = D_MODEL // N_HEADS
NSA_KV_GROUPS = 4
NSA_HPG = N_HEADS // NSA_KV_GROUPS
CMP_LEN = 32
CMP_STRIDE = 16
CMP_HIDDEN = 2 * HEAD_DIM
SEL_LEN = 64
N_SELECT = 16
WINDOW = 512
N_BUCKETS = 32
MAX_DISTANCE = 128
GLA_HEADS = 4
GLA_DK = D_MODEL // 2 // GLA_HEADS
GLA_DV = D_MODEL // GLA_HEADS
GLA_QK = GLA_HEADS * GLA_DK
GLA_RANK = 16
GLA_TAU = 16.0
GLA_CHUNK = 64
D_FF = 4 * D_MODEL
Q_BLOCK = 128
EPS = 1e-6
NEG = -1e30
BIG = 1e9
NSA_KV_COLS = 4 * NSA_KV_GROUPS * HEAD_DIM
NSA_WIN_COLS = 2 * NSA_KV_GROUPS * HEAD_DIM
GROUP_COLS = NSA_KV_GROUPS * HEAD_DIM

F32 = jnp.float32
BF16 = jnp.bfloat16

V7X_VMEM_LIMIT_BYTES = 48 * 1024 * 1024
V7X_LANES = 128
TOKEN_TILE = 512
FF_TILE = 1024
ATT_TILE = 256
CMP_Q_TILE = 128
SEL_PAD = V7X_LANES
T5_SATURATION_DIST = 113


def _cparams(*sem):
    return pltpu.CompilerParams(dimension_semantics=sem, vmem_limit_bytes=V7X_VMEM_LIMIT_BYTES)


def _rms(x, g):
    return x * lax.rsqrt(jnp.mean(x * x, axis=-1, keepdims=True) + EPS) * g


def _sigmoid(x):
    return 1.0 / (1.0 + jnp.exp(-x))


def _log_sigmoid(x):
    return jnp.minimum(x, 0.0) - jnp.log(1.0 + jnp.exp(-jnp.abs(x)))


def _gelu_tanh(x):
    return 0.5 * x * (1.0 + jnp.tanh(math.sqrt(2.0 / math.pi) * (x + 0.044715 * (x * x * x))))


def _dot(a, b):
    return jnp.dot(a, b, preferred_element_type=F32)


def _dot_nt(a, b):
    return lax.dot_general(a, b, (((1,), (1,)), ((), ())), preferred_element_type=F32)


def _split_bf16(x, parts):
    out = []
    for _ in range(parts):
        hi = x.astype(BF16)
        out.append(hi)
        x = x - hi.astype(F32)
    return out


def _t5_bucket(dist):
    n = jnp.maximum(dist, 0)
    max_exact = N_BUCKETS // 2
    nf = jnp.maximum(n, 1).astype(F32)
    log_part = max_exact + (jnp.log(nf / max_exact) / math.log(MAX_DISTANCE / max_exact)
                            * (N_BUCKETS - max_exact)).astype(jnp.int32)
    return jnp.where(n < max_exact, n, jnp.minimum(log_part, N_BUCKETS - 1))


def _bias_from_bucket(bucket, tbl_ref, h):
    val = jnp.full(bucket.shape, tbl_ref[0, h], F32)
    for k in range(1, N_BUCKETS):
        val = jnp.where(bucket == k, tbl_ref[k, h], val)
    return val


def _smem_spec():
    return pl.BlockSpec(memory_space=pltpu.SMEM)


def _norm_proj_kernel(x_ref, g_ref, w_ref, *out_refs, splits):
    h = _rms(x_ref[...], g_ref[...]).astype(BF16)
    for (lo, hi, _), o_ref in zip(splits, out_refs):
        o_ref[...] = _dot(h, w_ref[:, lo:hi]).astype(o_ref.dtype)


def norm_proj(x, g, w, splits):
    m = x.shape[0]
    n = w.shape[1]
    kern = functools.partial(_norm_proj_kernel, splits=splits)
    return pl.pallas_call(
        kern,
        grid=(m // TOKEN_TILE,),
        in_specs=[
            pl.BlockSpec((TOKEN_TILE, D_MODEL), lambda i: (i, 0)),
            pl.BlockSpec((1, D_MODEL), lambda i: (0, 0)),
            pl.BlockSpec((D_MODEL, n), lambda i: (0, 0)),
        ],
        out_specs=[pl.BlockSpec((TOKEN_TILE, hi - lo), lambda i: (i, 0)) for lo, hi, _ in splits],
        out_shape=[jax.ShapeDtypeStruct((m, hi - lo), dt) for lo, hi, dt in splits],
        compiler_params=_cparams("parallel"),
        name="norm_proj",
    )(x, g.reshape(1, D_MODEL), w)


def _out_mlp_kernel(x_ref, *refs, n_o):
    o_refs = refs[:n_o]
    wo_ref, g_ref, w1_ref, w2_ref, y_ref, x1_ref, h_ref, acc_ref = refs[n_o:]
    j = pl.program_id(1)

    @pl.when(j == 0)
    def _():
        o = o_refs[0][...]
        for r in o_refs[1:]:
            o = o + r[...]
        x1 = x_ref[...] + _dot(o.astype(BF16), wo_ref[...])
        x1_ref[...] = x1
        h_ref[...] = _rms(x1, g_ref[...]).astype(BF16)
        acc_ref[...] = jnp.zeros_like(acc_ref)

    a = _dot(h_ref[...], w1_ref[...])
    a = jnp.square(jnp.maximum(a, 0.0)).astype(BF16)
    acc_ref[...] += _dot(a, w2_ref[...])

    @pl.when(j == pl.num_programs(1) - 1)
    def _():
        y_ref[...] = x1_ref[...] + acc_ref[...]


def out_mlp(x, o_parts, w_out, g, w1, w2):
    m = x.shape[0]
    n_o = len(o_parts)
    kern = functools.partial(_out_mlp_kernel, n_o=n_o)
    tok = pl.BlockSpec((TOKEN_TILE, D_MODEL), lambda i, j: (i, 0))
    return pl.pallas_call(
        kern,
        grid=(m // TOKEN_TILE, D_FF // FF_TILE),
        in_specs=[tok] + [tok] * n_o + [
            pl.BlockSpec((D_MODEL, D_MODEL), lambda i, j: (0, 0)),
            pl.BlockSpec((1, D_MODEL), lambda i, j: (0, 0)),
            pl.BlockSpec((D_MODEL, FF_TILE), lambda i, j: (0, j)),
            pl.BlockSpec((FF_TILE, D_MODEL), lambda i, j: (j, 0)),
        ],
        out_specs=tok,
        out_shape=jax.ShapeDtypeStruct((m, D_MODEL), F32),
        scratch_shapes=[
            pltpu.VMEM((TOKEN_TILE, D_MODEL), F32),
            pltpu.VMEM((TOKEN_TILE, D_MODEL), BF16),
            pltpu.VMEM((TOKEN_TILE, D_MODEL), F32),
        ],
        compiler_params=_cparams("parallel", "arbitrary"),
        name="out_mlp",
    )(x, *o_parts, w_out, g.reshape(1, D_MODEL), w1, w2)


def _final_norm_kernel(x_ref, g_ref, y_ref):
    y_ref[...] = _rms(x_ref[...], g_ref[...])


def final_norm(x, g):
    m = x.shape[0]
    return pl.pallas_call(
        _final_norm_kernel,
        grid=(m // TOKEN_TILE,),
        in_specs=[pl.BlockSpec((TOKEN_TILE, D_MODEL), lambda i: (i, 0)),
                  pl.BlockSpec((1, D_MODEL), lambda i: (0, 0))],
        out_specs=pl.BlockSpec((TOKEN_TILE, D_MODEL), lambda i: (i, 0)),
        out_shape=jax.ShapeDtypeStruct((m, D_MODEL), F32),
        compiler_params=_cparams("parallel"),
        name="final_norm",
    )(x, g.reshape(1, D_MODEL))


def _t5_tiles_kernel(tbl_ref, o_ref, *, tile):
    h = pl.program_id(0)
    iq = lax.broadcasted_iota(jnp.int32, (tile, tile), 0)
    ik = lax.broadcasted_iota(jnp.int32, (tile, tile), 1)
    for dd in range(2):
        o_ref[dd] = _bias_from_bucket(_t5_bucket(dd * tile + iq - ik), tbl_ref, h)


def t5_tiles(table, tile):
    assert 2 * tile - (tile - 1) >= T5_SATURATION_DIST
    return pl.pallas_call(
        functools.partial(_t5_tiles_kernel, tile=tile),
        grid=(N_HEADS,),
        in_specs=[_smem_spec()],
        out_specs=pl.BlockSpec((None, 2, tile, tile), lambda h: (h, 0, 0, 0)),
        out_shape=jax.ShapeDtypeStruct((N_HEADS, 2, tile, tile), F32),
        compiler_params=_cparams("arbitrary"),
        name="t5_tiles",
    )(table)


def _compress_kernel(x_ref, pe_ref, w1_ref, w2_ref, o_ref, *, rows):
    half = CMP_LEN // 2
    top = jnp.zeros((rows, NSA_KV_GROUPS * CMP_HIDDEN), F32)
    bot = jnp.zeros((rows, NSA_KV_GROUPS * CMP_HIDDEN), F32)
    for l in range(half):
        a = x_ref[:, l, :]
        top = top + _dot((a + pe_ref[l]).astype(BF16), w1_ref[l])
        bot = bot + _dot((a + pe_ref[half + l]).astype(BF16), w1_ref[half + l])
    hid = top + pltpu.roll(bot, rows - 1, 0)
    o_ref[...] = _dot(_gelu_tanh(hid).astype(BF16), w2_ref[...])


def compress_prompt(kvn, n_batch, seq, pe_t, w1_bd, w2_bd):
    rows = seq // CMP_STRIDE
    return pl.pallas_call(
        functools.partial(_compress_kernel, rows=rows),
        grid=(n_batch, 2),
        in_specs=[
            pl.BlockSpec((rows, CMP_STRIDE, GROUP_COLS), lambda b, j: (b, 0, j)),
            pl.BlockSpec((None, CMP_LEN, 1, GROUP_COLS), lambda b, j: (j, 0, 0, 0)),
            pl.BlockSpec((None, CMP_LEN, GROUP_COLS, NSA_KV_GROUPS * CMP_HIDDEN), lambda b, j: (j, 0, 0, 0)),
            pl.BlockSpec((None, NSA_KV_GROUPS * CMP_HIDDEN, GROUP_COLS), lambda b, j: (j, 0, 0)),
        ],
        out_specs=pl.BlockSpec((None, None, rows, GROUP_COLS), lambda b, j: (b, j, 0, 0)),
        out_shape=jax.ShapeDtypeStruct((n_batch, 2, rows, GROUP_COLS), F32),
        compiler_params=_cparams("parallel", "arbitrary"),
        name="nsa_compress",
    )(kvn.reshape(-1, CMP_STRIDE, kvn.shape[-1]), pe_t, w1_bd, w2_bd)


def compress_weights(phi_pe, phi_w1, phi_w2):
    eye = jnp.eye(NSA_KV_GROUPS, dtype=F32)
    w1 = phi_w1.reshape(2, CMP_LEN, HEAD_DIM, CMP_HIDDEN)
    w1_bd = jnp.einsum('gh,jldc->jlgdhc', eye, w1).reshape(2, CMP_LEN, GROUP_COLS, NSA_KV_GROUPS * CMP_HIDDEN)
    w2_bd = jnp.einsum('gh,jcd->jgchd', eye, phi_w2).reshape(2, NSA_KV_GROUPS * CMP_HIDDEN, GROUP_COLS)
    pe_t = jnp.tile(phi_pe, (1, 1, NSA_KV_GROUPS)).reshape(2, CMP_LEN, 1, GROUP_COLS)
    return pe_t, w1_bd.astype(BF16), w2_bd.astype(BF16)


def _cmp_select_kernel(tbl_ref, q_ref, kt_ref, v_ref, gate_ref, o_ref, sel_ref, bias_ref, score_ref, *,
                       tq, n_cmp_pad, n_cmp, n_sel, pos0):
    i = pl.program_id(0)
    b = pl.program_id(1)
    q0 = pos0 + i * tq
    qpos = q0 + lax.broadcasted_iota(jnp.int32, (tq, n_cmp_pad), 0)
    c_idx = lax.broadcasted_iota(jnp.int32, (tq, n_cmp_pad), 1)
    dist = qpos - (c_idx * CMP_STRIDE + CMP_LEN - 1)
    valid = (dist >= 0) & (c_idx < n_cmp)

    @pl.when(b == 0)
    def _():
        bucket = _t5_bucket(dist)
        for h in range(N_HEADS):
            bias_ref[h] = _bias_from_bucket(bucket, tbl_ref, h)

    j_t = lax.broadcasted_iota(jnp.int32, (SEL_PAD, tq), 0)
    q_t = q0 + lax.broadcasted_iota(jnp.int32, (SEL_PAD, tq), 1)
    cur = jnp.right_shift(q_t, 6)
    forced = (j_t == 0) | (j_t == cur) | (j_t == cur - 1)
    avail = j_t <= cur
    j_i = lax.broadcasted_iota(jnp.int32, (SEL_PAD, n_cmp_pad), 0) * SEL_LEN
    c_i = lax.broadcasted_iota(jnp.int32, (SEL_PAD, n_cmp_pad), 1) * CMP_STRIDE
    inc_t = jnp.where((c_i < j_i + SEL_LEN) & (c_i + CMP_LEN > j_i) & (j_i < n_sel * SEL_LEN)
                      & (c_i < n_cmp * CMP_STRIDE), 1.0, 0.0).astype(BF16)
    j_s = lax.broadcasted_iota(jnp.int32, (n_sel, tq), 0)

    for g in range(NSA_KV_GROUPS):
        psum = jnp.zeros((tq, n_cmp_pad), F32)
        for h in range(NSA_HPG):
            hh = g * NSA_HPG + h
            qh = (q_ref[:, hh * HEAD_DIM:(hh + 1) * HEAD_DIM].astype(F32) * HEAD_DIM ** -0.5).astype(BF16)
            s = _dot(qh, kt_ref[g]) + bias_ref[hh]
            s = jnp.where(valid, s, NEG)
            m = jnp.max(s, axis=-1, keepdims=True)
            e = jnp.where(valid, jnp.exp(s - m), 0.0)
            p = e / jnp.maximum(jnp.sum(e, axis=-1, keepdims=True), 1e-30)
            gate = _sigmoid(gate_ref[:, hh:hh + 1])
            o_ref[:, hh * HEAD_DIM:(hh + 1) * HEAD_DIM] = _dot(p.astype(BF16), v_ref[g]) * gate
            psum = psum + p
        p_hi, p_lo = _split_bf16(psum, 2)
        imp_t = _dot_nt(inc_t, p_hi) + _dot_nt(inc_t, p_lo)
        score_t = jnp.where(forced, BIG, jnp.where(avail, imp_t, -BIG))
        score_ref[...] = score_t
        sc = score_t[:n_sel]

        def rank_step(r, rank):
            row = jnp.broadcast_to(score_ref[pl.ds(r, 1), :], (n_sel, tq))
            ahead = jnp.where(row > sc, 1.0, jnp.where((row == sc) & (j_s > r), 1.0, 0.0))
            return rank + ahead

        rank = lax.fori_loop(0, n_sel, rank_step, jnp.zeros((n_sel, tq), F32))
        chosen = jnp.where(rank < float(N_SELECT), 1.0, 0.0)
        chosen = jnp.concatenate([chosen, jnp.zeros((SEL_PAD - n_sel, tq), F32)], axis=0)
        sel_ref[:, g * SEL_PAD:(g + 1) * SEL_PAD] = chosen.T.astype(BF16)


def cmp_select_prompt(table, q_bf, kt_cmp, v_cmp, gates, n_batch, seq, n_cmp, n_sel):
    tq = CMP_Q_TILE
    nq = seq // tq
    m = n_batch * seq
    n_cmp_pad = kt_cmp.shape[-1]
    kern = functools.partial(_cmp_select_kernel, tq=tq, n_cmp_pad=n_cmp_pad, n_cmp=n_cmp, n_sel=n_sel, pos0=0)
    return pl.pallas_call(
        kern,
        grid=(nq, n_batch),
        in_specs=[
            _smem_spec(),
            pl.BlockSpec((tq, D_MODEL), lambda i, b: (b * nq + i, 0)),
            pl.BlockSpec((None, NSA_KV_GROUPS, HEAD_DIM, n_cmp_pad), lambda i, b: (b, 0, 0, 0)),
            pl.BlockSpec((None, NSA_KV_GROUPS, n_cmp_pad, HEAD_DIM), lambda i, b: (b, 0, 0, 0)),
            pl.BlockSpec((tq, 3 * N_HEADS), lambda i, b: (b * nq + i, 0)),
        ],
        out_specs=[
            pl.BlockSpec((tq, D_MODEL), lambda i, b: (b * nq + i, 0)),
            pl.BlockSpec((tq, NSA_KV_GROUPS * SEL_PAD), lambda i, b: (b * nq + i, 0)),
        ],
        out_shape=[
            jax.ShapeDtypeStruct((m, D_MODEL), F32),
            jax.ShapeDtypeStruct((m, NSA_KV_GROUPS * SEL_PAD), BF16),
        ],
        scratch_shapes=[
            pltpu.VMEM((N_HEADS, tq, n_cmp_pad), F32),
            pltpu.VMEM((SEL_PAD, tq), F32),
        ],
        compiler_params=_cparams("arbitrary", "arbitrary"),
        name="nsa_cmp_select",
    )(table, q_bf, kt_cmp, v_cmp, gates)


def _nsa_attn_kernel(tbl_ref, q_ref, k_ref, v_ref, gate_ref, toep_ref, *rest, mode, tile, branch):
    if mode == "sel":
        sel_ref, o_ref, qs_ref, m_ref, l_ref, acc_ref = rest
    else:
        o_ref, qs_ref, m_ref, l_ref, acc_ref = rest
    i = pl.program_id(1)
    iq = lax.broadcasted_iota(jnp.int32, (tile, tile), 0)
    ik = lax.broadcasted_iota(jnp.int32, (tile, tile), 1)
    causal = iq >= ik
    above = ik > iq

    for hh in range(N_HEADS):
        qs_ref[hh] = (q_ref[:, hh * HEAD_DIM:(hh + 1) * HEAD_DIM].astype(F32) * HEAD_DIM ** -0.5).astype(BF16)

    for g in range(NSA_KV_GROUPS):
        m_ref[...] = jnp.full(m_ref.shape, NEG, F32)
        l_ref[...] = jnp.zeros(l_ref.shape, F32)
        acc_ref[...] = jnp.zeros(acc_ref.shape, F32)

        def attend(kt, dd, g=g):
            r0 = pl.multiple_of(kt * tile, tile)
            kg = k_ref[pl.ds(r0, tile), :][:, g * HEAD_DIM:(g + 1) * HEAD_DIM]
            vg = v_ref[pl.ds(r0, tile), :][:, g * HEAD_DIM:(g + 1) * HEAD_DIM]
            if mode == "sel":
                jrow = lax.broadcasted_iota(jnp.int32, (SEL_PAD, tile), 0)
                kcol = lax.broadcasted_iota(jnp.int32, (SEL_PAD, tile), 1)
                expand = jnp.where(jrow == jnp.right_shift(r0 + kcol, 6), 1.0, 0.0).astype(BF16)
                mk = _dot(sel_ref[:, g * SEL_PAD:(g + 1) * SEL_PAD], expand) > 0.5
                if dd == 0:
                    mk = mk & causal
            else:
                mk = causal if dd == 0 else (above if dd == 2 else None)
            for h in range(NSA_HPG):
                hh = g * NSA_HPG + h
                s = _dot_nt(qs_ref[hh], kg)
                if dd == 2:
                    s = s + tbl_ref[N_BUCKETS - 1, hh]
                else:
                    s = s + toep_ref[hh, dd]
                if mk is not None:
                    s = jnp.where(mk, s, NEG)
                m_old = m_ref[h]
                m_new = jnp.maximum(m_old, jnp.max(s, axis=-1, keepdims=True))
                alpha = jnp.exp(m_old - m_new)
                p = jnp.exp(s - m_new)
                if mk is not None:
                    p = jnp.where(mk, p, 0.0)
                l_ref[h] = alpha * l_ref[h] + jnp.sum(p, axis=-1, keepdims=True)
                acc_ref[h] = alpha * acc_ref[h] + _dot(p.astype(BF16), vg)
                m_ref[h] = m_new

        if mode == "sel":
            def far_body(kt, carry):
                attend(kt, 2)
                return carry
            lax.fori_loop(0, jnp.maximum(i - 1, 0), far_body, 0)
        else:
            @pl.when(i >= 2)
            def _():
                attend(i - 2, 2)

        @pl.when(i >= 1)
        def _():
            attend(i - 1, 1)

        attend(i, 0)

        for h in range(NSA_HPG):
            hh = g * NSA_HPG + h
            gate = _sigmoid(gate_ref[:, branch * N_HEADS + hh:branch * N_HEADS + hh + 1])
            o_ref[:, hh * HEAD_DIM:(hh + 1) * HEAD_DIM] = acc_ref[h] / jnp.maximum(l_ref[h], 1e-30) * gate


def nsa_attn_prompt(mode, table, toep, q_bf, kv_src, k_col, v_col, gates, sel, n_batch, seq):
    tile = ATT_TILE
    assert WINDOW == 2 * tile
    nq = seq // tile
    m = n_batch * seq
    branch = 1 if mode == "sel" else 2
    kern = functools.partial(_nsa_attn_kernel, mode=mode, tile=tile, branch=branch)
    in_specs = [
        _smem_spec(),
        pl.BlockSpec((tile, D_MODEL), lambda b, i: (b * nq + i, 0)),
        pl.BlockSpec((seq, GROUP_COLS), lambda b, i: (b, k_col)),
        pl.BlockSpec((seq, GROUP_COLS), lambda b, i: (b, v_col)),
        pl.BlockSpec((tile, 3 * N_HEADS), lambda b, i: (b * nq + i, 0)),
        pl.BlockSpec((N_HEADS, 2, tile, tile), lambda b, i: (0, 0, 0, 0)),
    ]
    args = [table, q_bf, kv_src, kv_src, gates, toep]
    if mode == "sel":
        in_specs.append(pl.BlockSpec((tile, NSA_KV_GROUPS * SEL_PAD), lambda b, i: (b * nq + i, 0)))
        args.append(sel)
    return pl.pallas_call(
        kern,
        grid=(n_batch, nq),
        in_specs=in_specs,
        out_specs=pl.BlockSpec((tile, D_MODEL), lambda b, i: (b * nq + i, 0)),
        out_shape=jax.ShapeDtypeStruct((m, D_MODEL), F32),
        scratch_shapes=[
            pltpu.VMEM((N_HEADS, tile, HEAD_DIM), BF16),
            pltpu.VMEM((NSA_HPG, tile, 1), F32),
            pltpu.VMEM((NSA_HPG, tile, 1), F32),
            pltpu.VMEM((NSA_HPG, tile, HEAD_DIM), F32),
        ],
        compiler_params=_cparams("parallel", "arbitrary"),
        name="nsa_attn_" + mode,
    )(*args)


def nsa_prompt(table, toep, q_bf, kvn, kv_bf, win_bf, gates, cw, n_batch, seq):
    pe_t, w1_bd, w2_bd = cw
    n_cmp = (seq - CMP_LEN) // CMP_STRIDE + 1
    n_sel = -(-seq // SEL_LEN)
    assert n_sel <= SEL_PAD and n_sel % 8 == 0
    ctok = compress_prompt(kvn, n_batch, seq, pe_t, w1_bd, w2_bd)
    rows = ctok.shape[2]
    ctok = ctok.reshape(n_batch, 2, rows, NSA_KV_GROUPS, HEAD_DIM).astype(BF16)
    kt_cmp = ctok[:, 0].transpose(0, 2, 3, 1)
    v_cmp = ctok[:, 1].transpose(0, 2, 1, 3)
    o_cmp, sel = cmp_select_prompt(table, q_bf, kt_cmp, v_cmp, gates, n_batch, seq, n_cmp, n_sel)
    o_slc = nsa_attn_prompt("sel", table, toep, q_bf, kv_bf, 2, 3, gates, sel, n_batch, seq)
    o_win = nsa_attn_prompt("win", table, toep, q_bf, win_bf, 0, 1, gates, None, n_batch, seq)
    return o_cmp, o_slc, o_win


CUMSUM_CHUNK = 256
HEAD_BLOCK = 4


def _tril_bf16(n):
    r = lax.broadcasted_iota(jnp.int32, (n, n), 0)
    c = lax.broadcasted_iota(jnp.int32, (n, n), 1)
    return jnp.where(r >= c, 1.0, 0.0).astype(BF16)


def _cumsum_rows(x, tri):
    return sum(_dot(tri, part) for part in _split_bf16(x, 3))


def _fox_prep_kernel(f_ref, b_ref, lf_ref, d_ref, *, seq):
    tri = _tril_bf16(CUMSUM_CHUNK)
    carry = jnp.zeros((1, N_HEADS), F32)
    for c in range(seq // CUMSUM_CHUNK):
        rows = slice(c * CUMSUM_CHUNK, (c + 1) * CUMSUM_CHUNK)
        lf = _log_sigmoid(f_ref[rows, :] + b_ref[...])
        lf_ref[rows, :] = lf
        d = carry + _cumsum_rows(lf, tri)
        d_ref[rows, :] = d
        carry = d[CUMSUM_CHUNK - 1:CUMSUM_CHUNK, :]


def fox_prep_prompt(f_raw, b_f, n_batch, seq):
    m = n_batch * seq
    blk = pl.BlockSpec((seq, N_HEADS), lambda b: (b, 0))
    return pl.pallas_call(
        functools.partial(_fox_prep_kernel, seq=seq),
        grid=(n_batch,),
        in_specs=[blk, pl.BlockSpec((1, N_HEADS), lambda b: (0, 0))],
        out_specs=[blk, blk],
        out_shape=[jax.ShapeDtypeStruct((m, N_HEADS), F32)] * 2,
        compiler_params=_cparams("parallel"),
        name="fox_prep",
    )(f_raw, b_f.reshape(1, N_HEADS))


def _fox_attn_kernel(q_ref, k_ref, v_ref, dq_ref, dk_ref, o_ref, qs_ref, m_ref, l_ref, acc_ref, *, tile):
    i = pl.program_id(2)
    iq = lax.broadcasted_iota(jnp.int32, (tile, tile), 0)
    ik = lax.broadcasted_iota(jnp.int32, (tile, tile), 1)
    causal = iq >= ik
    for h in range(HEAD_BLOCK):
        qs_ref[h] = (q_ref[:, h * HEAD_DIM:(h + 1) * HEAD_DIM].astype(F32) * HEAD_DIM ** -0.5).astype(BF16)
    m_ref[...] = jnp.full(m_ref.shape, NEG, F32)
    l_ref[...] = jnp.zeros(l_ref.shape, F32)
    acc_ref[...] = jnp.zeros(acc_ref.shape, F32)

    def attend(kt, diag):
        r0 = pl.multiple_of(kt * tile, tile)
        kb = k_ref[pl.ds(r0, tile), :]
        vb = v_ref[pl.ds(r0, tile), :]
        dk = dk_ref[kt]
        for h in range(HEAD_BLOCK):
            lanes = slice(h * HEAD_DIM, (h + 1) * HEAD_DIM)
            s = _dot_nt(qs_ref[h], kb[:, lanes]) + dq_ref[:, h:h + 1] - dk[h:h + 1, :]
            if diag:
                s = jnp.where(causal, s, NEG)
            m_old = m_ref[h]
            m_new = jnp.maximum(m_old, jnp.max(s, axis=-1, keepdims=True))
            alpha = jnp.exp(m_old - m_new)
            p = jnp.exp(s - m_new)
            if diag:
                p = jnp.where(causal, p, 0.0)
            l_ref[h] = alpha * l_ref[h] + jnp.sum(p, axis=-1, keepdims=True)
            acc_ref[h] = alpha * acc_ref[h] + _dot(p.astype(BF16), vb[:, lanes])
            m_ref[h] = m_new

    def body(kt, carry):
        attend(kt, False)
        return carry

    lax.fori_loop(0, i, body, 0)
    attend(i, True)
    for h in range(HEAD_BLOCK):
        o_ref[:, h * HEAD_DIM:(h + 1) * HEAD_DIM] = acc_ref[h] / jnp.maximum(l_ref[h], 1e-30)


def fox_attn_prompt(q_bf, kv_bf, d_cum, n_batch, seq):
    tile = ATT_TILE
    nq = seq // tile
    m = n_batch * seq
    nhb = N_HEADS // HEAD_BLOCK
    d4 = d_cum[:m].reshape(n_batch, seq, nhb, HEAD_BLOCK)
    dq = d4.transpose(0, 2, 1, 3)
    dk = d4.reshape(n_batch, nq, tile, nhb, HEAD_BLOCK).transpose(0, 3, 1, 4, 2)
    cols = HEAD_BLOCK * HEAD_DIM
    return pl.pallas_call(
        functools.partial(_fox_attn_kernel, tile=tile),
        grid=(n_batch, nhb, nq),
        in_specs=[
            pl.BlockSpec((tile, cols), lambda b, g, i: (b * nq + i, g)),
            pl.BlockSpec((seq, cols), lambda b, g, i: (b, g)),
            pl.BlockSpec((seq, cols), lambda b, g, i: (b, nhb + g)),
            pl.BlockSpec((None, None, tile, HEAD_BLOCK), lambda b, g, i: (b, g, i, 0)),
            pl.BlockSpec((None, None, nq, HEAD_BLOCK, tile), lambda b, g, i: (b, g, 0, 0, 0)),
        ],
        out_specs=pl.BlockSpec((tile, cols), lambda b, g, i: (b * nq + i, g)),
        out_shape=jax.ShapeDtypeStruct((m, D_MODEL), F32),
        scratch_shapes=[
            pltpu.VMEM((HEAD_BLOCK, tile, HEAD_DIM), BF16),
            pltpu.VMEM((HEAD_BLOCK, tile, 1), F32),
            pltpu.VMEM((HEAD_BLOCK, tile, 1), F32),
            pltpu.VMEM((HEAD_BLOCK, tile, HEAD_DIM), F32),
        ],
        compiler_params=_cparams("parallel", "parallel", "arbitrary"),
        name="fox_attn",
    )(q_bf, kv_bf, kv_bf, dq, dk)


GLA_SUB = 16
GLA_BLOCK = 512


def _gla_kernel(q_ref, k_ref, v_ref, r_ref, a_ref, wa_ref, ba_ref, ng_ref, s0_ref, o_ref, sfin_ref, s_ref, *,
                n_chunks, chunk):
    t_blk = pl.program_id(2)

    @pl.when(t_blk == 0)
    def _():
        s_ref[...] = s0_ref[...]

    tri = _tril_bf16(chunk)
    it = lax.broadcasted_iota(jnp.int32, (chunk, chunk), 0)
    js = lax.broadcasted_iota(jnp.int32, (chunk, chunk), 1)
    same_sub = jnp.right_shift(it, 4) == jnp.right_shift(js, 4)
    n_sub = chunk // GLA_SUB

    def chunk_step(c, carry):
        r0 = pl.multiple_of(c * chunk, chunk)
        rows = pl.ds(r0, chunk)
        q = q_ref[rows, :] * GLA_DK ** -0.5
        k = k_ref[rows, :]
        v = v_ref[rows, :].astype(BF16)
        g = _log_sigmoid(_dot(a_ref[rows, :].astype(BF16), wa_ref[...]) + ba_ref[...]) / GLA_TAU
        b = _cumsum_rows(g, tri)
        s_old = s_ref[...]
        o = _dot((q * jnp.exp(b)).astype(BF16), s_old.astype(BF16))
        a_mat = jnp.zeros((chunk, chunk), F32)
        for d in range(GLA_SUB):
            if d == 0:
                val = jnp.sum(q * k, axis=-1, keepdims=True)
            else:
                k_sh = pltpu.roll(k, d, 0)
                b_sh = pltpu.roll(b, d, 0)
                val = jnp.sum(q * k_sh * jnp.exp(jnp.minimum(b - b_sh, 0.0)), axis=-1, keepdims=True)
            a_mat = a_mat + jnp.where((it - js == d) & same_sub, val, 0.0)
        if n_sub > 1:
            blocks = [jnp.zeros((GLA_SUB, chunk), F32)]
            for a in range(1, n_sub):
                ref_b = b[a * GLA_SUB - 1:a * GLA_SUB, :]
                qa = q[a * GLA_SUB:(a + 1) * GLA_SUB] * jnp.exp(b[a * GLA_SUB:(a + 1) * GLA_SUB] - ref_b)
                kk = k * jnp.exp(jnp.minimum(ref_b - b, 0.0))
                blocks.append(_dot_nt(qa.astype(BF16), kk.astype(BF16)))
            off = jnp.concatenate(blocks, axis=0)
            a_mat = a_mat + jnp.where(jnp.right_shift(js, 4) < jnp.right_shift(it, 4), off, 0.0)
        o = o + _dot(a_mat.astype(BF16), v)
        b_last = b[chunk - 1:chunk, :]
        kd = k * jnp.exp(b_last - b)
        pad_k = jnp.zeros((GLA_DK - chunk, GLA_DK), F32)
        pad_v = jnp.zeros((GLA_DK - chunk, GLA_DV), BF16)
        kd_t = jnp.concatenate([kd, pad_k], axis=0).T
        dec = jnp.broadcast_to(jnp.exp(b_last), (GLA_DK, GLA_DK)).T
        s_ref[...] = (s_old * jnp.concatenate([dec] * (GLA_DV // GLA_DK), axis=1)
                      + _dot(kd_t.astype(BF16), jnp.concatenate([v, pad_v], axis=0)))
        y = o * lax.rsqrt(jnp.mean(o * o, axis=-1, keepdims=True) + EPS) * ng_ref[...]
        rr = r_ref[rows, :]
        o_ref[rows, :] = y * (rr * _sigmoid(rr))
        return carry

    lax.fori_loop(0, n_chunks, chunk_step, 0)

    @pl.when(t_blk == pl.num_programs(2) - 1)
    def _():
        sfin_ref[...] = s_ref[...]


def gla_prompt(q, k, v, r, a_lr, w_a2, b_a, norm_g, s0, n_batch, seq, row_block0=0):
    chunk = GLA_CHUNK if seq % GLA_CHUNK == 0 else seq
    blk = min(GLA_BLOCK, seq)
    nt = seq // blk
    m = n_batch * seq
    kern = functools.partial(_gla_kernel, n_chunks=blk // chunk, chunk=chunk)
    row = lambda b, h, t: row_block0 + b * nt + t
    return pl.pallas_call(
        kern,
        grid=(n_batch, GLA_HEADS, nt),
        in_specs=[
            pl.BlockSpec((blk, GLA_DK), lambda b, h, t: (row(b, h, t), h)),
            pl.BlockSpec((blk, GLA_DK), lambda b, h, t: (row(b, h, t), h)),
            pl.BlockSpec((blk, GLA_DV), lambda b, h, t: (row(b, h, t), h)),
            pl.BlockSpec((blk, GLA_DV), lambda b, h, t: (row(b, h, t), h)),
            pl.BlockSpec((blk, GLA_RANK), lambda b, h, t: (row(b, h, t), 0)),
            pl.BlockSpec((GLA_RANK, GLA_DK), lambda b, h, t: (0, h)),
            pl.BlockSpec((1, GLA_DK), lambda b, h, t: (0, h)),
            pl.BlockSpec((1, GLA_DV), lambda b, h, t: (0, 0)),
            pl.BlockSpec((None, None, GLA_DK, GLA_DV), lambda b, h, t: (b, h, 0, 0)),
        ],
        out_specs=[
            pl.BlockSpec((blk, GLA_DV), lambda b, h, t: (b * nt + t, h)),
            pl.BlockSpec((None, None, GLA_DK, GLA_DV), lambda b, h, t: (b, h, 0, 0)),
        ],
        out_shape=[
            jax.ShapeDtypeStruct((m, D_MODEL), F32),
            jax.ShapeDtypeStruct((n_batch, GLA_HEADS, GLA_DK, GLA_DV), F32),
        ],
        scratch_shapes=[pltpu.VMEM((GLA_DK, GLA_DV), F32)],
        compiler_params=_cparams("parallel", "parallel", "arbitrary"),
        name="gla",
    )(q, k, v, r, a_lr, w_a2.astype(BF16), b_a.reshape(1, GLA_QK), norm_g.reshape(1, GLA_DV), s0)


def masked_softmax(logits, mask, axis):
    logits = jnp.where(mask, logits.astype(jnp.float32), NEG)
    m = jnp.max(logits, axis=axis, keepdims=True)
    e = jnp.exp(logits - m) * mask
    return e / jnp.maximum(jnp.sum(e, axis=axis, keepdims=True), 1e-30)


def t5_bucket(dist):
    n = jnp.maximum(dist, 0)
    max_exact = N_BUCKETS // 2
    nf = jnp.maximum(n, 1).astype(jnp.float32)
    log_part = max_exact + (jnp.log(nf / max_exact) / math.log(MAX_DISTANCE / max_exact)
                            * (N_BUCKETS - max_exact)).astype(jnp.int32)
    return jnp.where(n < max_exact, n, jnp.minimum(log_part, N_BUCKETS - 1))


def head_bias(dist, table):
    b = table.astype(jnp.float32)[t5_bucket(dist)]
    return b.reshape(*dist.shape, NSA_KV_GROUPS, NSA_HPG).transpose(2, 3, 0, 1)


def gather_pages(pool, page_table):
    rows = pool[page_table]
    return rows.reshape(page_table.shape[0], -1, *pool.shape[2:])


def query_blocks(fn, args, axes):
    T = args[0].shape[axes[0]]
    if T % Q_BLOCK != 0 or T <= Q_BLOCK:
        return fn(*args)
    nb = T // Q_BLOCK

    def split(a, ax):
        a = a.reshape(a.shape[:ax] + (nb, Q_BLOCK) + a.shape[ax + 1:])
        return jnp.moveaxis(a, ax, 0)

    out = lax.map(lambda blk: fn(*blk), tuple(split(a, ax) for a, ax in zip(args, axes)))
    out = jnp.moveaxis(out, 0, 1)
    return out.reshape(out.shape[:1] + (T,) + out.shape[3:])


def compress_tokens(kv, pe, w1, w2):
    B, S = kv.shape[:2]
    n_cmp = (S - CMP_LEN) // CMP_STRIDE + 1
    idx = jnp.arange(n_cmp)[:, None] * CMP_STRIDE + jnp.arange(CMP_LEN)[None, :]
    blocks = kv[:, idx] + pe[None, None, :, None, :]
    z = blocks.transpose(0, 1, 3, 2, 4).reshape(B, n_cmp, NSA_KV_GROUPS, CMP_LEN * HEAD_DIM)
    return jax.nn.gelu(z @ w1) @ w2


def gqa_window_attend(q, k, v, q_pos, k_pos, table):
    B, Tq = q.shape[:2]
    qg = q.reshape(B, Tq, NSA_KV_GROUPS, NSA_HPG, HEAD_DIM)
    logits = jnp.einsum('bqghd,bkgd->bghqk', qg, k).astype(jnp.float32) * HEAD_DIM ** -0.5
    dist = q_pos[:, None] - k_pos[None, :]
    mask = (dist >= 0) & (dist < WINDOW) & (k_pos[None, :] >= 0)
    p = masked_softmax(logits + head_bias(dist, table), mask, -1)
    o = jnp.einsum('bghqk,bkgd->bqghd', p, v)
    return o.reshape(B, Tq, N_HEADS, HEAD_DIM)


def nsa_core_sample(q, kv_new, win_new, gates, phi_pe, phi_w1, phi_w2, table, past_kv, past_win, pos0):
    B, T = q.shape[:2]
    G = NSA_KV_GROUPS
    gates = jax.nn.sigmoid(gates).reshape(B, T, 3, N_HEADS)
    kv = jnp.concatenate([past_kv, kv_new], axis=1)
    S = kv.shape[1]
    q_pos = pos0 + jnp.arange(T)
    qg = q.reshape(B, T, G, NSA_HPG, HEAD_DIM)
    k_cmp = compress_tokens(kv[:, :, 0], phi_pe[0], phi_w1[0], phi_w2[0])
    v_cmp = compress_tokens(kv[:, :, 1], phi_pe[1], phi_w1[1], phi_w2[1])
    n_cmp = k_cmp.shape[1]
    c_start = jnp.arange(n_cmp) * CMP_STRIDE
    dist_c = q_pos[:, None] - (c_start + CMP_LEN - 1)[None, :]
    logits_c = jnp.einsum('bqghd,bcgd->bghqc', qg, k_cmp).astype(jnp.float32) * HEAD_DIM ** -0.5
    p_cmp = masked_softmax(logits_c + head_bias(dist_c, table), dist_c >= 0, -1)
    o_cmp = jnp.einsum('bghqc,bcgd->bqghd', p_cmp, v_cmp).reshape(B, T, N_HEADS, HEAD_DIM)
    n_sel = -(-S // SEL_LEN)
    s_start = jnp.arange(n_sel) * SEL_LEN
    incidence = ((c_start[:, None] < s_start[None, :] + SEL_LEN)
                 & (c_start[:, None] + CMP_LEN > s_start[None, :])).astype(jnp.float32)
    imp = jnp.einsum('bghqc,cj->bgqj', p_cmp, incidence)
    cur = q_pos // SEL_LEN
    j = jnp.arange(n_sel)
    forced = (j[None, :] == 0) | (j[None, :] == cur[:, None]) | (j[None, :] == cur[:, None] - 1)
    avail = j[None, :] <= cur[:, None]
    score = jnp.where(forced, BIG, jnp.where(avail, imp, -BIG))
    _, sel = lax.top_k(score, min(N_SELECT, n_sel))
    member = jnp.sum(jax.nn.one_hot(sel, n_sel, dtype=jnp.float32), axis=-2) > 0.5
    k_pos = jnp.arange(S)
    key_member = jnp.take(member, k_pos // SEL_LEN, axis=-1)
    dist = q_pos[:, None] - k_pos[None, :]
    mask = key_member[:, :, None] & (dist >= 0)[None, None, None]
    logits = jnp.einsum('bqghd,bkgd->bghqk', qg, kv[:, :, 2]).astype(jnp.float32) * HEAD_DIM ** -0.5
    p = masked_softmax(logits + head_bias(dist, table), mask, -1)
    o_slc = jnp.einsum('bghqk,bkgd->bqghd', p, kv[:, :, 3]).reshape(B, T, N_HEADS, HEAD_DIM)
    win_all = jnp.concatenate([past_win, win_new], axis=1)
    lw = win_all.shape[1]
    kw_pos = pos0 + T - lw + jnp.arange(lw)
    o_win = gqa_window_attend(q, win_all[:, :, 0], win_all[:, :, 1], q_pos, kw_pos, table)
    win_state = win_all[:, win_all.shape[1] - min(WINDOW, win_all.shape[1]):]
    o = (gates[:, :, 0, :, None] * o_cmp + gates[:, :, 1, :, None] * o_slc + gates[:, :, 2, :, None] * o_win)
    return o.reshape(B * T, D_MODEL), win_state


def fox_attend(q, dq, q_pos, k, v, dk, k_pos):
    logits = jnp.einsum('bqhd,bkhd->bhqk', q, k).astype(jnp.float32) * HEAD_DIM ** -0.5
    logits = logits + dq.transpose(0, 2, 1)[..., None] - dk.transpose(0, 2, 1)[:, :, None, :]
    p = masked_softmax(logits, q_pos[:, None] >= k_pos[None, :], -1)
    return jnp.einsum('bhqk,bkhd->bqhd', p, v)


def fox_core(q, kv_new, f_raw, b_f, past_kv, past_logf, pos0):
    B, T = q.shape[:2]
    logf_new = jax.nn.log_sigmoid(f_raw + b_f.astype(jnp.float32))
    if past_kv is None:
        kv, logf = kv_new, logf_new
    else:
        kv = jnp.concatenate([past_kv, kv_new], axis=1)
        logf = jnp.concatenate([past_logf.astype(jnp.float32), logf_new], axis=1)
    S = kv.shape[1]
    cum = jnp.cumsum(logf, axis=1)
    k_pos = jnp.arange(S)
    q_pos = pos0 + jnp.arange(T)
    k, v = kv[:, :, 0], kv[:, :, 1]
    o = query_blocks(lambda q_, dq_, p_: fox_attend(q_, dq_, p_, k, v, cum, k_pos), [q, cum[:, S - T:], q_pos], [1, 1, 0])
    return o.reshape(B * T, D_MODEL), logf_new


def gla_chunk(S, q, k, v, g):
    C = q.shape[1]
    b = jnp.cumsum(g, axis=1)
    causal = jnp.tril(jnp.ones((C, C), dtype=bool))
    diff = b[:, :, None] - b[:, None, :]
    decay = jnp.exp(jnp.where(causal[None, :, :, None, None], diff, NEG))
    A = jnp.einsum('bthd,bshd,btshd->bhts', q, k, decay)
    o = jnp.einsum('bhts,bshv->bthv', A, v) + jnp.einsum('bthd,bhdv->bthv', q * jnp.exp(b), S)
    b_last = b[:, -1]
    S_new = jnp.exp(b_last)[..., None] * S + jnp.einsum('bshd,bshv->bhdv', k * jnp.exp(b_last[:, None] - b), v)
    return o, S_new


def gla_scan(S0, q, k, v, g):
    B, T = q.shape[:2]
    c = GLA_CHUNK if T % GLA_CHUNK == 0 else T
    nc = T // c

    def to_chunks(a):
        return jnp.moveaxis(a.reshape(B, nc, c, *a.shape[2:]), 1, 0)

    def step(S, blk):
        o, S = gla_chunk(S, *blk)
        return S, o

    S_fin, o = lax.scan(step, S0, tuple(to_chunks(a) for a in (q, k, v, g)))
    return jnp.moveaxis(o, 0, 1).reshape(B, T, GLA_HEADS, GLA_DV), S_fin


def gla_core(q, k, v, r, a_lr, w_a2, b_a, norm_g, S0):
    B, T = q.shape[:2]
    q = q.reshape(B, T, GLA_HEADS, GLA_DK) * GLA_DK ** -0.5
    k = k.reshape(B, T, GLA_HEADS, GLA_DK)
    v = v.reshape(B, T, GLA_HEADS, GLA_DV)
    g = jax.nn.log_sigmoid(a_lr @ w_a2.astype(jnp.float32) + b_a.astype(jnp.float32)) / GLA_TAU
    g = g.reshape(B, T, GLA_HEADS, GLA_DK)
    if S0 is None:
        S0 = jnp.zeros((B, GLA_HEADS, GLA_DK, GLA_DV), jnp.float32)
    o, S_fin = gla_scan(S0.astype(jnp.float32), q, k, v, g)
    xf = o
    y = xf * lax.rsqrt(jnp.mean(xf * xf, axis=-1, keepdims=True) + EPS) * norm_g
    o = y.reshape(B, T, D_MODEL) * jax.nn.silu(r)
    return o.reshape(B * T, D_MODEL), S_fin


_C1 = D_MODEL + NSA_KV_COLS
_C2 = _C1 + NSA_WIN_COLS
NSA_SPLITS = ((0, D_MODEL, BF16), (D_MODEL, _C1, F32), (D_MODEL, _C1, BF16), (_C1, _C2, F32), (_C1, _C2, BF16),
              (_C2, _C2 + 3 * N_HEADS, F32))
FOX_SPLITS = ((0, D_MODEL, BF16), (D_MODEL, 3 * D_MODEL, F32), (D_MODEL, 3 * D_MODEL, BF16),
              (3 * D_MODEL, 3 * D_MODEL + N_HEADS, F32))
GLA_SPLITS = ((0, GLA_QK, F32), (GLA_QK, 2 * GLA_QK, F32), (2 * GLA_QK, 2 * GLA_QK + D_MODEL, F32),
              (2 * GLA_QK + D_MODEL, 2 * GLA_QK + 2 * D_MODEL, F32),
              (2 * GLA_QK + 2 * D_MODEL, 2 * GLA_QK + 2 * D_MODEL + GLA_RANK, F32))


def kernel(x_prompt, x_sample, cache_nsa_kv, state_nsa_win, cache_fox_kv, cache_fox_logf, state_gla, page_table, rel_bias, norm_g, final_g, nsa_w_in, nsa_phi_pe, nsa_phi_w1, nsa_phi_w2, nsa_w_out, fox_w_in, fox_b_f, fox_w_out, gla_w_in, gla_w_a2, gla_b_a, gla_norm_g, gla_w_out, mlp_w1, mlp_w2):
    bp, tp, _ = x_prompt.shape
    bs, ts, _ = x_sample.shape
    mp, ms = bp * tp, bs * ts
    pos_s = page_table.shape[1] * PAGE_SIZE
    x = jnp.concatenate([x_prompt.reshape(mp, D_MODEL), x_sample.reshape(ms, D_MODEL)], axis=0)
    toep = t5_tiles(rel_bias, ATT_TILE)

    def both(a, shape_p, shape_s):
        return a[:mp].reshape(shape_p), a[mp:].reshape(shape_s)

    nsa_kv_p, nsa_kv_s, nsa_win_p, nsa_win_s = [], [], [], []
    fox_kv_p, fox_kv_s, fox_lf_p, fox_lf_s, gla_p, gla_s = [], [], [], [], [], []
    G = NSA_KV_GROUPS
    for i in range(DEPTH):
        kind, li = i % N_MIXERS, i // N_MIXERS
        if kind == 0:
            q_bf, kvn, kv_bf, winn, win_bf, gt = norm_proj(x, norm_g[i, 0], nsa_w_in[li].astype(BF16), NSA_SPLITS)
            cw = compress_weights(nsa_phi_pe[li], nsa_phi_w1[li], nsa_phi_w2[li])
            o_parts_p = nsa_prompt(rel_bias, toep, q_bf, kvn, kv_bf, win_bf, gt, cw, bp, tp)
            kv_p, kv_s = both(kvn, (bp, tp, 4, G, HEAD_DIM), (bs, ts, 4, G, HEAD_DIM))
            w_p, w_s = both(winn, (bp, tp, 2, G, HEAD_DIM), (bs, ts, 2, G, HEAD_DIM))
            q_s = q_bf[mp:].astype(F32).reshape(bs, ts, N_HEADS, HEAD_DIM)
            g_s = gt[mp:].reshape(bs, ts, 3 * N_HEADS)
            o_s, ws_s = nsa_core_sample(q_s, kv_s, w_s, g_s, nsa_phi_pe[li], nsa_phi_w1[li], nsa_phi_w2[li],
                                        rel_bias, gather_pages(cache_nsa_kv[li], page_table), state_nsa_win[li],
                                        pos_s)
            ws_p = w_p[:, tp - min(WINDOW, tp):]
            nsa_kv_p.append(kv_p); nsa_kv_s.append(kv_s); nsa_win_p.append(ws_p); nsa_win_s.append(ws_s)
            zero_s = jnp.zeros((ms, D_MODEL), F32)
            o_parts = [jnp.concatenate([o_parts_p[0], o_s], axis=0)] + [
                jnp.concatenate([op, zero_s], axis=0) for op in o_parts_p[1:]]
            w_out = nsa_w_out[li]
        elif kind == 1:
            q_bf, kvn, kv_bf, fr = norm_proj(x, norm_g[i, 0], fox_w_in[li].astype(BF16), FOX_SPLITS)
            kv_p, kv_s = both(kvn, (bp, tp, 2, N_HEADS, HEAD_DIM), (bs, ts, 2, N_HEADS, HEAD_DIM))
            lf_all, d_all = fox_prep_prompt(fr, fox_b_f[li], bp, tp)
            o_p = fox_attn_prompt(q_bf, kv_bf, d_all, bp, tp)
            lf_p = lf_all.reshape(bp, tp, N_HEADS)
            q_s = q_bf[mp:].astype(F32).reshape(bs, ts, N_HEADS, HEAD_DIM)
            f_s = fr[mp:].reshape(bs, ts, N_HEADS)
            o_s, lf_s = fox_core(q_s, kv_s, f_s, fox_b_f[li], gather_pages(cache_fox_kv[li], page_table),
                                 gather_pages(cache_fox_logf[li], page_table), pos_s)
            fox_kv_p.append(kv_p); fox_kv_s.append(kv_s); fox_lf_p.append(lf_p); fox_lf_s.append(lf_s)
            o_parts = [jnp.concatenate([o_p, o_s], axis=0)]
            w_out = fox_w_out[li]
        else:
            q, k, v, r, a = norm_proj(x, norm_g[i, 0], gla_w_in[li].astype(BF16), GLA_SPLITS)
            s_zero = jnp.zeros((bp, GLA_HEADS, GLA_DK, GLA_DV), F32)
            o_p, s_p = gla_prompt(q, k, v, r, a, gla_w_a2[li], gla_b_a[li], gla_norm_g[li], s_zero, bp, tp)
            q_s, k_s, v_s, r_s, a_s = (t[mp:].reshape(bs, ts, -1) for t in (q, k, v, r, a))
            o_s, s_s = gla_core(q_s, k_s, v_s, r_s, a_s, gla_w_a2[li], gla_b_a[li], gla_norm_g[li], state_gla[li])
            gla_p.append(s_p); gla_s.append(s_s)
            o_parts = [jnp.concatenate([o_p, o_s], axis=0)]
            w_out = gla_w_out[li]
        x = out_mlp(x, o_parts, w_out.astype(BF16), norm_g[i, 1], mlp_w1[i].astype(BF16), mlp_w2[i].astype(BF16))
    y = final_norm(x, final_g)
    y_p, y_s = both(y, (bp, tp, D_MODEL), (bs, ts, D_MODEL))
    return (y_p, y_s, jnp.stack(nsa_kv_p), jnp.stack(nsa_kv_s), jnp.stack(nsa_win_p), jnp.stack(nsa_win_s),
            jnp.stack(fox_kv_p), jnp.stack(fox_kv_s), jnp.stack(fox_lf_p), jnp.stack(fox_lf_s),
            jnp.stack(gla_p), jnp.stack(gla_s))
```

```python
import functools
import math

import jax
import jax.numpy as jnp
from jax import lax
from jax.experimental import pallas as pl
from jax.experimental.pallas import tpu as pltpu

D_MODEL = 1024
DEPTH = 4
PAGE_SIZE = 128
N_MIXERS = 3
N_HEADS = 16
HEAD_DIM = D_MODEL // N_HEADS
NSA_KV_GROUPS = 4
NSA_HPG = N_HEADS // NSA_KV_GROUPS
CMP_LEN = 32
CMP_STRIDE = 16
CMP_HIDDEN = 2 * HEAD_DIM
SEL_LEN = 64
N_SELECT = 16
WINDOW = 512
N_BUCKETS = 32
MAX_DISTANCE = 128
GLA_HEADS = 4
GLA_DK = D_MODEL // 2 // GLA_HEADS
GLA_DV = D_MODEL // GLA_HEADS
GLA_QK = GLA_HEADS * GLA_DK
GLA_RANK = 16
GLA_TAU = 16.0
GLA_CHUNK = 64
D_FF = 4 * D_MODEL
Q_BLOCK = 128
EPS = 1e-6
NEG = -1e30
BIG = 1e9
NSA_KV_COLS = 4 * NSA_KV_GROUPS * HEAD_DIM
NSA_WIN_COLS = 2 * NSA_KV_GROUPS * HEAD_DIM
GROUP_COLS = NSA_KV_GROUPS * HEAD_DIM

F32 = jnp.float32
BF16 = jnp.bfloat16

V7X_VMEM_LIMIT_BYTES = 48 * 1024 * 1024
V7X_LANES = 128
TOKEN_TILE = 512
FF_TILE = 1024
ATT_TILE = 256
CMP_Q_TILE = 128
SEL_PAD = V7X_LANES
T5_SATURATION_DIST = 113


def _cparams(*sem):
    return pltpu.CompilerParams(dimension_semantics=sem, vmem_limit_bytes=V7X_VMEM_LIMIT_BYTES)


def _rms(x, g):
    return x * lax.rsqrt(jnp.mean(x * x, axis=-1, keepdims=True) + EPS) * g


def _sigmoid(x):
    return 1.0 / (1.0 + jnp.exp(-x))


def _log_sigmoid(x):
    return jnp.minimum(x, 0.0) - jnp.log(1.0 + jnp.exp(-jnp.abs(x)))


def _gelu_tanh(x):
    return 0.5 * x * (1.0 + jnp.tanh(math.sqrt(2.0 / math.pi) * (x + 0.044715 * (x * x * x))))


def _dot(a, b):
    return jnp.dot(a, b, preferred_element_type=F32)


def _dot_nt(a, b):
    return lax.dot_general(a, b, (((1,), (1,)), ((), ())), preferred_element_type=F32)


def _split_bf16(x, parts):
    out = []
    for _ in range(parts):
        hi = x.astype(BF16)
        out.append(hi)
        x = x - hi.astype(F32)
    return out


def _t5_bucket(dist):
    n = jnp.maximum(dist, 0)
    max_exact = N_BUCKETS // 2
    nf = jnp.maximum(n, 1).astype(F32)
    log_part = max_exact + (jnp.log(nf / max_exact) / math.log(MAX_DISTANCE / max_exact)
                            * (N_BUCKETS - max_exact)).astype(jnp.int32)
    return jnp.where(n < max_exact, n, jnp.minimum(log_part, N_BUCKETS - 1))


def _bias_from_bucket(bucket, tbl_ref, h):
    val = jnp.full(bucket.shape, tbl_ref[0, h], F32)
    for k in range(1, N_BUCKETS):
        val = jnp.where(bucket == k, tbl_ref[k, h], val)
    return val


def _div_pow2(x, d):
    assert d & (d - 1) == 0
    return jnp.right_shift(x, d.bit_length() - 1)


def _mod_pow2(x, d):
    assert d & (d - 1) == 0
    return jnp.bitwise_and(x, d - 1)


def _smem_spec():
    return pl.BlockSpec(memory_space=pltpu.SMEM)


def _norm_proj_kernel(x_ref, g_ref, w_ref, wt_ref, *out_refs, splits, splits_t):
    h = _rms(x_ref[...], g_ref[...]).astype(BF16)
    for (lo, hi, _), o_ref in zip(splits, out_refs):
        o_ref[...] = _dot(h, w_ref[:, lo:hi]).astype(o_ref.dtype)
    for (lo, hi, _), o_ref in zip(splits_t, out_refs[len(splits):]):
        o_ref[...] = _dot_nt(wt_ref[lo:hi, :], h).astype(o_ref.dtype)


def norm_proj(x, g, w, splits, splits_t=()):
    m = x.shape[0]
    n = w.shape[1]
    w_bf = w.astype(BF16)
    kern = functools.partial(_norm_proj_kernel, splits=splits, splits_t=splits_t)
    return pl.pallas_call(
        kern,
        grid=(m // TOKEN_TILE,),
        in_specs=[
            pl.BlockSpec((TOKEN_TILE, D_MODEL), lambda i: (i, 0)),
            pl.BlockSpec((1, D_MODEL), lambda i: (0, 0)),
            pl.BlockSpec((D_MODEL, n), lambda i: (0, 0)),
            pl.BlockSpec((n, D_MODEL), lambda i: (0, 0)),
        ],
        out_specs=[pl.BlockSpec((TOKEN_TILE, hi - lo), lambda i: (i, 0)) for lo, hi, _ in splits]
        + [pl.BlockSpec((hi - lo, TOKEN_TILE), lambda i: (0, i)) for lo, hi, _ in splits_t],
        out_shape=[jax.ShapeDtypeStruct((m, hi - lo), dt) for lo, hi, dt in splits]
        + [jax.ShapeDtypeStruct((hi - lo, m), dt) for lo, hi, dt in splits_t],
        compiler_params=_cparams("parallel"),
        name="norm_proj",
    )(x, g.reshape(1, D_MODEL), w_bf, w_bf.T)


def _out_mlp_kernel(x_ref, *refs, n_o):
    o_refs = refs[:n_o]
    wo_ref, g_ref, w1_ref, w2_ref, y_ref, x1_ref, h_ref, acc_ref = refs[n_o:]
    j = pl.program_id(1)

    @pl.when(j == 0)
    def _():
        o = o_refs[0][...]
        for r in o_refs[1:]:
            o = o + r[...]
        x1 = x_ref[...] + _dot(o.astype(BF16), wo_ref[...])
        x1_ref[...] = x1
        h_ref[...] = _rms(x1, g_ref[...]).astype(BF16)
        acc_ref[...] = jnp.zeros_like(acc_ref)

    a = _dot(h_ref[...], w1_ref[...])
    a = jnp.square(jnp.maximum(a, 0.0)).astype(BF16)
    acc_ref[...] += _dot(a, w2_ref[...])

    @pl.when(j == pl.num_programs(1) - 1)
    def _():
        y_ref[...] = x1_ref[...] + acc_ref[...]


def out_mlp(x, o_parts, w_out, g, w1, w2):
    m = x.shape[0]
    n_o = len(o_parts)
    kern = functools.partial(_out_mlp_kernel, n_o=n_o)
    tok = pl.BlockSpec((TOKEN_TILE, D_MODEL), lambda i, j: (i, 0))
    return pl.pallas_call(
        kern,
        grid=(m // TOKEN_TILE, D_FF // FF_TILE),
        in_specs=[tok] + [tok] * n_o + [
            pl.BlockSpec((D_MODEL, D_MODEL), lambda i, j: (0, 0)),
            pl.BlockSpec((1, D_MODEL), lambda i, j: (0, 0)),
            pl.BlockSpec((D_MODEL, FF_TILE), lambda i, j: (0, j)),
            pl.BlockSpec((FF_TILE, D_MODEL), lambda i, j: (j, 0)),
        ],
        out_specs=tok,
        out_shape=jax.ShapeDtypeStruct((m, D_MODEL), F32),
        scratch_shapes=[
            pltpu.VMEM((TOKEN_TILE, D_MODEL), F32),
            pltpu.VMEM((TOKEN_TILE, D_MODEL), BF16),
            pltpu.VMEM((TOKEN_TILE, D_MODEL), F32),
        ],
        compiler_params=_cparams("parallel", "arbitrary"),
        name="out_mlp",
    )(x, *o_parts, w_out, g.reshape(1, D_MODEL), w1, w2)


def _final_norm_kernel(x_ref, g_ref, y_ref):
    y_ref[...] = _rms(x_ref[...], g_ref[...])


def final_norm(x, g):
    m = x.shape[0]
    return pl.pallas_call(
        _final_norm_kernel,
        grid=(m // TOKEN_TILE,),
        in_specs=[pl.BlockSpec((TOKEN_TILE, D_MODEL), lambda i: (i, 0)),
                  pl.BlockSpec((1, D_MODEL), lambda i: (0, 0))],
        out_specs=pl.BlockSpec((TOKEN_TILE, D_MODEL), lambda i: (i, 0)),
        out_shape=jax.ShapeDtypeStruct((m, D_MODEL), F32),
        compiler_params=_cparams("parallel"),
        name="final_norm",
    )(x, g.reshape(1, D_MODEL))


def _t5_tiles_kernel(tbl_ref, o_ref, *, tile):
    h = pl.program_id(0)
    ik = lax.broadcasted_iota(jnp.int32, (tile, tile), 0)
    iq = lax.broadcasted_iota(jnp.int32, (tile, tile), 1)
    for dd in range(2):
        o_ref[dd] = _bias_from_bucket(_t5_bucket(dd * tile + iq - ik), tbl_ref, h)


def t5_tiles(table, tile):
    assert 2 * tile - (tile - 1) >= T5_SATURATION_DIST
    return pl.pallas_call(
        functools.partial(_t5_tiles_kernel, tile=tile),
        grid=(N_HEADS,),
        in_specs=[_smem_spec()],
        out_specs=pl.BlockSpec((None, 2, tile, tile), lambda h: (h, 0, 0, 0)),
        out_shape=jax.ShapeDtypeStruct((N_HEADS, 2, tile, tile), F32),
        compiler_params=_cparams("arbitrary"),
        name="t5_tiles",
    )(table)


def _compress_kernel(x_ref, pe_ref, w1_ref, w2_ref, o_ref, *, rows, l_major=False):
    half = CMP_LEN // 2
    top = jnp.zeros((rows, NSA_KV_GROUPS * CMP_HIDDEN), F32)
    bot = jnp.zeros((rows, NSA_KV_GROUPS * CMP_HIDDEN), F32)
    for l in range(half):
        a = x_ref[l] if l_major else x_ref[:, l, :]
        top = top + _dot((a + pe_ref[l]).astype(BF16), w1_ref[l])
        bot = bot + _dot((a + pe_ref[half + l]).astype(BF16), w1_ref[half + l])
    hid = top + pltpu.roll(bot, rows - 1, 0)
    o_ref[...] = _dot(_gelu_tanh(hid).astype(BF16), w2_ref[...])


def compress_prompt(kvn, n_batch, seq, pe_t, w1_bd, w2_bd):
    rows = seq // CMP_STRIDE
    return pl.pallas_call(
        functools.partial(_compress_kernel, rows=rows),
        grid=(n_batch, 2),
        in_specs=[
            pl.BlockSpec((rows, CMP_STRIDE, GROUP_COLS), lambda b, j: (b, 0, j)),
            pl.BlockSpec((None, CMP_LEN, 1, GROUP_COLS), lambda b, j: (j, 0, 0, 0)),
            pl.BlockSpec((None, CMP_LEN, GROUP_COLS, NSA_KV_GROUPS * CMP_HIDDEN), lambda b, j: (j, 0, 0, 0)),
            pl.BlockSpec((None, NSA_KV_GROUPS * CMP_HIDDEN, GROUP_COLS), lambda b, j: (j, 0, 0)),
        ],
        out_specs=pl.BlockSpec((None, None, rows, GROUP_COLS), lambda b, j: (b, j, 0, 0)),
        out_shape=jax.ShapeDtypeStruct((n_batch, 2, rows, GROUP_COLS), F32),
        compiler_params=_cparams("parallel", "arbitrary"),
        name="nsa_compress",
    )(kvn.reshape(-1, CMP_STRIDE, kvn.shape[-1]), pe_t, w1_bd, w2_bd)


def compress_weights(phi_pe, phi_w1, phi_w2):
    eye = jnp.eye(NSA_KV_GROUPS, dtype=F32)
    w1 = phi_w1.reshape(2, CMP_LEN, HEAD_DIM, CMP_HIDDEN)
    w1_bd = jnp.einsum('gh,jldc->jlgdhc', eye, w1).reshape(2, CMP_LEN, GROUP_COLS, NSA_KV_GROUPS * CMP_HIDDEN)
    w2_bd = jnp.einsum('gh,jcd->jgchd', eye, phi_w2).reshape(2, NSA_KV_GROUPS * CMP_HIDDEN, GROUP_COLS)
    pe_t = jnp.tile(phi_pe, (1, 1, NSA_KV_GROUPS)).reshape(2, CMP_LEN, 1, GROUP_COLS)
    return pe_t, w1_bd.astype(BF16), w2_bd.astype(BF16)


def _fill_qz(qz_ref, qt_rows, row0, n_heads, tq):
    qz_ref[...] = jnp.zeros(qz_ref.shape, BF16)
    for h in range(n_heads):
        r = row0(h)
        qz_ref[r:r + HEAD_DIM, h * tq:(h + 1) * tq] = (qt_rows(h).astype(F32) * HEAD_DIM ** -0.5).astype(BF16)


def _online_softmax_step(s_t, mk, v_t, m_ref, l_ref, acc_ref, h):
    if mk is not None:
        s_t = jnp.where(mk, s_t, NEG)
    m_old = m_ref[h]
    m_new = jnp.maximum(m_old, jnp.max(s_t, axis=0, keepdims=True))
    alpha = jnp.exp(m_old - m_new)
    p = jnp.exp(s_t - m_new)
    if mk is not None:
        p = jnp.where(mk, p, 0.0)
    l_ref[h] = alpha * l_ref[h] + jnp.sum(p, axis=0, keepdims=True)
    acc_ref[h] = alpha * acc_ref[h] + _dot(v_t, p.astype(BF16))
    m_ref[h] = m_new


def _reset_softmax(m_ref, l_ref, acc_ref):
    m_ref[...] = jnp.full(m_ref.shape, NEG, F32)
    l_ref[...] = jnp.zeros(l_ref.shape, F32)
    acc_ref[...] = jnp.zeros(acc_ref.shape, F32)


def _cmp_select_t_kernel(tbl_ref, qt_ref, kc_ref, vct_ref, gt_ref, o_ref, selt_ref, bias_ref, score_ref, qz_ref, *,
                         tq, n_cmp_pad, n_cmp, n_sel, pos0):
    i = pl.program_id(0)
    b = pl.program_id(1)
    q0 = pos0 + i * tq
    c_idx = lax.broadcasted_iota(jnp.int32, (n_cmp_pad, tq), 0)
    qpos = q0 + lax.broadcasted_iota(jnp.int32, (n_cmp_pad, tq), 1)
    dist = qpos - (c_idx * CMP_STRIDE + CMP_LEN - 1)
    valid = (dist >= 0) & (c_idx < n_cmp)

    @pl.when(b == 0)
    def _():
        bucket = _t5_bucket(dist)
        for h in range(N_HEADS):
            bias_ref[h] = _bias_from_bucket(bucket, tbl_ref, h)

    j_t = lax.broadcasted_iota(jnp.int32, (SEL_PAD, tq), 0)
    q_t = q0 + lax.broadcasted_iota(jnp.int32, (SEL_PAD, tq), 1)
    cur = jnp.right_shift(q_t, 6)
    forced = (j_t == 0) | (j_t == cur) | (j_t == cur - 1)
    avail = j_t <= cur
    j_i = lax.broadcasted_iota(jnp.int32, (SEL_PAD, n_cmp_pad), 0) * SEL_LEN
    c_i = lax.broadcasted_iota(jnp.int32, (SEL_PAD, n_cmp_pad), 1) * CMP_STRIDE
    inc_t = jnp.where((c_i < j_i + SEL_LEN) & (c_i + CMP_LEN > j_i) & (j_i < n_sel * SEL_LEN)
                      & (c_i < n_cmp * CMP_STRIDE), 1.0, 0.0).astype(BF16)
    j_s = lax.broadcasted_iota(jnp.int32, (n_sel, tq), 0)
    kc = kc_ref[...]

    for g in range(NSA_KV_GROUPS):
        _fill_qz(qz_ref, lambda h: qt_ref[(g * NSA_HPG + h) * HEAD_DIM:(g * NSA_HPG + h + 1) * HEAD_DIM, :],
                 lambda h: g * HEAD_DIM, NSA_HPG, tq)
        s_all = _dot(kc, qz_ref[...])
        psum = jnp.zeros((n_cmp_pad, tq), F32)
        outs = []
        for h in range(NSA_HPG):
            hh = g * NSA_HPG + h
            s = s_all[:, h * tq:(h + 1) * tq] + bias_ref[hh]
            s = jnp.where(valid, s, NEG)
            m = jnp.max(s, axis=0, keepdims=True)
            e = jnp.where(valid, jnp.exp(s - m), 0.0)
            p = e / jnp.maximum(jnp.sum(e, axis=0, keepdims=True), 1e-30)
            gate = _sigmoid(gt_ref[hh:hh + 1, :])
            outs.append(_dot(vct_ref[g * HEAD_DIM:(g + 1) * HEAD_DIM, :], p.astype(BF16)) * gate)
            psum = psum + p
        o_ref[:, g * GROUP_COLS:(g + 1) * GROUP_COLS] = jnp.concatenate(outs, axis=0).T
        p_hi, p_lo = _split_bf16(psum, 2)
        imp_t = _dot(inc_t, p_hi) + _dot(inc_t, p_lo)
        score_t = jnp.where(forced, BIG, jnp.where(avail, imp_t, -BIG))
        score_ref[...] = score_t
        sc = score_t[:n_sel]

        def rank_step(r, rank):
            row = jnp.broadcast_to(score_ref[pl.ds(r, 1), :], (n_sel, tq))
            ahead = jnp.where(row > sc, 1.0, jnp.where((row == sc) & (j_s > r), 1.0, 0.0))
            return rank + ahead

        rank = lax.fori_loop(0, n_sel, rank_step, jnp.zeros((n_sel, tq), F32))
        chosen = jnp.where(rank < float(N_SELECT), 1.0, 0.0)
        chosen = jnp.concatenate([chosen, jnp.zeros((SEL_PAD - n_sel, tq), F32)], axis=0)
        selt_ref[g * SEL_PAD:(g + 1) * SEL_PAD, :] = chosen.astype(BF16)


def nsa_cmp_select_t(table, q_t, k_cmp, v_cmp_t, gates_t, n_batch, seq, n_cmp, n_sel):
    tq = ATT_TILE
    nq = seq // tq
    m = n_batch * seq
    n_cmp_pad = k_cmp.shape[1]
    kern = functools.partial(_cmp_select_t_kernel, tq=tq, n_cmp_pad=n_cmp_pad, n_cmp=n_cmp, n_sel=n_sel, pos0=0)
    return pl.pallas_call(
        kern,
        grid=(nq, n_batch),
        in_specs=[
            _smem_spec(),
            pl.BlockSpec((D_MODEL, tq), lambda i, b: (0, b * nq + i)),
            pl.BlockSpec((None, n_cmp_pad, GROUP_COLS), lambda i, b: (b, 0, 0)),
            pl.BlockSpec((None, GROUP_COLS, n_cmp_pad), lambda i, b: (b, 0, 0)),
            pl.BlockSpec((3 * N_HEADS, tq), lambda i, b: (0, b * nq + i)),
        ],
        out_specs=[
            pl.BlockSpec((tq, D_MODEL), lambda i, b: (b * nq + i, 0)),
            pl.BlockSpec((NSA_KV_GROUPS * SEL_PAD, tq), lambda i, b: (0, b * nq + i)),
        ],
        out_shape=[
            jax.ShapeDtypeStruct((m, D_MODEL), F32),
            jax.ShapeDtypeStruct((NSA_KV_GROUPS * SEL_PAD, m), BF16),
        ],
        scratch_shapes=[
            pltpu.VMEM((N_HEADS, n_cmp_pad, tq), F32),
            pltpu.VMEM((SEL_PAD, tq), F32),
            pltpu.VMEM((GROUP_COLS, NSA_HPG * tq), BF16),
        ],
        compiler_params=_cparams("arbitrary", "arbitrary"),
        name="nsa_cmp_select",
    )(table, q_t, k_cmp, v_cmp_t, gates_t)


def _nsa_attn_t_kernel(tbl_ref, qt_ref, k_ref, vt_ref, gt_ref, toep_ref, *rest, mode, tile, branch):
    if mode == "sel":
        selt_ref, o_ref, qz_ref, m_ref, l_ref, acc_ref = rest
    else:
        o_ref, qz_ref, m_ref, l_ref, acc_ref = rest
    i = pl.program_id(1)
    ik = lax.broadcasted_iota(jnp.int32, (tile, tile), 0)
    iq = lax.broadcasted_iota(jnp.int32, (tile, tile), 1)
    causal = iq >= ik
    above = ik > iq

    for g in range(NSA_KV_GROUPS):
        _fill_qz(qz_ref, lambda h: qt_ref[(g * NSA_HPG + h) * HEAD_DIM:(g * NSA_HPG + h + 1) * HEAD_DIM, :],
                 lambda h: g * HEAD_DIM, NSA_HPG, tile)
        _reset_softmax(m_ref, l_ref, acc_ref)

        def attend(kt, dd, g=g):
            r0 = pl.multiple_of(kt * tile, tile)
            s_all = _dot(k_ref[pl.ds(r0, tile), :], qz_ref[...])
            v_t = vt_ref[g * HEAD_DIM:(g + 1) * HEAD_DIM, pl.ds(r0, tile)]
            if mode == "sel":
                krow = lax.broadcasted_iota(jnp.int32, (tile, SEL_PAD), 0)
                jcol = lax.broadcasted_iota(jnp.int32, (tile, SEL_PAD), 1)
                expand = jnp.where(jcol == jnp.right_shift(r0 + krow, 6), 1.0, 0.0).astype(BF16)
                mk = _dot(expand, selt_ref[g * SEL_PAD:(g + 1) * SEL_PAD, :]) > 0.5
                if dd == 0:
                    mk = mk & causal
            else:
                mk = causal if dd == 0 else (above if dd == 2 else None)
            for h in range(NSA_HPG):
                hh = g * NSA_HPG + h
                s = s_all[:, h * tile:(h + 1) * tile]
                if dd == 2:
                    s = s + tbl_ref[N_BUCKETS - 1, hh]
                else:
                    s = s + toep_ref[hh, dd]
                _online_softmax_step(s, mk, v_t, m_ref, l_ref, acc_ref, h)

        if mode == "sel":
            def far_body(kt, carry):
                attend(kt, 2)
                return carry
            lax.fori_loop(0, jnp.maximum(i - 1, 0), far_body, 0)
        else:
            @pl.when(i >= 2)
            def _():
                attend(i - 2, 2)

        @pl.when(i >= 1)
        def _():
            attend(i - 1, 1)

        attend(i, 0)

        outs = []
        for h in range(NSA_HPG):
            hh = g * NSA_HPG + h
            gate = _sigmoid(gt_ref[branch * N_HEADS + hh:branch * N_HEADS + hh + 1, :])
            outs.append(acc_ref[h] / jnp.maximum(l_ref[h], 1e-30) * gate)
        o_ref[:, g * GROUP_COLS:(g + 1) * GROUP_COLS] = jnp.concatenate(outs, axis=0).T


def nsa_attn_t(mode, table, toep, q_t, k_src, k_col, v_t, gates_t, sel_t, n_batch, seq):
    tile = ATT_TILE
    assert WINDOW == 2 * tile
    nq = seq // tile
    m = n_batch * seq
    branch = 1 if mode == "sel" else 2
    kern = functools.partial(_nsa_attn_t_kernel, mode=mode, tile=tile, branch=branch)
    in_specs = [
        _smem_spec(),
        pl.BlockSpec((D_MODEL, tile), lambda b, i: (0, b * nq + i)),
        pl.BlockSpec((seq, GROUP_COLS), lambda b, i: (b, k_col)),
        pl.BlockSpec((GROUP_COLS, seq), lambda b, i: (0, b)),
        pl.BlockSpec((3 * N_HEADS, tile), lambda b, i: (0, b * nq + i)),
        pl.BlockSpec((N_HEADS, 2, tile, tile), lambda b, i: (0, 0, 0, 0)),
    ]
    args = [table, q_t, k_src, v_t, gates_t, toep]
    if mode == "sel":
        in_specs.append(pl.BlockSpec((NSA_KV_GROUPS * SEL_PAD, tile), lambda b, i: (0, b * nq + i)))
        args.append(sel_t)
    return pl.pallas_call(
        kern,
        grid=(n_batch, nq),
        in_specs=in_specs,
        out_specs=pl.BlockSpec((tile, D_MODEL), lambda b, i: (b * nq + i, 0)),
        out_shape=jax.ShapeDtypeStruct((m, D_MODEL), F32),
        scratch_shapes=[
            pltpu.VMEM((GROUP_COLS, NSA_HPG * tile), BF16),
            pltpu.VMEM((NSA_HPG, 1, tile), F32),
            pltpu.VMEM((NSA_HPG, 1, tile), F32),
            pltpu.VMEM((NSA_HPG, HEAD_DIM, tile), F32),
        ],
        compiler_params=_cparams("parallel", "arbitrary"),
        name="nsa_attn_" + mode,
    )(*args)


def nsa_prompt_t(table, toep, q_t, kvn, kv_bf, win_bf, vslc_t, vwin_t, gates_t, cw, n_batch, seq):
    pe_t, w1_bd, w2_bd = cw
    n_cmp = (seq - CMP_LEN) // CMP_STRIDE + 1
    n_sel = -(-seq // SEL_LEN)
    assert n_sel <= SEL_PAD and n_sel % 8 == 0
    ctok = compress_prompt(kvn, n_batch, seq, pe_t, w1_bd, w2_bd).astype(BF16)
    k_cmp = ctok[:, 0]
    v_cmp_t = ctok[:, 1].transpose(0, 2, 1)
    o_cmp, sel_t = nsa_cmp_select_t(table, q_t, k_cmp, v_cmp_t, gates_t, n_batch, seq, n_cmp, n_sel)
    o_slc = nsa_attn_t("sel", table, toep, q_t, kv_bf, 2, vslc_t, gates_t, sel_t, n_batch, seq)
    o_win = nsa_attn_t("win", table, toep, q_t, win_bf, 0, vwin_t, gates_t, None, n_batch, seq)
    return o_cmp, o_slc, o_win


def _cmp_select_kernel(tbl_ref, q_ref, kt_ref, v_ref, gate_ref, o_ref, sel_ref, bias_ref, score_ref, *,
                       tq, n_cmp_pad, n_cmp, n_sel, pos0):
    i = pl.program_id(0)
    b = pl.program_id(1)
    q0 = pos0 + i * tq
    qpos = q0 + lax.broadcasted_iota(jnp.int32, (tq, n_cmp_pad), 0)
    c_idx = lax.broadcasted_iota(jnp.int32, (tq, n_cmp_pad), 1)
    dist = qpos - (c_idx * CMP_STRIDE + CMP_LEN - 1)
    valid = (dist >= 0) & (c_idx < n_cmp)

    @pl.when(b == 0)
    def _():
        bucket = _t5_bucket(dist)
        for h in range(N_HEADS):
            bias_ref[h] = _bias_from_bucket(bucket, tbl_ref, h)

    j_t = lax.broadcasted_iota(jnp.int32, (SEL_PAD, tq), 0)
    q_t = q0 + lax.broadcasted_iota(jnp.int32, (SEL_PAD, tq), 1)
    cur = jnp.right_shift(q_t, 6)
    forced = (j_t == 0) | (j_t == cur) | (j_t == cur - 1)
    avail = j_t <= cur
    j_i = lax.broadcasted_iota(jnp.int32, (SEL_PAD, n_cmp_pad), 0) * SEL_LEN
    c_i = lax.broadcasted_iota(jnp.int32, (SEL_PAD, n_cmp_pad), 1) * CMP_STRIDE
    inc_t = jnp.where((c_i < j_i + SEL_LEN) & (c_i + CMP_LEN > j_i) & (j_i < n_sel * SEL_LEN)
                      & (c_i < n_cmp * CMP_STRIDE), 1.0, 0.0).astype(BF16)
    j_s = lax.broadcasted_iota(jnp.int32, (n_sel, tq), 0)

    for g in range(NSA_KV_GROUPS):
        psum = jnp.zeros((tq, n_cmp_pad), F32)
        for h in range(NSA_HPG):
            hh = g * NSA_HPG + h
            qh = (q_ref[:, hh * HEAD_DIM:(hh + 1) * HEAD_DIM].astype(F32) * HEAD_DIM ** -0.5).astype(BF16)
            s = _dot(qh, kt_ref[g]) + bias_ref[hh]
            s = jnp.where(valid, s, NEG)
            m = jnp.max(s, axis=-1, keepdims=True)
            e = jnp.where(valid, jnp.exp(s - m), 0.0)
            p = e / jnp.maximum(jnp.sum(e, axis=-1, keepdims=True), 1e-30)
            gate = _sigmoid(gate_ref[:, hh:hh + 1])
            o_ref[:, hh * HEAD_DIM:(hh + 1) * HEAD_DIM] = _dot(p.astype(BF16), v_ref[g]) * gate
            psum = psum + p
        p_hi, p_lo = _split_bf16(psum, 2)
        imp_t = _dot_nt(inc_t, p_hi) + _dot_nt(inc_t, p_lo)
        score_t = jnp.where(forced, BIG, jnp.where(avail, imp_t, -BIG))
        score_ref[...] = score_t
        sc = score_t[:n_sel]

        def rank_step(r, rank):
            row = jnp.broadcast_to(score_ref[pl.ds(r, 1), :], (n_sel, tq))
            ahead = jnp.where(row > sc, 1.0, jnp.where((row == sc) & (j_s > r), 1.0, 0.0))
            return rank + ahead

        rank = lax.fori_loop(0, n_sel, rank_step, jnp.zeros((n_sel, tq), F32))
        chosen = jnp.where(rank < float(N_SELECT), 1.0, 0.0)
        chosen = jnp.concatenate([chosen, jnp.zeros((SEL_PAD - n_sel, tq), F32)], axis=0)
        sel_ref[:, g * SEL_PAD:(g + 1) * SEL_PAD] = chosen.T.astype(BF16)


def cmp_select_prompt(table, q_bf, kt_cmp, v_cmp, gates, n_batch, seq, n_cmp, n_sel):
    tq = CMP_Q_TILE
    nq = seq // tq
    m = n_batch * seq
    n_cmp_pad = kt_cmp.shape[-1]
    kern = functools.partial(_cmp_select_kernel, tq=tq, n_cmp_pad=n_cmp_pad, n_cmp=n_cmp, n_sel=n_sel, pos0=0)
    return pl.pallas_call(
        kern,
        grid=(nq, n_batch),
        in_specs=[
            _smem_spec(),
            pl.BlockSpec((tq, D_MODEL), lambda i, b: (b * nq + i, 0)),
            pl.BlockSpec((None, NSA_KV_GROUPS, HEAD_DIM, n_cmp_pad), lambda i, b: (b, 0, 0, 0)),
            pl.BlockSpec((None, NSA_KV_GROUPS, n_cmp_pad, HEAD_DIM), lambda i, b: (b, 0, 0, 0)),
            pl.BlockSpec((tq, 3 * N_HEADS), lambda i, b: (b * nq + i, 0)),
        ],
        out_specs=[
            pl.BlockSpec((tq, D_MODEL), lambda i, b: (b * nq + i, 0)),
            pl.BlockSpec((tq, NSA_KV_GROUPS * SEL_PAD), lambda i, b: (b * nq + i, 0)),
        ],
        out_shape=[
            jax.ShapeDtypeStruct((m, D_MODEL), F32),
            jax.ShapeDtypeStruct((m, NSA_KV_GROUPS * SEL_PAD), BF16),
        ],
        scratch_shapes=[
            pltpu.VMEM((N_HEADS, tq, n_cmp_pad), F32),
            pltpu.VMEM((SEL_PAD, tq), F32),
        ],
        compiler_params=_cparams("arbitrary", "arbitrary"),
        name="nsa_cmp_select",
    )(table, q_bf, kt_cmp, v_cmp, gates)


def _nsa_attn_kernel(tbl_ref, q_ref, k_ref, v_ref, gate_ref, toep_ref, *rest, mode, tile, branch):
    if mode == "sel":
        sel_ref, o_ref, qs_ref, m_ref, l_ref, acc_ref = rest
    else:
        o_ref, qs_ref, m_ref, l_ref, acc_ref = rest
    i = pl.program_id(1)
    iq = lax.broadcasted_iota(jnp.int32, (tile, tile), 0)
    ik = lax.broadcasted_iota(jnp.int32, (tile, tile), 1)
    causal = iq >= ik
    above = ik > iq

    for hh in range(N_HEADS):
        qs_ref[hh] = (q_ref[:, hh * HEAD_DIM:(hh + 1) * HEAD_DIM].astype(F32) * HEAD_DIM ** -0.5).astype(BF16)

    for g in range(NSA_KV_GROUPS):
        m_ref[...] = jnp.full(m_ref.shape, NEG, F32)
        l_ref[...] = jnp.zeros(l_ref.shape, F32)
        acc_ref[...] = jnp.zeros(acc_ref.shape, F32)

        def attend(kt, dd, g=g):
            r0 = pl.multiple_of(kt * tile, tile)
            kg = k_ref[pl.ds(r0, tile), :][:, g * HEAD_DIM:(g + 1) * HEAD_DIM]
            vg = v_ref[pl.ds(r0, tile), :][:, g * HEAD_DIM:(g + 1) * HEAD_DIM]
            if mode == "sel":
                jrow = lax.broadcasted_iota(jnp.int32, (SEL_PAD, tile), 0)
                kcol = lax.broadcasted_iota(jnp.int32, (SEL_PAD, tile), 1)
                expand = jnp.where(jrow == jnp.right_shift(r0 + kcol, 6), 1.0, 0.0).astype(BF16)
                mk = _dot(sel_ref[:, g * SEL_PAD:(g + 1) * SEL_PAD], expand) > 0.5
                if dd == 0:
                    mk = mk & causal
            else:
                mk = causal if dd == 0 else (above if dd == 2 else None)
            for h in range(NSA_HPG):
                hh = g * NSA_HPG + h
                s = _dot_nt(qs_ref[hh], kg)
                if dd == 2:
                    s = s + tbl_ref[N_BUCKETS - 1, hh]
                else:
                    s = s + toep_ref[hh, dd]
                if mk is not None:
                    s = jnp.where(mk, s, NEG)
                m_old = m_ref[h]
                m_new = jnp.maximum(m_old, jnp.max(s, axis=-1, keepdims=True))
                alpha = jnp.exp(m_old - m_new)
                p = jnp.exp(s - m_new)
                if mk is not None:
                    p = jnp.where(mk, p, 0.0)
                l_ref[h] = alpha * l_ref[h] + jnp.sum(p, axis=-1, keepdims=True)
                acc_ref[h] = alpha * acc_ref[h] + _dot(p.astype(BF16), vg)
                m_ref[h] = m_new

        if mode == "sel":
            def far_body(kt, carry):
                attend(kt, 2)
                return carry
            lax.fori_loop(0, jnp.maximum(i - 1, 0), far_body, 0)
        else:
            @pl.when(i >= 2)
            def _():
                attend(i - 2, 2)

        @pl.when(i >= 1)
        def _():
            attend(i - 1, 1)

        attend(i, 0)

        for h in range(NSA_HPG):
            hh = g * NSA_HPG + h
            gate = _sigmoid(gate_ref[:, branch * N_HEADS + hh:branch * N_HEADS + hh + 1])
            o_ref[:, hh * HEAD_DIM:(hh + 1) * HEAD_DIM] = acc_ref[h] / jnp.maximum(l_ref[h], 1e-30) * gate


def nsa_attn_prompt(mode, table, toep, q_bf, kv_src, k_col, v_col, gates, sel, n_batch, seq):
    tile = ATT_TILE
    assert WINDOW == 2 * tile
    nq = seq // tile
    m = n_batch * seq
    branch = 1 if mode == "sel" else 2
    kern = functools.partial(_nsa_attn_kernel, mode=mode, tile=tile, branch=branch)
    in_specs = [
        _smem_spec(),
        pl.BlockSpec((tile, D_MODEL), lambda b, i: (b * nq + i, 0)),
        pl.BlockSpec((seq, GROUP_COLS), lambda b, i: (b, k_col)),
        pl.BlockSpec((seq, GROUP_COLS), lambda b, i: (b, v_col)),
        pl.BlockSpec((tile, 3 * N_HEADS), lambda b, i: (b * nq + i, 0)),
        pl.BlockSpec((N_HEADS, 2, tile, tile), lambda b, i: (0, 0, 0, 0)),
    ]
    args = [table, q_bf, kv_src, kv_src, gates, toep]
    if mode == "sel":
        in_specs.append(pl.BlockSpec((tile, NSA_KV_GROUPS * SEL_PAD), lambda b, i: (b * nq + i, 0)))
        args.append(sel)
    return pl.pallas_call(
        kern,
        grid=(n_batch, nq),
        in_specs=in_specs,
        out_specs=pl.BlockSpec((tile, D_MODEL), lambda b, i: (b * nq + i, 0)),
        out_shape=jax.ShapeDtypeStruct((m, D_MODEL), F32),
        scratch_shapes=[
            pltpu.VMEM((N_HEADS, tile, HEAD_DIM), BF16),
            pltpu.VMEM((NSA_HPG, tile, 1), F32),
            pltpu.VMEM((NSA_HPG, tile, 1), F32),
            pltpu.VMEM((NSA_HPG, tile, HEAD_DIM), F32),
        ],
        compiler_params=_cparams("parallel", "arbitrary"),
        name="nsa_attn_" + mode,
    )(*args)


def nsa_prompt(table, toep, q_bf, kvn, kv_bf, win_bf, gates, cw, n_batch, seq):
    pe_t, w1_bd, w2_bd = cw
    n_cmp = (seq - CMP_LEN) // CMP_STRIDE + 1
    n_sel = -(-seq // SEL_LEN)
    assert n_sel <= SEL_PAD and n_sel % 8 == 0
    ctok = compress_prompt(kvn, n_batch, seq, pe_t, w1_bd, w2_bd)
    rows = ctok.shape[2]
    ctok = ctok.reshape(n_batch, 2, rows, NSA_KV_GROUPS, HEAD_DIM).astype(BF16)
    kt_cmp = ctok[:, 0].transpose(0, 2, 3, 1)
    v_cmp = ctok[:, 1].transpose(0, 2, 1, 3)
    o_cmp, sel = cmp_select_prompt(table, q_bf, kt_cmp, v_cmp, gates, n_batch, seq, n_cmp, n_sel)
    o_slc = nsa_attn_prompt("sel", table, toep, q_bf, kv_bf, 2, 3, gates, sel, n_batch, seq)
    o_win = nsa_attn_prompt("win", table, toep, q_bf, win_bf, 0, 1, gates, None, n_batch, seq)
    return o_cmp, o_slc, o_win


CUMSUM_CHUNK = 256
HEAD_BLOCK = 4


def _tril_bf16(n):
    r = lax.broadcasted_iota(jnp.int32, (n, n), 0)
    c = lax.broadcasted_iota(jnp.int32, (n, n), 1)
    return jnp.where(r >= c, 1.0, 0.0).astype(BF16)


def _cumsum_rows(x, tri):
    return sum(_dot(tri, part) for part in _split_bf16(x, 3))


def _fox_prep_kernel(f_ref, b_ref, lf_ref, d_ref, *, seq):
    tri = _tril_bf16(CUMSUM_CHUNK)
    carry = jnp.zeros((1, N_HEADS), F32)
    for c in range(seq // CUMSUM_CHUNK):
        rows = slice(c * CUMSUM_CHUNK, (c + 1) * CUMSUM_CHUNK)
        lf = _log_sigmoid(f_ref[rows, :] + b_ref[...])
        lf_ref[rows, :] = lf
        d = carry + _cumsum_rows(lf, tri)
        d_ref[rows, :] = d
        carry = d[CUMSUM_CHUNK - 1:CUMSUM_CHUNK, :]


def fox_prep_prompt(f_raw, b_f, n_batch, seq):
    m = n_batch * seq
    blk = pl.BlockSpec((seq, N_HEADS), lambda b: (b, 0))
    return pl.pallas_call(
        functools.partial(_fox_prep_kernel, seq=seq),
        grid=(n_batch,),
        in_specs=[blk, pl.BlockSpec((1, N_HEADS), lambda b: (0, 0))],
        out_specs=[blk, blk],
        out_shape=[jax.ShapeDtypeStruct((m, N_HEADS), F32)] * 2,
        compiler_params=_cparams("parallel"),
        name="fox_prep",
    )(f_raw, b_f.reshape(1, N_HEADS))


def _fox_attn_t_kernel(qt_ref, k_ref, vt_ref, drow_ref, dcol_ref, o_ref, qz_ref, m_ref, l_ref, acc_ref, *, tile):
    i = pl.program_id(2)
    ik = lax.broadcasted_iota(jnp.int32, (tile, tile), 0)
    iq = lax.broadcasted_iota(jnp.int32, (tile, tile), 1)
    causal = iq >= ik
    _fill_qz(qz_ref, lambda h: qt_ref[h * HEAD_DIM:(h + 1) * HEAD_DIM, :], lambda h: h * HEAD_DIM, HEAD_BLOCK, tile)
    _reset_softmax(m_ref, l_ref, acc_ref)

    def attend(kt, diag):
        r0 = pl.multiple_of(kt * tile, tile)
        s_all = _dot(k_ref[pl.ds(r0, tile), :], qz_ref[...])
        for h in range(HEAD_BLOCK):
            s = s_all[:, h * tile:(h + 1) * tile] + drow_ref[h:h + 1, :] - dcol_ref[pl.ds(r0, tile), h:h + 1]
            v_t = vt_ref[h * HEAD_DIM:(h + 1) * HEAD_DIM, pl.ds(r0, tile)]
            _online_softmax_step(s, causal if diag else None, v_t, m_ref, l_ref, acc_ref, h)

    def body(kt, carry):
        attend(kt, False)
        return carry

    lax.fori_loop(0, i, body, 0)
    attend(i, True)
    outs = [acc_ref[h] / jnp.maximum(l_ref[h], 1e-30) for h in range(HEAD_BLOCK)]
    o_ref[...] = jnp.concatenate(outs, axis=0).T


def fox_attn_t(q_t, k_bf, v_t, d_cum, n_batch, seq):
    tile = ATT_TILE
    nq = seq // tile
    m = n_batch * seq
    nhb = N_HEADS // HEAD_BLOCK
    d4 = d_cum[:m].reshape(n_batch, seq, nhb, HEAD_BLOCK)
    d_col = d4.transpose(0, 2, 1, 3)
    d_row = d4.transpose(0, 2, 3, 1)
    cols = HEAD_BLOCK * HEAD_DIM
    return pl.pallas_call(
        functools.partial(_fox_attn_t_kernel, tile=tile),
        grid=(n_batch, nhb, nq),
        in_specs=[
            pl.BlockSpec((cols, tile), lambda b, g, i: (g, b * nq + i)),
            pl.BlockSpec((seq, cols), lambda b, g, i: (b, g)),
            pl.BlockSpec((cols, seq), lambda b, g, i: (g, b)),
            pl.BlockSpec((None, None, HEAD_BLOCK, tile), lambda b, g, i: (b, g, 0, i)),
            pl.BlockSpec((None, None, seq, HEAD_BLOCK), lambda b, g, i: (b, g, 0, 0)),
        ],
        out_specs=pl.BlockSpec((tile, cols), lambda b, g, i: (b * nq + i, g)),
        out_shape=jax.ShapeDtypeStruct((m, D_MODEL), F32),
        scratch_shapes=[
            pltpu.VMEM((cols, HEAD_BLOCK * tile), BF16),
            pltpu.VMEM((HEAD_BLOCK, 1, tile), F32),
            pltpu.VMEM((HEAD_BLOCK, 1, tile), F32),
            pltpu.VMEM((HEAD_BLOCK, HEAD_DIM, tile), F32),
        ],
        compiler_params=_cparams("parallel", "parallel", "arbitrary"),
        name="fox_attn",
    )(q_t, k_bf, v_t, d_row, d_col)


def _fox_attn_kernel(q_ref, k_ref, v_ref, dq_ref, dk_ref, o_ref, qs_ref, m_ref, l_ref, acc_ref, *, tile):
    i = pl.program_id(2)
    iq = lax.broadcasted_iota(jnp.int32, (tile, tile), 0)
    ik = lax.broadcasted_iota(jnp.int32, (tile, tile), 1)
    causal = iq >= ik
    for h in range(HEAD_BLOCK):
        qs_ref[h] = (q_ref[:, h * HEAD_DIM:(h + 1) * HEAD_DIM].astype(F32) * HEAD_DIM ** -0.5).astype(BF16)
    m_ref[...] = jnp.full(m_ref.shape, NEG, F32)
    l_ref[...] = jnp.zeros(l_ref.shape, F32)
    acc_ref[...] = jnp.zeros(acc_ref.shape, F32)

    def attend(kt, diag):
        r0 = pl.multiple_of(kt * tile, tile)
        kb = k_ref[pl.ds(r0, tile), :]
        vb = v_ref[pl.ds(r0, tile), :]
        dk = dk_ref[kt]
        for h in range(HEAD_BLOCK):
            lanes = slice(h * HEAD_DIM, (h + 1) * HEAD_DIM)
            s = _dot_nt(qs_ref[h], kb[:, lanes]) + dq_ref[:, h:h + 1] - dk[h:h + 1, :]
            if diag:
                s = jnp.where(causal, s, NEG)
            m_old = m_ref[h]
            m_new = jnp.maximum(m_old, jnp.max(s, axis=-1, keepdims=True))
            alpha = jnp.exp(m_old - m_new)
            p = jnp.exp(s - m_new)
            if diag:
                p = jnp.where(causal, p, 0.0)
            l_ref[h] = alpha * l_ref[h] + jnp.sum(p, axis=-1, keepdims=True)
            acc_ref[h] = alpha * acc_ref[h] + _dot(p.astype(BF16), vb[:, lanes])
            m_ref[h] = m_new

    def body(kt, carry):
        attend(kt, False)
        return carry

    lax.fori_loop(0, i, body, 0)
    attend(i, True)
    for h in range(HEAD_BLOCK):
        o_ref[:, h * HEAD_DIM:(h + 1) * HEAD_DIM] = acc_ref[h] / jnp.maximum(l_ref[h], 1e-30)


def fox_attn_prompt(q_bf, kv_bf, d_cum, n_batch, seq):
    tile = ATT_TILE
    nq = seq // tile
    m = n_batch * seq
    nhb = N_HEADS // HEAD_BLOCK
    d4 = d_cum[:m].reshape(n_batch, seq, nhb, HEAD_BLOCK)
    dq = d4.transpose(0, 2, 1, 3)
    dk = d4.reshape(n_batch, nq, tile, nhb, HEAD_BLOCK).transpose(0, 3, 1, 4, 2)
    cols = HEAD_BLOCK * HEAD_DIM
    return pl.pallas_call(
        functools.partial(_fox_attn_kernel, tile=tile),
        grid=(n_batch, nhb, nq),
        in_specs=[
            pl.BlockSpec((tile, cols), lambda b, g, i: (b * nq + i, g)),
            pl.BlockSpec((seq, cols), lambda b, g, i: (b, g)),
            pl.BlockSpec((seq, cols), lambda b, g, i: (b, nhb + g)),
            pl.BlockSpec((None, None, tile, HEAD_BLOCK), lambda b, g, i: (b, g, i, 0)),
            pl.BlockSpec((None, None, nq, HEAD_BLOCK, tile), lambda b, g, i: (b, g, 0, 0, 0)),
        ],
        out_specs=pl.BlockSpec((tile, cols), lambda b, g, i: (b * nq + i, g)),
        out_shape=jax.ShapeDtypeStruct((m, D_MODEL), F32),
        scratch_shapes=[
            pltpu.VMEM((HEAD_BLOCK, tile, HEAD_DIM), BF16),
            pltpu.VMEM((HEAD_BLOCK, tile, 1), F32),
            pltpu.VMEM((HEAD_BLOCK, tile, 1), F32),
            pltpu.VMEM((HEAD_BLOCK, tile, HEAD_DIM), F32),
        ],
        compiler_params=_cparams("parallel", "parallel", "arbitrary"),
        name="fox_attn",
    )(q_bf, kv_bf, kv_bf, dq, dk)


HT_LANES = V7X_LANES


def _lane_head_onehot(n_heads, per_head):
    r = lax.broadcasted_iota(jnp.int32, (n_heads, n_heads * per_head), 0)
    c = lax.broadcasted_iota(jnp.int32, (n_heads, n_heads * per_head), 1)
    return jnp.where(_div_pow2(c, per_head) == r, 1.0, 0.0).astype(BF16)


def _expand_heads(x, onehot):
    return sum(_dot(part, onehot) for part in _split_bf16(x, 3))


def _diag_blocks_to_rows(acc_t, n_heads, feat, per_head):
    rows, lanes = acc_t.shape
    r = lax.broadcasted_iota(jnp.int32, (rows, lanes), 0)
    c = lax.broadcasted_iota(jnp.int32, (rows, lanes), 1)
    z = jnp.where(_div_pow2(r, feat) == _div_pow2(c, per_head), acc_t, 0.0)
    cc = lax.broadcasted_iota(jnp.int32, (lanes, lanes), 0)
    tt = lax.broadcasted_iota(jnp.int32, (lanes, lanes), 1)
    fold = jnp.where(_mod_pow2(cc, per_head) == tt, 1.0, 0.0).astype(BF16)
    out_t = sum(_dot(part, fold) for part in _split_bf16(z, 3))
    return out_t.T


def block_diag_queries(q_rows, n_seq, per_seq, n_heads):
    q4 = q_rows.reshape(n_seq, per_seq, n_heads, HEAD_DIM).transpose(0, 2, 3, 1)
    eye = jnp.eye(n_heads, dtype=q_rows.dtype)
    q_bd = q4[:, :, :, None, :] * eye[None, :, None, :, None]
    return q_bd.reshape(n_seq, n_heads * HEAD_DIM, n_heads * per_seq).astype(BF16)


def _fox_decode_kernel(pt_ref, qbd_ref, k_ref, v_ref, lf_ref, kn_ref, vn_ref, fn_ref, bf_ref, o_ref, lfn_ref,
                       qz_ref, m_ref, l_ref, acc_ref, carry_ref, *, n_pages, per_seq):
    p = pl.program_id(1)

    @pl.when(p == 0)
    def _():
        qz_ref[...] = (qbd_ref[...].astype(F32) * HEAD_DIM ** -0.5).astype(BF16)
        _reset_softmax(m_ref, l_ref, acc_ref)
        carry_ref[...] = jnp.zeros(carry_ref.shape, F32)

    tri = _tril_bf16(PAGE_SIZE)
    onehot = _lane_head_onehot(N_HEADS, per_seq)

    def attend(k_rows, v_rows, lf_rows, mk):
        d = carry_ref[...] + _cumsum_rows(lf_rows, tri)
        s_t = _dot(k_rows.astype(BF16), qz_ref[...]) - _expand_heads(d, onehot)
        if mk is not None:
            s_t = jnp.where(mk, s_t, NEG)
        m_old = m_ref[0]
        m_new = jnp.maximum(m_old, jnp.max(s_t, axis=0, keepdims=True))
        alpha = jnp.exp(m_old - m_new)
        pr = jnp.exp(s_t - m_new)
        if mk is not None:
            pr = jnp.where(mk, pr, 0.0)
        l_ref[0] = alpha * l_ref[0] + jnp.sum(pr, axis=0, keepdims=True)
        pv = lax.dot_general(v_rows.astype(BF16), pr.astype(BF16), (((0,), (0,)), ((), ())),
                             preferred_element_type=F32)
        acc_ref[0] = alpha * acc_ref[0] + pv
        m_ref[0] = m_new
        return d

    @pl.when(p < n_pages)
    def _():
        d = attend(k_ref[...], v_ref[...], lf_ref[...], None)
        carry_ref[...] = d[PAGE_SIZE - 1:PAGE_SIZE, :]

    @pl.when(p == n_pages)
    def _():
        lf_new = _log_sigmoid(fn_ref[...] + bf_ref[...])
        lfn_ref[...] = lf_new
        pad = PAGE_SIZE - per_seq
        k_rows = jnp.concatenate([kn_ref[...], jnp.zeros((pad, D_MODEL), F32)], axis=0)
        v_rows = jnp.concatenate([vn_ref[...], jnp.zeros((pad, D_MODEL), F32)], axis=0)
        lf_rows = jnp.concatenate([lf_new, jnp.zeros((pad, N_HEADS), F32)], axis=0)
        srow = lax.broadcasted_iota(jnp.int32, (PAGE_SIZE, HT_LANES), 0)
        tq = _mod_pow2(lax.broadcasted_iota(jnp.int32, (PAGE_SIZE, HT_LANES), 1), per_seq)
        attend(k_rows, v_rows, lf_rows, srow <= tq)
        o_t = acc_ref[0] / jnp.maximum(l_ref[0], 1e-30)
        o_ref[...] = _diag_blocks_to_rows(o_t, N_HEADS, HEAD_DIM, per_seq)[:per_seq, :]


def fox_decode(page_table, q_bd, kv_pool, lf_pool, kv_new, f_new, b_f, row_block0, n_seq, per_seq):
    n_pages = page_table.shape[1]
    assert N_HEADS * per_seq == HT_LANES
    kern = functools.partial(_fox_decode_kernel, n_pages=n_pages, per_seq=per_seq)
    page = lambda b, p, pt: pt[b, jnp.minimum(p, n_pages - 1)]
    grid_spec = pltpu.PrefetchScalarGridSpec(
        num_scalar_prefetch=1,
        grid=(n_seq, n_pages + 1),
        in_specs=[
            pl.BlockSpec((None, D_MODEL, HT_LANES), lambda b, p, pt: (b, 0, 0)),
            pl.BlockSpec((None, PAGE_SIZE, D_MODEL), lambda b, p, pt: (page(b, p, pt), 0, 0)),
            pl.BlockSpec((None, PAGE_SIZE, D_MODEL), lambda b, p, pt: (page(b, p, pt), 0, 1)),
            pl.BlockSpec((None, PAGE_SIZE, N_HEADS), lambda b, p, pt: (page(b, p, pt), 0, 0)),
            pl.BlockSpec((per_seq, D_MODEL), lambda b, p, pt: (row_block0 + b, 0)),
            pl.BlockSpec((per_seq, D_MODEL), lambda b, p, pt: (row_block0 + b, 1)),
            pl.BlockSpec((per_seq, N_HEADS), lambda b, p, pt: (row_block0 + b, 0)),
            pl.BlockSpec((1, N_HEADS), lambda b, p, pt: (0, 0)),
        ],
        out_specs=[
            pl.BlockSpec((per_seq, D_MODEL), lambda b, p, pt: (b, 0)),
            pl.BlockSpec((per_seq, N_HEADS), lambda b, p, pt: (b, 0)),
        ],
        scratch_shapes=[
            pltpu.VMEM((D_MODEL, HT_LANES), BF16),
            pltpu.VMEM((1, 1, HT_LANES), F32),
            pltpu.VMEM((1, 1, HT_LANES), F32),
            pltpu.VMEM((1, D_MODEL, HT_LANES), F32),
            pltpu.VMEM((1, N_HEADS), F32),
        ],
    )
    return pl.pallas_call(
        kern,
        grid_spec=grid_spec,
        out_shape=[jax.ShapeDtypeStruct((n_seq * per_seq, D_MODEL), F32),
                   jax.ShapeDtypeStruct((n_seq * per_seq, N_HEADS), F32)],
        compiler_params=_cparams("parallel", "arbitrary"),
        name="fox_decode",
    )(page_table, q_bd, kv_pool, kv_pool, lf_pool, kv_new, kv_new, f_new, b_f.reshape(1, N_HEADS))


DEC_ROW_TILE = 512


def _group_rows_to_heads(acc_t):
    return jnp.concatenate([acc_t[(h // NSA_HPG) * HEAD_DIM:(h // NSA_HPG + 1) * HEAD_DIM] for h in range(N_HEADS)],
                           axis=0)


def group_block_diag_queries(q_rows, n_seq, per_seq):
    q5 = q_rows.reshape(n_seq, per_seq, NSA_KV_GROUPS, NSA_HPG, HEAD_DIM).transpose(0, 2, 4, 3, 1)
    eye = jnp.eye(NSA_KV_GROUPS, dtype=q_rows.dtype)
    q_bd = q5[:, :, :, None, :, :] * eye[None, :, None, :, None, None]
    return q_bd.reshape(n_seq, GROUP_COLS, N_HEADS * per_seq).astype(BF16)


def _nsa_dec_bias_kernel(tbl_ref, o_ref, *, pos0, per_seq):
    onehot = _lane_head_onehot(N_HEADS, per_seq)
    tab = _expand_heads(tbl_ref[...], onehot)
    row = lax.broadcasted_iota(jnp.int32, (PAGE_SIZE, HT_LANES), 0)
    t = _mod_pow2(lax.broadcasted_iota(jnp.int32, (PAGE_SIZE, HT_LANES), 1), per_seq)
    dists = (PAGE_SIZE + t - row, t - row, pos0 + t - (row * CMP_STRIDE + CMP_LEN - 1))
    for n, dist in enumerate(dists):
        bucket = _t5_bucket(dist)
        val = jnp.broadcast_to(tab[0:1, :], (PAGE_SIZE, HT_LANES))
        for k in range(1, N_BUCKETS):
            val = jnp.where(bucket == k, tab[k:k + 1, :], val)
        o_ref[n] = val
    o_ref[3] = jnp.broadcast_to(tab[N_BUCKETS - 1:N_BUCKETS, :], (PAGE_SIZE, HT_LANES))


def nsa_dec_bias(table, pos0, per_seq):
    assert PAGE_SIZE >= T5_SATURATION_DIST
    return pl.pallas_call(
        functools.partial(_nsa_dec_bias_kernel, pos0=pos0, per_seq=per_seq),
        out_shape=jax.ShapeDtypeStruct((4, PAGE_SIZE, HT_LANES), F32),
        name="nsa_dec_bias",
    )(table)


def _dec_gather_kernel(pt_ref, x_ref, o_ref):
    for l in range(CMP_STRIDE):
        o_ref[l] = x_ref[:, l, :]


def nsa_dec_gather(page_table, pool):
    n_seq, n_pages = page_table.shape
    n_pool = pool.shape[0]
    rpp = PAGE_SIZE // CMP_STRIDE
    pool4 = pool.reshape(n_pool, rpp, CMP_STRIDE, pool.shape[-1])
    grid_spec = pltpu.PrefetchScalarGridSpec(
        num_scalar_prefetch=1,
        grid=(n_seq, n_pages),
        in_specs=[pl.BlockSpec((None, rpp, CMP_STRIDE, 2 * GROUP_COLS), lambda b, p, pt: (pt[b, p], 0, 0, 0))],
        out_specs=pl.BlockSpec((CMP_STRIDE, rpp, 2 * GROUP_COLS), lambda b, p, pt: (0, b * n_pages + p, 0)),
    )
    return pl.pallas_call(
        _dec_gather_kernel,
        grid_spec=grid_spec,
        out_shape=jax.ShapeDtypeStruct((CMP_STRIDE, n_seq * n_pages * rpp, 2 * GROUP_COLS), F32),
        compiler_params=_cparams("parallel", "arbitrary"),
        name="nsa_dec_gather",
    )(page_table, pool4)


def nsa_dec_compress(xd, pe_t, w1_bd, w2_bd):
    r_total = xd.shape[1]
    rows = min(DEC_ROW_TILE, r_total)
    return pl.pallas_call(
        functools.partial(_compress_kernel, rows=rows, l_major=True),
        grid=(r_total // rows, 2),
        in_specs=[
            pl.BlockSpec((CMP_STRIDE, rows, GROUP_COLS), lambda i, j: (0, i, j)),
            pl.BlockSpec((None, CMP_LEN, 1, GROUP_COLS), lambda i, j: (j, 0, 0, 0)),
            pl.BlockSpec((None, CMP_LEN, GROUP_COLS, NSA_KV_GROUPS * CMP_HIDDEN), lambda i, j: (j, 0, 0, 0)),
            pl.BlockSpec((None, NSA_KV_GROUPS * CMP_HIDDEN, GROUP_COLS), lambda i, j: (j, 0, 0)),
        ],
        out_specs=pl.BlockSpec((None, rows, GROUP_COLS), lambda i, j: (j, i, 0)),
        out_shape=jax.ShapeDtypeStruct((2, r_total, GROUP_COLS), F32),
        compiler_params=_cparams("parallel", "arbitrary"),
        name="nsa_dec_compress",
    )(xd, pe_t, w1_bd, w2_bd)


def _dec_softmax_step(s_t, mk, v_rows, m_ref, l_ref, acc_ref):
    if mk is not None:
        s_t = jnp.where(mk, s_t, NEG)
    m_old = m_ref[0]
    m_new = jnp.maximum(m_old, jnp.max(s_t, axis=0, keepdims=True))
    alpha = jnp.exp(m_old - m_new)
    pr = jnp.exp(s_t - m_new)
    if mk is not None:
        pr = jnp.where(mk, pr, 0.0)
    l_ref[0] = alpha * l_ref[0] + jnp.sum(pr, axis=0, keepdims=True)
    pv = lax.dot_general(v_rows, pr.astype(BF16), (((0,), (0,)), ((), ())), preferred_element_type=F32)
    acc_ref[0] = alpha * acc_ref[0] + pv
    m_ref[0] = m_new


def _dec_finish(m_ref, l_ref, acc_ref, gate_row, per_seq):
    o_t = acc_ref[0] / jnp.maximum(l_ref[0], 1e-30) * _sigmoid(gate_row)
    return _diag_blocks_to_rows(_group_rows_to_heads(o_t), N_HEADS, HEAD_DIM, per_seq)[:per_seq, :]


def _nsa_dec_cmp_kernel(qbd_ref, kc_ref, vc_ref, g_ref, bias_ref, o_ref, sel_ref, score_ref, *,
                        n_cmp, n_sel, pos0, per_seq):
    n_c = kc_ref.shape[0]
    c_idx = lax.broadcasted_iota(jnp.int32, (n_c, HT_LANES), 0)
    lane = lax.broadcasted_iota(jnp.int32, (n_c, HT_LANES), 1)
    t = _mod_pow2(lane, per_seq)
    dist = pos0 + t - (c_idx * CMP_STRIDE + CMP_LEN - 1)
    valid = (dist >= 0) & (c_idx < n_cmp)
    qz = (qbd_ref[...].astype(F32) * HEAD_DIM ** -0.5).astype(BF16)
    s = _dot(kc_ref[...], qz) + bias_ref[2]
    s = jnp.where(valid, s, NEG)
    m = jnp.max(s, axis=0, keepdims=True)
    e = jnp.where(valid, jnp.exp(s - m), 0.0)
    p = e / jnp.maximum(jnp.sum(e, axis=0, keepdims=True), 1e-30)
    o_t = lax.dot_general(vc_ref[...], p.astype(BF16), (((0,), (0,)), ((), ())), preferred_element_type=F32)
    o_t = o_t * _sigmoid(g_ref[0])
    o_ref[...] = _diag_blocks_to_rows(_group_rows_to_heads(o_t), N_HEADS, HEAD_DIM, per_seq)[:per_seq, :]
    src = lax.broadcasted_iota(jnp.int32, (HT_LANES, HT_LANES), 0)
    dst = lax.broadcasted_iota(jnp.int32, (HT_LANES, HT_LANES), 1)
    same_t = _mod_pow2(src, per_seq) == _mod_pow2(dst, per_seq)
    g_of_src = _div_pow2(src, per_seq * NSA_HPG)
    n_gt = NSA_KV_GROUPS * per_seq
    head_sum = jnp.where(same_t & (g_of_src == _div_pow2(dst, per_seq)) & (dst < n_gt), 1.0, 0.0).astype(BF16)
    psum = sum(_dot(part, head_sum) for part in _split_bf16(p, 2))
    j_i = lax.broadcasted_iota(jnp.int32, (SEL_PAD, n_c), 0) * SEL_LEN
    c_i = lax.broadcasted_iota(jnp.int32, (SEL_PAD, n_c), 1) * CMP_STRIDE
    inc_t = jnp.where((c_i < j_i + SEL_LEN) & (c_i + CMP_LEN > j_i) & (j_i < n_sel * SEL_LEN)
                      & (c_i < n_cmp * CMP_STRIDE), 1.0, 0.0).astype(BF16)
    imp_t = sum(_dot(inc_t, part) for part in _split_bf16(psum, 2))
    j_t = lax.broadcasted_iota(jnp.int32, (SEL_PAD, HT_LANES), 0)
    q_t = pos0 + _mod_pow2(lax.broadcasted_iota(jnp.int32, (SEL_PAD, HT_LANES), 1), per_seq)
    cur = jnp.right_shift(q_t, 6)
    forced = (j_t == 0) | (j_t == cur) | (j_t == cur - 1)
    avail = j_t <= cur
    score_t = jnp.where(forced, BIG, jnp.where(avail, imp_t, -BIG))
    score_ref[...] = score_t
    n_rank = -(-n_sel // 8) * 8
    sc = score_t[:n_rank]
    j_s = lax.broadcasted_iota(jnp.int32, (n_rank, HT_LANES), 0)

    def rank_step(r, rank):
        row = jnp.broadcast_to(score_ref[pl.ds(r, 1), :], (n_rank, HT_LANES))
        ahead = jnp.where(row > sc, 1.0, jnp.where((row == sc) & (j_s > r), 1.0, 0.0))
        return rank + ahead

    rank = lax.fori_loop(0, n_sel, rank_step, jnp.zeros((n_rank, HT_LANES), F32))
    chosen = jnp.where((rank < float(N_SELECT)) & (j_s < n_sel), 1.0, 0.0)
    chosen = jnp.concatenate([chosen, jnp.zeros((SEL_PAD - n_rank, HT_LANES), F32)], axis=0).astype(BF16)
    spread = jnp.where(same_t & (src == _div_pow2(dst, per_seq * NSA_HPG) * per_seq + _mod_pow2(dst, per_seq)),
                       1.0, 0.0).astype(BF16)
    sel_ref[...] = _dot(chosen, spread).astype(BF16)


def nsa_dec_cmp(q_bd, k_cmp, v_cmp, gate_rows, bias, n_cmp, n_sel, pos0, per_seq):
    n_seq, n_c = k_cmp.shape[:2]
    kern = functools.partial(_nsa_dec_cmp_kernel, n_cmp=n_cmp, n_sel=n_sel, pos0=pos0, per_seq=per_seq)
    return pl.pallas_call(
        kern,
        grid=(n_seq,),
        in_specs=[
            pl.BlockSpec((None, GROUP_COLS, HT_LANES), lambda b: (b, 0, 0)),
            pl.BlockSpec((None, n_c, GROUP_COLS), lambda b: (b, 0, 0)),
            pl.BlockSpec((None, n_c, GROUP_COLS), lambda b: (b, 0, 0)),
            pl.BlockSpec((None, 3, 1, HT_LANES), lambda b: (b, 0, 0, 0)),
            pl.BlockSpec((4, PAGE_SIZE, HT_LANES), lambda b: (0, 0, 0)),
        ],
        out_specs=[
            pl.BlockSpec((per_seq, D_MODEL), lambda b: (b, 0)),
            pl.BlockSpec((None, SEL_PAD, HT_LANES), lambda b: (b, 0, 0)),
        ],
        out_shape=[
            jax.ShapeDtypeStruct((n_seq * per_seq, D_MODEL), F32),
            jax.ShapeDtypeStruct((n_seq, SEL_PAD, HT_LANES), BF16),
        ],
        scratch_shapes=[pltpu.VMEM((SEL_PAD, HT_LANES), F32)],
        compiler_params=_cparams("parallel"),
        name="nsa_dec_cmp",
    )(q_bd, k_cmp, v_cmp, gate_rows, bias)


def _nsa_dec_sel_kernel(pt_ref, qbd_ref, k_ref, v_ref, kn_ref, vn_ref, sel_ref, g_ref, bias_ref, o_ref,
                        qz_ref, m_ref, l_ref, acc_ref, *, n_pages, per_seq):
    p = pl.program_id(1)

    @pl.when(p == 0)
    def _():
        qz_ref[...] = (qbd_ref[...].astype(F32) * HEAD_DIM ** -0.5).astype(BF16)
        _reset_softmax(m_ref, l_ref, acc_ref)

    srow = lax.broadcasted_iota(jnp.int32, (PAGE_SIZE, SEL_PAD), 0)
    jcol = lax.broadcasted_iota(jnp.int32, (PAGE_SIZE, SEL_PAD), 1)
    blocks_per_page = PAGE_SIZE // SEL_LEN

    def member(first_block):
        expand = jnp.where(jcol == first_block + jnp.right_shift(srow, 6), 1.0, 0.0).astype(BF16)
        return _dot(expand, sel_ref[...]) > 0.5

    @pl.when(p < n_pages)
    def _():
        bias = jnp.where(p == n_pages - 1, bias_ref[0], bias_ref[3])
        s_t = _dot(k_ref[...].astype(BF16), qz_ref[...]) + bias
        _dec_softmax_step(s_t, member(p * blocks_per_page), v_ref[...].astype(BF16), m_ref, l_ref, acc_ref)

    @pl.when(p == n_pages)
    def _():
        pad = PAGE_SIZE - per_seq
        k_rows = jnp.concatenate([kn_ref[...], jnp.zeros((pad, GROUP_COLS), F32)], axis=0).astype(BF16)
        v_rows = jnp.concatenate([vn_ref[...], jnp.zeros((pad, GROUP_COLS), F32)], axis=0).astype(BF16)
        krow = lax.broadcasted_iota(jnp.int32, (PAGE_SIZE, HT_LANES), 0)
        tq = _mod_pow2(lax.broadcasted_iota(jnp.int32, (PAGE_SIZE, HT_LANES), 1), per_seq)
        mk = member(n_pages * blocks_per_page) & (krow <= tq)
        s_t = _dot(k_rows, qz_ref[...]) + bias_ref[1]
        _dec_softmax_step(s_t, mk, v_rows, m_ref, l_ref, acc_ref)
        o_ref[...] = _dec_finish(m_ref, l_ref, acc_ref, g_ref[1], per_seq)


def nsa_dec_sel(page_table, q_bd, pool, kv_new, sel, gate_rows, bias, row_block0, per_seq):
    n_seq, n_pages = page_table.shape
    kern = functools.partial(_nsa_dec_sel_kernel, n_pages=n_pages, per_seq=per_seq)
    page = lambda b, p, pt: pt[b, jnp.minimum(p, n_pages - 1)]
    grid_spec = pltpu.PrefetchScalarGridSpec(
        num_scalar_prefetch=1,
        grid=(n_seq, n_pages + 1),
        in_specs=[
            pl.BlockSpec((None, GROUP_COLS, HT_LANES), lambda b, p, pt: (b, 0, 0)),
            pl.BlockSpec((None, PAGE_SIZE, GROUP_COLS), lambda b, p, pt: (page(b, p, pt), 0, 2)),
            pl.BlockSpec((None, PAGE_SIZE, GROUP_COLS), lambda b, p, pt: (page(b, p, pt), 0, 3)),
            pl.BlockSpec((per_seq, GROUP_COLS), lambda b, p, pt: (row_block0 + b, 2)),
            pl.BlockSpec((per_seq, GROUP_COLS), lambda b, p, pt: (row_block0 + b, 3)),
            pl.BlockSpec((None, SEL_PAD, HT_LANES), lambda b, p, pt: (b, 0, 0)),
            pl.BlockSpec((None, 3, 1, HT_LANES), lambda b, p, pt: (b, 0, 0, 0)),
            pl.BlockSpec((4, PAGE_SIZE, HT_LANES), lambda b, p, pt: (0, 0, 0)),
        ],
        out_specs=pl.BlockSpec((per_seq, D_MODEL), lambda b, p, pt: (b, 0)),
        scratch_shapes=[
            pltpu.VMEM((GROUP_COLS, HT_LANES), BF16),
            pltpu.VMEM((1, 1, HT_LANES), F32),
            pltpu.VMEM((1, 1, HT_LANES), F32),
            pltpu.VMEM((1, GROUP_COLS, HT_LANES), F32),
        ],
    )
    return pl.pallas_call(
        kern,
        grid_spec=grid_spec,
        out_shape=jax.ShapeDtypeStruct((n_seq * per_seq, D_MODEL), F32),
        compiler_params=_cparams("parallel", "arbitrary"),
        name="nsa_dec_sel",
    )(page_table, q_bd, pool, pool, kv_new, kv_new, sel, gate_rows, bias)


def _nsa_dec_win_kernel(qbd_ref, w_ref, wn_ref, g_ref, bias_ref, o_ref, ws_ref, m_ref, l_ref, acc_ref, *, per_seq):
    n_past = w_ref.shape[0]
    n_tiles = n_past // PAGE_SIZE
    qz = (qbd_ref[...].astype(F32) * HEAD_DIM ** -0.5).astype(BF16)
    _reset_softmax(m_ref, l_ref, acc_ref)
    krow = lax.broadcasted_iota(jnp.int32, (PAGE_SIZE, HT_LANES), 0)
    tq = _mod_pow2(lax.broadcasted_iota(jnp.int32, (PAGE_SIZE, HT_LANES), 1), per_seq)
    for w in range(n_tiles):
        rows = w_ref[w * PAGE_SIZE:(w + 1) * PAGE_SIZE, :]
        mk = (krow > tq) if w == 0 else None
        bias = bias_ref[0] if w == n_tiles - 1 else bias_ref[3]
        s_t = _dot(rows[:, :GROUP_COLS].astype(BF16), qz) + bias
        _dec_softmax_step(s_t, mk, rows[:, GROUP_COLS:].astype(BF16), m_ref, l_ref, acc_ref)
    pad = PAGE_SIZE - per_seq
    new = jnp.concatenate([wn_ref[...], jnp.zeros((pad, 2 * GROUP_COLS), F32)], axis=0)
    s_t = _dot(new[:, :GROUP_COLS].astype(BF16), qz) + bias_ref[1]
    _dec_softmax_step(s_t, krow <= tq, new[:, GROUP_COLS:].astype(BF16), m_ref, l_ref, acc_ref)
    o_ref[...] = _dec_finish(m_ref, l_ref, acc_ref, g_ref[2], per_seq)
    ws_ref[0:n_past - per_seq, :] = w_ref[per_seq:n_past, :]
    ws_ref[n_past - per_seq:n_past, :] = wn_ref[...]


def nsa_dec_win(q_bd, win_past, win_new, gate_rows, bias, row_block0, per_seq):
    n_seq, n_past = win_past.shape[:2]
    assert n_past == WINDOW and n_past % PAGE_SIZE == 0 and PAGE_SIZE + 1 >= T5_SATURATION_DIST
    kern = functools.partial(_nsa_dec_win_kernel, per_seq=per_seq)
    return pl.pallas_call(
        kern,
        grid=(n_seq,),
        in_specs=[
            pl.BlockSpec((None, GROUP_COLS, HT_LANES), lambda b: (b, 0, 0)),
            pl.BlockSpec((None, n_past, 2 * GROUP_COLS), lambda b: (b, 0, 0)),
            pl.BlockSpec((per_seq, 2 * GROUP_COLS), lambda b: (row_block0 + b, 0)),
            pl.BlockSpec((None, 3, 1, HT_LANES), lambda b: (b, 0, 0, 0)),
            pl.BlockSpec((4, PAGE_SIZE, HT_LANES), lambda b: (0, 0, 0)),
        ],
        out_specs=[
            pl.BlockSpec((per_seq, D_MODEL), lambda b: (b, 0)),
            pl.BlockSpec((None, n_past, 2 * GROUP_COLS), lambda b: (b, 0, 0)),
        ],
        out_shape=[
            jax.ShapeDtypeStruct((n_seq * per_seq, D_MODEL), F32),
            jax.ShapeDtypeStruct((n_seq, n_past, 2 * GROUP_COLS), F32),
        ],
        scratch_shapes=[
            pltpu.VMEM((1, 1, HT_LANES), F32),
            pltpu.VMEM((1, 1, HT_LANES), F32),
            pltpu.VMEM((1, GROUP_COLS, HT_LANES), F32),
        ],
        compiler_params=_cparams("parallel"),
        name="nsa_dec_win",
    )(q_bd, win_past, win_new, gate_rows, bias)


def nsa_decode(table, page_table, pool, win_past, q_rows, kv_new, win_new, gates_rows, cw, row_block0, pos0, per_seq):
    n_seq, n_pages = page_table.shape
    pe_t, w1_bd, w2_bd = cw
    past = n_pages * PAGE_SIZE
    s_total = past + per_seq
    n_cmp = (s_total - CMP_LEN) // CMP_STRIDE + 1
    n_sel = -(-s_total // SEL_LEN)
    assert n_cmp == past // CMP_STRIDE - 1 and n_sel == past // SEL_LEN + 1 and per_seq <= CMP_STRIDE
    assert pos0 == past and pos0 % SEL_LEN == 0 and pos0 >= WINDOW and past // CMP_STRIDE == PAGE_SIZE
    bias = nsa_dec_bias(table, pos0, per_seq)
    ctok = nsa_dec_compress(nsa_dec_gather(page_table, pool), pe_t, w1_bd, w2_bd).astype(BF16)
    rows_per_seq = past // CMP_STRIDE
    k_cmp = ctok[0].reshape(n_seq, rows_per_seq, GROUP_COLS)
    v_cmp = ctok[1].reshape(n_seq, rows_per_seq, GROUP_COLS)
    q_bd = group_block_diag_queries(q_rows, n_seq, per_seq)
    gate_rows = gates_rows.reshape(n_seq, per_seq, 3, N_HEADS).transpose(0, 2, 3, 1).reshape(n_seq, 3, 1, HT_LANES)
    o_cmp, sel = nsa_dec_cmp(q_bd, k_cmp, v_cmp, gate_rows, bias, n_cmp, n_sel, pos0, per_seq)
    o_slc = nsa_dec_sel(page_table, q_bd, pool, kv_new, sel, gate_rows, bias, row_block0, per_seq)
    o_win, win_state = nsa_dec_win(q_bd, win_past, win_new, gate_rows, bias, row_block0, per_seq)
    return o_cmp, o_slc, o_win, win_state


GLA_SUB = 16
GLA_BLOCK = 512


def _gla_kernel(q_ref, k_ref, v_ref, r_ref, a_ref, wa_ref, ba_ref, ng_ref, s0_ref, o_ref, sfin_ref, s_ref, *,
                n_chunks, chunk):
    t_blk = pl.program_id(2)

    @pl.when(t_blk == 0)
    def _():
        s_ref[...] = s0_ref[...]

    tri = _tril_bf16(chunk)
    it = lax.broadcasted_iota(jnp.int32, (chunk, chunk), 0)
    js = lax.broadcasted_iota(jnp.int32, (chunk, chunk), 1)
    same_sub = jnp.right_shift(it, 4) == jnp.right_shift(js, 4)
    n_sub = chunk // GLA_SUB

    def chunk_step(c, carry):
        r0 = pl.multiple_of(c * chunk, chunk)
        rows = pl.ds(r0, chunk)
        q = q_ref[rows, :] * GLA_DK ** -0.5
        k = k_ref[rows, :]
        v = v_ref[rows, :].astype(BF16)
        g = _log_sigmoid(_dot(a_ref[rows, :].astype(BF16), wa_ref[...]) + ba_ref[...]) / GLA_TAU
        b = _cumsum_rows(g, tri)
        s_old = s_ref[...]
        o = _dot((q * jnp.exp(b)).astype(BF16), s_old.astype(BF16))
        a_mat = jnp.zeros((chunk, chunk), F32)
        for d in range(GLA_SUB):
            if d == 0:
                val = jnp.sum(q * k, axis=-1, keepdims=True)
            else:
                k_sh = pltpu.roll(k, d, 0)
                b_sh = pltpu.roll(b, d, 0)
                val = jnp.sum(q * k_sh * jnp.exp(jnp.minimum(b - b_sh, 0.0)), axis=-1, keepdims=True)
            a_mat = a_mat + jnp.where((it - js == d) & same_sub, val, 0.0)
        if n_sub > 1:
            blocks = [jnp.zeros((GLA_SUB, chunk), F32)]
            for a in range(1, n_sub):
                ref_b = b[a * GLA_SUB - 1:a * GLA_SUB, :]
                qa = q[a * GLA_SUB:(a + 1) * GLA_SUB] * jnp.exp(b[a * GLA_SUB:(a + 1) * GLA_SUB] - ref_b)
                kk = k * jnp.exp(jnp.minimum(ref_b - b, 0.0))
                blocks.append(_dot_nt(qa.astype(BF16), kk.astype(BF16)))
            off = jnp.concatenate(blocks, axis=0)
            a_mat = a_mat + jnp.where(jnp.right_shift(js, 4) < jnp.right_shift(it, 4), off, 0.0)
        o = o + _dot(a_mat.astype(BF16), v)
        b_last = b[chunk - 1:chunk, :]
        kd = k * jnp.exp(b_last - b)
        pad_k = jnp.zeros((GLA_DK - chunk, GLA_DK), F32)
        pad_v = jnp.zeros((GLA_DK - chunk, GLA_DV), BF16)
        kd_t = jnp.concatenate([kd, pad_k], axis=0).T
        dec = jnp.broadcast_to(jnp.exp(b_last), (GLA_DK, GLA_DK)).T
        s_ref[...] = (s_old * jnp.concatenate([dec] * (GLA_DV // GLA_DK), axis=1)
                      + _dot(kd_t.astype(BF16), jnp.concatenate([v, pad_v], axis=0)))
        y = o * lax.rsqrt(jnp.mean(o * o, axis=-1, keepdims=True) + EPS) * ng_ref[...]
        rr = r_ref[rows, :]
        o_ref[rows, :] = y * (rr * _sigmoid(rr))
        return carry

    lax.fori_loop(0, n_chunks, chunk_step, 0)

    @pl.when(t_blk == pl.num_programs(2) - 1)
    def _():
        sfin_ref[...] = s_ref[...]


def gla_prompt(q, k, v, r, a_lr, w_a2, b_a, norm_g, s0, n_batch, seq, row_block0=0):
    chunk = GLA_CHUNK if seq % GLA_CHUNK == 0 else seq
    blk = min(GLA_BLOCK, seq)
    nt = seq // blk
    m = n_batch * seq
    kern = functools.partial(_gla_kernel, n_chunks=blk // chunk, chunk=chunk)
    row = lambda b, h, t: row_block0 + b * nt + t
    return pl.pallas_call(
        kern,
        grid=(n_batch, GLA_HEADS, nt),
        in_specs=[
            pl.BlockSpec((blk, GLA_DK), lambda b, h, t: (row(b, h, t), h)),
            pl.BlockSpec((blk, GLA_DK), lambda b, h, t: (row(b, h, t), h)),
            pl.BlockSpec((blk, GLA_DV), lambda b, h, t: (row(b, h, t), h)),
            pl.BlockSpec((blk, GLA_DV), lambda b, h, t: (row(b, h, t), h)),
            pl.BlockSpec((blk, GLA_RANK), lambda b, h, t: (row(b, h, t), 0)),
            pl.BlockSpec((GLA_RANK, GLA_DK), lambda b, h, t: (0, h)),
            pl.BlockSpec((1, GLA_DK), lambda b, h, t: (0, h)),
            pl.BlockSpec((1, GLA_DV), lambda b, h, t: (0, 0)),
            pl.BlockSpec((None, None, GLA_DK, GLA_DV), lambda b, h, t: (b, h, 0, 0)),
        ],
        out_specs=[
            pl.BlockSpec((blk, GLA_DV), lambda b, h, t: (b * nt + t, h)),
            pl.BlockSpec((None, None, GLA_DK, GLA_DV), lambda b, h, t: (b, h, 0, 0)),
        ],
        out_shape=[
            jax.ShapeDtypeStruct((m, D_MODEL), F32),
            jax.ShapeDtypeStruct((n_batch, GLA_HEADS, GLA_DK, GLA_DV), F32),
        ],
        scratch_shapes=[pltpu.VMEM((GLA_DK, GLA_DV), F32)],
        compiler_params=_cparams("parallel", "parallel", "arbitrary"),
        name="gla",
    )(q, k, v, r, a_lr, w_a2.astype(BF16), b_a.reshape(1, GLA_QK), norm_g.reshape(1, GLA_DV), s0)


def _gla_decode_kernel(q_ref, k_ref, v_ref, r_ref, a_ref, wa_ref, ba_ref, ng_ref, s0_ref, o_ref, s_ref, *, n_rows):
    pad_rows = 2 * n_rows

    def pad_to(x, rows):
        return jnp.concatenate([x, jnp.zeros((rows - x.shape[0], x.shape[1]), x.dtype)], axis=0)

    row = lax.broadcasted_iota(jnp.int32, (n_rows, GLA_DK), 0)
    row_v = lax.broadcasted_iota(jnp.int32, (n_rows, GLA_DV), 0)
    q = q_ref[...] * GLA_DK ** -0.5
    k = k_ref[...]
    v = v_ref[...]
    g = _dot(pad_to(a_ref[...], pad_rows).astype(BF16), wa_ref[...])[:n_rows]
    g = _log_sigmoid(g + ba_ref[...]) / GLA_TAU
    b = g
    shift = 1
    while shift < n_rows:
        b = b + jnp.where(row >= shift, pltpu.roll(b, shift, 0), 0.0)
        shift *= 2
    s_old = s0_ref[...]
    o = _dot(pad_to(q * jnp.exp(b), pad_rows).astype(BF16), s_old.astype(BF16))[:n_rows]
    for d in range(n_rows):
        if d == 0:
            val = jnp.sum(q * k, axis=-1, keepdims=True)
            o = o + val * v
        else:
            k_sh = pltpu.roll(k, d, 0)
            b_sh = pltpu.roll(b, d, 0)
            val = jnp.sum(q * k_sh * jnp.exp(jnp.minimum(b - b_sh, 0.0)), axis=-1, keepdims=True)
            o = o + jnp.where(row_v >= d, val * pltpu.roll(v, d, 0), 0.0)
    b_last = b[n_rows - 1:n_rows, :]
    kd_t = pad_to(k * jnp.exp(b_last - b), GLA_DK).T
    dec = jnp.broadcast_to(jnp.exp(b_last), (GLA_DK, GLA_DK)).T
    s_ref[...] = (s_old * jnp.concatenate([dec] * (GLA_DV // GLA_DK), axis=1)
                  + _dot(kd_t.astype(BF16), pad_to(v, GLA_DK).astype(BF16)))
    y = o * lax.rsqrt(jnp.mean(o * o, axis=-1, keepdims=True) + EPS) * ng_ref[...]
    rr = r_ref[...]
    o_ref[...] = y * (rr * _sigmoid(rr))


def gla_decode(q, k, v, r, a_lr, w_a2, b_a, norm_g, s0, n_seq, per_seq, row_block0):
    assert per_seq % GLA_CHUNK != 0 and per_seq % 8 == 0
    kern = functools.partial(_gla_decode_kernel, n_rows=per_seq)
    row = lambda b, h: (row_block0 + b, h)
    return pl.pallas_call(
        kern,
        grid=(n_seq, GLA_HEADS),
        in_specs=[
            pl.BlockSpec((per_seq, GLA_DK), row),
            pl.BlockSpec((per_seq, GLA_DK), row),
            pl.BlockSpec((per_seq, GLA_DV), row),
            pl.BlockSpec((per_seq, GLA_DV), row),
            pl.BlockSpec((per_seq, GLA_RANK), lambda b, h: (row_block0 + b, 0)),
            pl.BlockSpec((GLA_RANK, GLA_DK), lambda b, h: (0, h)),
            pl.BlockSpec((1, GLA_DK), lambda b, h: (0, h)),
            pl.BlockSpec((1, GLA_DV), lambda b, h: (0, 0)),
            pl.BlockSpec((None, None, GLA_DK, GLA_DV), lambda b, h: (b, h, 0, 0)),
        ],
        out_specs=[
            pl.BlockSpec((per_seq, GLA_DV), lambda b, h: (b, h)),
            pl.BlockSpec((None, None, GLA_DK, GLA_DV), lambda b, h: (b, h, 0, 0)),
        ],
        out_shape=[
            jax.ShapeDtypeStruct((n_seq * per_seq, D_MODEL), F32),
            jax.ShapeDtypeStruct((n_seq, GLA_HEADS, GLA_DK, GLA_DV), F32),
        ],
        compiler_params=_cparams("parallel", "parallel"),
        name="gla_decode",
    )(q, k, v, r, a_lr, w_a2.astype(BF16), b_a.reshape(1, GLA_QK), norm_g.reshape(1, GLA_DV), s0)


def masked_softmax(logits, mask, axis):
    logits = jnp.where(mask, logits.astype(jnp.float32), NEG)
    m = jnp.max(logits, axis=axis, keepdims=True)
    e = jnp.exp(logits - m) * mask
    return e / jnp.maximum(jnp.sum(e, axis=axis, keepdims=True), 1e-30)


def t5_bucket(dist):
    n = jnp.maximum(dist, 0)
    max_exact = N_BUCKETS // 2
    nf = jnp.maximum(n, 1).astype(jnp.float32)
    log_part = max_exact + (jnp.log(nf / max_exact) / math.log(MAX_DISTANCE / max_exact)
                            * (N_BUCKETS - max_exact)).astype(jnp.int32)
    return jnp.where(n < max_exact, n, jnp.minimum(log_part, N_BUCKETS - 1))


def head_bias(dist, table):
    b = table.astype(jnp.float32)[t5_bucket(dist)]
    return b.reshape(*dist.shape, NSA_KV_GROUPS, NSA_HPG).transpose(2, 3, 0, 1)


def gather_pages(pool, page_table):
    rows = pool[page_table]
    return rows.reshape(page_table.shape[0], -1, *pool.shape[2:])


def query_blocks(fn, args, axes):
    T = args[0].shape[axes[0]]
    if T % Q_BLOCK != 0 or T <= Q_BLOCK:
        return fn(*args)
    nb = T // Q_BLOCK

    def split(a, ax):
        a = a.reshape(a.shape[:ax] + (nb, Q_BLOCK) + a.shape[ax + 1:])
        return jnp.moveaxis(a, ax, 0)

    out = lax.map(lambda blk: fn(*blk), tuple(split(a, ax) for a, ax in zip(args, axes)))
    out = jnp.moveaxis(out, 0, 1)
    return out.reshape(out.shape[:1] + (T,) + out.shape[3:])


def compress_tokens(kv, pe, w1, w2):
    B, S = kv.shape[:2]
    n_cmp = (S - CMP_LEN) // CMP_STRIDE + 1
    idx = jnp.arange(n_cmp)[:, None] * CMP_STRIDE + jnp.arange(CMP_LEN)[None, :]
    blocks = kv[:, idx] + pe[None, None, :, None, :]
    z = blocks.transpose(0, 1, 3, 2, 4).reshape(B, n_cmp, NSA_KV_GROUPS, CMP_LEN * HEAD_DIM)
    return jax.nn.gelu(z @ w1) @ w2


def gqa_window_attend(q, k, v, q_pos, k_pos, table):
    B, Tq = q.shape[:2]
    qg = q.reshape(B, Tq, NSA_KV_GROUPS, NSA_HPG, HEAD_DIM)
    logits = jnp.einsum('bqghd,bkgd->bghqk', qg, k).astype(jnp.float32) * HEAD_DIM ** -0.5
    dist = q_pos[:, None] - k_pos[None, :]
    mask = (dist >= 0) & (dist < WINDOW) & (k_pos[None, :] >= 0)
    p = masked_softmax(logits + head_bias(dist, table), mask, -1)
    o = jnp.einsum('bghqk,bkgd->bqghd', p, v)
    return o.reshape(B, Tq, N_HEADS, HEAD_DIM)


def nsa_core_sample(q, kv_new, win_new, gates, phi_pe, phi_w1, phi_w2, table, past_kv, past_win, pos0):
    B, T = q.shape[:2]
    G = NSA_KV_GROUPS
    gates = jax.nn.sigmoid(gates).reshape(B, T, 3, N_HEADS)
    kv = jnp.concatenate([past_kv, kv_new], axis=1)
    S = kv.shape[1]
    q_pos = pos0 + jnp.arange(T)
    qg = q.reshape(B, T, G, NSA_HPG, HEAD_DIM)
    k_cmp = compress_tokens(kv[:, :, 0], phi_pe[0], phi_w1[0], phi_w2[0])
    v_cmp = compress_tokens(kv[:, :, 1], phi_pe[1], phi_w1[1], phi_w2[1])
    n_cmp = k_cmp.shape[1]
    c_start = jnp.arange(n_cmp) * CMP_STRIDE
    dist_c = q_pos[:, None] - (c_start + CMP_LEN - 1)[None, :]
    logits_c = jnp.einsum('bqghd,bcgd->bghqc', qg, k_cmp).astype(jnp.float32) * HEAD_DIM ** -0.5
    p_cmp = masked_softmax(logits_c + head_bias(dist_c, table), dist_c >= 0, -1)
    o_cmp = jnp.einsum('bghqc,bcgd->bqghd', p_cmp, v_cmp).reshape(B, T, N_HEADS, HEAD_DIM)
    n_sel = -(-S // SEL_LEN)
    s_start = jnp.arange(n_sel) * SEL_LEN
    incidence = ((c_start[:, None] < s_start[None, :] + SEL_LEN)
                 & (c_start[:, None] + CMP_LEN > s_start[None, :])).astype(jnp.float32)
    imp = jnp.einsum('bghqc,cj->bgqj', p_cmp, incidence)
    cur = q_pos // SEL_LEN
    j = jnp.arange(n_sel)
    forced = (j[None, :] == 0) | (j[None, :] == cur[:, None]) | (j[None, :] == cur[:, None] - 1)
    avail = j[None, :] <= cur[:, None]
    score = jnp.where(forced, BIG, jnp.where(avail, imp, -BIG))
    _, sel = lax.top_k(score, min(N_SELECT, n_sel))
    member = jnp.sum(jax.nn.one_hot(sel, n_sel, dtype=jnp.float32), axis=-2) > 0.5
    k_pos = jnp.arange(S)
    key_member = jnp.take(member, k_pos // SEL_LEN, axis=-1)
    dist = q_pos[:, None] - k_pos[None, :]
    mask = key_member[:, :, None] & (dist >= 0)[None, None, None]
    logits = jnp.einsum('bqghd,bkgd->bghqk', qg, kv[:, :, 2]).astype(jnp.float32) * HEAD_DIM ** -0.5
    p = masked_softmax(logits + head_bias(dist, table), mask, -1)
    o_slc = jnp.einsum('bghqk,bkgd->bqghd', p, kv[:, :, 3]).reshape(B, T, N_HEADS, HEAD_DIM)
    win_all = jnp.concatenate([past_win, win_new], axis=1)
    lw = win_all.shape[1]
    kw_pos = pos0 + T - lw + jnp.arange(lw)
    o_win = gqa_window_attend(q, win_all[:, :, 0], win_all[:, :, 1], q_pos, kw_pos, table)
    win_state = win_all[:, win_all.shape[1] - min(WINDOW, win_all.shape[1]):]
    o = (gates[:, :, 0, :, None] * o_cmp + gates[:, :, 1, :, None] * o_slc + gates[:, :, 2, :, None] * o_win)
    return o.reshape(B * T, D_MODEL), win_state


def fox_attend(q, dq, q_pos, k, v, dk, k_pos):
    logits = jnp.einsum('bqhd,bkhd->bhqk', q, k).astype(jnp.float32) * HEAD_DIM ** -0.5
    logits = logits + dq.transpose(0, 2, 1)[..., None] - dk.transpose(0, 2, 1)[:, :, None, :]
    p = masked_softmax(logits, q_pos[:, None] >= k_pos[None, :], -1)
    return jnp.einsum('bhqk,bkhd->bqhd', p, v)


def fox_core(q, kv_new, f_raw, b_f, past_kv, past_logf, pos0):
    B, T = q.shape[:2]
    logf_new = jax.nn.log_sigmoid(f_raw + b_f.astype(jnp.float32))
    if past_kv is None:
        kv, logf = kv_new, logf_new
    else:
        kv = jnp.concatenate([past_kv, kv_new], axis=1)
        logf = jnp.concatenate([past_logf.astype(jnp.float32), logf_new], axis=1)
    S = kv.shape[1]
    cum = jnp.cumsum(logf, axis=1)
    k_pos = jnp.arange(S)
    q_pos = pos0 + jnp.arange(T)
    k, v = kv[:, :, 0], kv[:, :, 1]
    o = query_blocks(lambda q_, dq_, p_: fox_attend(q_, dq_, p_, k, v, cum, k_pos), [q, cum[:, S - T:], q_pos], [1, 1, 0])
    return o.reshape(B * T, D_MODEL), logf_new


def gla_chunk(S, q, k, v, g):
    C = q.shape[1]
    b = jnp.cumsum(g, axis=1)
    causal = jnp.tril(jnp.ones((C, C), dtype=bool))
    diff = b[:, :, None] - b[:, None, :]
    decay = jnp.exp(jnp.where(causal[None, :, :, None, None], diff, NEG))
    A = jnp.einsum('bthd,bshd,btshd->bhts', q, k, decay)
    o = jnp.einsum('bhts,bshv->bthv', A, v) + jnp.einsum('bthd,bhdv->bthv', q * jnp.exp(b), S)
    b_last = b[:, -1]
    S_new = jnp.exp(b_last)[..., None] * S + jnp.einsum('bshd,bshv->bhdv', k * jnp.exp(b_last[:, None] - b), v)
    return o, S_new


def gla_scan(S0, q, k, v, g):
    B, T = q.shape[:2]
    c = GLA_CHUNK if T % GLA_CHUNK == 0 else T
    nc = T // c

    def to_chunks(a):
        return jnp.moveaxis(a.reshape(B, nc, c, *a.shape[2:]), 1, 0)

    def step(S, blk):
        o, S = gla_chunk(S, *blk)
        return S, o

    S_fin, o = lax.scan(step, S0, tuple(to_chunks(a) for a in (q, k, v, g)))
    return jnp.moveaxis(o, 0, 1).reshape(B, T, GLA_HEADS, GLA_DV), S_fin


def gla_core(q, k, v, r, a_lr, w_a2, b_a, norm_g, S0):
    B, T = q.shape[:2]
    q = q.reshape(B, T, GLA_HEADS, GLA_DK) * GLA_DK ** -0.5
    k = k.reshape(B, T, GLA_HEADS, GLA_DK)
    v = v.reshape(B, T, GLA_HEADS, GLA_DV)
    g = jax.nn.log_sigmoid(a_lr @ w_a2.astype(jnp.float32) + b_a.astype(jnp.float32)) / GLA_TAU
    g = g.reshape(B, T, GLA_HEADS, GLA_DK)
    if S0 is None:
        S0 = jnp.zeros((B, GLA_HEADS, GLA_DK, GLA_DV), jnp.float32)
    o, S_fin = gla_scan(S0.astype(jnp.float32), q, k, v, g)
    xf = o
    y = xf * lax.rsqrt(jnp.mean(xf * xf, axis=-1, keepdims=True) + EPS) * norm_g
    o = y.reshape(B, T, D_MODEL) * jax.nn.silu(r)
    return o.reshape(B * T, D_MODEL), S_fin


_C1 = D_MODEL + NSA_KV_COLS
_C2 = _C1 + NSA_WIN_COLS
NSA_SPLITS = ((D_MODEL, _C1, F32), (D_MODEL, _C1, BF16), (_C1, _C2, F32), (_C1, _C2, BF16))
NSA_SPLITS_T = ((0, D_MODEL, BF16), (_C1 - GROUP_COLS, _C1, BF16), (_C2 - GROUP_COLS, _C2, BF16),
                (_C2, _C2 + 3 * N_HEADS, F32))
FOX_SPLITS = ((D_MODEL, 3 * D_MODEL, F32), (D_MODEL, 2 * D_MODEL, BF16), (3 * D_MODEL, 3 * D_MODEL + N_HEADS, F32))
FOX_SPLITS_T = ((0, D_MODEL, BF16), (2 * D_MODEL, 3 * D_MODEL, BF16))
GLA_SPLITS = ((0, GLA_QK, F32), (GLA_QK, 2 * GLA_QK, F32), (2 * GLA_QK, 2 * GLA_QK + D_MODEL, F32),
              (2 * GLA_QK + D_MODEL, 2 * GLA_QK + 2 * D_MODEL, F32),
              (2 * GLA_QK + 2 * D_MODEL, 2 * GLA_QK + 2 * D_MODEL + GLA_RANK, F32))


def kernel(x_prompt, x_sample, cache_nsa_kv, state_nsa_win, cache_fox_kv, cache_fox_logf, state_gla, page_table, rel_bias, norm_g, final_g, nsa_w_in, nsa_phi_pe, nsa_phi_w1, nsa_phi_w2, nsa_w_out, fox_w_in, fox_b_f, fox_w_out, gla_w_in, gla_w_a2, gla_b_a, gla_norm_g, gla_w_out, mlp_w1, mlp_w2):
    bp, tp, _ = x_prompt.shape
    bs, ts, _ = x_sample.shape
    mp, ms = bp * tp, bs * ts
    pos_s = page_table.shape[1] * PAGE_SIZE
    x = jnp.concatenate([x_prompt.reshape(mp, D_MODEL), x_sample.reshape(ms, D_MODEL)], axis=0)
    toep = t5_tiles(rel_bias, ATT_TILE)

    def both(a, shape_p, shape_s):
        return a[:mp].reshape(shape_p), a[mp:].reshape(shape_s)

    nsa_kv_p, nsa_kv_s, nsa_win_p, nsa_win_s = [], [], [], []
    fox_kv_p, fox_kv_s, fox_lf_p, fox_lf_s, gla_p, gla_s = [], [], [], [], [], []
    G = NSA_KV_GROUPS
    for i in range(DEPTH):
        kind, li = i % N_MIXERS, i // N_MIXERS
        if kind == 0:
            kvn, kv_bf, winn, win_bf, q_t, vslc_t, vwin_t, gates_t = norm_proj(
                x, norm_g[i, 0], nsa_w_in[li], NSA_SPLITS, NSA_SPLITS_T)
            cw = compress_weights(nsa_phi_pe[li], nsa_phi_w1[li], nsa_phi_w2[li])
            o_parts_p = nsa_prompt_t(rel_bias, toep, q_t, kvn, kv_bf, win_bf, vslc_t, vwin_t, gates_t, cw, bp, tp)
            kv_p, kv_s = both(kvn, (bp, tp, 4, G, HEAD_DIM), (bs, ts, 4, G, HEAD_DIM))
            w_p, w_s = both(winn, (bp, tp, 2, G, HEAD_DIM), (bs, ts, 2, G, HEAD_DIM))
            n_pool = cache_nsa_kv.shape[1]
            *o_parts_s, ws_s = nsa_decode(
                rel_bias, page_table, cache_nsa_kv[li].reshape(n_pool, PAGE_SIZE, NSA_KV_COLS),
                state_nsa_win[li].reshape(bs, state_nsa_win.shape[2], NSA_WIN_COLS), q_t[:, mp:].T, kvn, winn,
                gates_t[:, mp:].T, cw, mp // ts, pos_s, ts)
            ws_p = w_p[:, tp - min(WINDOW, tp):]
            ws_s = ws_s.reshape(bs, -1, 2, G, HEAD_DIM)
            nsa_kv_p.append(kv_p); nsa_kv_s.append(kv_s); nsa_win_p.append(ws_p); nsa_win_s.append(ws_s)
            o_parts = [jnp.concatenate([op, os_], axis=0) for op, os_ in zip(o_parts_p, o_parts_s)]
            w_out = nsa_w_out[li]
        elif kind == 1:
            kvn, k_bf, fr, q_t, v_t = norm_proj(x, norm_g[i, 0], fox_w_in[li], FOX_SPLITS, FOX_SPLITS_T)
            kv_p, kv_s = both(kvn, (bp, tp, 2, N_HEADS, HEAD_DIM), (bs, ts, 2, N_HEADS, HEAD_DIM))
            lf_all, d_all = fox_prep_prompt(fr, fox_b_f[li], bp, tp)
            o_p = fox_attn_t(q_t, k_bf, v_t, d_all, bp, tp)
            lf_p = lf_all.reshape(bp, tp, N_HEADS)
            n_pool = cache_fox_kv.shape[1]
            q_bd = block_diag_queries(q_t[:, mp:].T, bs, ts, N_HEADS)
            o_s, lf_s = fox_decode(page_table, q_bd, cache_fox_kv[li].reshape(n_pool, PAGE_SIZE, 2 * D_MODEL),
                                   cache_fox_logf[li], kvn, fr, fox_b_f[li], mp // ts, bs, ts)
            lf_s = lf_s.reshape(bs, ts, N_HEADS)
            fox_kv_p.append(kv_p); fox_kv_s.append(kv_s); fox_lf_p.append(lf_p); fox_lf_s.append(lf_s)
            o_parts = [jnp.concatenate([o_p, o_s], axis=0)]
            w_out = fox_w_out[li]
        else:
            q, k, v, r, a = norm_proj(x, norm_g[i, 0], gla_w_in[li], GLA_SPLITS)
            s_zero = jnp.zeros((bp, GLA_HEADS, GLA_DK, GLA_DV), F32)
            o_p, s_p = gla_prompt(q, k, v, r, a, gla_w_a2[li], gla_b_a[li], gla_norm_g[li], s_zero, bp, tp)
            o_s, s_s = gla_decode(q, k, v, r, a, gla_w_a2[li], gla_b_a[li], gla_norm_g[li], state_gla[li], bs, ts,
                                  mp // ts)
            gla_p.append(s_p); gla_s.append(s_s)
            o_parts = [jnp.concatenate([o_p, o_s], axis=0)]
            w_out = gla_w_out[li]
        x = out_mlp(x, o_parts, w_out.astype(BF16), norm_g[i, 1], mlp_w1[i].astype(BF16), mlp_w2[i].astype(BF16))
    y = final_norm(x, final_g)
    y_p, y_s = both(y, (bp, tp, D_MODEL), (bs, ts, D_MODEL))
    return (y_p, y_s, jnp.stack(nsa_kv_p), jnp.stack(nsa_kv_s), jnp.stack(nsa_win_p), jnp.stack(nsa_win_s),
            jnp.stack(fox_kv_p), jnp.stack(fox_kv_s), jnp.stack(fox_lf_p), jnp.stack(fox_lf_s),
            jnp.stack(gla_p), jnp.stack(gla_s))
```

```python
import functools
import math

import jax
import jax.numpy as jnp
from jax import lax
from jax.experimental import pallas as pl
from jax.experimental.pallas import tpu as pltpu

D_MODEL = 1024
DEPTH = 4
PAGE_SIZE = 128
N_MIXERS = 3
N_HEADS = 16
HEAD_DIM = D_MODEL // N_HEADS
NSA_KV_GROUPS = 4
NSA_HPG = N_HEADS // NSA_KV_GROUPS
CMP_LEN = 32
CMP_STRIDE = 16
CMP_HIDDEN = 2 * HEAD_DIM
SEL_LEN = 64
N_SELECT = 16
WINDOW = 512
N_BUCKETS = 32
MAX_DISTANCE = 128
GLA_HEADS = 4
GLA_DK = D_MODEL // 2 // GLA_HEADS
GLA_DV = D_MODEL // GLA_HEADS
GLA_QK = GLA_HEADS * GLA_DK
GLA_RANK = 16
GLA_TAU = 16.0
GLA_CHUNK = 64
D_FF = 4 * D_MODEL
Q_BLOCK = 128
EPS = 1e-6
NEG = -1e30
BIG = 1e9
NSA_KV_COLS = 4 * NSA_KV_GROUPS * HEAD_DIM
NSA_WIN_COLS = 2 * NSA_KV_GROUPS * HEAD_DIM
GROUP_COLS = NSA_KV_GROUPS * HEAD_DIM

F32 = jnp.float32
BF16 = jnp.bfloat16

V7X_VMEM_LIMIT_BYTES = 48 * 1024 * 1024
V7X_LANES = 128
TOKEN_TILE = 512
FF_TILE = 1024
ATT_TILE = 256
CMP_Q_TILE = 128
SEL_PAD = V7X_LANES
T5_SATURATION_DIST = 113


def _cparams(*sem):
    return pltpu.CompilerParams(dimension_semantics=sem, vmem_limit_bytes=V7X_VMEM_LIMIT_BYTES)


def _rms(x, g):
    return x * lax.rsqrt(jnp.mean(x * x, axis=-1, keepdims=True) + EPS) * g


def _sigmoid(x):
    return 1.0 / (1.0 + jnp.exp(-x))


def _log_sigmoid(x):
    return jnp.minimum(x, 0.0) - jnp.log(1.0 + jnp.exp(-jnp.abs(x)))


def _gelu_tanh(x):
    return 0.5 * x * (1.0 + jnp.tanh(math.sqrt(2.0 / math.pi) * (x + 0.044715 * (x * x * x))))


def _dot(a, b):
    return jnp.dot(a, b, preferred_element_type=F32)


def _dot_nt(a, b):
    return lax.dot_general(a, b, (((1,), (1,)), ((), ())), preferred_element_type=F32)


def _split_bf16(x, parts):
    out = []
    for _ in range(parts):
        hi = x.astype(BF16)
        out.append(hi)
        x = x - hi.astype(F32)
    return out


def _t5_bucket(dist):
    n = jnp.maximum(dist, 0)
    max_exact = N_BUCKETS // 2
    nf = jnp.maximum(n, 1).astype(F32)
    log_part = max_exact + (jnp.log(nf / max_exact) / math.log(MAX_DISTANCE / max_exact)
                            * (N_BUCKETS - max_exact)).astype(jnp.int32)
    return jnp.where(n < max_exact, n, jnp.minimum(log_part, N_BUCKETS - 1))


def _bias_from_bucket(bucket, tbl_ref, h):
    val = jnp.full(bucket.shape, tbl_ref[0, h], F32)
    for k in range(1, N_BUCKETS):
        val = jnp.where(bucket == k, tbl_ref[k, h], val)
    return val


def _div_pow2(x, d):
    assert d & (d - 1) == 0
    return jnp.right_shift(x, d.bit_length() - 1)


def _mod_pow2(x, d):
    assert d & (d - 1) == 0
    return jnp.bitwise_and(x, d - 1)


def _smem_spec():
    return pl.BlockSpec(memory_space=pltpu.SMEM)


def _norm_proj_kernel(x_ref, g_ref, w_ref, wt_ref, *out_refs, splits, splits_t):
    h = _rms(x_ref[...], g_ref[...]).astype(BF16)
    for (lo, hi, _), o_ref in zip(splits, out_refs):
        o_ref[...] = _dot(h, w_ref[:, lo:hi]).astype(o_ref.dtype)
    for (lo, hi, _), o_ref in zip(splits_t, out_refs[len(splits):]):
        o_ref[...] = _dot_nt(wt_ref[lo:hi, :], h).astype(o_ref.dtype)


def norm_proj(x, g, w, splits, splits_t=()):
    m = x.shape[0]
    n = w.shape[1]
    w_bf = w.astype(BF16)
    kern = functools.partial(_norm_proj_kernel, splits=splits, splits_t=splits_t)
    return pl.pallas_call(
        kern,
        grid=(m // TOKEN_TILE,),
        in_specs=[
            pl.BlockSpec((TOKEN_TILE, D_MODEL), lambda i: (i, 0)),
            pl.BlockSpec((1, D_MODEL), lambda i: (0, 0)),
            pl.BlockSpec((D_MODEL, n), lambda i: (0, 0)),
            pl.BlockSpec((n, D_MODEL), lambda i: (0, 0)),
        ],
        out_specs=[pl.BlockSpec((TOKEN_TILE, hi - lo), lambda i: (i, 0)) for lo, hi, _ in splits]
        + [pl.BlockSpec((hi - lo, TOKEN_TILE), lambda i: (0, i)) for lo, hi, _ in splits_t],
        out_shape=[jax.ShapeDtypeStruct((m, hi - lo), dt) for lo, hi, dt in splits]
        + [jax.ShapeDtypeStruct((hi - lo, m), dt) for lo, hi, dt in splits_t],
        compiler_params=_cparams("parallel"),
        name="norm_proj",
    )(x, g.reshape(1, D_MODEL), w_bf, w_bf.T)


def _out_mlp_kernel(x_ref, *refs, n_o, n_first):
    oa_refs = refs[:n_o]
    ob_refs = refs[n_o:2 * n_o]
    wo_ref, g_ref, w1_ref, w2_ref, y_ref, x1_ref, h_ref, acc_ref = refs[2 * n_o:]
    i = pl.program_id(0)
    j = pl.program_id(1)

    @pl.when(j == 0)
    def _():
        first = i < n_first
        o = jnp.where(first, oa_refs[0][...], ob_refs[0][...])
        for ra, rb in zip(oa_refs[1:], ob_refs[1:]):
            o = o + jnp.where(first, ra[...], rb[...])
        x1 = x_ref[...] + _dot(o.astype(BF16), wo_ref[...])
        x1_ref[...] = x1
        h_ref[...] = _rms(x1, g_ref[...]).astype(BF16)
        acc_ref[...] = jnp.zeros_like(acc_ref)

    a = _dot(h_ref[...], w1_ref[...])
    a = jnp.square(jnp.maximum(a, 0.0)).astype(BF16)
    acc_ref[...] += _dot(a, w2_ref[...])

    @pl.when(j == pl.num_programs(1) - 1)
    def _():
        y_ref[...] = x1_ref[...] + acc_ref[...]


def out_mlp(x, o_first, o_rest, w_out, g, w1, w2):
    m = x.shape[0]
    n_o = len(o_first)
    n_first = o_first[0].shape[0] // TOKEN_TILE
    n_rest = o_rest[0].shape[0] // TOKEN_TILE
    assert o_first[0].shape[0] % TOKEN_TILE == 0 and n_first + n_rest == m // TOKEN_TILE
    kern = functools.partial(_out_mlp_kernel, n_o=n_o, n_first=n_first)
    tok = pl.BlockSpec((TOKEN_TILE, D_MODEL), lambda i, j: (i, 0))
    tok_a = pl.BlockSpec((TOKEN_TILE, D_MODEL), lambda i, j: (jnp.minimum(i, n_first - 1), 0))
    tok_b = pl.BlockSpec((TOKEN_TILE, D_MODEL), lambda i, j: (jnp.maximum(i - n_first, 0), 0))
    return pl.pallas_call(
        kern,
        grid=(m // TOKEN_TILE, D_FF // FF_TILE),
        in_specs=[tok] + [tok_a] * n_o + [tok_b] * n_o + [
            pl.BlockSpec((D_MODEL, D_MODEL), lambda i, j: (0, 0)),
            pl.BlockSpec((1, D_MODEL), lambda i, j: (0, 0)),
            pl.BlockSpec((D_MODEL, FF_TILE), lambda i, j: (0, j)),
            pl.BlockSpec((FF_TILE, D_MODEL), lambda i, j: (j, 0)),
        ],
        out_specs=tok,
        out_shape=jax.ShapeDtypeStruct((m, D_MODEL), F32),
        scratch_shapes=[
            pltpu.VMEM((TOKEN_TILE, D_MODEL), F32),
            pltpu.VMEM((TOKEN_TILE, D_MODEL), BF16),
            pltpu.VMEM((TOKEN_TILE, D_MODEL), F32),
        ],
        compiler_params=_cparams("parallel", "arbitrary"),
        name="out_mlp",
    )(x, *o_first, *o_rest, w_out, g.reshape(1, D_MODEL), w1, w2)


def _final_norm_kernel(x_ref, g_ref, y_ref):
    y_ref[...] = _rms(x_ref[...], g_ref[...])


def final_norm(x, g):
    m = x.shape[0]
    return pl.pallas_call(
        _final_norm_kernel,
        grid=(m // TOKEN_TILE,),
        in_specs=[pl.BlockSpec((TOKEN_TILE, D_MODEL), lambda i: (i, 0)),
                  pl.BlockSpec((1, D_MODEL), lambda i: (0, 0))],
        out_specs=pl.BlockSpec((TOKEN_TILE, D_MODEL), lambda i: (i, 0)),
        out_shape=jax.ShapeDtypeStruct((m, D_MODEL), F32),
        compiler_params=_cparams("parallel"),
        name="final_norm",
    )(x, g.reshape(1, D_MODEL))


def _t5_tiles_kernel(tbl_ref, o_ref, *, tile):
    h = pl.program_id(0)
    ik = lax.broadcasted_iota(jnp.int32, (tile, tile), 0)
    iq = lax.broadcasted_iota(jnp.int32, (tile, tile), 1)
    for dd in range(2):
        o_ref[dd] = _bias_from_bucket(_t5_bucket(dd * tile + iq - ik), tbl_ref, h)


def t5_tiles(table, tile):
    assert 2 * tile - (tile - 1) >= T5_SATURATION_DIST
    return pl.pallas_call(
        functools.partial(_t5_tiles_kernel, tile=tile),
        grid=(N_HEADS,),
        in_specs=[_smem_spec()],
        out_specs=pl.BlockSpec((None, 2, tile, tile), lambda h: (h, 0, 0, 0)),
        out_shape=jax.ShapeDtypeStruct((N_HEADS, 2, tile, tile), F32),
        compiler_params=_cparams("arbitrary"),
        name="t5_tiles",
    )(table)


def _compress_kernel(x_ref, pe_ref, w1_ref, w2_ref, o_ref, *, rows, l_major=False):
    half = CMP_LEN // 2
    top = jnp.zeros((rows, NSA_KV_GROUPS * CMP_HIDDEN), F32)
    bot = jnp.zeros((rows, NSA_KV_GROUPS * CMP_HIDDEN), F32)
    for l in range(half):
        a = x_ref[l] if l_major else x_ref[:, l, :]
        top = top + _dot((a + pe_ref[l]).astype(BF16), w1_ref[l])
        bot = bot + _dot((a + pe_ref[half + l]).astype(BF16), w1_ref[half + l])
    hid = top + pltpu.roll(bot, rows - 1, 0)
    o_ref[...] = _dot(_gelu_tanh(hid).astype(BF16), w2_ref[...])


def compress_prompt(kvn, n_batch, seq, pe_t, w1_bd, w2_bd):
    rows = seq // CMP_STRIDE
    return pl.pallas_call(
        functools.partial(_compress_kernel, rows=rows),
        grid=(n_batch, 2),
        in_specs=[
            pl.BlockSpec((rows, CMP_STRIDE, GROUP_COLS), lambda b, j: (b, 0, j)),
            pl.BlockSpec((None, CMP_LEN, 1, GROUP_COLS), lambda b, j: (j, 0, 0, 0)),
            pl.BlockSpec((None, CMP_LEN, GROUP_COLS, NSA_KV_GROUPS * CMP_HIDDEN), lambda b, j: (j, 0, 0, 0)),
            pl.BlockSpec((None, NSA_KV_GROUPS * CMP_HIDDEN, GROUP_COLS), lambda b, j: (j, 0, 0)),
        ],
        out_specs=pl.BlockSpec((None, None, rows, GROUP_COLS), lambda b, j: (b, j, 0, 0)),
        out_shape=jax.ShapeDtypeStruct((n_batch, 2, rows, GROUP_COLS), F32),
        compiler_params=_cparams("parallel", "arbitrary"),
        name="nsa_compress",
    )(kvn.reshape(-1, CMP_STRIDE, kvn.shape[-1]), pe_t, w1_bd, w2_bd)


def compress_weights(phi_pe, phi_w1, phi_w2):
    eye = jnp.eye(NSA_KV_GROUPS, dtype=F32)
    w1 = phi_w1.reshape(2, CMP_LEN, HEAD_DIM, CMP_HIDDEN)
    w1_bd = jnp.einsum('gh,jldc->jlgdhc', eye, w1).reshape(2, CMP_LEN, GROUP_COLS, NSA_KV_GROUPS * CMP_HIDDEN)
    w2_bd = jnp.einsum('gh,jcd->jgchd', eye, phi_w2).reshape(2, NSA_KV_GROUPS * CMP_HIDDEN, GROUP_COLS)
    pe_t = jnp.tile(phi_pe, (1, 1, NSA_KV_GROUPS)).reshape(2, CMP_LEN, 1, GROUP_COLS)
    return pe_t, w1_bd.astype(BF16), w2_bd.astype(BF16)


def _fill_qz(qz_ref, qt_rows, row0, n_heads, tq):
    qz_ref[...] = jnp.zeros(qz_ref.shape, BF16)
    for h in range(n_heads):
        r = row0(h)
        qz_ref[r:r + HEAD_DIM, h * tq:(h + 1) * tq] = (qt_rows(h).astype(F32) * HEAD_DIM ** -0.5).astype(BF16)


def _online_softmax_step(s_t, mk, v_t, m_ref, l_ref, acc_ref, h):
    if mk is not None:
        s_t = jnp.where(mk, s_t, NEG)
    m_old = m_ref[h]
    m_new = jnp.maximum(m_old, jnp.max(s_t, axis=0, keepdims=True))
    alpha = jnp.exp(m_old - m_new)
    p = jnp.exp(s_t - m_new)
    l_ref[h] = alpha * l_ref[h] + jnp.sum(p, axis=0, keepdims=True)
    acc_ref[h] = alpha * acc_ref[h] + _dot(v_t, p.astype(BF16))
    m_ref[h] = m_new


def _reset_softmax(m_ref, l_ref, acc_ref):
    m_ref[...] = jnp.full(m_ref.shape, NEG, F32)
    l_ref[...] = jnp.zeros(l_ref.shape, F32)
    acc_ref[...] = jnp.zeros(acc_ref.shape, F32)


def _cmp_select_t_kernel(tbl_ref, qt_ref, kc_ref, vct_ref, gt_ref, o_ref, selt_ref, bias_ref, score_ref, qz_ref, *,
                         tq, n_cmp_pad, n_cmp, n_sel, pos0):
    i = pl.program_id(0)
    b = pl.program_id(1)
    q0 = pos0 + i * tq
    c_idx = lax.broadcasted_iota(jnp.int32, (n_cmp_pad, tq), 0)
    qpos = q0 + lax.broadcasted_iota(jnp.int32, (n_cmp_pad, tq), 1)
    dist = qpos - (c_idx * CMP_STRIDE + CMP_LEN - 1)
    valid = (dist >= 0) & (c_idx < n_cmp)

    @pl.when(b == 0)
    def _():
        bucket = _t5_bucket(dist)
        for h in range(N_HEADS):
            bias_ref[h] = _bias_from_bucket(bucket, tbl_ref, h)

    j_t = lax.broadcasted_iota(jnp.int32, (SEL_PAD, tq), 0)
    q_t = q0 + lax.broadcasted_iota(jnp.int32, (SEL_PAD, tq), 1)
    cur = jnp.right_shift(q_t, 6)
    forced = (j_t == 0) | (j_t == cur) | (j_t == cur - 1)
    avail = j_t <= cur
    j_i = lax.broadcasted_iota(jnp.int32, (SEL_PAD, n_cmp_pad), 0) * SEL_LEN
    c_i = lax.broadcasted_iota(jnp.int32, (SEL_PAD, n_cmp_pad), 1) * CMP_STRIDE
    inc_t = jnp.where((c_i < j_i + SEL_LEN) & (c_i + CMP_LEN > j_i) & (j_i < n_sel * SEL_LEN)
                      & (c_i < n_cmp * CMP_STRIDE), 1.0, 0.0).astype(BF16)
    j_s = lax.broadcasted_iota(jnp.int32, (n_sel, tq), 0)
    kc = kc_ref[...]

    for g in range(NSA_KV_GROUPS):
        _fill_qz(qz_ref, lambda h: qt_ref[(g * NSA_HPG + h) * HEAD_DIM:(g * NSA_HPG + h + 1) * HEAD_DIM, :],
                 lambda h: g * HEAD_DIM, NSA_HPG, tq)
        s_all = _dot(kc, qz_ref[...])
        psum = jnp.zeros((n_cmp_pad, tq), F32)
        outs = []
        for h in range(NSA_HPG):
            hh = g * NSA_HPG + h
            s = s_all[:, h * tq:(h + 1) * tq] + bias_ref[hh]
            s = jnp.where(valid, s, NEG)
            m = jnp.max(s, axis=0, keepdims=True)
            e = jnp.where(valid, jnp.exp(s - m), 0.0)
            p = e / jnp.maximum(jnp.sum(e, axis=0, keepdims=True), 1e-30)
            gate = _sigmoid(gt_ref[hh:hh + 1, :])
            outs.append(_dot(vct_ref[g * HEAD_DIM:(g + 1) * HEAD_DIM, :], p.astype(BF16)) * gate)
            psum = psum + p
        o_ref[:, g * GROUP_COLS:(g + 1) * GROUP_COLS] = jnp.concatenate(outs, axis=0).T
        p_hi, p_lo = _split_bf16(psum, 2)
        imp_t = _dot(inc_t, p_hi) + _dot(inc_t, p_lo)
        score_t = jnp.where(forced, BIG, jnp.where(avail, imp_t, -BIG))
        score_ref[...] = score_t
        sc = score_t[:n_sel]

        def rank_step(r, rank):
            row = jnp.broadcast_to(score_ref[pl.ds(r, 1), :], (n_sel, tq))
            ahead = jnp.where(row > sc, 1.0, jnp.where((row == sc) & (j_s > r), 1.0, 0.0))
            return rank + ahead

        rank = lax.fori_loop(0, n_sel, rank_step, jnp.zeros((n_sel, tq), F32))
        chosen = jnp.where(rank < float(N_SELECT), 1.0, 0.0)
        chosen = jnp.concatenate([chosen, jnp.zeros((SEL_PAD - n_sel, tq), F32)], axis=0)
        selt_ref[g * SEL_PAD:(g + 1) * SEL_PAD, :] = chosen.astype(BF16)


def nsa_cmp_select_t(table, q_t, k_cmp, v_cmp_t, gates_t, n_batch, seq, n_cmp, n_sel):
    tq = ATT_TILE
    nq = seq // tq
    m = n_batch * seq
    n_cmp_pad = k_cmp.shape[1]
    kern = functools.partial(_cmp_select_t_kernel, tq=tq, n_cmp_pad=n_cmp_pad, n_cmp=n_cmp, n_sel=n_sel, pos0=0)
    return pl.pallas_call(
        kern,
        grid=(nq, n_batch),
        in_specs=[
            _smem_spec(),
            pl.BlockSpec((D_MODEL, tq), lambda i, b: (0, b * nq + i)),
            pl.BlockSpec((None, n_cmp_pad, GROUP_COLS), lambda i, b: (b, 0, 0)),
            pl.BlockSpec((None, GROUP_COLS, n_cmp_pad), lambda i, b: (b, 0, 0)),
            pl.BlockSpec((3 * N_HEADS, tq), lambda i, b: (0, b * nq + i)),
        ],
        out_specs=[
            pl.BlockSpec((tq, D_MODEL), lambda i, b: (b * nq + i, 0)),
            pl.BlockSpec((NSA_KV_GROUPS * SEL_PAD, tq), lambda i, b: (0, b * nq + i)),
        ],
        out_shape=[
            jax.ShapeDtypeStruct((m, D_MODEL), F32),
            jax.ShapeDtypeStruct((NSA_KV_GROUPS * SEL_PAD, m), BF16),
        ],
        scratch_shapes=[
            pltpu.VMEM((N_HEADS, n_cmp_pad, tq), F32),
            pltpu.VMEM((SEL_PAD, tq), F32),
            pltpu.VMEM((GROUP_COLS, NSA_HPG * tq), BF16),
        ],
        compiler_params=_cparams("arbitrary", "arbitrary"),
        name="nsa_cmp_select",
    )(table, q_t, k_cmp, v_cmp_t, gates_t)


def _nsa_attn_t_kernel(tbl_ref, qt_ref, k_ref, vt_ref, gt_ref, toep_ref, *rest, mode, tile, branch):
    if mode == "sel":
        selt_ref, o_ref, qz_ref, m_ref, l_ref, acc_ref = rest
    else:
        o_ref, qz_ref, m_ref, l_ref, acc_ref = rest
    i = pl.program_id(1)
    ik = lax.broadcasted_iota(jnp.int32, (tile, tile), 0)
    iq = lax.broadcasted_iota(jnp.int32, (tile, tile), 1)
    causal = iq >= ik
    above = ik > iq

    for g in range(NSA_KV_GROUPS):
        _fill_qz(qz_ref, lambda h: qt_ref[(g * NSA_HPG + h) * HEAD_DIM:(g * NSA_HPG + h + 1) * HEAD_DIM, :],
                 lambda h: g * HEAD_DIM, NSA_HPG, tile)
        _reset_softmax(m_ref, l_ref, acc_ref)

        def attend(kt, dd, g=g):
            r0 = pl.multiple_of(kt * tile, tile)
            s_all = _dot(k_ref[pl.ds(r0, tile), :], qz_ref[...])
            v_t = vt_ref[g * HEAD_DIM:(g + 1) * HEAD_DIM, pl.ds(r0, tile)]
            if mode == "sel":
                krow = lax.broadcasted_iota(jnp.int32, (tile, SEL_PAD), 0)
                jcol = lax.broadcasted_iota(jnp.int32, (tile, SEL_PAD), 1)
                expand = jnp.where(jcol == jnp.right_shift(r0 + krow, 6), 1.0, 0.0).astype(BF16)
                mk = _dot(expand, selt_ref[g * SEL_PAD:(g + 1) * SEL_PAD, :]) > 0.5
                if dd == 0:
                    mk = mk & causal
            else:
                mk = causal if dd == 0 else (above if dd == 2 else None)
            for h in range(NSA_HPG):
                hh = g * NSA_HPG + h
                s = s_all[:, h * tile:(h + 1) * tile]
                if dd == 2:
                    s = s + tbl_ref[N_BUCKETS - 1, hh]
                else:
                    s = s + toep_ref[hh, dd]
                _online_softmax_step(s, mk, v_t, m_ref, l_ref, acc_ref, h)

        attend(i, 0)

        @pl.when(i >= 1)
        def _():
            attend(i - 1, 1)

        if mode == "sel":
            def far_body(kt, carry):
                attend(kt, 2)
                return carry
            lax.fori_loop(0, jnp.maximum(i - 1, 0), far_body, 0)
        else:
            @pl.when(i >= 2)
            def _():
                attend(i - 2, 2)

        outs = []
        for h in range(NSA_HPG):
            hh = g * NSA_HPG + h
            gate = _sigmoid(gt_ref[branch * N_HEADS + hh:branch * N_HEADS + hh + 1, :])
            outs.append(acc_ref[h] / jnp.maximum(l_ref[h], 1e-30) * gate)
        o_ref[:, g * GROUP_COLS:(g + 1) * GROUP_COLS] = jnp.concatenate(outs, axis=0).T


def nsa_attn_t(mode, table, toep, q_t, k_src, k_col, v_t, gates_t, sel_t, n_batch, seq):
    tile = ATT_TILE
    assert WINDOW == 2 * tile
    nq = seq // tile
    m = n_batch * seq
    branch = 1 if mode == "sel" else 2
    kern = functools.partial(_nsa_attn_t_kernel, mode=mode, tile=tile, branch=branch)
    in_specs = [
        _smem_spec(),
        pl.BlockSpec((D_MODEL, tile), lambda b, i: (0, b * nq + i)),
        pl.BlockSpec((seq, GROUP_COLS), lambda b, i: (b, k_col)),
        pl.BlockSpec((GROUP_COLS, seq), lambda b, i: (0, b)),
        pl.BlockSpec((3 * N_HEADS, tile), lambda b, i: (0, b * nq + i)),
        pl.BlockSpec((N_HEADS, 2, tile, tile), lambda b, i: (0, 0, 0, 0)),
    ]
    args = [table, q_t, k_src, v_t, gates_t, toep]
    if mode == "sel":
        in_specs.append(pl.BlockSpec((NSA_KV_GROUPS * SEL_PAD, tile), lambda b, i: (0, b * nq + i)))
        args.append(sel_t)
    return pl.pallas_call(
        kern,
        grid=(n_batch, nq),
        in_specs=in_specs,
        out_specs=pl.BlockSpec((tile, D_MODEL), lambda b, i: (b * nq + i, 0)),
        out_shape=jax.ShapeDtypeStruct((m, D_MODEL), F32),
        scratch_shapes=[
            pltpu.VMEM((GROUP_COLS, NSA_HPG * tile), BF16),
            pltpu.VMEM((NSA_HPG, 1, tile), F32),
            pltpu.VMEM((NSA_HPG, 1, tile), F32),
            pltpu.VMEM((NSA_HPG, HEAD_DIM, tile), F32),
        ],
        compiler_params=_cparams("parallel", "arbitrary"),
        name="nsa_attn_" + mode,
    )(*args)


def nsa_prompt_t(table, toep, q_t, kvn, kv_bf, win_bf, vslc_t, vwin_t, gates_t, cw, n_batch, seq):
    pe_t, w1_bd, w2_bd = cw
    n_cmp = (seq - CMP_LEN) // CMP_STRIDE + 1
    n_sel = -(-seq // SEL_LEN)
    assert n_sel <= SEL_PAD and n_sel % 8 == 0
    ctok = compress_prompt(kvn, n_batch, seq, pe_t, w1_bd, w2_bd).astype(BF16)
    k_cmp = ctok[:, 0]
    v_cmp_t = ctok[:, 1].transpose(0, 2, 1)
    o_cmp, sel_t = nsa_cmp_select_t(table, q_t, k_cmp, v_cmp_t, gates_t, n_batch, seq, n_cmp, n_sel)
    o_slc = nsa_attn_t("sel", table, toep, q_t, kv_bf, 2, vslc_t, gates_t, sel_t, n_batch, seq)
    o_win = nsa_attn_t("win", table, toep, q_t, win_bf, 0, vwin_t, gates_t, None, n_batch, seq)
    return o_cmp, o_slc, o_win


def _cmp_select_kernel(tbl_ref, q_ref, kt_ref, v_ref, gate_ref, o_ref, sel_ref, bias_ref, score_ref, *,
                       tq, n_cmp_pad, n_cmp, n_sel, pos0):
    i = pl.program_id(0)
    b = pl.program_id(1)
    q0 = pos0 + i * tq
    qpos = q0 + lax.broadcasted_iota(jnp.int32, (tq, n_cmp_pad), 0)
    c_idx = lax.broadcasted_iota(jnp.int32, (tq, n_cmp_pad), 1)
    dist = qpos - (c_idx * CMP_STRIDE + CMP_LEN - 1)
    valid = (dist >= 0) & (c_idx < n_cmp)

    @pl.when(b == 0)
    def _():
        bucket = _t5_bucket(dist)
        for h in range(N_HEADS):
            bias_ref[h] = _bias_from_bucket(bucket, tbl_ref, h)

    j_t = lax.broadcasted_iota(jnp.int32, (SEL_PAD, tq), 0)
    q_t = q0 + lax.broadcasted_iota(jnp.int32, (SEL_PAD, tq), 1)
    cur = jnp.right_shift(q_t, 6)
    forced = (j_t == 0) | (j_t == cur) | (j_t == cur - 1)
    avail = j_t <= cur
    j_i = lax.broadcasted_iota(jnp.int32, (SEL_PAD, n_cmp_pad), 0) * SEL_LEN
    c_i = lax.broadcasted_iota(jnp.int32, (SEL_PAD, n_cmp_pad), 1) * CMP_STRIDE
    inc_t = jnp.where((c_i < j_i + SEL_LEN) & (c_i + CMP_LEN > j_i) & (j_i < n_sel * SEL_LEN)
                      & (c_i < n_cmp * CMP_STRIDE), 1.0, 0.0).astype(BF16)
    j_s = lax.broadcasted_iota(jnp.int32, (n_sel, tq), 0)

    for g in range(NSA_KV_GROUPS):
        psum = jnp.zeros((tq, n_cmp_pad), F32)
        for h in range(NSA_HPG):
            hh = g * NSA_HPG + h
            qh = (q_ref[:, hh * HEAD_DIM:(hh + 1) * HEAD_DIM].astype(F32) * HEAD_DIM ** -0.5).astype(BF16)
            s = _dot(qh, kt_ref[g]) + bias_ref[hh]
            s = jnp.where(valid, s, NEG)
            m = jnp.max(s, axis=-1, keepdims=True)
            e = jnp.where(valid, jnp.exp(s - m), 0.0)
            p = e / jnp.maximum(jnp.sum(e, axis=-1, keepdims=True), 1e-30)
            gate = _sigmoid(gate_ref[:, hh:hh + 1])
            o_ref[:, hh * HEAD_DIM:(hh + 1) * HEAD_DIM] = _dot(p.astype(BF16), v_ref[g]) * gate
            psum = psum + p
        p_hi, p_lo = _split_bf16(psum, 2)
        imp_t = _dot_nt(inc_t, p_hi) + _dot_nt(inc_t, p_lo)
        score_t = jnp.where(forced, BIG, jnp.where(avail, imp_t, -BIG))
        score_ref[...] = score_t
        sc = score_t[:n_sel]

        def rank_step(r, rank):
            row = jnp.broadcast_to(score_ref[pl.ds(r, 1), :], (n_sel, tq))
            ahead = jnp.where(row > sc, 1.0, jnp.where((row == sc) & (j_s > r), 1.0, 0.0))
            return rank + ahead

        rank = lax.fori_loop(0, n_sel, rank_step, jnp.zeros((n_sel, tq), F32))
        chosen = jnp.where(rank < float(N_SELECT), 1.0, 0.0)
        chosen = jnp.concatenate([chosen, jnp.zeros((SEL_PAD - n_sel, tq), F32)], axis=0)
        sel_ref[:, g * SEL_PAD:(g + 1) * SEL_PAD] = chosen.T.astype(BF16)


def cmp_select_prompt(table, q_bf, kt_cmp, v_cmp, gates, n_batch, seq, n_cmp, n_sel):
    tq = CMP_Q_TILE
    nq = seq // tq
    m = n_batch * seq
    n_cmp_pad = kt_cmp.shape[-1]
    kern = functools.partial(_cmp_select_kernel, tq=tq, n_cmp_pad=n_cmp_pad, n_cmp=n_cmp, n_sel=n_sel, pos0=0)
    return pl.pallas_call(
        kern,
        grid=(nq, n_batch),
        in_specs=[
            _smem_spec(),
            pl.BlockSpec((tq, D_MODEL), lambda i, b: (b * nq + i, 0)),
            pl.BlockSpec((None, NSA_KV_GROUPS, HEAD_DIM, n_cmp_pad), lambda i, b: (b, 0, 0, 0)),
            pl.BlockSpec((None, NSA_KV_GROUPS, n_cmp_pad, HEAD_DIM), lambda i, b: (b, 0, 0, 0)),
            pl.BlockSpec((tq, 3 * N_HEADS), lambda i, b: (b * nq + i, 0)),
        ],
        out_specs=[
            pl.BlockSpec((tq, D_MODEL), lambda i, b: (b * nq + i, 0)),
            pl.BlockSpec((tq, NSA_KV_GROUPS * SEL_PAD), lambda i, b: (b * nq + i, 0)),
        ],
        out_shape=[
            jax.ShapeDtypeStruct((m, D_MODEL), F32),
            jax.ShapeDtypeStruct((m, NSA_KV_GROUPS * SEL_PAD), BF16),
        ],
        scratch_shapes=[
            pltpu.VMEM((N_HEADS, tq, n_cmp_pad), F32),
            pltpu.VMEM((SEL_PAD, tq), F32),
        ],
        compiler_params=_cparams("arbitrary", "arbitrary"),
        name="nsa_cmp_select",
    )(table, q_bf, kt_cmp, v_cmp, gates)


def _nsa_attn_kernel(tbl_ref, q_ref, k_ref, v_ref, gate_ref, toep_ref, *rest, mode, tile, branch):
    if mode == "sel":
        sel_ref, o_ref, qs_ref, m_ref, l_ref, acc_ref = rest
    else:
        o_ref, qs_ref, m_ref, l_ref, acc_ref = rest
    i = pl.program_id(1)
    iq = lax.broadcasted_iota(jnp.int32, (tile, tile), 0)
    ik = lax.broadcasted_iota(jnp.int32, (tile, tile), 1)
    causal = iq >= ik
    above = ik > iq

    for hh in range(N_HEADS):
        qs_ref[hh] = (q_ref[:, hh * HEAD_DIM:(hh + 1) * HEAD_DIM].astype(F32) * HEAD_DIM ** -0.5).astype(BF16)

    for g in range(NSA_KV_GROUPS):
        m_ref[...] = jnp.full(m_ref.shape, NEG, F32)
        l_ref[...] = jnp.zeros(l_ref.shape, F32)
        acc_ref[...] = jnp.zeros(acc_ref.shape, F32)

        def attend(kt, dd, g=g):
            r0 = pl.multiple_of(kt * tile, tile)
            kg = k_ref[pl.ds(r0, tile), :][:, g * HEAD_DIM:(g + 1) * HEAD_DIM]
            vg = v_ref[pl.ds(r0, tile), :][:, g * HEAD_DIM:(g + 1) * HEAD_DIM]
            if mode == "sel":
                jrow = lax.broadcasted_iota(jnp.int32, (SEL_PAD, tile), 0)
                kcol = lax.broadcasted_iota(jnp.int32, (SEL_PAD, tile), 1)
                expand = jnp.where(jrow == jnp.right_shift(r0 + kcol, 6), 1.0, 0.0).astype(BF16)
                mk = _dot(sel_ref[:, g * SEL_PAD:(g + 1) * SEL_PAD], expand) > 0.5
                if dd == 0:
                    mk = mk & causal
            else:
                mk = causal if dd == 0 else (above if dd == 2 else None)
            for h in range(NSA_HPG):
                hh = g * NSA_HPG + h
                s = _dot_nt(qs_ref[hh], kg)
                if dd == 2:
                    s = s + tbl_ref[N_BUCKETS - 1, hh]
                else:
                    s = s + toep_ref[hh, dd]
                if mk is not None:
                    s = jnp.where(mk, s, NEG)
                m_old = m_ref[h]
                m_new = jnp.maximum(m_old, jnp.max(s, axis=-1, keepdims=True))
                alpha = jnp.exp(m_old - m_new)
                p = jnp.exp(s - m_new)
                if mk is not None:
                    p = jnp.where(mk, p, 0.0)
                l_ref[h] = alpha * l_ref[h] + jnp.sum(p, axis=-1, keepdims=True)
                acc_ref[h] = alpha * acc_ref[h] + _dot(p.astype(BF16), vg)
                m_ref[h] = m_new

        if mode == "sel":
            def far_body(kt, carry):
                attend(kt, 2)
                return carry
            lax.fori_loop(0, jnp.maximum(i - 1, 0), far_body, 0)
        else:
            @pl.when(i >= 2)
            def _():
                attend(i - 2, 2)

        @pl.when(i >= 1)
        def _():
            attend(i - 1, 1)

        attend(i, 0)

        for h in range(NSA_HPG):
            hh = g * NSA_HPG + h
            gate = _sigmoid(gate_ref[:, branch * N_HEADS + hh:branch * N_HEADS + hh + 1])
            o_ref[:, hh * HEAD_DIM:(hh + 1) * HEAD_DIM] = acc_ref[h] / jnp.maximum(l_ref[h], 1e-30) * gate


def nsa_attn_prompt(mode, table, toep, q_bf, kv_src, k_col, v_col, gates, sel, n_batch, seq):
    tile = ATT_TILE
    assert WINDOW == 2 * tile
    nq = seq // tile
    m = n_batch * seq
    branch = 1 if mode == "sel" else 2
    kern = functools.partial(_nsa_attn_kernel, mode=mode, tile=tile, branch=branch)
    in_specs = [
        _smem_spec(),
        pl.BlockSpec((tile, D_MODEL), lambda b, i: (b * nq + i, 0)),
        pl.BlockSpec((seq, GROUP_COLS), lambda b, i: (b, k_col)),
        pl.BlockSpec((seq, GROUP_COLS), lambda b, i: (b, v_col)),
        pl.BlockSpec((tile, 3 * N_HEADS), lambda b, i: (b * nq + i, 0)),
        pl.BlockSpec((N_HEADS, 2, tile, tile), lambda b, i: (0, 0, 0, 0)),
    ]
    args = [table, q_bf, kv_src, kv_src, gates, toep]
    if mode == "sel":
        in_specs.append(pl.BlockSpec((tile, NSA_KV_GROUPS * SEL_PAD), lambda b, i: (b * nq + i, 0)))
        args.append(sel)
    return pl.pallas_call(
        kern,
        grid=(n_batch, nq),
        in_specs=in_specs,
        out_specs=pl.BlockSpec((tile, D_MODEL), lambda b, i: (b * nq + i, 0)),
        out_shape=jax.ShapeDtypeStruct((m, D_MODEL), F32),
        scratch_shapes=[
            pltpu.VMEM((N_HEADS, tile, HEAD_DIM), BF16),
            pltpu.VMEM((NSA_HPG, tile, 1), F32),
            pltpu.VMEM((NSA_HPG, tile, 1), F32),
            pltpu.VMEM((NSA_HPG, tile, HEAD_DIM), F32),
        ],
        compiler_params=_cparams("parallel", "arbitrary"),
        name="nsa_attn_" + mode,
    )(*args)


def nsa_prompt(table, toep, q_bf, kvn, kv_bf, win_bf, gates, cw, n_batch, seq):
    pe_t, w1_bd, w2_bd = cw
    n_cmp = (seq - CMP_LEN) // CMP_STRIDE + 1
    n_sel = -(-seq // SEL_LEN)
    assert n_sel <= SEL_PAD and n_sel % 8 == 0
    ctok = compress_prompt(kvn, n_batch, seq, pe_t, w1_bd, w2_bd)
    rows = ctok.shape[2]
    ctok = ctok.reshape(n_batch, 2, rows, NSA_KV_GROUPS, HEAD_DIM).astype(BF16)
    kt_cmp = ctok[:, 0].transpose(0, 2, 3, 1)
    v_cmp = ctok[:, 1].transpose(0, 2, 1, 3)
    o_cmp, sel = cmp_select_prompt(table, q_bf, kt_cmp, v_cmp, gates, n_batch, seq, n_cmp, n_sel)
    o_slc = nsa_attn_prompt("sel", table, toep, q_bf, kv_bf, 2, 3, gates, sel, n_batch, seq)
    o_win = nsa_attn_prompt("win", table, toep, q_bf, win_bf, 0, 1, gates, None, n_batch, seq)
    return o_cmp, o_slc, o_win


CUMSUM_CHUNK = 256
HEAD_BLOCK = 4


def _tril_bf16(n):
    r = lax.broadcasted_iota(jnp.int32, (n, n), 0)
    c = lax.broadcasted_iota(jnp.int32, (n, n), 1)
    return jnp.where(r >= c, 1.0, 0.0).astype(BF16)


def _cumsum_rows(x, tri):
    return sum(_dot(tri, part) for part in _split_bf16(x, 3))


def _fox_prep_kernel(f_ref, b_ref, lf_ref, d_ref, *, seq):
    tri = _tril_bf16(CUMSUM_CHUNK)
    carry = jnp.zeros((1, N_HEADS), F32)
    for c in range(seq // CUMSUM_CHUNK):
        rows = slice(c * CUMSUM_CHUNK, (c + 1) * CUMSUM_CHUNK)
        lf = _log_sigmoid(f_ref[rows, :] + b_ref[...])
        lf_ref[rows, :] = lf
        d = carry + _cumsum_rows(lf, tri)
        d_ref[rows, :] = d
        carry = d[CUMSUM_CHUNK - 1:CUMSUM_CHUNK, :]


def fox_prep_prompt(f_raw, b_f, n_batch, seq):
    m = n_batch * seq
    blk = pl.BlockSpec((seq, N_HEADS), lambda b: (b, 0))
    return pl.pallas_call(
        functools.partial(_fox_prep_kernel, seq=seq),
        grid=(n_batch,),
        in_specs=[blk, pl.BlockSpec((1, N_HEADS), lambda b: (0, 0))],
        out_specs=[blk, blk],
        out_shape=[jax.ShapeDtypeStruct((m, N_HEADS), F32)] * 2,
        compiler_params=_cparams("parallel"),
        name="fox_prep",
    )(f_raw, b_f.reshape(1, N_HEADS))


def _fox_attn_t_kernel(qt_ref, k_ref, vt_ref, drow_ref, dcol_ref, o_ref, qz_ref, m_ref, l_ref, acc_ref, *, tile):
    i = pl.program_id(2)
    ik = lax.broadcasted_iota(jnp.int32, (tile, tile), 0)
    iq = lax.broadcasted_iota(jnp.int32, (tile, tile), 1)
    causal = iq >= ik
    _fill_qz(qz_ref, lambda h: qt_ref[h * HEAD_DIM:(h + 1) * HEAD_DIM, :], lambda h: h * HEAD_DIM, HEAD_BLOCK, tile)
    _reset_softmax(m_ref, l_ref, acc_ref)

    def attend(kt, diag):
        r0 = pl.multiple_of(kt * tile, tile)
        s_all = _dot(k_ref[pl.ds(r0, tile), :], qz_ref[...])
        for h in range(HEAD_BLOCK):
            s = s_all[:, h * tile:(h + 1) * tile] + drow_ref[h:h + 1, :] - dcol_ref[pl.ds(r0, tile), h:h + 1]
            v_t = vt_ref[h * HEAD_DIM:(h + 1) * HEAD_DIM, pl.ds(r0, tile)]
            _online_softmax_step(s, causal if diag else None, v_t, m_ref, l_ref, acc_ref, h)

    def body(kt, carry):
        attend(kt, False)
        return carry

    attend(i, True)
    lax.fori_loop(0, i, body, 0)
    outs = [acc_ref[h] / jnp.maximum(l_ref[h], 1e-30) for h in range(HEAD_BLOCK)]
    o_ref[...] = jnp.concatenate(outs, axis=0).T


def fox_attn_t(q_t, k_bf, v_t, d_cum, n_batch, seq):
    tile = ATT_TILE
    nq = seq // tile
    m = n_batch * seq
    nhb = N_HEADS // HEAD_BLOCK
    d4 = d_cum[:m].reshape(n_batch, seq, nhb, HEAD_BLOCK)
    d_col = d4.transpose(0, 2, 1, 3)
    d_row = d4.transpose(0, 2, 3, 1)
    cols = HEAD_BLOCK * HEAD_DIM
    return pl.pallas_call(
        functools.partial(_fox_attn_t_kernel, tile=tile),
        grid=(n_batch, nhb, nq),
        in_specs=[
            pl.BlockSpec((cols, tile), lambda b, g, i: (g, b * nq + i)),
            pl.BlockSpec((seq, cols), lambda b, g, i: (b, g)),
            pl.BlockSpec((cols, seq), lambda b, g, i: (g, b)),
            pl.BlockSpec((None, None, HEAD_BLOCK, tile), lambda b, g, i: (b, g, 0, i)),
            pl.BlockSpec((None, None, seq, HEAD_BLOCK), lambda b, g, i: (b, g, 0, 0)),
        ],
        out_specs=pl.BlockSpec((tile, cols), lambda b, g, i: (b * nq + i, g)),
        out_shape=jax.ShapeDtypeStruct((m, D_MODEL), F32),
        scratch_shapes=[
            pltpu.VMEM((cols, HEAD_BLOCK * tile), BF16),
            pltpu.VMEM((HEAD_BLOCK, 1, tile), F32),
            pltpu.VMEM((HEAD_BLOCK, 1, tile), F32),
            pltpu.VMEM((HEAD_BLOCK, HEAD_DIM, tile), F32),
        ],
        compiler_params=_cparams("parallel", "parallel", "arbitrary"),
        name="fox_attn",
    )(q_t, k_bf, v_t, d_row, d_col)


def _fox_attn_kernel(q_ref, k_ref, v_ref, dq_ref, dk_ref, o_ref, qs_ref, m_ref, l_ref, acc_ref, *, tile):
    i = pl.program_id(2)
    iq = lax.broadcasted_iota(jnp.int32, (tile, tile), 0)
    ik = lax.broadcasted_iota(jnp.int32, (tile, tile), 1)
    causal = iq >= ik
    for h in range(HEAD_BLOCK):
        qs_ref[h] = (q_ref[:, h * HEAD_DIM:(h + 1) * HEAD_DIM].astype(F32) * HEAD_DIM ** -0.5).astype(BF16)
    m_ref[...] = jnp.full(m_ref.shape, NEG, F32)
    l_ref[...] = jnp.zeros(l_ref.shape, F32)
    acc_ref[...] = jnp.zeros(acc_ref.shape, F32)

    def attend(kt, diag):
        r0 = pl.multiple_of(kt * tile, tile)
        kb = k_ref[pl.ds(r0, tile), :]
        vb = v_ref[pl.ds(r0, tile), :]
        dk = dk_ref[kt]
        for h in range(HEAD_BLOCK):
            lanes = slice(h * HEAD_DIM, (h + 1) * HEAD_DIM)
            s = _dot_nt(qs_ref[h], kb[:, lanes]) + dq_ref[:, h:h + 1] - dk[h:h + 1, :]
            if diag:
                s = jnp.where(causal, s, NEG)
            m_old = m_ref[h]
            m_new = jnp.maximum(m_old, jnp.max(s, axis=-1, keepdims=True))
            alpha = jnp.exp(m_old - m_new)
            p = jnp.exp(s - m_new)
            if diag:
                p = jnp.where(causal, p, 0.0)
            l_ref[h] = alpha * l_ref[h] + jnp.sum(p, axis=-1, keepdims=True)
            acc_ref[h] = alpha * acc_ref[h] + _dot(p.astype(BF16), vb[:, lanes])
            m_ref[h] = m_new

    def body(kt, carry):
        attend(kt, False)
        return carry

    lax.fori_loop(0, i, body, 0)
    attend(i, True)
    for h in range(HEAD_BLOCK):
        o_ref[:, h * HEAD_DIM:(h + 1) * HEAD_DIM] = acc_ref[h] / jnp.maximum(l_ref[h], 1e-30)


def fox_attn_prompt(q_bf, kv_bf, d_cum, n_batch, seq):
    tile = ATT_TILE
    nq = seq // tile
    m = n_batch * seq
    nhb = N_HEADS // HEAD_BLOCK
    d4 = d_cum[:m].reshape(n_batch, seq, nhb, HEAD_BLOCK)
    dq = d4.transpose(0, 2, 1, 3)
    dk = d4.reshape(n_batch, nq, tile, nhb, HEAD_BLOCK).transpose(0, 3, 1, 4, 2)
    cols = HEAD_BLOCK * HEAD_DIM
    return pl.pallas_call(
        functools.partial(_fox_attn_kernel, tile=tile),
        grid=(n_batch, nhb, nq),
        in_specs=[
            pl.BlockSpec((tile, cols), lambda b, g, i: (b * nq + i, g)),
            pl.BlockSpec((seq, cols), lambda b, g, i: (b, g)),
            pl.BlockSpec((seq, cols), lambda b, g, i: (b, nhb + g)),
            pl.BlockSpec((None, None, tile, HEAD_BLOCK), lambda b, g, i: (b, g, i, 0)),
            pl.BlockSpec((None, None, nq, HEAD_BLOCK, tile), lambda b, g, i: (b, g, 0, 0, 0)),
        ],
        out_specs=pl.BlockSpec((tile, cols), lambda b, g, i: (b * nq + i, g)),
        out_shape=jax.ShapeDtypeStruct((m, D_MODEL), F32),
        scratch_shapes=[
            pltpu.VMEM((HEAD_BLOCK, tile, HEAD_DIM), BF16),
            pltpu.VMEM((HEAD_BLOCK, tile, 1), F32),
            pltpu.VMEM((HEAD_BLOCK, tile, 1), F32),
            pltpu.VMEM((HEAD_BLOCK, tile, HEAD_DIM), F32),
        ],
        compiler_params=_cparams("parallel", "parallel", "arbitrary"),
        name="fox_attn",
    )(q_bf, kv_bf, kv_bf, dq, dk)


HT_LANES = V7X_LANES


def _lane_head_onehot(n_heads, per_head):
    r = lax.broadcasted_iota(jnp.int32, (n_heads, n_heads * per_head), 0)
    c = lax.broadcasted_iota(jnp.int32, (n_heads, n_heads * per_head), 1)
    return jnp.where(_div_pow2(c, per_head) == r, 1.0, 0.0).astype(BF16)


def _expand_heads(x, onehot):
    return sum(_dot(part, onehot) for part in _split_bf16(x, 3))


def _diag_blocks_to_rows(acc_t, n_heads, feat, per_head):
    rows, lanes = acc_t.shape
    r = lax.broadcasted_iota(jnp.int32, (rows, lanes), 0)
    c = lax.broadcasted_iota(jnp.int32, (rows, lanes), 1)
    z = jnp.where(_div_pow2(r, feat) == _div_pow2(c, per_head), acc_t, 0.0)
    cc = lax.broadcasted_iota(jnp.int32, (lanes, lanes), 0)
    tt = lax.broadcasted_iota(jnp.int32, (lanes, lanes), 1)
    fold = jnp.where(_mod_pow2(cc, per_head) == tt, 1.0, 0.0).astype(BF16)
    out_t = sum(_dot(part, fold) for part in _split_bf16(z, 3))
    return out_t.T


def block_diag_queries(q_rows, n_seq, per_seq, n_heads):
    q4 = q_rows.reshape(n_seq, per_seq, n_heads, HEAD_DIM).transpose(0, 2, 3, 1)
    eye = jnp.eye(n_heads, dtype=q_rows.dtype)
    q_bd = q4[:, :, :, None, :] * eye[None, :, None, :, None]
    return q_bd.reshape(n_seq, n_heads * HEAD_DIM, n_heads * per_seq).astype(BF16)


def _fox_decode_kernel(pt_ref, qbd_ref, k_ref, v_ref, lf_ref, kn_ref, vn_ref, fn_ref, bf_ref, o_ref, lfn_ref,
                       qz_ref, m_ref, l_ref, acc_ref, carry_ref, *, n_pages, per_seq):
    p = pl.program_id(1)

    @pl.when(p == 0)
    def _():
        qz_ref[...] = (qbd_ref[...].astype(F32) * HEAD_DIM ** -0.5).astype(BF16)
        _reset_softmax(m_ref, l_ref, acc_ref)
        carry_ref[...] = jnp.zeros(carry_ref.shape, F32)

    tri = _tril_bf16(PAGE_SIZE)
    onehot = _lane_head_onehot(N_HEADS, per_seq)

    def attend(k_rows, v_rows, lf_rows, mk):
        d = carry_ref[...] + _cumsum_rows(lf_rows, tri)
        s_t = _dot(k_rows.astype(BF16), qz_ref[...]) - _expand_heads(d, onehot)
        if mk is not None:
            s_t = jnp.where(mk, s_t, NEG)
        m_old = m_ref[0]
        m_new = jnp.maximum(m_old, jnp.max(s_t, axis=0, keepdims=True))
        alpha = jnp.exp(m_old - m_new)
        pr = jnp.exp(s_t - m_new)
        if mk is not None:
            pr = jnp.where(mk, pr, 0.0)
        l_ref[0] = alpha * l_ref[0] + jnp.sum(pr, axis=0, keepdims=True)
        pv = lax.dot_general(v_rows.astype(BF16), pr.astype(BF16), (((0,), (0,)), ((), ())),
                             preferred_element_type=F32)
        acc_ref[0] = alpha * acc_ref[0] + pv
        m_ref[0] = m_new
        return d

    @pl.when(p < n_pages)
    def _():
        d = attend(k_ref[...], v_ref[...], lf_ref[...], None)
        carry_ref[...] = d[PAGE_SIZE - 1:PAGE_SIZE, :]

    @pl.when(p == n_pages)
    def _():
        lf_new = _log_sigmoid(fn_ref[...] + bf_ref[...])
        lfn_ref[...] = lf_new
        pad = PAGE_SIZE - per_seq
        k_rows = jnp.concatenate([kn_ref[...], jnp.zeros((pad, D_MODEL), F32)], axis=0)
        v_rows = jnp.concatenate([vn_ref[...], jnp.zeros((pad, D_MODEL), F32)], axis=0)
        lf_rows = jnp.concatenate([lf_new, jnp.zeros((pad, N_HEADS), F32)], axis=0)
        srow = lax.broadcasted_iota(jnp.int32, (PAGE_SIZE, HT_LANES), 0)
        tq = _mod_pow2(lax.broadcasted_iota(jnp.int32, (PAGE_SIZE, HT_LANES), 1), per_seq)
        attend(k_rows, v_rows, lf_rows, srow <= tq)
        o_t = acc_ref[0] / jnp.maximum(l_ref[0], 1e-30)
        o_ref[...] = _diag_blocks_to_rows(o_t, N_HEADS, HEAD_DIM, per_seq)[:per_seq, :]


def fox_decode(page_table, q_bd, kv_pool, lf_pool, kv_new, f_new, b_f, row_block0, n_seq, per_seq):
    n_pages = page_table.shape[1]
    assert N_HEADS * per_seq == HT_LANES
    kern = functools.partial(_fox_decode_kernel, n_pages=n_pages, per_seq=per_seq)
    page = lambda b, p, pt: pt[b, jnp.minimum(p, n_pages - 1)]
    grid_spec = pltpu.PrefetchScalarGridSpec(
        num_scalar_prefetch=1,
        grid=(n_seq, n_pages + 1),
        in_specs=[
            pl.BlockSpec((None, D_MODEL, HT_LANES), lambda b, p, pt: (b, 0, 0)),
            pl.BlockSpec((None, PAGE_SIZE, D_MODEL), lambda b, p, pt: (page(b, p, pt), 0, 0)),
            pl.BlockSpec((None, PAGE_SIZE, D_MODEL), lambda b, p, pt: (page(b, p, pt), 0, 1)),
            pl.BlockSpec((None, PAGE_SIZE, N_HEADS), lambda b, p, pt: (page(b, p, pt), 0, 0)),
            pl.BlockSpec((per_seq, D_MODEL), lambda b, p, pt: (row_block0 + b, 0)),
            pl.BlockSpec((per_seq, D_MODEL), lambda b, p, pt: (row_block0 + b, 1)),
            pl.BlockSpec((per_seq, N_HEADS), lambda b, p, pt: (row_block0 + b, 0)),
            pl.BlockSpec((1, N_HEADS), lambda b, p, pt: (0, 0)),
        ],
        out_specs=[
            pl.BlockSpec((per_seq, D_MODEL), lambda b, p, pt: (b, 0)),
            pl.BlockSpec((per_seq, N_HEADS), lambda b, p, pt: (b, 0)),
        ],
        scratch_shapes=[
            pltpu.VMEM((D_MODEL, HT_LANES), BF16),
            pltpu.VMEM((1, 1, HT_LANES), F32),
            pltpu.VMEM((1, 1, HT_LANES), F32),
            pltpu.VMEM((1, D_MODEL, HT_LANES), F32),
            pltpu.VMEM((1, N_HEADS), F32),
        ],
    )
    return pl.pallas_call(
        kern,
        grid_spec=grid_spec,
        out_shape=[jax.ShapeDtypeStruct((n_seq * per_seq, D_MODEL), F32),
                   jax.ShapeDtypeStruct((n_seq * per_seq, N_HEADS), F32)],
        compiler_params=_cparams("parallel", "arbitrary"),
        name="fox_decode",
    )(page_table, q_bd, kv_pool, kv_pool, lf_pool, kv_new, kv_new, f_new, b_f.reshape(1, N_HEADS))


PAGES_PER_STEP = 4


def feature_major_pages(pool):
    nd = pool.ndim
    return pool.transpose(0, 1, *range(3, nd), 2).reshape(pool.shape[0], pool.shape[1], -1, pool.shape[2])


def _decay_rows(lf_rows_exp, carry_row, tri):
    return carry_row + _cumsum_rows(lf_rows_exp, tri)


def _fox_decode_fm_kernel(pt_ref, qbd_ref, qbdt_ref, *refs, n_steps, per_seq, pps):
    k_refs = refs[:pps]
    v_refs = refs[pps:2 * pps]
    lf_refs = refs[2 * pps:3 * pps]
    kn_ref, vn_ref, fn_ref, bf_ref, o_ref, lfn_ref, qz_ref, qzt_ref, m_ref, l_ref, acc_ref, carry_ref = refs[3 * pps:]
    p = pl.program_id(1)

    @pl.when(p == 0)
    def _():
        qz_ref[...] = (qbd_ref[...].astype(F32) * HEAD_DIM ** -0.5).astype(BF16)
        qzt_ref[...] = (qbdt_ref[...].astype(F32) * HEAD_DIM ** -0.5).astype(BF16)
        _reset_softmax(m_ref, l_ref, acc_ref)
        carry_ref[...] = jnp.zeros(carry_ref.shape, F32)

    tri = _tril_bf16(PAGE_SIZE)
    onehot = _lane_head_onehot(N_HEADS, per_seq)

    def update(s_t, mk, pv_of):
        if mk is not None:
            s_t = jnp.where(mk, s_t, NEG)
        m_old = m_ref[0]
        m_new = jnp.maximum(m_old, jnp.max(s_t, axis=0, keepdims=True))
        alpha = jnp.exp(m_old - m_new)
        pr = jnp.exp(s_t - m_new)
        if mk is not None:
            pr = jnp.where(mk, pr, 0.0)
        l_ref[0] = alpha * l_ref[0] + jnp.sum(pr, axis=0, keepdims=True)
        acc_ref[0] = alpha * acc_ref[0] + pv_of(pr.astype(BF16))
        m_ref[0] = m_new

    @pl.when(p < n_steps)
    def _():
        for i in range(pps):
            lf_t = lf_refs[i][...]
            lf_exp = sum(lax.dot_general(part, onehot, (((0,), (0,)), ((), ())), preferred_element_type=F32)
                         for part in _split_bf16(lf_t, 3))
            d = _decay_rows(lf_exp, carry_ref[...], tri)
            carry_ref[...] = d[PAGE_SIZE - 1:PAGE_SIZE, :]
            s_t = _dot(qzt_ref[...], k_refs[i][...].astype(BF16)).T - d
            v_t = v_refs[i][...].astype(BF16)
            update(s_t, None, lambda pr, v_t=v_t: _dot(v_t, pr))

    @pl.when(p == n_steps)
    def _():
        lf_new = _log_sigmoid(fn_ref[...] + bf_ref[...])
        lfn_ref[...] = lf_new
        pad = PAGE_SIZE - per_seq
        k_rows = jnp.concatenate([kn_ref[...], jnp.zeros((pad, D_MODEL), F32)], axis=0).astype(BF16)
        v_rows = jnp.concatenate([vn_ref[...], jnp.zeros((pad, D_MODEL), F32)], axis=0).astype(BF16)
        lf_rows = jnp.concatenate([lf_new, jnp.zeros((pad, N_HEADS), F32)], axis=0)
        d = _decay_rows(_expand_heads(lf_rows, onehot), carry_ref[...], tri)
        srow = lax.broadcasted_iota(jnp.int32, (PAGE_SIZE, HT_LANES), 0)
        tq = _mod_pow2(lax.broadcasted_iota(jnp.int32, (PAGE_SIZE, HT_LANES), 1), per_seq)
        s_t = _dot(k_rows, qz_ref[...]) - d
        update(s_t, srow <= tq, lambda pr: lax.dot_general(v_rows, pr, (((0,), (0,)), ((), ())),
                                                           preferred_element_type=F32))
        o_t = acc_ref[0] / jnp.maximum(l_ref[0], 1e-30)
        o_ref[...] = _diag_blocks_to_rows(o_t, N_HEADS, HEAD_DIM, per_seq)[:per_seq, :]


def fox_decode_fm(page_table, q_bd, kv_pool_t, lf_pool_t, layer, kv_new, f_new, b_f, row_block0, n_seq, per_seq):
    n_pages = page_table.shape[1]
    pps = PAGES_PER_STEP
    assert N_HEADS * per_seq == HT_LANES and n_pages % pps == 0
    n_steps = n_pages // pps
    kern = functools.partial(_fox_decode_fm_kernel, n_steps=n_steps, per_seq=per_seq, pps=pps)

    def page(i):
        return lambda b, p, pt: pt[b, jnp.minimum(p, n_steps - 1) * pps + i]

    def page_spec(rows, blk, i):
        return pl.BlockSpec((None, None, rows, PAGE_SIZE), lambda b, p, pt, i=i: (layer, page(i)(b, p, pt), blk, 0))

    grid_spec = pltpu.PrefetchScalarGridSpec(
        num_scalar_prefetch=1,
        grid=(n_seq, n_steps + 1),
        in_specs=[pl.BlockSpec((None, D_MODEL, HT_LANES), lambda b, p, pt: (b, 0, 0)),
                  pl.BlockSpec((None, HT_LANES, D_MODEL), lambda b, p, pt: (b, 0, 0))]
        + [page_spec(D_MODEL, 0, i) for i in range(pps)]
        + [page_spec(D_MODEL, 1, i) for i in range(pps)]
        + [page_spec(N_HEADS, 0, i) for i in range(pps)]
        + [
            pl.BlockSpec((per_seq, D_MODEL), lambda b, p, pt: (row_block0 + b, 0)),
            pl.BlockSpec((per_seq, D_MODEL), lambda b, p, pt: (row_block0 + b, 1)),
            pl.BlockSpec((per_seq, N_HEADS), lambda b, p, pt: (row_block0 + b, 0)),
            pl.BlockSpec((1, N_HEADS), lambda b, p, pt: (0, 0)),
        ],
        out_specs=[
            pl.BlockSpec((per_seq, D_MODEL), lambda b, p, pt: (b, 0)),
            pl.BlockSpec((per_seq, N_HEADS), lambda b, p, pt: (b, 0)),
        ],
        scratch_shapes=[
            pltpu.VMEM((D_MODEL, HT_LANES), BF16),
            pltpu.VMEM((HT_LANES, D_MODEL), BF16),
            pltpu.VMEM((1, 1, HT_LANES), F32),
            pltpu.VMEM((1, 1, HT_LANES), F32),
            pltpu.VMEM((1, D_MODEL, HT_LANES), F32),
            pltpu.VMEM((1, HT_LANES), F32),
        ],
    )
    return pl.pallas_call(
        kern,
        grid_spec=grid_spec,
        out_shape=[jax.ShapeDtypeStruct((n_seq * per_seq, D_MODEL), F32),
                   jax.ShapeDtypeStruct((n_seq * per_seq, N_HEADS), F32)],
        compiler_params=_cparams("parallel", "arbitrary"),
        name="fox_decode",
    )(page_table, q_bd, q_bd.transpose(0, 2, 1), *([kv_pool_t] * (2 * pps)), *([lf_pool_t] * pps),
      kv_new, kv_new, f_new, b_f.reshape(1, N_HEADS))


DEC_ROW_TILE = 512


def _group_rows_to_heads(acc_t):
    return jnp.concatenate([acc_t[(h // NSA_HPG) * HEAD_DIM:(h // NSA_HPG + 1) * HEAD_DIM] for h in range(N_HEADS)],
                           axis=0)


def group_block_diag_queries(q_rows, n_seq, per_seq):
    q5 = q_rows.reshape(n_seq, per_seq, NSA_KV_GROUPS, NSA_HPG, HEAD_DIM).transpose(0, 2, 4, 3, 1)
    eye = jnp.eye(NSA_KV_GROUPS, dtype=q_rows.dtype)
    q_bd = q5[:, :, :, None, :, :] * eye[None, :, None, :, None, None]
    return q_bd.reshape(n_seq, GROUP_COLS, N_HEADS * per_seq).astype(BF16)


def _nsa_dec_bias_kernel(tbl_ref, o_ref, *, pos0, per_seq):
    onehot = _lane_head_onehot(N_HEADS, per_seq)
    tab = _expand_heads(tbl_ref[...], onehot)
    row = lax.broadcasted_iota(jnp.int32, (PAGE_SIZE, HT_LANES), 0)
    t = _mod_pow2(lax.broadcasted_iota(jnp.int32, (PAGE_SIZE, HT_LANES), 1), per_seq)
    dists = (PAGE_SIZE + t - row, t - row, pos0 + t - (row * CMP_STRIDE + CMP_LEN - 1))
    for n, dist in enumerate(dists):
        bucket = _t5_bucket(dist)
        val = jnp.broadcast_to(tab[0:1, :], (PAGE_SIZE, HT_LANES))
        for k in range(1, N_BUCKETS):
            val = jnp.where(bucket == k, tab[k:k + 1, :], val)
        o_ref[n] = val
    o_ref[3] = jnp.broadcast_to(tab[N_BUCKETS - 1:N_BUCKETS, :], (PAGE_SIZE, HT_LANES))


def nsa_dec_bias(table, pos0, per_seq):
    assert PAGE_SIZE >= T5_SATURATION_DIST
    return pl.pallas_call(
        functools.partial(_nsa_dec_bias_kernel, pos0=pos0, per_seq=per_seq),
        out_shape=jax.ShapeDtypeStruct((4, PAGE_SIZE, HT_LANES), F32),
        name="nsa_dec_bias",
    )(table)


def _dec_gather_kernel(pt_ref, x_ref, o_ref):
    for l in range(CMP_STRIDE):
        o_ref[l] = x_ref[:, l, :]


def nsa_dec_gather(page_table, pool):
    n_seq, n_pages = page_table.shape
    n_pool = pool.shape[0]
    rpp = PAGE_SIZE // CMP_STRIDE
    pool4 = pool.reshape(n_pool, rpp, CMP_STRIDE, pool.shape[-1])
    grid_spec = pltpu.PrefetchScalarGridSpec(
        num_scalar_prefetch=1,
        grid=(n_seq, n_pages),
        in_specs=[pl.BlockSpec((None, rpp, CMP_STRIDE, 2 * GROUP_COLS), lambda b, p, pt: (pt[b, p], 0, 0, 0))],
        out_specs=pl.BlockSpec((CMP_STRIDE, rpp, 2 * GROUP_COLS), lambda b, p, pt: (0, b * n_pages + p, 0)),
    )
    return pl.pallas_call(
        _dec_gather_kernel,
        grid_spec=grid_spec,
        out_shape=jax.ShapeDtypeStruct((CMP_STRIDE, n_seq * n_pages * rpp, 2 * GROUP_COLS), F32),
        compiler_params=_cparams("parallel", "arbitrary"),
        name="nsa_dec_gather",
    )(page_table, pool4)


def nsa_dec_compress(xd, pe_t, w1_bd, w2_bd):
    r_total = xd.shape[1]
    rows = min(DEC_ROW_TILE, r_total)
    return pl.pallas_call(
        functools.partial(_compress_kernel, rows=rows, l_major=True),
        grid=(r_total // rows, 2),
        in_specs=[
            pl.BlockSpec((CMP_STRIDE, rows, GROUP_COLS), lambda i, j: (0, i, j)),
            pl.BlockSpec((None, CMP_LEN, 1, GROUP_COLS), lambda i, j: (j, 0, 0, 0)),
            pl.BlockSpec((None, CMP_LEN, GROUP_COLS, NSA_KV_GROUPS * CMP_HIDDEN), lambda i, j: (j, 0, 0, 0)),
            pl.BlockSpec((None, NSA_KV_GROUPS * CMP_HIDDEN, GROUP_COLS), lambda i, j: (j, 0, 0)),
        ],
        out_specs=pl.BlockSpec((None, rows, GROUP_COLS), lambda i, j: (j, i, 0)),
        out_shape=jax.ShapeDtypeStruct((2, r_total, GROUP_COLS), F32),
        compiler_params=_cparams("parallel", "arbitrary"),
        name="nsa_dec_compress",
    )(xd, pe_t, w1_bd, w2_bd)


def _dec_softmax_step(s_t, mk, v_rows, m_ref, l_ref, acc_ref):
    if mk is not None:
        s_t = jnp.where(mk, s_t, NEG)
    m_old = m_ref[0]
    m_new = jnp.maximum(m_old, jnp.max(s_t, axis=0, keepdims=True))
    alpha = jnp.exp(m_old - m_new)
    pr = jnp.exp(s_t - m_new)
    if mk is not None:
        pr = jnp.where(mk, pr, 0.0)
    l_ref[0] = alpha * l_ref[0] + jnp.sum(pr, axis=0, keepdims=True)
    pv = lax.dot_general(v_rows, pr.astype(BF16), (((0,), (0,)), ((), ())), preferred_element_type=F32)
    acc_ref[0] = alpha * acc_ref[0] + pv
    m_ref[0] = m_new


def _dec_finish(m_ref, l_ref, acc_ref, gate_row, per_seq):
    o_t = acc_ref[0] / jnp.maximum(l_ref[0], 1e-30) * _sigmoid(gate_row)
    return _diag_blocks_to_rows(_group_rows_to_heads(o_t), N_HEADS, HEAD_DIM, per_seq)[:per_seq, :]


def _nsa_dec_cmp_kernel(qbd_ref, kc_ref, vc_ref, g_ref, bias_ref, o_ref, sel_ref, score_ref, *,
                        n_cmp, n_sel, pos0, per_seq):
    n_c = kc_ref.shape[0]
    c_idx = lax.broadcasted_iota(jnp.int32, (n_c, HT_LANES), 0)
    lane = lax.broadcasted_iota(jnp.int32, (n_c, HT_LANES), 1)
    t = _mod_pow2(lane, per_seq)
    dist = pos0 + t - (c_idx * CMP_STRIDE + CMP_LEN - 1)
    valid = (dist >= 0) & (c_idx < n_cmp)
    qz = (qbd_ref[...].astype(F32) * HEAD_DIM ** -0.5).astype(BF16)
    s = _dot(kc_ref[...], qz) + bias_ref[2]
    s = jnp.where(valid, s, NEG)
    m = jnp.max(s, axis=0, keepdims=True)
    e = jnp.where(valid, jnp.exp(s - m), 0.0)
    p = e / jnp.maximum(jnp.sum(e, axis=0, keepdims=True), 1e-30)
    o_t = lax.dot_general(vc_ref[...], p.astype(BF16), (((0,), (0,)), ((), ())), preferred_element_type=F32)
    o_t = o_t * _sigmoid(g_ref[0])
    o_ref[...] = _diag_blocks_to_rows(_group_rows_to_heads(o_t), N_HEADS, HEAD_DIM, per_seq)[:per_seq, :]
    src = lax.broadcasted_iota(jnp.int32, (HT_LANES, HT_LANES), 0)
    dst = lax.broadcasted_iota(jnp.int32, (HT_LANES, HT_LANES), 1)
    same_t = _mod_pow2(src, per_seq) == _mod_pow2(dst, per_seq)
    g_of_src = _div_pow2(src, per_seq * NSA_HPG)
    n_gt = NSA_KV_GROUPS * per_seq
    head_sum = jnp.where(same_t & (g_of_src == _div_pow2(dst, per_seq)) & (dst < n_gt), 1.0, 0.0).astype(BF16)
    psum = sum(_dot(part, head_sum) for part in _split_bf16(p, 2))
    j_i = lax.broadcasted_iota(jnp.int32, (SEL_PAD, n_c), 0) * SEL_LEN
    c_i = lax.broadcasted_iota(jnp.int32, (SEL_PAD, n_c), 1) * CMP_STRIDE
    inc_t = jnp.where((c_i < j_i + SEL_LEN) & (c_i + CMP_LEN > j_i) & (j_i < n_sel * SEL_LEN)
                      & (c_i < n_cmp * CMP_STRIDE), 1.0, 0.0).astype(BF16)
    imp_t = sum(_dot(inc_t, part) for part in _split_bf16(psum, 2))
    j_t = lax.broadcasted_iota(jnp.int32, (SEL_PAD, HT_LANES), 0)
    q_t = pos0 + _mod_pow2(lax.broadcasted_iota(jnp.int32, (SEL_PAD, HT_LANES), 1), per_seq)
    cur = jnp.right_shift(q_t, 6)
    forced = (j_t == 0) | (j_t == cur) | (j_t == cur - 1)
    avail = j_t <= cur
    score_t = jnp.where(forced, BIG, jnp.where(avail, imp_t, -BIG))
    score_ref[...] = score_t
    n_rank = -(-n_sel // 8) * 8
    sc = score_t[:n_rank]
    j_s = lax.broadcasted_iota(jnp.int32, (n_rank, HT_LANES), 0)

    def rank_step(r, rank):
        row = jnp.broadcast_to(score_ref[pl.ds(r, 1), :], (n_rank, HT_LANES))
        ahead = jnp.where(row > sc, 1.0, jnp.where((row == sc) & (j_s > r), 1.0, 0.0))
        return rank + ahead

    rank = lax.fori_loop(0, n_sel, rank_step, jnp.zeros((n_rank, HT_LANES), F32))
    chosen = jnp.where((rank < float(N_SELECT)) & (j_s < n_sel), 1.0, 0.0)
    chosen = jnp.concatenate([chosen, jnp.zeros((SEL_PAD - n_rank, HT_LANES), F32)], axis=0).astype(BF16)
    spread = jnp.where(same_t & (src == _div_pow2(dst, per_seq * NSA_HPG) * per_seq + _mod_pow2(dst, per_seq)),
                       1.0, 0.0).astype(BF16)
    sel_ref[...] = _dot(chosen, spread).astype(BF16)


def nsa_dec_cmp(q_bd, k_cmp, v_cmp, gate_rows, bias, n_cmp, n_sel, pos0, per_seq):
    n_seq, n_c = k_cmp.shape[:2]
    kern = functools.partial(_nsa_dec_cmp_kernel, n_cmp=n_cmp, n_sel=n_sel, pos0=pos0, per_seq=per_seq)
    return pl.pallas_call(
        kern,
        grid=(n_seq,),
        in_specs=[
            pl.BlockSpec((None, GROUP_COLS, HT_LANES), lambda b: (b, 0, 0)),
            pl.BlockSpec((None, n_c, GROUP_COLS), lambda b: (b, 0, 0)),
            pl.BlockSpec((None, n_c, GROUP_COLS), lambda b: (b, 0, 0)),
            pl.BlockSpec((None, 3, 1, HT_LANES), lambda b: (b, 0, 0, 0)),
            pl.BlockSpec((4, PAGE_SIZE, HT_LANES), lambda b: (0, 0, 0)),
        ],
        out_specs=[
            pl.BlockSpec((per_seq, D_MODEL), lambda b: (b, 0)),
            pl.BlockSpec((None, SEL_PAD, HT_LANES), lambda b: (b, 0, 0)),
        ],
        out_shape=[
            jax.ShapeDtypeStruct((n_seq * per_seq, D_MODEL), F32),
            jax.ShapeDtypeStruct((n_seq, SEL_PAD, HT_LANES), BF16),
        ],
        scratch_shapes=[pltpu.VMEM((SEL_PAD, HT_LANES), F32)],
        compiler_params=_cparams("parallel"),
        name="nsa_dec_cmp",
    )(q_bd, k_cmp, v_cmp, gate_rows, bias)


def _nsa_dec_sel_kernel(pt_ref, qbd_ref, k_ref, v_ref, kn_ref, vn_ref, sel_ref, g_ref, bias_ref, o_ref,
                        qz_ref, m_ref, l_ref, acc_ref, *, n_pages, per_seq):
    p = pl.program_id(1)

    @pl.when(p == 0)
    def _():
        qz_ref[...] = (qbd_ref[...].astype(F32) * HEAD_DIM ** -0.5).astype(BF16)
        _reset_softmax(m_ref, l_ref, acc_ref)

    srow = lax.broadcasted_iota(jnp.int32, (PAGE_SIZE, SEL_PAD), 0)
    jcol = lax.broadcasted_iota(jnp.int32, (PAGE_SIZE, SEL_PAD), 1)
    blocks_per_page = PAGE_SIZE // SEL_LEN

    def member(first_block):
        expand = jnp.where(jcol == first_block + jnp.right_shift(srow, 6), 1.0, 0.0).astype(BF16)
        return _dot(expand, sel_ref[...]) > 0.5

    @pl.when(p < n_pages)
    def _():
        bias = jnp.where(p == n_pages - 1, bias_ref[0], bias_ref[3])
        s_t = _dot(k_ref[...].astype(BF16), qz_ref[...]) + bias
        _dec_softmax_step(s_t, member(p * blocks_per_page), v_ref[...].astype(BF16), m_ref, l_ref, acc_ref)

    @pl.when(p == n_pages)
    def _():
        pad = PAGE_SIZE - per_seq
        k_rows = jnp.concatenate([kn_ref[...], jnp.zeros((pad, GROUP_COLS), F32)], axis=0).astype(BF16)
        v_rows = jnp.concatenate([vn_ref[...], jnp.zeros((pad, GROUP_COLS), F32)], axis=0).astype(BF16)
        krow = lax.broadcasted_iota(jnp.int32, (PAGE_SIZE, HT_LANES), 0)
        tq = _mod_pow2(lax.broadcasted_iota(jnp.int32, (PAGE_SIZE, HT_LANES), 1), per_seq)
        mk = member(n_pages * blocks_per_page) & (krow <= tq)
        s_t = _dot(k_rows, qz_ref[...]) + bias_ref[1]
        _dec_softmax_step(s_t, mk, v_rows, m_ref, l_ref, acc_ref)
        o_ref[...] = _dec_finish(m_ref, l_ref, acc_ref, g_ref[1], per_seq)


def nsa_dec_sel(page_table, q_bd, pool, kv_new, sel, gate_rows, bias, row_block0, per_seq):
    n_seq, n_pages = page_table.shape
    kern = functools.partial(_nsa_dec_sel_kernel, n_pages=n_pages, per_seq=per_seq)
    page = lambda b, p, pt: pt[b, jnp.minimum(p, n_pages - 1)]
    grid_spec = pltpu.PrefetchScalarGridSpec(
        num_scalar_prefetch=1,
        grid=(n_seq, n_pages + 1),
        in_specs=[
            pl.BlockSpec((None, GROUP_COLS, HT_LANES), lambda b, p, pt: (b, 0, 0)),
            pl.BlockSpec((None, PAGE_SIZE, GROUP_COLS), lambda b, p, pt: (page(b, p, pt), 0, 2)),
            pl.BlockSpec((None, PAGE_SIZE, GROUP_COLS), lambda b, p, pt: (page(b, p, pt), 0, 3)),
            pl.BlockSpec((per_seq, GROUP_COLS), lambda b, p, pt: (row_block0 + b, 2)),
            pl.BlockSpec((per_seq, GROUP_COLS), lambda b, p, pt: (row_block0 + b, 3)),
            pl.BlockSpec((None, SEL_PAD, HT_LANES), lambda b, p, pt: (b, 0, 0)),
            pl.BlockSpec((None, 3, 1, HT_LANES), lambda b, p, pt: (b, 0, 0, 0)),
            pl.BlockSpec((4, PAGE_SIZE, HT_LANES), lambda b, p, pt: (0, 0, 0)),
        ],
        out_specs=pl.BlockSpec((per_seq, D_MODEL), lambda b, p, pt: (b, 0)),
        scratch_shapes=[
            pltpu.VMEM((GROUP_COLS, HT_LANES), BF16),
            pltpu.VMEM((1, 1, HT_LANES), F32),
            pltpu.VMEM((1, 1, HT_LANES), F32),
            pltpu.VMEM((1, GROUP_COLS, HT_LANES), F32),
        ],
    )
    return pl.pallas_call(
        kern,
        grid_spec=grid_spec,
        out_shape=jax.ShapeDtypeStruct((n_seq * per_seq, D_MODEL), F32),
        compiler_params=_cparams("parallel", "arbitrary"),
        name="nsa_dec_sel",
    )(page_table, q_bd, pool, pool, kv_new, kv_new, sel, gate_rows, bias)


def _nsa_dec_win_kernel(qbd_ref, w_ref, wn_ref, g_ref, bias_ref, o_ref, ws_ref, m_ref, l_ref, acc_ref, *, per_seq):
    n_past = w_ref.shape[0]
    n_tiles = n_past // PAGE_SIZE
    qz = (qbd_ref[...].astype(F32) * HEAD_DIM ** -0.5).astype(BF16)
    _reset_softmax(m_ref, l_ref, acc_ref)
    krow = lax.broadcasted_iota(jnp.int32, (PAGE_SIZE, HT_LANES), 0)
    tq = _mod_pow2(lax.broadcasted_iota(jnp.int32, (PAGE_SIZE, HT_LANES), 1), per_seq)
    for w in range(n_tiles):
        rows = w_ref[w * PAGE_SIZE:(w + 1) * PAGE_SIZE, :]
        mk = (krow > tq) if w == 0 else None
        bias = bias_ref[0] if w == n_tiles - 1 else bias_ref[3]
        s_t = _dot(rows[:, :GROUP_COLS].astype(BF16), qz) + bias
        _dec_softmax_step(s_t, mk, rows[:, GROUP_COLS:].astype(BF16), m_ref, l_ref, acc_ref)
    pad = PAGE_SIZE - per_seq
    new = jnp.concatenate([wn_ref[...], jnp.zeros((pad, 2 * GROUP_COLS), F32)], axis=0)
    s_t = _dot(new[:, :GROUP_COLS].astype(BF16), qz) + bias_ref[1]
    _dec_softmax_step(s_t, krow <= tq, new[:, GROUP_COLS:].astype(BF16), m_ref, l_ref, acc_ref)
    o_ref[...] = _dec_finish(m_ref, l_ref, acc_ref, g_ref[2], per_seq)
    ws_ref[0:n_past - per_seq, :] = w_ref[per_seq:n_past, :]
    ws_ref[n_past - per_seq:n_past, :] = wn_ref[...]


def nsa_dec_win(q_bd, win_past, win_new, gate_rows, bias, row_block0, per_seq):
    n_seq, n_past = win_past.shape[:2]
    assert n_past == WINDOW and n_past % PAGE_SIZE == 0 and PAGE_SIZE + 1 >= T5_SATURATION_DIST
    kern = functools.partial(_nsa_dec_win_kernel, per_seq=per_seq)
    return pl.pallas_call(
        kern,
        grid=(n_seq,),
        in_specs=[
            pl.BlockSpec((None, GROUP_COLS, HT_LANES), lambda b: (b, 0, 0)),
            pl.BlockSpec((None, n_past, 2 * GROUP_COLS), lambda b: (b, 0, 0)),
            pl.BlockSpec((per_seq, 2 * GROUP_COLS), lambda b: (row_block0 + b, 0)),
            pl.BlockSpec((None, 3, 1, HT_LANES), lambda b: (b, 0, 0, 0)),
            pl.BlockSpec((4, PAGE_SIZE, HT_LANES), lambda b: (0, 0, 0)),
        ],
        out_specs=[
            pl.BlockSpec((per_seq, D_MODEL), lambda b: (b, 0)),
            pl.BlockSpec((None, n_past, 2 * GROUP_COLS), lambda b: (b, 0, 0)),
        ],
        out_shape=[
            jax.ShapeDtypeStruct((n_seq * per_seq, D_MODEL), F32),
            jax.ShapeDtypeStruct((n_seq, n_past, 2 * GROUP_COLS), F32),
        ],
        scratch_shapes=[
            pltpu.VMEM((1, 1, HT_LANES), F32),
            pltpu.VMEM((1, 1, HT_LANES), F32),
            pltpu.VMEM((1, GROUP_COLS, HT_LANES), F32),
        ],
        compiler_params=_cparams("parallel"),
        name="nsa_dec_win",
    )(q_bd, win_past, win_new, gate_rows, bias)


def nsa_decode(table, page_table, pool, win_past, q_rows, kv_new, win_new, gates_rows, cw, row_block0, pos0, per_seq):
    n_seq, n_pages = page_table.shape
    pe_t, w1_bd, w2_bd = cw
    past = n_pages * PAGE_SIZE
    s_total = past + per_seq
    n_cmp = (s_total - CMP_LEN) // CMP_STRIDE + 1
    n_sel = -(-s_total // SEL_LEN)
    assert n_cmp == past // CMP_STRIDE - 1 and n_sel == past // SEL_LEN + 1 and per_seq <= CMP_STRIDE
    assert pos0 == past and pos0 % SEL_LEN == 0 and pos0 >= WINDOW and past // CMP_STRIDE == PAGE_SIZE
    bias = nsa_dec_bias(table, pos0, per_seq)
    ctok = nsa_dec_compress(nsa_dec_gather(page_table, pool), pe_t, w1_bd, w2_bd).astype(BF16)
    rows_per_seq = past // CMP_STRIDE
    k_cmp = ctok[0].reshape(n_seq, rows_per_seq, GROUP_COLS)
    v_cmp = ctok[1].reshape(n_seq, rows_per_seq, GROUP_COLS)
    q_bd = group_block_diag_queries(q_rows, n_seq, per_seq)
    gate_rows = gates_rows.reshape(n_seq, per_seq, 3, N_HEADS).transpose(0, 2, 3, 1).reshape(n_seq, 3, 1, HT_LANES)
    o_cmp, sel = nsa_dec_cmp(q_bd, k_cmp, v_cmp, gate_rows, bias, n_cmp, n_sel, pos0, per_seq)
    o_slc = nsa_dec_sel(page_table, q_bd, pool, kv_new, sel, gate_rows, bias, row_block0, per_seq)
    o_win, win_state = nsa_dec_win(q_bd, win_past, win_new, gate_rows, bias, row_block0, per_seq)
    return o_cmp, o_slc, o_win, win_state


def _dec_gather_fm_kernel(pt_ref, *refs, pps):
    o_ref = refs[pps]
    rpp = PAGE_SIZE // CMP_STRIDE
    for i in range(pps):
        x3 = refs[i][...].T.reshape(rpp, CMP_STRIDE, 2 * GROUP_COLS)
        for l in range(CMP_STRIDE):
            o_ref[l, i * rpp:(i + 1) * rpp, :] = x3[:, l, :]


def nsa_dec_gather_fm(page_table, pool_t, layer):
    n_seq, n_pages = page_table.shape
    pps = PAGES_PER_STEP
    assert n_pages % pps == 0
    rpp = PAGE_SIZE // CMP_STRIDE
    grid_spec = pltpu.PrefetchScalarGridSpec(
        num_scalar_prefetch=1,
        grid=(n_seq, n_pages // pps),
        in_specs=[pl.BlockSpec((None, None, 2 * GROUP_COLS, PAGE_SIZE),
                               lambda b, p, pt, i=i: (layer, pt[b, p * pps + i], 0, 0)) for i in range(pps)],
        out_specs=pl.BlockSpec((CMP_STRIDE, rpp * pps, 2 * GROUP_COLS),
                               lambda b, p, pt: (0, b * (n_pages // pps) + p, 0)),
    )
    return pl.pallas_call(
        functools.partial(_dec_gather_fm_kernel, pps=pps),
        grid_spec=grid_spec,
        out_shape=jax.ShapeDtypeStruct((CMP_STRIDE, n_seq * n_pages * rpp, 2 * GROUP_COLS), F32),
        compiler_params=_cparams("parallel", "arbitrary"),
        name="nsa_dec_gather",
    )(page_table, *([pool_t] * pps))


def _dec_update(s_t, mk, pv_of, m_ref, l_ref, acc_ref):
    if mk is not None:
        s_t = jnp.where(mk, s_t, NEG)
    m_old = m_ref[0]
    m_new = jnp.maximum(m_old, jnp.max(s_t, axis=0, keepdims=True))
    alpha = jnp.exp(m_old - m_new)
    pr = jnp.exp(s_t - m_new)
    if mk is not None:
        pr = jnp.where(mk, pr, 0.0)
    l_ref[0] = alpha * l_ref[0] + jnp.sum(pr, axis=0, keepdims=True)
    acc_ref[0] = alpha * acc_ref[0] + pv_of(pr.astype(BF16))
    m_ref[0] = m_new


def _pv_key_major(v_rows):
    return lambda pr: lax.dot_general(v_rows, pr, (((0,), (0,)), ((), ())), preferred_element_type=F32)


def _nsa_dec_sel_fm_kernel(pt_ref, qbd_ref, qbdt_ref, *refs, n_steps, per_seq, pps):
    k_refs = refs[:pps]
    v_refs = refs[pps:2 * pps]
    kn_ref, vn_ref, sel_ref, g_ref, bias_ref, o_ref, qz_ref, qzt_ref, m_ref, l_ref, acc_ref = refs[2 * pps:]
    p = pl.program_id(1)

    @pl.when(p == 0)
    def _():
        qz_ref[...] = (qbd_ref[...].astype(F32) * HEAD_DIM ** -0.5).astype(BF16)
        qzt_ref[...] = (qbdt_ref[...].astype(F32) * HEAD_DIM ** -0.5).astype(BF16)
        _reset_softmax(m_ref, l_ref, acc_ref)

    srow = lax.broadcasted_iota(jnp.int32, (PAGE_SIZE, SEL_PAD), 0)
    jcol = lax.broadcasted_iota(jnp.int32, (PAGE_SIZE, SEL_PAD), 1)
    blocks_per_page = PAGE_SIZE // SEL_LEN

    def member(first_block):
        expand = jnp.where(jcol == first_block + jnp.right_shift(srow, 6), 1.0, 0.0).astype(BF16)
        return _dot(expand, sel_ref[...]) > 0.5

    @pl.when(p < n_steps)
    def _():
        for i in range(pps):
            page_no = p * pps + i
            bias = jnp.where(page_no == n_steps * pps - 1, bias_ref[0], bias_ref[3])
            s_t = _dot(qzt_ref[...], k_refs[i][...].astype(BF16)).T + bias
            v_t = v_refs[i][...].astype(BF16)
            _dec_update(s_t, member(page_no * blocks_per_page), lambda pr, v_t=v_t: _dot(v_t, pr),
                        m_ref, l_ref, acc_ref)

    @pl.when(p == n_steps)
    def _():
        pad = PAGE_SIZE - per_seq
        k_rows = jnp.concatenate([kn_ref[...], jnp.zeros((pad, GROUP_COLS), F32)], axis=0).astype(BF16)
        v_rows = jnp.concatenate([vn_ref[...], jnp.zeros((pad, GROUP_COLS), F32)], axis=0).astype(BF16)
        krow = lax.broadcasted_iota(jnp.int32, (PAGE_SIZE, HT_LANES), 0)
        tq = _mod_pow2(lax.broadcasted_iota(jnp.int32, (PAGE_SIZE, HT_LANES), 1), per_seq)
        mk = member(n_steps * pps * blocks_per_page) & (krow <= tq)
        s_t = _dot(k_rows, qz_ref[...]) + bias_ref[1]
        _dec_update(s_t, mk, _pv_key_major(v_rows), m_ref, l_ref, acc_ref)
        o_ref[...] = _dec_finish(m_ref, l_ref, acc_ref, g_ref[1], per_seq)


def nsa_dec_sel_fm(page_table, q_bd, pool_t, layer, kv_new, sel, gate_rows, bias, row_block0, per_seq):
    n_seq, n_pages = page_table.shape
    pps = PAGES_PER_STEP
    assert n_pages % pps == 0
    n_steps = n_pages // pps
    kern = functools.partial(_nsa_dec_sel_fm_kernel, n_steps=n_steps, per_seq=per_seq, pps=pps)

    def page_spec(blk, i):
        return pl.BlockSpec((None, None, GROUP_COLS, PAGE_SIZE),
                            lambda b, p, pt, i=i: (layer, pt[b, jnp.minimum(p, n_steps - 1) * pps + i], blk, 0))

    grid_spec = pltpu.PrefetchScalarGridSpec(
        num_scalar_prefetch=1,
        grid=(n_seq, n_steps + 1),
        in_specs=[pl.BlockSpec((None, GROUP_COLS, HT_LANES), lambda b, p, pt: (b, 0, 0)),
                  pl.BlockSpec((None, HT_LANES, GROUP_COLS), lambda b, p, pt: (b, 0, 0))]
        + [page_spec(2, i) for i in range(pps)] + [page_spec(3, i) for i in range(pps)]
        + [
            pl.BlockSpec((per_seq, GROUP_COLS), lambda b, p, pt: (row_block0 + b, 2)),
            pl.BlockSpec((per_seq, GROUP_COLS), lambda b, p, pt: (row_block0 + b, 3)),
            pl.BlockSpec((None, SEL_PAD, HT_LANES), lambda b, p, pt: (b, 0, 0)),
            pl.BlockSpec((None, 3, 1, HT_LANES), lambda b, p, pt: (b, 0, 0, 0)),
            pl.BlockSpec((4, PAGE_SIZE, HT_LANES), lambda b, p, pt: (0, 0, 0)),
        ],
        out_specs=pl.BlockSpec((per_seq, D_MODEL), lambda b, p, pt: (b, 0)),
        scratch_shapes=[
            pltpu.VMEM((GROUP_COLS, HT_LANES), BF16),
            pltpu.VMEM((HT_LANES, GROUP_COLS), BF16),
            pltpu.VMEM((1, 1, HT_LANES), F32),
            pltpu.VMEM((1, 1, HT_LANES), F32),
            pltpu.VMEM((1, GROUP_COLS, HT_LANES), F32),
        ],
    )
    return pl.pallas_call(
        kern,
        grid_spec=grid_spec,
        out_shape=jax.ShapeDtypeStruct((n_seq * per_seq, D_MODEL), F32),
        compiler_params=_cparams("parallel", "arbitrary"),
        name="nsa_dec_sel",
    )(page_table, q_bd, q_bd.transpose(0, 2, 1), *([pool_t] * (2 * pps)), kv_new, kv_new, sel, gate_rows, bias)


def _nsa_dec_win_fm_kernel(qbd_ref, qbdt_ref, wt_ref, wn_ref, g_ref, bias_ref, o_ref, ws_ref, m_ref, l_ref, acc_ref, *,
                           per_seq):
    n_past = wt_ref.shape[1]
    n_tiles = n_past // PAGE_SIZE
    qz = (qbd_ref[...].astype(F32) * HEAD_DIM ** -0.5).astype(BF16)
    qzt = (qbdt_ref[...].astype(F32) * HEAD_DIM ** -0.5).astype(BF16)
    _reset_softmax(m_ref, l_ref, acc_ref)
    krow = lax.broadcasted_iota(jnp.int32, (PAGE_SIZE, HT_LANES), 0)
    tq = _mod_pow2(lax.broadcasted_iota(jnp.int32, (PAGE_SIZE, HT_LANES), 1), per_seq)
    s_qm = _dot(qzt, wt_ref[:GROUP_COLS, :].astype(BF16))
    for w in range(n_tiles):
        keys = slice(w * PAGE_SIZE, (w + 1) * PAGE_SIZE)
        mk = (krow > tq) if w == 0 else None
        bias = bias_ref[0] if w == n_tiles - 1 else bias_ref[3]
        v_t = wt_ref[GROUP_COLS:, keys].astype(BF16)
        _dec_update(s_qm[:, keys].T + bias, mk, lambda pr, v_t=v_t: _dot(v_t, pr), m_ref, l_ref, acc_ref)
    pad = PAGE_SIZE - per_seq
    new = jnp.concatenate([wn_ref[...], jnp.zeros((pad, 2 * GROUP_COLS), F32)], axis=0)
    s_t = _dot(new[:, :GROUP_COLS].astype(BF16), qz) + bias_ref[1]
    _dec_update(s_t, krow <= tq, _pv_key_major(new[:, GROUP_COLS:].astype(BF16)), m_ref, l_ref, acc_ref)
    o_ref[...] = _dec_finish(m_ref, l_ref, acc_ref, g_ref[2], per_seq)
    shifted = pltpu.roll(wt_ref[...], n_past - per_seq, 1)
    ws_ref[...] = shifted
    lane = lax.broadcasted_iota(jnp.int32, (2 * GROUP_COLS, PAGE_SIZE), 1)
    new_t = pltpu.roll(new.T, PAGE_SIZE - per_seq, 1)
    ws_ref[:, n_past - PAGE_SIZE:] = jnp.where(lane >= PAGE_SIZE - per_seq, new_t, shifted[:, n_past - PAGE_SIZE:])


def nsa_dec_win_fm(q_bd, win_t, layer, win_new, gate_rows, bias, row_block0, per_seq):
    _, n_seq, n_feat, n_past = win_t.shape
    assert n_past == WINDOW and n_past % PAGE_SIZE == 0 and PAGE_SIZE + 1 >= T5_SATURATION_DIST
    kern = functools.partial(_nsa_dec_win_fm_kernel, per_seq=per_seq)
    return pl.pallas_call(
        kern,
        grid=(n_seq,),
        in_specs=[
            pl.BlockSpec((None, GROUP_COLS, HT_LANES), lambda b: (b, 0, 0)),
            pl.BlockSpec((None, HT_LANES, GROUP_COLS), lambda b: (b, 0, 0)),
            pl.BlockSpec((None, None, n_feat, n_past), lambda b: (layer, b, 0, 0)),
            pl.BlockSpec((per_seq, n_feat), lambda b: (row_block0 + b, 0)),
            pl.BlockSpec((None, 3, 1, HT_LANES), lambda b: (b, 0, 0, 0)),
            pl.BlockSpec((4, PAGE_SIZE, HT_LANES), lambda b: (0, 0, 0)),
        ],
        out_specs=[
            pl.BlockSpec((per_seq, D_MODEL), lambda b: (b, 0)),
            pl.BlockSpec((None, n_feat, n_past), lambda b: (b, 0, 0)),
        ],
        out_shape=[
            jax.ShapeDtypeStruct((n_seq * per_seq, D_MODEL), F32),
            jax.ShapeDtypeStruct((n_seq, n_feat, n_past), F32),
        ],
        scratch_shapes=[
            pltpu.VMEM((1, 1, HT_LANES), F32),
            pltpu.VMEM((1, 1, HT_LANES), F32),
            pltpu.VMEM((1, GROUP_COLS, HT_LANES), F32),
        ],
        compiler_params=_cparams("parallel"),
        name="nsa_dec_win",
    )(q_bd, q_bd.transpose(0, 2, 1), win_t, win_new, gate_rows, bias)


def nsa_decode_fm(table, page_table, pool_t, win_t, layer, q_rows, kv_new, win_new, gates_rows, cw, row_block0, pos0,
                  per_seq):
    n_seq, n_pages = page_table.shape
    pe_t, w1_bd, w2_bd = cw
    past = n_pages * PAGE_SIZE
    s_total = past + per_seq
    n_cmp = (s_total - CMP_LEN) // CMP_STRIDE + 1
    n_sel = -(-s_total // SEL_LEN)
    assert n_cmp == past // CMP_STRIDE - 1 and n_sel == past // SEL_LEN + 1 and per_seq <= CMP_STRIDE
    assert pos0 == past and pos0 % SEL_LEN == 0 and pos0 >= WINDOW and past // CMP_STRIDE == PAGE_SIZE
    bias = nsa_dec_bias(table, pos0, per_seq)
    ctok = nsa_dec_compress(nsa_dec_gather_fm(page_table, pool_t, layer), pe_t, w1_bd, w2_bd).astype(BF16)
    rows_per_seq = past // CMP_STRIDE
    k_cmp = ctok[0].reshape(n_seq, rows_per_seq, GROUP_COLS)
    v_cmp = ctok[1].reshape(n_seq, rows_per_seq, GROUP_COLS)
    q_bd = group_block_diag_queries(q_rows, n_seq, per_seq)
    gate_rows = gates_rows.reshape(n_seq, per_seq, 3, N_HEADS).transpose(0, 2, 3, 1).reshape(n_seq, 3, 1, HT_LANES)
    o_cmp, sel = nsa_dec_cmp(q_bd, k_cmp, v_cmp, gate_rows, bias, n_cmp, n_sel, pos0, per_seq)
    o_slc = nsa_dec_sel_fm(page_table, q_bd, pool_t, layer, kv_new, sel, gate_rows, bias, row_block0, per_seq)
    o_win, win_state_t = nsa_dec_win_fm(q_bd, win_t, layer, win_new, gate_rows, bias, row_block0, per_seq)
    return o_cmp, o_slc, o_win, win_state_t


GLA_SUB = 16
GLA_BLOCK = 512


def _gla_kernel(q_ref, k_ref, v_ref, r_ref, a_ref, wa_ref, ba_ref, ng_ref, s0_ref, o_ref, sfin_ref, s_ref, *,
                n_chunks, chunk):
    t_blk = pl.program_id(2)

    @pl.when(t_blk == 0)
    def _():
        s_ref[...] = s0_ref[...]

    tri = _tril_bf16(chunk)
    it = lax.broadcasted_iota(jnp.int32, (chunk, chunk), 0)
    js = lax.broadcasted_iota(jnp.int32, (chunk, chunk), 1)
    same_sub = jnp.right_shift(it, 4) == jnp.right_shift(js, 4)
    n_sub = chunk // GLA_SUB

    def chunk_step(c, carry):
        r0 = pl.multiple_of(c * chunk, chunk)
        rows = pl.ds(r0, chunk)
        q = q_ref[rows, :] * GLA_DK ** -0.5
        k = k_ref[rows, :]
        v = v_ref[rows, :].astype(BF16)
        g = _log_sigmoid(_dot(a_ref[rows, :].astype(BF16), wa_ref[...]) + ba_ref[...]) / GLA_TAU
        b = _cumsum_rows(g, tri)
        s_old = s_ref[...]
        o = _dot((q * jnp.exp(b)).astype(BF16), s_old.astype(BF16))
        a_mat = jnp.zeros((chunk, chunk), F32)
        for d in range(GLA_SUB):
            if d == 0:
                val = jnp.sum(q * k, axis=-1, keepdims=True)
            else:
                k_sh = pltpu.roll(k, d, 0)
                b_sh = pltpu.roll(b, d, 0)
                val = jnp.sum(q * k_sh * jnp.exp(jnp.minimum(b - b_sh, 0.0)), axis=-1, keepdims=True)
            a_mat = a_mat + jnp.where((it - js == d) & same_sub, val, 0.0)
        if n_sub > 1:
            blocks = [jnp.zeros((GLA_SUB, chunk), F32)]
            for a in range(1, n_sub):
                ref_b = b[a * GLA_SUB - 1:a * GLA_SUB, :]
                qa = q[a * GLA_SUB:(a + 1) * GLA_SUB] * jnp.exp(b[a * GLA_SUB:(a + 1) * GLA_SUB] - ref_b)
                kk = k * jnp.exp(jnp.minimum(ref_b - b, 0.0))
                blocks.append(_dot_nt(qa.astype(BF16), kk.astype(BF16)))
            off = jnp.concatenate(blocks, axis=0)
            a_mat = a_mat + jnp.where(jnp.right_shift(js, 4) < jnp.right_shift(it, 4), off, 0.0)
        o = o + _dot(a_mat.astype(BF16), v)
        b_last = b[chunk - 1:chunk, :]
        kd = k * jnp.exp(b_last - b)
        pad_k = jnp.zeros((GLA_DK - chunk, GLA_DK), F32)
        pad_v = jnp.zeros((GLA_DK - chunk, GLA_DV), BF16)
        kd_t = jnp.concatenate([kd, pad_k], axis=0).T
        dec = jnp.broadcast_to(jnp.exp(b_last), (GLA_DK, GLA_DK)).T
        s_ref[...] = (s_old * jnp.concatenate([dec] * (GLA_DV // GLA_DK), axis=1)
                      + _dot(kd_t.astype(BF16), jnp.concatenate([v, pad_v], axis=0)))
        y = o * lax.rsqrt(jnp.mean(o * o, axis=-1, keepdims=True) + EPS) * ng_ref[...]
        rr = r_ref[rows, :]
        o_ref[rows, :] = y * (rr * _sigmoid(rr))
        return carry

    lax.fori_loop(0, n_chunks, chunk_step, 0)

    @pl.when(t_blk == pl.num_programs(2) - 1)
    def _():
        sfin_ref[...] = s_ref[...]


def gla_prompt(q, k, v, r, a_lr, w_a2, b_a, norm_g, s0, n_batch, seq, row_block0=0):
    chunk = GLA_CHUNK if seq % GLA_CHUNK == 0 else seq
    blk = min(GLA_BLOCK, seq)
    nt = seq // blk
    m = n_batch * seq
    kern = functools.partial(_gla_kernel, n_chunks=blk // chunk, chunk=chunk)
    row = lambda b, h, t: row_block0 + b * nt + t
    return pl.pallas_call(
        kern,
        grid=(n_batch, GLA_HEADS, nt),
        in_specs=[
            pl.BlockSpec((blk, GLA_DK), lambda b, h, t: (row(b, h, t), h)),
            pl.BlockSpec((blk, GLA_DK), lambda b, h, t: (row(b, h, t), h)),
            pl.BlockSpec((blk, GLA_DV), lambda b, h, t: (row(b, h, t), h)),
            pl.BlockSpec((blk, GLA_DV), lambda b, h, t: (row(b, h, t), h)),
            pl.BlockSpec((blk, GLA_RANK), lambda b, h, t: (row(b, h, t), 0)),
            pl.BlockSpec((GLA_RANK, GLA_DK), lambda b, h, t: (0, h)),
            pl.BlockSpec((1, GLA_DK), lambda b, h, t: (0, h)),
            pl.BlockSpec((1, GLA_DV), lambda b, h, t: (0, 0)),
            pl.BlockSpec((None, None, GLA_DK, GLA_DV), lambda b, h, t: (b, h, 0, 0)),
        ],
        out_specs=[
            pl.BlockSpec((blk, GLA_DV), lambda b, h, t: (b * nt + t, h)),
            pl.BlockSpec((None, None, GLA_DK, GLA_DV), lambda b, h, t: (b, h, 0, 0)),
        ],
        out_shape=[
            jax.ShapeDtypeStruct((m, D_MODEL), F32),
            jax.ShapeDtypeStruct((n_batch, GLA_HEADS, GLA_DK, GLA_DV), F32),
        ],
        scratch_shapes=[pltpu.VMEM((GLA_DK, GLA_DV), F32)],
        compiler_params=_cparams("parallel", "parallel", "arbitrary"),
        name="gla",
    )(q, k, v, r, a_lr, w_a2.astype(BF16), b_a.reshape(1, GLA_QK), norm_g.reshape(1, GLA_DV), s0)


def _gla_decode_kernel(q_ref, k_ref, v_ref, r_ref, a_ref, wa_ref, ba_ref, ng_ref, s0_ref, o_ref, s_ref, *, n_rows):
    pad_rows = 2 * n_rows

    def pad_to(x, rows):
        return jnp.concatenate([x, jnp.zeros((rows - x.shape[0], x.shape[1]), x.dtype)], axis=0)

    row = lax.broadcasted_iota(jnp.int32, (n_rows, GLA_DK), 0)
    row_v = lax.broadcasted_iota(jnp.int32, (n_rows, GLA_DV), 0)
    q = q_ref[...] * GLA_DK ** -0.5
    k = k_ref[...]
    v = v_ref[...]
    g = _dot(pad_to(a_ref[...], pad_rows).astype(BF16), wa_ref[...])[:n_rows]
    g = _log_sigmoid(g + ba_ref[...]) / GLA_TAU
    b = g
    shift = 1
    while shift < n_rows:
        b = b + jnp.where(row >= shift, pltpu.roll(b, shift, 0), 0.0)
        shift *= 2
    s_old = s0_ref[...]
    o = _dot(pad_to(q * jnp.exp(b), pad_rows).astype(BF16), s_old.astype(BF16))[:n_rows]
    for d in range(n_rows):
        if d == 0:
            val = jnp.sum(q * k, axis=-1, keepdims=True)
            o = o + val * v
        else:
            k_sh = pltpu.roll(k, d, 0)
            b_sh = pltpu.roll(b, d, 0)
            val = jnp.sum(q * k_sh * jnp.exp(jnp.minimum(b - b_sh, 0.0)), axis=-1, keepdims=True)
            o = o + jnp.where(row_v >= d, val * pltpu.roll(v, d, 0), 0.0)
    b_last = b[n_rows - 1:n_rows, :]
    kd_t = pad_to(k * jnp.exp(b_last - b), GLA_DK).T
    dec = jnp.broadcast_to(jnp.exp(b_last), (GLA_DK, GLA_DK)).T
    s_ref[...] = (s_old * jnp.concatenate([dec] * (GLA_DV // GLA_DK), axis=1)
                  + _dot(kd_t.astype(BF16), pad_to(v, GLA_DK).astype(BF16)))
    y = o * lax.rsqrt(jnp.mean(o * o, axis=-1, keepdims=True) + EPS) * ng_ref[...]
    rr = r_ref[...]
    o_ref[...] = y * (rr * _sigmoid(rr))


def gla_decode(q, k, v, r, a_lr, w_a2, b_a, norm_g, s0, n_seq, per_seq, row_block0):
    assert per_seq % GLA_CHUNK != 0 and per_seq % 8 == 0
    kern = functools.partial(_gla_decode_kernel, n_rows=per_seq)
    row = lambda b, h: (row_block0 + b, h)
    return pl.pallas_call(
        kern,
        grid=(n_seq, GLA_HEADS),
        in_specs=[
            pl.BlockSpec((per_seq, GLA_DK), row),
            pl.BlockSpec((per_seq, GLA_DK), row),
            pl.BlockSpec((per_seq, GLA_DV), row),
            pl.BlockSpec((per_seq, GLA_DV), row),
            pl.BlockSpec((per_seq, GLA_RANK), lambda b, h: (row_block0 + b, 0)),
            pl.BlockSpec((GLA_RANK, GLA_DK), lambda b, h: (0, h)),
            pl.BlockSpec((1, GLA_DK), lambda b, h: (0, h)),
            pl.BlockSpec((1, GLA_DV), lambda b, h: (0, 0)),
            pl.BlockSpec((None, None, GLA_DK, GLA_DV), lambda b, h: (b, h, 0, 0)),
        ],
        out_specs=[
            pl.BlockSpec((per_seq, GLA_DV), lambda b, h: (b, h)),
            pl.BlockSpec((None, None, GLA_DK, GLA_DV), lambda b, h: (b, h, 0, 0)),
        ],
        out_shape=[
            jax.ShapeDtypeStruct((n_seq * per_seq, D_MODEL), F32),
            jax.ShapeDtypeStruct((n_seq, GLA_HEADS, GLA_DK, GLA_DV), F32),
        ],
        compiler_params=_cparams("parallel", "parallel"),
        name="gla_decode",
    )(q, k, v, r, a_lr, w_a2.astype(BF16), b_a.reshape(1, GLA_QK), norm_g.reshape(1, GLA_DV), s0)


def masked_softmax(logits, mask, axis):
    logits = jnp.where(mask, logits.astype(jnp.float32), NEG)
    m = jnp.max(logits, axis=axis, keepdims=True)
    e = jnp.exp(logits - m) * mask
    return e / jnp.maximum(jnp.sum(e, axis=axis, keepdims=True), 1e-30)


def t5_bucket(dist):
    n = jnp.maximum(dist, 0)
    max_exact = N_BUCKETS // 2
    nf = jnp.maximum(n, 1).astype(jnp.float32)
    log_part = max_exact + (jnp.log(nf / max_exact) / math.log(MAX_DISTANCE / max_exact)
                            * (N_BUCKETS - max_exact)).astype(jnp.int32)
    return jnp.where(n < max_exact, n, jnp.minimum(log_part, N_BUCKETS - 1))


def head_bias(dist, table):
    b = table.astype(jnp.float32)[t5_bucket(dist)]
    return b.reshape(*dist.shape, NSA_KV_GROUPS, NSA_HPG).transpose(2, 3, 0, 1)


def gather_pages(pool, page_table):
    rows = pool[page_table]
    return rows.reshape(page_table.shape[0], -1, *pool.shape[2:])


def query_blocks(fn, args, axes):
    T = args[0].shape[axes[0]]
    if T % Q_BLOCK != 0 or T <= Q_BLOCK:
        return fn(*args)
    nb = T // Q_BLOCK

    def split(a, ax):
        a = a.reshape(a.shape[:ax] + (nb, Q_BLOCK) + a.shape[ax + 1:])
        return jnp.moveaxis(a, ax, 0)

    out = lax.map(lambda blk: fn(*blk), tuple(split(a, ax) for a, ax in zip(args, axes)))
    out = jnp.moveaxis(out, 0, 1)
    return out.reshape(out.shape[:1] + (T,) + out.shape[3:])


def compress_tokens(kv, pe, w1, w2):
    B, S = kv.shape[:2]
    n_cmp = (S - CMP_LEN) // CMP_STRIDE + 1
    idx = jnp.arange(n_cmp)[:, None] * CMP_STRIDE + jnp.arange(CMP_LEN)[None, :]
    blocks = kv[:, idx] + pe[None, None, :, None, :]
    z = blocks.transpose(0, 1, 3, 2, 4).reshape(B, n_cmp, NSA_KV_GROUPS, CMP_LEN * HEAD_DIM)
    return jax.nn.gelu(z @ w1) @ w2


def gqa_window_attend(q, k, v, q_pos, k_pos, table):
    B, Tq = q.shape[:2]
    qg = q.reshape(B, Tq, NSA_KV_GROUPS, NSA_HPG, HEAD_DIM)
    logits = jnp.einsum('bqghd,bkgd->bghqk', qg, k).astype(jnp.float32) * HEAD_DIM ** -0.5
    dist = q_pos[:, None] - k_pos[None, :]
    mask = (dist >= 0) & (dist < WINDOW) & (k_pos[None, :] >= 0)
    p = masked_softmax(logits + head_bias(dist, table), mask, -1)
    o = jnp.einsum('bghqk,bkgd->bqghd', p, v)
    return o.reshape(B, Tq, N_HEADS, HEAD_DIM)


def nsa_core_sample(q, kv_new, win_new, gates, phi_pe, phi_w1, phi_w2, table, past_kv, past_win, pos0):
    B, T = q.shape[:2]
    G = NSA_KV_GROUPS
    gates = jax.nn.sigmoid(gates).reshape(B, T, 3, N_HEADS)
    kv = jnp.concatenate([past_kv, kv_new], axis=1)
    S = kv.shape[1]
    q_pos = pos0 + jnp.arange(T)
    qg = q.reshape(B, T, G, NSA_HPG, HEAD_DIM)
    k_cmp = compress_tokens(kv[:, :, 0], phi_pe[0], phi_w1[0], phi_w2[0])
    v_cmp = compress_tokens(kv[:, :, 1], phi_pe[1], phi_w1[1], phi_w2[1])
    n_cmp = k_cmp.shape[1]
    c_start = jnp.arange(n_cmp) * CMP_STRIDE
    dist_c = q_pos[:, None] - (c_start + CMP_LEN - 1)[None, :]
    logits_c = jnp.einsum('bqghd,bcgd->bghqc', qg, k_cmp).astype(jnp.float32) * HEAD_DIM ** -0.5
    p_cmp = masked_softmax(logits_c + head_bias(dist_c, table), dist_c >= 0, -1)
    o_cmp = jnp.einsum('bghqc,bcgd->bqghd', p_cmp, v_cmp).reshape(B, T, N_HEADS, HEAD_DIM)
    n_sel = -(-S // SEL_LEN)
    s_start = jnp.arange(n_sel) * SEL_LEN
    incidence = ((c_start[:, None] < s_start[None, :] + SEL_LEN)
                 & (c_start[:, None] + CMP_LEN > s_start[None, :])).astype(jnp.float32)
    imp = jnp.einsum('bghqc,cj->bgqj', p_cmp, incidence)
    cur = q_pos // SEL_LEN
    j = jnp.arange(n_sel)
    forced = (j[None, :] == 0) | (j[None, :] == cur[:, None]) | (j[None, :] == cur[:, None] - 1)
    avail = j[None, :] <= cur[:, None]
    score = jnp.where(forced, BIG, jnp.where(avail, imp, -BIG))
    _, sel = lax.top_k(score, min(N_SELECT, n_sel))
    member = jnp.sum(jax.nn.one_hot(sel, n_sel, dtype=jnp.float32), axis=-2) > 0.5
    k_pos = jnp.arange(S)
    key_member = jnp.take(member, k_pos // SEL_LEN, axis=-1)
    dist = q_pos[:, None] - k_pos[None, :]
    mask = key_member[:, :, None] & (dist >= 0)[None, None, None]
    logits = jnp.einsum('bqghd,bkgd->bghqk', qg, kv[:, :, 2]).astype(jnp.float32) * HEAD_DIM ** -0.5
    p = masked_softmax(logits + head_bias(dist, table), mask, -1)
    o_slc = jnp.einsum('bghqk,bkgd->bqghd', p, kv[:, :, 3]).reshape(B, T, N_HEADS, HEAD_DIM)
    win_all = jnp.concatenate([past_win, win_new], axis=1)
    lw = win_all.shape[1]
    kw_pos = pos0 + T - lw + jnp.arange(lw)
    o_win = gqa_window_attend(q, win_all[:, :, 0], win_all[:, :, 1], q_pos, kw_pos, table)
    win_state = win_all[:, win_all.shape[1] - min(WINDOW, win_all.shape[1]):]
    o = (gates[:, :, 0, :, None] * o_cmp + gates[:, :, 1, :, None] * o_slc + gates[:, :, 2, :, None] * o_win)
    return o.reshape(B * T, D_MODEL), win_state


def fox_attend(q, dq, q_pos, k, v, dk, k_pos):
    logits = jnp.einsum('bqhd,bkhd->bhqk', q, k).astype(jnp.float32) * HEAD_DIM ** -0.5
    logits = logits + dq.transpose(0, 2, 1)[..., None] - dk.transpose(0, 2, 1)[:, :, None, :]
    p = masked_softmax(logits, q_pos[:, None] >= k_pos[None, :], -1)
    return jnp.einsum('bhqk,bkhd->bqhd', p, v)


def fox_core(q, kv_new, f_raw, b_f, past_kv, past_logf, pos0):
    B, T = q.shape[:2]
    logf_new = jax.nn.log_sigmoid(f_raw + b_f.astype(jnp.float32))
    if past_kv is None:
        kv, logf = kv_new, logf_new
    else:
        kv = jnp.concatenate([past_kv, kv_new], axis=1)
        logf = jnp.concatenate([past_logf.astype(jnp.float32), logf_new], axis=1)
    S = kv.shape[1]
    cum = jnp.cumsum(logf, axis=1)
    k_pos = jnp.arange(S)
    q_pos = pos0 + jnp.arange(T)
    k, v = kv[:, :, 0], kv[:, :, 1]
    o = query_blocks(lambda q_, dq_, p_: fox_attend(q_, dq_, p_, k, v, cum, k_pos), [q, cum[:, S - T:], q_pos], [1, 1, 0])
    return o.reshape(B * T, D_MODEL), logf_new


def gla_chunk(S, q, k, v, g):
    C = q.shape[1]
    b = jnp.cumsum(g, axis=1)
    causal = jnp.tril(jnp.ones((C, C), dtype=bool))
    diff = b[:, :, None] - b[:, None, :]
    decay = jnp.exp(jnp.where(causal[None, :, :, None, None], diff, NEG))
    A = jnp.einsum('bthd,bshd,btshd->bhts', q, k, decay)
    o = jnp.einsum('bhts,bshv->bthv', A, v) + jnp.einsum('bthd,bhdv->bthv', q * jnp.exp(b), S)
    b_last = b[:, -1]
    S_new = jnp.exp(b_last)[..., None] * S + jnp.einsum('bshd,bshv->bhdv', k * jnp.exp(b_last[:, None] - b), v)
    return o, S_new


def gla_scan(S0, q, k, v, g):
    B, T = q.shape[:2]
    c = GLA_CHUNK if T % GLA_CHUNK == 0 else T
    nc = T // c

    def to_chunks(a):
        return jnp.moveaxis(a.reshape(B, nc, c, *a.shape[2:]), 1, 0)

    def step(S, blk):
        o, S = gla_chunk(S, *blk)
        return S, o

    S_fin, o = lax.scan(step, S0, tuple(to_chunks(a) for a in (q, k, v, g)))
    return jnp.moveaxis(o, 0, 1).reshape(B, T, GLA_HEADS, GLA_DV), S_fin


def gla_core(q, k, v, r, a_lr, w_a2, b_a, norm_g, S0):
    B, T = q.shape[:2]
    q = q.reshape(B, T, GLA_HEADS, GLA_DK) * GLA_DK ** -0.5
    k = k.reshape(B, T, GLA_HEADS, GLA_DK)
    v = v.reshape(B, T, GLA_HEADS, GLA_DV)
    g = jax.nn.log_sigmoid(a_lr @ w_a2.astype(jnp.float32) + b_a.astype(jnp.float32)) / GLA_TAU
    g = g.reshape(B, T, GLA_HEADS, GLA_DK)
    if S0 is None:
        S0 = jnp.zeros((B, GLA_HEADS, GLA_DK, GLA_DV), jnp.float32)
    o, S_fin = gla_scan(S0.astype(jnp.float32), q, k, v, g)
    xf = o
    y = xf * lax.rsqrt(jnp.mean(xf * xf, axis=-1, keepdims=True) + EPS) * norm_g
    o = y.reshape(B, T, D_MODEL) * jax.nn.silu(r)
    return o.reshape(B * T, D_MODEL), S_fin


_C1 = D_MODEL + NSA_KV_COLS
_C2 = _C1 + NSA_WIN_COLS
NSA_SPLITS = ((D_MODEL, _C1, F32), (D_MODEL, _C1, BF16), (_C1, _C2, F32), (_C1, _C2, BF16))
NSA_SPLITS_T = ((0, D_MODEL, BF16), (_C1 - GROUP_COLS, _C1, BF16), (_C2 - GROUP_COLS, _C2, BF16),
                (_C2, _C2 + 3 * N_HEADS, F32))
FOX_SPLITS = ((D_MODEL, 3 * D_MODEL, F32), (D_MODEL, 2 * D_MODEL, BF16), (3 * D_MODEL, 3 * D_MODEL + N_HEADS, F32))
FOX_SPLITS_T = ((0, D_MODEL, BF16), (2 * D_MODEL, 3 * D_MODEL, BF16))
GLA_SPLITS = ((0, GLA_QK, F32), (GLA_QK, 2 * GLA_QK, F32), (2 * GLA_QK, 2 * GLA_QK + D_MODEL, F32),
              (2 * GLA_QK + D_MODEL, 2 * GLA_QK + 2 * D_MODEL, F32),
              (2 * GLA_QK + 2 * D_MODEL, 2 * GLA_QK + 2 * D_MODEL + GLA_RANK, F32))


def kernel(x_prompt, x_sample, cache_nsa_kv, state_nsa_win, cache_fox_kv, cache_fox_logf, state_gla, page_table, rel_bias, norm_g, final_g, nsa_w_in, nsa_phi_pe, nsa_phi_w1, nsa_phi_w2, nsa_w_out, fox_w_in, fox_b_f, fox_w_out, gla_w_in, gla_w_a2, gla_b_a, gla_norm_g, gla_w_out, mlp_w1, mlp_w2):
    bp, tp, _ = x_prompt.shape
    bs, ts, _ = x_sample.shape
    mp, ms = bp * tp, bs * ts
    pos_s = page_table.shape[1] * PAGE_SIZE
    x = jnp.concatenate([x_prompt.reshape(mp, D_MODEL), x_sample.reshape(ms, D_MODEL)], axis=0)
    toep = t5_tiles(rel_bias, ATT_TILE)
    nsa_pool_t = feature_major_pages(cache_nsa_kv)
    nsa_win_t = feature_major_pages(state_nsa_win)
    fox_pool_t = feature_major_pages(cache_fox_kv)
    fox_lf_t = feature_major_pages(cache_fox_logf)

    def both(a, shape_p, shape_s):
        return a[:mp].reshape(shape_p), a[mp:].reshape(shape_s)

    nsa_kv_p, nsa_kv_s, nsa_win_p, nsa_win_s = [], [], [], []
    fox_kv_p, fox_kv_s, fox_lf_p, fox_lf_s, gla_p, gla_s = [], [], [], [], [], []
    G = NSA_KV_GROUPS
    for i in range(DEPTH):
        kind, li = i % N_MIXERS, i // N_MIXERS
        if kind == 0:
            kvn, kv_bf, winn, win_bf, q_t, vslc_t, vwin_t, gates_t = norm_proj(
                x, norm_g[i, 0], nsa_w_in[li], NSA_SPLITS, NSA_SPLITS_T)
            cw = compress_weights(nsa_phi_pe[li], nsa_phi_w1[li], nsa_phi_w2[li])
            o_parts_p = nsa_prompt_t(rel_bias, toep, q_t, kvn, kv_bf, win_bf, vslc_t, vwin_t, gates_t, cw, bp, tp)
            kv_p, kv_s = both(kvn, (bp, tp, 4, G, HEAD_DIM), (bs, ts, 4, G, HEAD_DIM))
            w_p, w_s = both(winn, (bp, tp, 2, G, HEAD_DIM), (bs, ts, 2, G, HEAD_DIM))
            *o_parts_s, ws_t = nsa_decode_fm(rel_bias, page_table, nsa_pool_t, nsa_win_t, li, q_t[:, mp:].T, kvn, winn,
                                             gates_t[:, mp:].T, cw, mp // ts, pos_s, ts)
            ws_p = w_p[:, tp - min(WINDOW, tp):]
            ws_s = ws_t.reshape(bs, 2, G, HEAD_DIM, -1).transpose(0, 4, 1, 2, 3)
            nsa_kv_p.append(kv_p); nsa_kv_s.append(kv_s); nsa_win_p.append(ws_p); nsa_win_s.append(ws_s)
            o_parts_p = list(o_parts_p)
            w_out = nsa_w_out[li]
        elif kind == 1:
            kvn, k_bf, fr, q_t, v_t = norm_proj(x, norm_g[i, 0], fox_w_in[li], FOX_SPLITS, FOX_SPLITS_T)
            kv_p, kv_s = both(kvn, (bp, tp, 2, N_HEADS, HEAD_DIM), (bs, ts, 2, N_HEADS, HEAD_DIM))
            lf_all, d_all = fox_prep_prompt(fr, fox_b_f[li], bp, tp)
            o_p = fox_attn_t(q_t, k_bf, v_t, d_all, bp, tp)
            lf_p = lf_all.reshape(bp, tp, N_HEADS)
            q_bd = block_diag_queries(q_t[:, mp:].T, bs, ts, N_HEADS)
            o_s, lf_s = fox_decode_fm(page_table, q_bd, fox_pool_t, fox_lf_t, li, kvn, fr, fox_b_f[li], mp // ts,
                                      bs, ts)
            lf_s = lf_s.reshape(bs, ts, N_HEADS)
            fox_kv_p.append(kv_p); fox_kv_s.append(kv_s); fox_lf_p.append(lf_p); fox_lf_s.append(lf_s)
            o_parts_p, o_parts_s = [o_p], [o_s]
            w_out = fox_w_out[li]
        else:
            q, k, v, r, a = norm_proj(x, norm_g[i, 0], gla_w_in[li], GLA_SPLITS)
            s_zero = jnp.zeros((bp, GLA_HEADS, GLA_DK, GLA_DV), F32)
            o_p, s_p = gla_prompt(q, k, v, r, a, gla_w_a2[li], gla_b_a[li], gla_norm_g[li], s_zero, bp, tp)
            o_s, s_s = gla_decode(q, k, v, r, a, gla_w_a2[li], gla_b_a[li], gla_norm_g[li], state_gla[li], bs, ts,
                                  mp // ts)
            gla_p.append(s_p); gla_s.append(s_s)
            o_parts_p, o_parts_s = [o_p], [o_s]
            w_out = gla_w_out[li]
        x = out_mlp(x, o_parts_p, o_parts_s, w_out.astype(BF16), norm_g[i, 1], mlp_w1[i].astype(BF16),
                    mlp_w2[i].astype(BF16))
    y = final_norm(x, final_g)
    y_p, y_s = both(y, (bp, tp, D_MODEL), (bs, ts, D_MODEL))
    return (y_p, y_s, jnp.stack(nsa_kv_p), jnp.stack(nsa_kv_s), jnp.stack(nsa_win_p), jnp.stack(nsa_win_s),
            jnp.stack(fox_kv_p), jnp.stack(fox_kv_s), jnp.stack(fox_lf_p), jnp.stack(fox_lf_s),
            jnp.stack(gla_p), jnp.stack(gla_s))
```

```python
import functools
import math

import jax
import jax.numpy as jnp
from jax import lax
from jax.experimental import pallas as pl
from jax.experimental.pallas import tpu as pltpu

D_MODEL = 1024
DEPTH = 4
PAGE_SIZE = 128
N_MIXERS = 3
N_HEADS = 16
HEAD_DIM = D_MODEL // N_HEADS
NSA_KV_GROUPS = 4
NSA_HPG = N_HEADS // NSA_KV_GROUPS
CMP_LEN = 32
CMP_STRIDE = 16
CMP_HIDDEN = 2 * HEAD_DIM
SEL_LEN = 64
N_SELECT = 16
WINDOW = 512
N_BUCKETS = 32
MAX_DISTANCE = 128
GLA_HEADS = 4
GLA_DK = D_MODEL // 2 // GLA_HEADS
GLA_DV = D_MODEL // GLA_HEADS
GLA_QK = GLA_HEADS * GLA_DK
GLA_RANK = 16
GLA_TAU = 16.0
GLA_CHUNK = 64
D_FF = 4 * D_MODEL
Q_BLOCK = 128
EPS = 1e-6
NEG = -1e30
BIG = 1e9
TAKEN = -3e38
NSA_KV_COLS = 4 * NSA_KV_GROUPS * HEAD_DIM
NSA_WIN_COLS = 2 * NSA_KV_GROUPS * HEAD_DIM
GROUP_COLS = NSA_KV_GROUPS * HEAD_DIM

F32 = jnp.float32
BF16 = jnp.bfloat16

V7X_VMEM_LIMIT_BYTES = 48 * 1024 * 1024
V7X_LANES = 128
TOKEN_TILE = 512
FF_TILE = 1024
ATT_TILE = 256
CMP_Q_TILE = 128
SEL_PAD = V7X_LANES
T5_SATURATION_DIST = 113


def _cparams(*sem):
    return pltpu.CompilerParams(dimension_semantics=sem, vmem_limit_bytes=V7X_VMEM_LIMIT_BYTES)


def _rms(x, g):
    return x * lax.rsqrt(jnp.mean(x * x, axis=-1, keepdims=True) + EPS) * g


def _sigmoid(x):
    return 1.0 / (1.0 + jnp.exp(-x))


def _log_sigmoid(x):
    return jnp.minimum(x, 0.0) - jnp.log(1.0 + jnp.exp(-jnp.abs(x)))


def _gelu_tanh(x):
    return 0.5 * x * (1.0 + jnp.tanh(math.sqrt(2.0 / math.pi) * (x + 0.044715 * (x * x * x))))


def _dot(a, b):
    return jnp.dot(a, b, preferred_element_type=F32)


def _dot_nt(a, b):
    return lax.dot_general(a, b, (((1,), (1,)), ((), ())), preferred_element_type=F32)


def _split_bf16(x, parts):
    out = []
    for _ in range(parts):
        hi = x.astype(BF16)
        out.append(hi)
        x = x - hi.astype(F32)
    return out


def _t5_bucket(dist):
    n = jnp.maximum(dist, 0)
    max_exact = N_BUCKETS // 2
    nf = jnp.maximum(n, 1).astype(F32)
    log_part = max_exact + (jnp.log(nf / max_exact) / math.log(MAX_DISTANCE / max_exact)
                            * (N_BUCKETS - max_exact)).astype(jnp.int32)
    return jnp.where(n < max_exact, n, jnp.minimum(log_part, N_BUCKETS - 1))


def _bias_from_bucket(bucket, tbl_ref, h):
    val = jnp.full(bucket.shape, tbl_ref[0, h], F32)
    for k in range(1, N_BUCKETS):
        val = jnp.where(bucket == k, tbl_ref[k, h], val)
    return val


def _div_pow2(x, d):
    assert d & (d - 1) == 0
    return jnp.right_shift(x, d.bit_length() - 1)


def _mod_pow2(x, d):
    assert d & (d - 1) == 0
    return jnp.bitwise_and(x, d - 1)


def _smem_spec():
    return pl.BlockSpec(memory_space=pltpu.SMEM)


def _norm_proj_kernel(x_ref, g_ref, w_ref, wt_ref, *out_refs, splits, splits_t):
    h = _rms(x_ref[...], g_ref[...]).astype(BF16)
    for (lo, hi, _), o_ref in zip(splits, out_refs):
        o_ref[...] = _dot(h, w_ref[:, lo:hi]).astype(o_ref.dtype)
    for (lo, hi, _), o_ref in zip(splits_t, out_refs[len(splits):]):
        o_ref[...] = _dot_nt(wt_ref[lo:hi, :], h).astype(o_ref.dtype)


def norm_proj(x, g, w, splits, splits_t=()):
    m = x.shape[0]
    n = w.shape[1]
    w_bf = w.astype(BF16)
    kern = functools.partial(_norm_proj_kernel, splits=splits, splits_t=splits_t)
    return pl.pallas_call(
        kern,
        grid=(m // TOKEN_TILE,),
        in_specs=[
            pl.BlockSpec((TOKEN_TILE, D_MODEL), lambda i: (i, 0)),
            pl.BlockSpec((1, D_MODEL), lambda i: (0, 0)),
            pl.BlockSpec((D_MODEL, n), lambda i: (0, 0)),
            pl.BlockSpec((n, D_MODEL), lambda i: (0, 0)),
        ],
        out_specs=[pl.BlockSpec((TOKEN_TILE, hi - lo), lambda i: (i, 0)) for lo, hi, _ in splits]
        + [pl.BlockSpec((hi - lo, TOKEN_TILE), lambda i: (0, i)) for lo, hi, _ in splits_t],
        out_shape=[jax.ShapeDtypeStruct((m, hi - lo), dt) for lo, hi, dt in splits]
        + [jax.ShapeDtypeStruct((hi - lo, m), dt) for lo, hi, dt in splits_t],
        compiler_params=_cparams("parallel"),
        name="norm_proj",
    )(x, g.reshape(1, D_MODEL), w_bf, w_bf.T)


def _out_mlp_kernel(x_ref, *refs, n_o, n_first):
    oa_refs = refs[:n_o]
    ob_refs = refs[n_o:2 * n_o]
    wo_ref, g_ref, w1_ref, w2_ref, y_ref, x1_ref, h_ref, acc_ref = refs[2 * n_o:]
    i = pl.program_id(0)
    j = pl.program_id(1)

    @pl.when(j == 0)
    def _():
        first = i < n_first
        o = jnp.where(first, oa_refs[0][...], ob_refs[0][...])
        for ra, rb in zip(oa_refs[1:], ob_refs[1:]):
            o = o + jnp.where(first, ra[...], rb[...])
        x1 = x_ref[...] + _dot(o.astype(BF16), wo_ref[...])
        x1_ref[...] = x1
        h_ref[...] = _rms(x1, g_ref[...]).astype(BF16)
        acc_ref[...] = jnp.zeros_like(acc_ref)

    a = _dot(h_ref[...], w1_ref[...])
    a = jnp.square(jnp.maximum(a, 0.0)).astype(BF16)
    acc_ref[...] += _dot(a, w2_ref[...])

    @pl.when(j == pl.num_programs(1) - 1)
    def _():
        y_ref[...] = x1_ref[...] + acc_ref[...]


def out_mlp(x, o_first, o_rest, w_out, g, w1, w2):
    m = x.shape[0]
    n_o = len(o_first)
    n_first = o_first[0].shape[0] // TOKEN_TILE
    n_rest = o_rest[0].shape[0] // TOKEN_TILE
    assert o_first[0].shape[0] % TOKEN_TILE == 0 and n_first + n_rest == m // TOKEN_TILE
    kern = functools.partial(_out_mlp_kernel, n_o=n_o, n_first=n_first)
    tok = pl.BlockSpec((TOKEN_TILE, D_MODEL), lambda i, j: (i, 0))
    tok_a = pl.BlockSpec((TOKEN_TILE, D_MODEL), lambda i, j: (jnp.minimum(i, n_first - 1), 0))
    tok_b = pl.BlockSpec((TOKEN_TILE, D_MODEL), lambda i, j: (jnp.maximum(i - n_first, 0), 0))
    return pl.pallas_call(
        kern,
        grid=(m // TOKEN_TILE, D_FF // FF_TILE),
        in_specs=[tok] + [tok_a] * n_o + [tok_b] * n_o + [
            pl.BlockSpec((D_MODEL, D_MODEL), lambda i, j: (0, 0)),
            pl.BlockSpec((1, D_MODEL), lambda i, j: (0, 0)),
            pl.BlockSpec((D_MODEL, FF_TILE), lambda i, j: (0, j)),
            pl.BlockSpec((FF_TILE, D_MODEL), lambda i, j: (j, 0)),
        ],
        out_specs=tok,
        out_shape=jax.ShapeDtypeStruct((m, D_MODEL), F32),
        scratch_shapes=[
            pltpu.VMEM((TOKEN_TILE, D_MODEL), F32),
            pltpu.VMEM((TOKEN_TILE, D_MODEL), BF16),
            pltpu.VMEM((TOKEN_TILE, D_MODEL), F32),
        ],
        compiler_params=_cparams("parallel", "arbitrary"),
        name="out_mlp",
    )(x, *o_first, *o_rest, w_out, g.reshape(1, D_MODEL), w1, w2)


def _final_norm_kernel(x_ref, g_ref, y_ref):
    y_ref[...] = _rms(x_ref[...], g_ref[...])


def final_norm(x, g):
    m = x.shape[0]
    return pl.pallas_call(
        _final_norm_kernel,
        grid=(m // TOKEN_TILE,),
        in_specs=[pl.BlockSpec((TOKEN_TILE, D_MODEL), lambda i: (i, 0)),
                  pl.BlockSpec((1, D_MODEL), lambda i: (0, 0))],
        out_specs=pl.BlockSpec((TOKEN_TILE, D_MODEL), lambda i: (i, 0)),
        out_shape=jax.ShapeDtypeStruct((m, D_MODEL), F32),
        compiler_params=_cparams("parallel"),
        name="final_norm",
    )(x, g.reshape(1, D_MODEL))


def _t5_tiles_kernel(tbl_ref, o_ref, *, tile):
    h = pl.program_id(0)
    ik = lax.broadcasted_iota(jnp.int32, (tile, tile), 0)
    iq = lax.broadcasted_iota(jnp.int32, (tile, tile), 1)
    for dd in range(2):
        o_ref[dd] = _bias_from_bucket(_t5_bucket(dd * tile + iq - ik), tbl_ref, h) - tbl_ref[N_BUCKETS - 1, h]


def t5_tiles(table, tile):
    assert 2 * tile - (tile - 1) >= T5_SATURATION_DIST
    return pl.pallas_call(
        functools.partial(_t5_tiles_kernel, tile=tile),
        grid=(N_HEADS,),
        in_specs=[_smem_spec()],
        out_specs=pl.BlockSpec((None, 2, tile, tile), lambda h: (h, 0, 0, 0)),
        out_shape=jax.ShapeDtypeStruct((N_HEADS, 2, tile, tile), F32),
        compiler_params=_cparams("arbitrary"),
        name="t5_tiles",
    )(table)


def _compress_kernel(x_ref, pe_ref, w1_ref, w2_ref, o_ref, *, rows, l_major=False):
    half = CMP_LEN // 2
    top = jnp.zeros((rows, NSA_KV_GROUPS * CMP_HIDDEN), F32)
    bot = jnp.zeros((rows, NSA_KV_GROUPS * CMP_HIDDEN), F32)
    for l in range(half):
        a = x_ref[l] if l_major else x_ref[:, l, :]
        top = top + _dot((a + pe_ref[l]).astype(BF16), w1_ref[l])
        bot = bot + _dot((a + pe_ref[half + l]).astype(BF16), w1_ref[half + l])
    hid = top + pltpu.roll(bot, rows - 1, 0)
    o_ref[...] = _dot(_gelu_tanh(hid).astype(BF16), w2_ref[...])


def compress_prompt(kvn, n_batch, seq, pe_t, w1_bd, w2_bd):
    rows = seq // CMP_STRIDE
    return pl.pallas_call(
        functools.partial(_compress_kernel, rows=rows),
        grid=(n_batch, 2),
        in_specs=[
            pl.BlockSpec((rows, CMP_STRIDE, GROUP_COLS), lambda b, j: (b, 0, j)),
            pl.BlockSpec((None, CMP_LEN, 1, GROUP_COLS), lambda b, j: (j, 0, 0, 0)),
            pl.BlockSpec((None, CMP_LEN, GROUP_COLS, NSA_KV_GROUPS * CMP_HIDDEN), lambda b, j: (j, 0, 0, 0)),
            pl.BlockSpec((None, NSA_KV_GROUPS * CMP_HIDDEN, GROUP_COLS), lambda b, j: (j, 0, 0)),
        ],
        out_specs=pl.BlockSpec((None, None, rows, GROUP_COLS), lambda b, j: (b, j, 0, 0)),
        out_shape=jax.ShapeDtypeStruct((n_batch, 2, rows, GROUP_COLS), F32),
        compiler_params=_cparams("parallel", "arbitrary"),
        name="nsa_compress",
    )(kvn.reshape(-1, CMP_STRIDE, kvn.shape[-1]), pe_t, w1_bd, w2_bd)


def compress_weights(phi_pe, phi_w1, phi_w2):
    eye = jnp.eye(NSA_KV_GROUPS, dtype=F32)
    w1 = phi_w1.reshape(2, CMP_LEN, HEAD_DIM, CMP_HIDDEN)
    w1_bd = jnp.einsum('gh,jldc->jlgdhc', eye, w1).reshape(2, CMP_LEN, GROUP_COLS, NSA_KV_GROUPS * CMP_HIDDEN)
    w2_bd = jnp.einsum('gh,jcd->jgchd', eye, phi_w2).reshape(2, NSA_KV_GROUPS * CMP_HIDDEN, GROUP_COLS)
    pe_t = jnp.tile(phi_pe, (1, 1, NSA_KV_GROUPS)).reshape(2, CMP_LEN, 1, GROUP_COLS)
    return pe_t, w1_bd.astype(BF16), w2_bd.astype(BF16)


def _fill_qz(qz_ref, qt_rows, row0, n_heads, tq):
    qz_ref[...] = jnp.zeros(qz_ref.shape, BF16)
    for h in range(n_heads):
        r = row0(h)
        qz_ref[r:r + HEAD_DIM, h * tq:(h + 1) * tq] = (qt_rows(h).astype(F32) * HEAD_DIM ** -0.5).astype(BF16)


def _online_softmax_step(s_t, mk, v_t, m_ref, l_ref, acc_ref, h):
    for c0 in range(0, s_t.shape[1], V7X_LANES):
        cols = slice(c0, c0 + V7X_LANES)
        s = s_t[:, cols]
        if mk is not None:
            s = jnp.where(mk[:, cols], s, NEG)
        m_old = m_ref[h, :, cols]
        m_new = jnp.maximum(m_old, jnp.max(s, axis=0, keepdims=True))
        alpha = jnp.exp(m_old - m_new)
        p = jnp.exp(s - m_new)
        l_ref[h, :, cols] = alpha * l_ref[h, :, cols] + jnp.sum(p, axis=0, keepdims=True)
        acc_ref[h, :, cols] = alpha * acc_ref[h, :, cols] + _dot(v_t, p.astype(BF16))
        m_ref[h, :, cols] = m_new


def _reset_softmax(m_ref, l_ref, acc_ref):
    m_ref[...] = jnp.full(m_ref.shape, NEG, F32)
    l_ref[...] = jnp.zeros(l_ref.shape, F32)
    acc_ref[...] = jnp.zeros(acc_ref.shape, F32)


def _cmp_select_t_kernel(tbl_ref, qt_ref, kc_ref, vct_ref, gt_ref, o_ref, selt_ref, bias_ref, qz_ref, *,
                         tq, n_cmp_pad, n_cmp, n_sel, pos0):
    i = pl.program_id(0)
    b = pl.program_id(1)
    q0 = pos0 + i * tq
    c_idx = lax.broadcasted_iota(jnp.int32, (n_cmp_pad, tq), 0)
    qpos = q0 + lax.broadcasted_iota(jnp.int32, (n_cmp_pad, tq), 1)
    dist = qpos - (c_idx * CMP_STRIDE + CMP_LEN - 1)
    valid = (dist >= 0) & (c_idx < n_cmp)

    @pl.when(b == 0)
    def _():
        bucket = _t5_bucket(dist)
        for h in range(N_HEADS):
            bias_ref[h] = _bias_from_bucket(bucket, tbl_ref, h)

    j_t = lax.broadcasted_iota(jnp.int32, (SEL_PAD, tq), 0)
    q_t = q0 + lax.broadcasted_iota(jnp.int32, (SEL_PAD, tq), 1)
    cur = jnp.right_shift(q_t, 6)
    forced = (j_t == 0) | (j_t == cur) | (j_t == cur - 1)
    avail = j_t <= cur
    j_i = lax.broadcasted_iota(jnp.int32, (SEL_PAD, n_cmp_pad), 0) * SEL_LEN
    c_i = lax.broadcasted_iota(jnp.int32, (SEL_PAD, n_cmp_pad), 1) * CMP_STRIDE
    inc_t = jnp.where((c_i < j_i + SEL_LEN) & (c_i + CMP_LEN > j_i) & (j_i < n_sel * SEL_LEN)
                      & (c_i < n_cmp * CMP_STRIDE), 1.0, 0.0).astype(BF16)
    j_s = lax.broadcasted_iota(jnp.int32, (n_sel, tq), 0)
    kc = kc_ref[...]

    for g in range(NSA_KV_GROUPS):
        _fill_qz(qz_ref, lambda h: qt_ref[(g * NSA_HPG + h) * HEAD_DIM:(g * NSA_HPG + h + 1) * HEAD_DIM, :],
                 lambda h: g * HEAD_DIM, NSA_HPG, tq)
        s_all = _dot(kc, qz_ref[...])
        psum = jnp.zeros((n_cmp_pad, tq), F32)
        outs = []
        for h in range(NSA_HPG):
            hh = g * NSA_HPG + h
            s = s_all[:, h * tq:(h + 1) * tq] + bias_ref[hh]
            s = jnp.where(valid, s, NEG)
            m = jnp.max(s, axis=0, keepdims=True)
            e = jnp.where(valid, jnp.exp(s - m), 0.0)
            p = e / jnp.maximum(jnp.sum(e, axis=0, keepdims=True), 1e-30)
            gate = _sigmoid(gt_ref[hh:hh + 1, :])
            outs.append(_dot(vct_ref[g * HEAD_DIM:(g + 1) * HEAD_DIM, :], p.astype(BF16)) * gate)
            psum = psum + p
        o_ref[:, g * GROUP_COLS:(g + 1) * GROUP_COLS] = jnp.concatenate(outs, axis=0).T
        p_hi, p_lo = _split_bf16(psum, 2)
        imp_t = _dot(inc_t, p_hi) + _dot(inc_t, p_lo)
        score_t = jnp.where(forced, BIG, jnp.where(avail, imp_t, -BIG))

        def pick_step(_, carry):
            remaining, chosen = carry
            top = jnp.max(remaining, axis=0, keepdims=True)
            first = jnp.min(jnp.where(remaining == top, j_s, n_sel), axis=0, keepdims=True)
            pick = j_s == first
            return jnp.where(pick, TAKEN, remaining), jnp.where(pick, 1.0, chosen)

        _, chosen = lax.fori_loop(0, N_SELECT, pick_step, (score_t[:n_sel], jnp.zeros((n_sel, tq), F32)))
        chosen = jnp.concatenate([chosen, jnp.zeros((SEL_PAD - n_sel, tq), F32)], axis=0)
        selt_ref[g * SEL_PAD:(g + 1) * SEL_PAD, :] = chosen.astype(BF16)


def nsa_cmp_select_t(table, q_t, k_cmp, v_cmp_t, gates_t, n_batch, seq, n_cmp, n_sel):
    tq = ATT_TILE
    nq = seq // tq
    m = n_batch * seq
    n_cmp_pad = k_cmp.shape[1]
    kern = functools.partial(_cmp_select_t_kernel, tq=tq, n_cmp_pad=n_cmp_pad, n_cmp=n_cmp, n_sel=n_sel, pos0=0)
    return pl.pallas_call(
        kern,
        grid=(nq, n_batch),
        in_specs=[
            _smem_spec(),
            pl.BlockSpec((D_MODEL, tq), lambda i, b: (0, b * nq + i)),
            pl.BlockSpec((None, n_cmp_pad, GROUP_COLS), lambda i, b: (b, 0, 0)),
            pl.BlockSpec((None, GROUP_COLS, n_cmp_pad), lambda i, b: (b, 0, 0)),
            pl.BlockSpec((3 * N_HEADS, tq), lambda i, b: (0, b * nq + i)),
        ],
        out_specs=[
            pl.BlockSpec((tq, D_MODEL), lambda i, b: (b * nq + i, 0)),
            pl.BlockSpec((NSA_KV_GROUPS * SEL_PAD, tq), lambda i, b: (0, b * nq + i)),
        ],
        out_shape=[
            jax.ShapeDtypeStruct((m, D_MODEL), F32),
            jax.ShapeDtypeStruct((NSA_KV_GROUPS * SEL_PAD, m), BF16),
        ],
        scratch_shapes=[
            pltpu.VMEM((N_HEADS, n_cmp_pad, tq), F32),
            pltpu.VMEM((GROUP_COLS, NSA_HPG * tq), BF16),
        ],
        compiler_params=_cparams("arbitrary", "arbitrary"),
        name="nsa_cmp_select",
    )(table, q_t, k_cmp, v_cmp_t, gates_t)


def _nsa_attn_t_kernel(qt_ref, k_ref, vt_ref, gt_ref, toep_ref, *rest, mode, tile, branch):
    if mode == "sel":
        selt_ref, o_ref, qz_ref, m_ref, l_ref, acc_ref = rest
    else:
        o_ref, qz_ref, m_ref, l_ref, acc_ref = rest
    i = pl.program_id(1)
    ik = lax.broadcasted_iota(jnp.int32, (tile, tile), 0)
    iq = lax.broadcasted_iota(jnp.int32, (tile, tile), 1)
    causal = iq >= ik
    above = ik > iq

    for g in range(NSA_KV_GROUPS):
        _fill_qz(qz_ref, lambda h: qt_ref[(g * NSA_HPG + h) * HEAD_DIM:(g * NSA_HPG + h + 1) * HEAD_DIM, :],
                 lambda h: g * HEAD_DIM, NSA_HPG, tile)
        _reset_softmax(m_ref, l_ref, acc_ref)

        def attend(kt, dd, g=g):
            r0 = pl.multiple_of(kt * tile, tile)
            s_all = _dot(k_ref[pl.ds(r0, tile), :], qz_ref[...])
            v_t = vt_ref[g * HEAD_DIM:(g + 1) * HEAD_DIM, pl.ds(r0, tile)]
            if mode == "sel":
                krow = lax.broadcasted_iota(jnp.int32, (tile, SEL_PAD), 0)
                jcol = lax.broadcasted_iota(jnp.int32, (tile, SEL_PAD), 1)
                expand = jnp.where(jcol == jnp.right_shift(r0 + krow, 6), 1.0, 0.0).astype(BF16)
                mk = _dot(expand, selt_ref[g * SEL_PAD:(g + 1) * SEL_PAD, :]) > 0.5
                if dd == 0:
                    mk = mk & causal
            else:
                mk = causal if dd == 0 else (above if dd == 2 else None)
            for h in range(NSA_HPG):
                hh = g * NSA_HPG + h
                s = s_all[:, h * tile:(h + 1) * tile]
                if dd < 2:
                    s = s + toep_ref[hh, dd]
                _online_softmax_step(s, mk, v_t, m_ref, l_ref, acc_ref, h)

        attend(i, 0)

        @pl.when(i >= 1)
        def _():
            attend(i - 1, 1)

        if mode == "sel":
            def far_body(kt, carry):
                attend(kt, 2)
                return carry
            lax.fori_loop(0, jnp.maximum(i - 1, 0), far_body, 0)
        else:
            @pl.when(i >= 2)
            def _():
                attend(i - 2, 2)

        outs = []
        for h in range(NSA_HPG):
            hh = g * NSA_HPG + h
            gate = _sigmoid(gt_ref[branch * N_HEADS + hh:branch * N_HEADS + hh + 1, :])
            outs.append(acc_ref[h] / jnp.maximum(l_ref[h], 1e-30) * gate)
        o_ref[:, g * GROUP_COLS:(g + 1) * GROUP_COLS] = jnp.concatenate(outs, axis=0).T


def nsa_attn_t(mode, toep, q_t, k_src, k_col, v_t, gates_t, sel_t, n_batch, seq):
    tile = ATT_TILE
    assert WINDOW == 2 * tile
    nq = seq // tile
    m = n_batch * seq
    branch = 1 if mode == "sel" else 2
    kern = functools.partial(_nsa_attn_t_kernel, mode=mode, tile=tile, branch=branch)
    in_specs = [
        pl.BlockSpec((D_MODEL, tile), lambda b, i: (0, b * nq + i)),
        pl.BlockSpec((seq, GROUP_COLS), lambda b, i: (b, k_col)),
        pl.BlockSpec((GROUP_COLS, seq), lambda b, i: (0, b)),
        pl.BlockSpec((3 * N_HEADS, tile), lambda b, i: (0, b * nq + i)),
        pl.BlockSpec((N_HEADS, 2, tile, tile), lambda b, i: (0, 0, 0, 0)),
    ]
    args = [q_t, k_src, v_t, gates_t, toep]
    if mode == "sel":
        in_specs.append(pl.BlockSpec((NSA_KV_GROUPS * SEL_PAD, tile), lambda b, i: (0, b * nq + i)))
        args.append(sel_t)
    return pl.pallas_call(
        kern,
        grid=(n_batch, nq),
        in_specs=in_specs,
        out_specs=pl.BlockSpec((tile, D_MODEL), lambda b, i: (b * nq + i, 0)),
        out_shape=jax.ShapeDtypeStruct((m, D_MODEL), F32),
        scratch_shapes=[
            pltpu.VMEM((GROUP_COLS, NSA_HPG * tile), BF16),
            pltpu.VMEM((NSA_HPG, 1, tile), F32),
            pltpu.VMEM((NSA_HPG, 1, tile), F32),
            pltpu.VMEM((NSA_HPG, HEAD_DIM, tile), F32),
        ],
        compiler_params=_cparams("parallel", "arbitrary"),
        name="nsa_attn_" + mode,
    )(*args)


def nsa_prompt_t(table, toep, q_t, kvn, kv_bf, win_bf, vslc_t, vwin_t, gates_t, cw, n_batch, seq):
    pe_t, w1_bd, w2_bd = cw
    n_cmp = (seq - CMP_LEN) // CMP_STRIDE + 1
    n_sel = -(-seq // SEL_LEN)
    assert n_sel <= SEL_PAD and n_sel % 8 == 0
    ctok = compress_prompt(kvn, n_batch, seq, pe_t, w1_bd, w2_bd).astype(BF16)
    k_cmp = ctok[:, 0]
    v_cmp_t = ctok[:, 1].transpose(0, 2, 1)
    o_cmp, sel_t = nsa_cmp_select_t(table, q_t, k_cmp, v_cmp_t, gates_t, n_batch, seq, n_cmp, n_sel)
    o_slc = nsa_attn_t("sel", toep, q_t, kv_bf, 2, vslc_t, gates_t, sel_t, n_batch, seq)
    o_win = nsa_attn_t("win", toep, q_t, win_bf, 0, vwin_t, gates_t, None, n_batch, seq)
    return o_cmp, o_slc, o_win


def _cmp_select_kernel(tbl_ref, q_ref, kt_ref, v_ref, gate_ref, o_ref, sel_ref, bias_ref, score_ref, *,
                       tq, n_cmp_pad, n_cmp, n_sel, pos0):
    i = pl.program_id(0)
    b = pl.program_id(1)
    q0 = pos0 + i * tq
    qpos = q0 + lax.broadcasted_iota(jnp.int32, (tq, n_cmp_pad), 0)
    c_idx = lax.broadcasted_iota(jnp.int32, (tq, n_cmp_pad), 1)
    dist = qpos - (c_idx * CMP_STRIDE + CMP_LEN - 1)
    valid = (dist >= 0) & (c_idx < n_cmp)

    @pl.when(b == 0)
    def _():
        bucket = _t5_bucket(dist)
        for h in range(N_HEADS):
            bias_ref[h] = _bias_from_bucket(bucket, tbl_ref, h)

    j_t = lax.broadcasted_iota(jnp.int32, (SEL_PAD, tq), 0)
    q_t = q0 + lax.broadcasted_iota(jnp.int32, (SEL_PAD, tq), 1)
    cur = jnp.right_shift(q_t, 6)
    forced = (j_t == 0) | (j_t == cur) | (j_t == cur - 1)
    avail = j_t <= cur
    j_i = lax.broadcasted_iota(jnp.int32, (SEL_PAD, n_cmp_pad), 0) * SEL_LEN
    c_i = lax.broadcasted_iota(jnp.int32, (SEL_PAD, n_cmp_pad), 1) * CMP_STRIDE
    inc_t = jnp.where((c_i < j_i + SEL_LEN) & (c_i + CMP_LEN > j_i) & (j_i < n_sel * SEL_LEN)
                      & (c_i < n_cmp * CMP_STRIDE), 1.0, 0.0).astype(BF16)
    j_s = lax.broadcasted_iota(jnp.int32, (n_sel, tq), 0)

    for g in range(NSA_KV_GROUPS):
        psum = jnp.zeros((tq, n_cmp_pad), F32)
        for h in range(NSA_HPG):
            hh = g * NSA_HPG + h
            qh = (q_ref[:, hh * HEAD_DIM:(hh + 1) * HEAD_DIM].astype(F32) * HEAD_DIM ** -0.5).astype(BF16)
            s = _dot(qh, kt_ref[g]) + bias_ref[hh]
            s = jnp.where(valid, s, NEG)
            m = jnp.max(s, axis=-1, keepdims=True)
            e = jnp.where(valid, jnp.exp(s - m), 0.0)
            p = e / jnp.maximum(jnp.sum(e, axis=-1, keepdims=True), 1e-30)
            gate = _sigmoid(gate_ref[:, hh:hh + 1])
            o_ref[:, hh * HEAD_DIM:(hh + 1) * HEAD_DIM] = _dot(p.astype(BF16), v_ref[g]) * gate
            psum = psum + p
        p_hi, p_lo = _split_bf16(psum, 2)
        imp_t = _dot_nt(inc_t, p_hi) + _dot_nt(inc_t, p_lo)
        score_t = jnp.where(forced, BIG, jnp.where(avail, imp_t, -BIG))
        score_ref[...] = score_t
        sc = score_t[:n_sel]

        def rank_step(r, rank):
            row = jnp.broadcast_to(score_ref[pl.ds(r, 1), :], (n_sel, tq))
            ahead = jnp.where(row > sc, 1.0, jnp.where((row == sc) & (j_s > r), 1.0, 0.0))
            return rank + ahead

        rank = lax.fori_loop(0, n_sel, rank_step, jnp.zeros((n_sel, tq), F32))
        chosen = jnp.where(rank < float(N_SELECT), 1.0, 0.0)
        chosen = jnp.concatenate([chosen, jnp.zeros((SEL_PAD - n_sel, tq), F32)], axis=0)
        sel_ref[:, g * SEL_PAD:(g + 1) * SEL_PAD] = chosen.T.astype(BF16)


def cmp_select_prompt(table, q_bf, kt_cmp, v_cmp, gates, n_batch, seq, n_cmp, n_sel):
    tq = CMP_Q_TILE
    nq = seq // tq
    m = n_batch * seq
    n_cmp_pad = kt_cmp.shape[-1]
    kern = functools.partial(_cmp_select_kernel, tq=tq, n_cmp_pad=n_cmp_pad, n_cmp=n_cmp, n_sel=n_sel, pos0=0)
    return pl.pallas_call(
        kern,
        grid=(nq, n_batch),
        in_specs=[
            _smem_spec(),
            pl.BlockSpec((tq, D_MODEL), lambda i, b: (b * nq + i, 0)),
            pl.BlockSpec((None, NSA_KV_GROUPS, HEAD_DIM, n_cmp_pad), lambda i, b: (b, 0, 0, 0)),
            pl.BlockSpec((None, NSA_KV_GROUPS, n_cmp_pad, HEAD_DIM), lambda i, b: (b, 0, 0, 0)),
            pl.BlockSpec((tq, 3 * N_HEADS), lambda i, b: (b * nq + i, 0)),
        ],
        out_specs=[
            pl.BlockSpec((tq, D_MODEL), lambda i, b: (b * nq + i, 0)),
            pl.BlockSpec((tq, NSA_KV_GROUPS * SEL_PAD), lambda i, b: (b * nq + i, 0)),
        ],
        out_shape=[
            jax.ShapeDtypeStruct((m, D_MODEL), F32),
            jax.ShapeDtypeStruct((m, NSA_KV_GROUPS * SEL_PAD), BF16),
        ],
        scratch_shapes=[
            pltpu.VMEM((N_HEADS, tq, n_cmp_pad), F32),
            pltpu.VMEM((SEL_PAD, tq), F32),
        ],
        compiler_params=_cparams("arbitrary", "arbitrary"),
        name="nsa_cmp_select",
    )(table, q_bf, kt_cmp, v_cmp, gates)


def _nsa_attn_kernel(tbl_ref, q_ref, k_ref, v_ref, gate_ref, toep_ref, *rest, mode, tile, branch):
    if mode == "sel":
        sel_ref, o_ref, qs_ref, m_ref, l_ref, acc_ref = rest
    else:
        o_ref, qs_ref, m_ref, l_ref, acc_ref = rest
    i = pl.program_id(1)
    iq = lax.broadcasted_iota(jnp.int32, (tile, tile), 0)
    ik = lax.broadcasted_iota(jnp.int32, (tile, tile), 1)
    causal = iq >= ik
    above = ik > iq

    for hh in range(N_HEADS):
        qs_ref[hh] = (q_ref[:, hh * HEAD_DIM:(hh + 1) * HEAD_DIM].astype(F32) * HEAD_DIM ** -0.5).astype(BF16)

    for g in range(NSA_KV_GROUPS):
        m_ref[...] = jnp.full(m_ref.shape, NEG, F32)
        l_ref[...] = jnp.zeros(l_ref.shape, F32)
        acc_ref[...] = jnp.zeros(acc_ref.shape, F32)

        def attend(kt, dd, g=g):
            r0 = pl.multiple_of(kt * tile, tile)
            kg = k_ref[pl.ds(r0, tile), :][:, g * HEAD_DIM:(g + 1) * HEAD_DIM]
            vg = v_ref[pl.ds(r0, tile), :][:, g * HEAD_DIM:(g + 1) * HEAD_DIM]
            if mode == "sel":
                jrow = lax.broadcasted_iota(jnp.int32, (SEL_PAD, tile), 0)
                kcol = lax.broadcasted_iota(jnp.int32, (SEL_PAD, tile), 1)
                expand = jnp.where(jrow == jnp.right_shift(r0 + kcol, 6), 1.0, 0.0).astype(BF16)
                mk = _dot(sel_ref[:, g * SEL_PAD:(g + 1) * SEL_PAD], expand) > 0.5
                if dd == 0:
                    mk = mk & causal
            else:
                mk = causal if dd == 0 else (above if dd == 2 else None)
            for h in range(NSA_HPG):
                hh = g * NSA_HPG + h
                s = _dot_nt(qs_ref[hh], kg)
                if dd == 2:
                    s = s + tbl_ref[N_BUCKETS - 1, hh]
                else:
                    s = s + toep_ref[hh, dd]
                if mk is not None:
                    s = jnp.where(mk, s, NEG)
                m_old = m_ref[h]
                m_new = jnp.maximum(m_old, jnp.max(s, axis=-1, keepdims=True))
                alpha = jnp.exp(m_old - m_new)
                p = jnp.exp(s - m_new)
                if mk is not None:
                    p = jnp.where(mk, p, 0.0)
                l_ref[h] = alpha * l_ref[h] + jnp.sum(p, axis=-1, keepdims=True)
                acc_ref[h] = alpha * acc_ref[h] + _dot(p.astype(BF16), vg)
                m_ref[h] = m_new

        if mode == "sel":
            def far_body(kt, carry):
                attend(kt, 2)
                return carry
            lax.fori_loop(0, jnp.maximum(i - 1, 0), far_body, 0)
        else:
            @pl.when(i >= 2)
            def _():
                attend(i - 2, 2)

        @pl.when(i >= 1)
        def _():
            attend(i - 1, 1)

        attend(i, 0)

        for h in range(NSA_HPG):
            hh = g * NSA_HPG + h
            gate = _sigmoid(gate_ref[:, branch * N_HEADS + hh:branch * N_HEADS + hh + 1])
            o_ref[:, hh * HEAD_DIM:(hh + 1) * HEAD_DIM] = acc_ref[h] / jnp.maximum(l_ref[h], 1e-30) * gate


def nsa_attn_prompt(mode, table, toep, q_bf, kv_src, k_col, v_col, gates, sel, n_batch, seq):
    tile = ATT_TILE
    assert WINDOW == 2 * tile
    nq = seq // tile
    m = n_batch * seq
    branch = 1 if mode == "sel" else 2
    kern = functools.partial(_nsa_attn_kernel, mode=mode, tile=tile, branch=branch)
    in_specs = [
        _smem_spec(),
        pl.BlockSpec((tile, D_MODEL), lambda b, i: (b * nq + i, 0)),
        pl.BlockSpec((seq, GROUP_COLS), lambda b, i: (b, k_col)),
        pl.BlockSpec((seq, GROUP_COLS), lambda b, i: (b, v_col)),
        pl.BlockSpec((tile, 3 * N_HEADS), lambda b, i: (b * nq + i, 0)),
        pl.BlockSpec((N_HEADS, 2, tile, tile), lambda b, i: (0, 0, 0, 0)),
    ]
    args = [table, q_bf, kv_src, kv_src, gates, toep]
    if mode == "sel":
        in_specs.append(pl.BlockSpec((tile, NSA_KV_GROUPS * SEL_PAD), lambda b, i: (b * nq + i, 0)))
        args.append(sel)
    return pl.pallas_call(
        kern,
        grid=(n_batch, nq),
        in_specs=in_specs,
        out_specs=pl.BlockSpec((tile, D_MODEL), lambda b, i: (b * nq + i, 0)),
        out_shape=jax.ShapeDtypeStruct((m, D_MODEL), F32),
        scratch_shapes=[
            pltpu.VMEM((N_HEADS, tile, HEAD_DIM), BF16),
            pltpu.VMEM((NSA_HPG, tile, 1), F32),
            pltpu.VMEM((NSA_HPG, tile, 1), F32),
            pltpu.VMEM((NSA_HPG, tile, HEAD_DIM), F32),
        ],
        compiler_params=_cparams("parallel", "arbitrary"),
        name="nsa_attn_" + mode,
    )(*args)


def nsa_prompt(table, toep, q_bf, kvn, kv_bf, win_bf, gates, cw, n_batch, seq):
    pe_t, w1_bd, w2_bd = cw
    n_cmp = (seq - CMP_LEN) // CMP_STRIDE + 1
    n_sel = -(-seq // SEL_LEN)
    assert n_sel <= SEL_PAD and n_sel % 8 == 0
    ctok = compress_prompt(kvn, n_batch, seq, pe_t, w1_bd, w2_bd)
    rows = ctok.shape[2]
    ctok = ctok.reshape(n_batch, 2, rows, NSA_KV_GROUPS, HEAD_DIM).astype(BF16)
    kt_cmp = ctok[:, 0].transpose(0, 2, 3, 1)
    v_cmp = ctok[:, 1].transpose(0, 2, 1, 3)
    o_cmp, sel = cmp_select_prompt(table, q_bf, kt_cmp, v_cmp, gates, n_batch, seq, n_cmp, n_sel)
    o_slc = nsa_attn_prompt("sel", table, toep, q_bf, kv_bf, 2, 3, gates, sel, n_batch, seq)
    o_win = nsa_attn_prompt("win", table, toep, q_bf, win_bf, 0, 1, gates, None, n_batch, seq)
    return o_cmp, o_slc, o_win


CUMSUM_CHUNK = 256
HEAD_BLOCK = 4


def _tril_bf16(n):
    r = lax.broadcasted_iota(jnp.int32, (n, n), 0)
    c = lax.broadcasted_iota(jnp.int32, (n, n), 1)
    return jnp.where(r >= c, 1.0, 0.0).astype(BF16)


def _cumsum_rows(x, tri):
    return sum(_dot(tri, part) for part in _split_bf16(x, 3))


def _fox_prep_kernel(f_ref, b_ref, lf_ref, d_ref, *, seq):
    tri = _tril_bf16(CUMSUM_CHUNK)
    carry = jnp.zeros((1, N_HEADS), F32)
    for c in range(seq // CUMSUM_CHUNK):
        rows = slice(c * CUMSUM_CHUNK, (c + 1) * CUMSUM_CHUNK)
        lf = _log_sigmoid(f_ref[rows, :] + b_ref[...])
        lf_ref[rows, :] = lf
        d = carry + _cumsum_rows(lf, tri)
        d_ref[rows, :] = d
        carry = d[CUMSUM_CHUNK - 1:CUMSUM_CHUNK, :]


def fox_prep_prompt(f_raw, b_f, n_batch, seq):
    m = n_batch * seq
    blk = pl.BlockSpec((seq, N_HEADS), lambda b: (b, 0))
    return pl.pallas_call(
        functools.partial(_fox_prep_kernel, seq=seq),
        grid=(n_batch,),
        in_specs=[blk, pl.BlockSpec((1, N_HEADS), lambda b: (0, 0))],
        out_specs=[blk, blk],
        out_shape=[jax.ShapeDtypeStruct((m, N_HEADS), F32)] * 2,
        compiler_params=_cparams("parallel"),
        name="fox_prep",
    )(f_raw, b_f.reshape(1, N_HEADS))


def _fox_attn_t_kernel(qt_ref, k_ref, vt_ref, drow_ref, dcol_ref, o_ref, qz_ref, m_ref, l_ref, acc_ref, *, tile):
    i = pl.program_id(2)
    ik = lax.broadcasted_iota(jnp.int32, (tile, tile), 0)
    iq = lax.broadcasted_iota(jnp.int32, (tile, tile), 1)
    causal = iq >= ik
    _fill_qz(qz_ref, lambda h: qt_ref[h * HEAD_DIM:(h + 1) * HEAD_DIM, :], lambda h: h * HEAD_DIM, HEAD_BLOCK, tile)
    _reset_softmax(m_ref, l_ref, acc_ref)

    def attend(kt, diag):
        r0 = pl.multiple_of(kt * tile, tile)
        s_all = _dot(k_ref[pl.ds(r0, tile), :], qz_ref[...])
        for h in range(HEAD_BLOCK):
            s = s_all[:, h * tile:(h + 1) * tile] + drow_ref[h:h + 1, :] - dcol_ref[pl.ds(r0, tile), h:h + 1]
            v_t = vt_ref[h * HEAD_DIM:(h + 1) * HEAD_DIM, pl.ds(r0, tile)]
            _online_softmax_step(s, causal if diag else None, v_t, m_ref, l_ref, acc_ref, h)

    def body(kt, carry):
        attend(kt, False)
        return carry

    attend(i, True)
    lax.fori_loop(0, i, body, 0)
    outs = [acc_ref[h] / jnp.maximum(l_ref[h], 1e-30) for h in range(HEAD_BLOCK)]
    o_ref[...] = jnp.concatenate(outs, axis=0).T


def fox_attn_t(q_t, k_bf, v_t, d_cum, n_batch, seq):
    tile = ATT_TILE
    nq = seq // tile
    m = n_batch * seq
    nhb = N_HEADS // HEAD_BLOCK
    d4 = d_cum[:m].reshape(n_batch, seq, nhb, HEAD_BLOCK)
    d_col = d4.transpose(0, 2, 1, 3)
    d_row = d4.transpose(0, 2, 3, 1)
    cols = HEAD_BLOCK * HEAD_DIM
    return pl.pallas_call(
        functools.partial(_fox_attn_t_kernel, tile=tile),
        grid=(n_batch, nhb, nq),
        in_specs=[
            pl.BlockSpec((cols, tile), lambda b, g, i: (g, b * nq + i)),
            pl.BlockSpec((seq, cols), lambda b, g, i: (b, g)),
            pl.BlockSpec((cols, seq), lambda b, g, i: (g, b)),
            pl.BlockSpec((None, None, HEAD_BLOCK, tile), lambda b, g, i: (b, g, 0, i)),
            pl.BlockSpec((None, None, seq, HEAD_BLOCK), lambda b, g, i: (b, g, 0, 0)),
        ],
        out_specs=pl.BlockSpec((tile, cols), lambda b, g, i: (b * nq + i, g)),
        out_shape=jax.ShapeDtypeStruct((m, D_MODEL), F32),
        scratch_shapes=[
            pltpu.VMEM((cols, HEAD_BLOCK * tile), BF16),
            pltpu.VMEM((HEAD_BLOCK, 1, tile), F32),
            pltpu.VMEM((HEAD_BLOCK, 1, tile), F32),
            pltpu.VMEM((HEAD_BLOCK, HEAD_DIM, tile), F32),
        ],
        compiler_params=_cparams("parallel", "parallel", "arbitrary"),
        name="fox_attn",
    )(q_t, k_bf, v_t, d_row, d_col)


def _fox_attn_kernel(q_ref, k_ref, v_ref, dq_ref, dk_ref, o_ref, qs_ref, m_ref, l_ref, acc_ref, *, tile):
    i = pl.program_id(2)
    iq = lax.broadcasted_iota(jnp.int32, (tile, tile), 0)
    ik = lax.broadcasted_iota(jnp.int32, (tile, tile), 1)
    causal = iq >= ik
    for h in range(HEAD_BLOCK):
        qs_ref[h] = (q_ref[:, h * HEAD_DIM:(h + 1) * HEAD_DIM].astype(F32) * HEAD_DIM ** -0.5).astype(BF16)
    m_ref[...] = jnp.full(m_ref.shape, NEG, F32)
    l_ref[...] = jnp.zeros(l_ref.shape, F32)
    acc_ref[...] = jnp.zeros(acc_ref.shape, F32)

    def attend(kt, diag):
        r0 = pl.multiple_of(kt * tile, tile)
        kb = k_ref[pl.ds(r0, tile), :]
        vb = v_ref[pl.ds(r0, tile), :]
        dk = dk_ref[kt]
        for h in range(HEAD_BLOCK):
            lanes = slice(h * HEAD_DIM, (h + 1) * HEAD_DIM)
            s = _dot_nt(qs_ref[h], kb[:, lanes]) + dq_ref[:, h:h + 1] - dk[h:h + 1, :]
            if diag:
                s = jnp.where(causal, s, NEG)
            m_old = m_ref[h]
            m_new = jnp.maximum(m_old, jnp.max(s, axis=-1, keepdims=True))
            alpha = jnp.exp(m_old - m_new)
            p = jnp.exp(s - m_new)
            if diag:
                p = jnp.where(causal, p, 0.0)
            l_ref[h] = alpha * l_ref[h] + jnp.sum(p, axis=-1, keepdims=True)
            acc_ref[h] = alpha * acc_ref[h] + _dot(p.astype(BF16), vb[:, lanes])
            m_ref[h] = m_new

    def body(kt, carry):
        attend(kt, False)
        return carry

    lax.fori_loop(0, i, body, 0)
    attend(i, True)
    for h in range(HEAD_BLOCK):
        o_ref[:, h * HEAD_DIM:(h + 1) * HEAD_DIM] = acc_ref[h] / jnp.maximum(l_ref[h], 1e-30)


def fox_attn_prompt(q_bf, kv_bf, d_cum, n_batch, seq):
    tile = ATT_TILE
    nq = seq // tile
    m = n_batch * seq
    nhb = N_HEADS // HEAD_BLOCK
    d4 = d_cum[:m].reshape(n_batch, seq, nhb, HEAD_BLOCK)
    dq = d4.transpose(0, 2, 1, 3)
    dk = d4.reshape(n_batch, nq, tile, nhb, HEAD_BLOCK).transpose(0, 3, 1, 4, 2)
    cols = HEAD_BLOCK * HEAD_DIM
    return pl.pallas_call(
        functools.partial(_fox_attn_kernel, tile=tile),
        grid=(n_batch, nhb, nq),
        in_specs=[
            pl.BlockSpec((tile, cols), lambda b, g, i: (b * nq + i, g)),
            pl.BlockSpec((seq, cols), lambda b, g, i: (b, g)),
            pl.BlockSpec((seq, cols), lambda b, g, i: (b, nhb + g)),
            pl.BlockSpec((None, None, tile, HEAD_BLOCK), lambda b, g, i: (b, g, i, 0)),
            pl.BlockSpec((None, None, nq, HEAD_BLOCK, tile), lambda b, g, i: (b, g, 0, 0, 0)),
        ],
        out_specs=pl.BlockSpec((tile, cols), lambda b, g, i: (b * nq + i, g)),
        out_shape=jax.ShapeDtypeStruct((m, D_MODEL), F32),
        scratch_shapes=[
            pltpu.VMEM((HEAD_BLOCK, tile, HEAD_DIM), BF16),
            pltpu.VMEM((HEAD_BLOCK, tile, 1), F32),
            pltpu.VMEM((HEAD_BLOCK, tile, 1), F32),
            pltpu.VMEM((HEAD_BLOCK, tile, HEAD_DIM), F32),
        ],
        compiler_params=_cparams("parallel", "parallel", "arbitrary"),
        name="fox_attn",
    )(q_bf, kv_bf, kv_bf, dq, dk)


HT_LANES = V7X_LANES


def _lane_head_onehot(n_heads, per_head):
    r = lax.broadcasted_iota(jnp.int32, (n_heads, n_heads * per_head), 0)
    c = lax.broadcasted_iota(jnp.int32, (n_heads, n_heads * per_head), 1)
    return jnp.where(_div_pow2(c, per_head) == r, 1.0, 0.0).astype(BF16)


def _expand_heads(x, onehot):
    return sum(_dot(part, onehot) for part in _split_bf16(x, 3))


def _diag_blocks_to_rows(acc_t, n_heads, feat, per_head):
    rows, lanes = acc_t.shape
    r = lax.broadcasted_iota(jnp.int32, (rows, lanes), 0)
    c = lax.broadcasted_iota(jnp.int32, (rows, lanes), 1)
    z = jnp.where(_div_pow2(r, feat) == _div_pow2(c, per_head), acc_t, 0.0)
    cc = lax.broadcasted_iota(jnp.int32, (lanes, lanes), 0)
    tt = lax.broadcasted_iota(jnp.int32, (lanes, lanes), 1)
    fold = jnp.where(_mod_pow2(cc, per_head) == tt, 1.0, 0.0).astype(BF16)
    out_t = sum(_dot(part, fold) for part in _split_bf16(z, 3))
    return out_t.T


def block_diag_queries(q_rows, n_seq, per_seq, n_heads):
    q4 = q_rows.reshape(n_seq, per_seq, n_heads, HEAD_DIM).transpose(0, 2, 3, 1)
    eye = jnp.eye(n_heads, dtype=q_rows.dtype)
    q_bd = q4[:, :, :, None, :] * eye[None, :, None, :, None]
    return q_bd.reshape(n_seq, n_heads * HEAD_DIM, n_heads * per_seq).astype(BF16)


def _fox_decode_kernel(pt_ref, qbd_ref, k_ref, v_ref, lf_ref, kn_ref, vn_ref, fn_ref, bf_ref, o_ref, lfn_ref,
                       qz_ref, m_ref, l_ref, acc_ref, carry_ref, *, n_pages, per_seq):
    p = pl.program_id(1)

    @pl.when(p == 0)
    def _():
        qz_ref[...] = (qbd_ref[...].astype(F32) * HEAD_DIM ** -0.5).astype(BF16)
        _reset_softmax(m_ref, l_ref, acc_ref)
        carry_ref[...] = jnp.zeros(carry_ref.shape, F32)

    tri = _tril_bf16(PAGE_SIZE)
    onehot = _lane_head_onehot(N_HEADS, per_seq)

    def attend(k_rows, v_rows, lf_rows, mk):
        d = carry_ref[...] + _cumsum_rows(lf_rows, tri)
        s_t = _dot(k_rows.astype(BF16), qz_ref[...]) - _expand_heads(d, onehot)
        if mk is not None:
            s_t = jnp.where(mk, s_t, NEG)
        m_old = m_ref[0]
        m_new = jnp.maximum(m_old, jnp.max(s_t, axis=0, keepdims=True))
        alpha = jnp.exp(m_old - m_new)
        pr = jnp.exp(s_t - m_new)
        if mk is not None:
            pr = jnp.where(mk, pr, 0.0)
        l_ref[0] = alpha * l_ref[0] + jnp.sum(pr, axis=0, keepdims=True)
        pv = lax.dot_general(v_rows.astype(BF16), pr.astype(BF16), (((0,), (0,)), ((), ())),
                             preferred_element_type=F32)
        acc_ref[0] = alpha * acc_ref[0] + pv
        m_ref[0] = m_new
        return d

    @pl.when(p < n_pages)
    def _():
        d = attend(k_ref[...], v_ref[...], lf_ref[...], None)
        carry_ref[...] = d[PAGE_SIZE - 1:PAGE_SIZE, :]

    @pl.when(p == n_pages)
    def _():
        lf_new = _log_sigmoid(fn_ref[...] + bf_ref[...])
        lfn_ref[...] = lf_new
        pad = PAGE_SIZE - per_seq
        k_rows = jnp.concatenate([kn_ref[...], jnp.zeros((pad, D_MODEL), F32)], axis=0)
        v_rows = jnp.concatenate([vn_ref[...], jnp.zeros((pad, D_MODEL), F32)], axis=0)
        lf_rows = jnp.concatenate([lf_new, jnp.zeros((pad, N_HEADS), F32)], axis=0)
        srow = lax.broadcasted_iota(jnp.int32, (PAGE_SIZE, HT_LANES), 0)
        tq = _mod_pow2(lax.broadcasted_iota(jnp.int32, (PAGE_SIZE, HT_LANES), 1), per_seq)
        attend(k_rows, v_rows, lf_rows, srow <= tq)
        o_t = acc_ref[0] / jnp.maximum(l_ref[0], 1e-30)
        o_ref[...] = _diag_blocks_to_rows(o_t, N_HEADS, HEAD_DIM, per_seq)[:per_seq, :]


def fox_decode(page_table, q_bd, kv_pool, lf_pool, kv_new, f_new, b_f, row_block0, n_seq, per_seq):
    n_pages = page_table.shape[1]
    assert N_HEADS * per_seq == HT_LANES
    kern = functools.partial(_fox_decode_kernel, n_pages=n_pages, per_seq=per_seq)
    page = lambda b, p, pt: pt[b, jnp.minimum(p, n_pages - 1)]
    grid_spec = pltpu.PrefetchScalarGridSpec(
        num_scalar_prefetch=1,
        grid=(n_seq, n_pages + 1),
        in_specs=[
            pl.BlockSpec((None, D_MODEL, HT_LANES), lambda b, p, pt: (b, 0, 0)),
            pl.BlockSpec((None, PAGE_SIZE, D_MODEL), lambda b, p, pt: (page(b, p, pt), 0, 0)),
            pl.BlockSpec((None, PAGE_SIZE, D_MODEL), lambda b, p, pt: (page(b, p, pt), 0, 1)),
            pl.BlockSpec((None, PAGE_SIZE, N_HEADS), lambda b, p, pt: (page(b, p, pt), 0, 0)),
            pl.BlockSpec((per_seq, D_MODEL), lambda b, p, pt: (row_block0 + b, 0)),
            pl.BlockSpec((per_seq, D_MODEL), lambda b, p, pt: (row_block0 + b, 1)),
            pl.BlockSpec((per_seq, N_HEADS), lambda b, p, pt: (row_block0 + b, 0)),
            pl.BlockSpec((1, N_HEADS), lambda b, p, pt: (0, 0)),
        ],
        out_specs=[
            pl.BlockSpec((per_seq, D_MODEL), lambda b, p, pt: (b, 0)),
            pl.BlockSpec((per_seq, N_HEADS), lambda b, p, pt: (b, 0)),
        ],
        scratch_shapes=[
            pltpu.VMEM((D_MODEL, HT_LANES), BF16),
            pltpu.VMEM((1, 1, HT_LANES), F32),
            pltpu.VMEM((1, 1, HT_LANES), F32),
            pltpu.VMEM((1, D_MODEL, HT_LANES), F32),
            pltpu.VMEM((1, N_HEADS), F32),
        ],
    )
    return pl.pallas_call(
        kern,
        grid_spec=grid_spec,
        out_shape=[jax.ShapeDtypeStruct((n_seq * per_seq, D_MODEL), F32),
                   jax.ShapeDtypeStruct((n_seq * per_seq, N_HEADS), F32)],
        compiler_params=_cparams("parallel", "arbitrary"),
        name="fox_decode",
    )(page_table, q_bd, kv_pool, kv_pool, lf_pool, kv_new, kv_new, f_new, b_f.reshape(1, N_HEADS))


PAGES_PER_STEP = 4


def feature_major_pages(pool):
    nd = pool.ndim
    return pool.transpose(0, 1, *range(3, nd), 2).reshape(pool.shape[0], pool.shape[1], -1, pool.shape[2])


def _joint_update(s_list, mk_list, vt_list, m_ref, l_ref, acc_ref):
    s_list = [s if mk is None else jnp.where(mk, s, NEG) for s, mk in zip(s_list, mk_list)]
    m_old = m_ref[0]
    m_new = m_old
    for s in s_list:
        m_new = jnp.maximum(m_new, jnp.max(s, axis=0, keepdims=True))
    alpha = jnp.exp(m_old - m_new)
    l_new = alpha * l_ref[0]
    pv = None
    for s, mk, v_t in zip(s_list, mk_list, vt_list):
        pr = jnp.exp(s - m_new)
        if mk is not None:
            pr = jnp.where(mk, pr, 0.0)
        l_new = l_new + jnp.sum(pr, axis=0, keepdims=True)
        term = _dot(v_t.astype(BF16), pr.astype(BF16))
        pv = term if pv is None else pv + term
    l_ref[0] = l_new
    acc_ref[0] = alpha * acc_ref[0] + pv
    m_ref[0] = m_new


def _decay_rows(lf_rows_exp, carry_row, tri):
    return carry_row + _cumsum_rows(lf_rows_exp, tri)


def _fox_decode_fm_kernel(pt_ref, qbd_ref, qbdt_ref, *refs, n_steps, per_seq, pps):
    k_refs = refs[:pps]
    v_refs = refs[pps:2 * pps]
    lf_refs = refs[2 * pps:3 * pps]
    kn_ref, vn_ref, fn_ref, bf_ref, o_ref, lfn_ref, qz_ref, qzt_ref, m_ref, l_ref, acc_ref, carry_ref = refs[3 * pps:]
    p = pl.program_id(1)

    @pl.when(p == 0)
    def _():
        qz_ref[...] = (qbd_ref[...].astype(F32) * HEAD_DIM ** -0.5).astype(BF16)
        qzt_ref[...] = (qbdt_ref[...].astype(F32) * HEAD_DIM ** -0.5).astype(BF16)
        _reset_softmax(m_ref, l_ref, acc_ref)
        carry_ref[...] = jnp.zeros(carry_ref.shape, F32)

    tri = _tril_bf16(PAGE_SIZE)
    onehot = _lane_head_onehot(N_HEADS, per_seq)

    def update(s_t, mk, pv_of):
        if mk is not None:
            s_t = jnp.where(mk, s_t, NEG)
        m_old = m_ref[0]
        m_new = jnp.maximum(m_old, jnp.max(s_t, axis=0, keepdims=True))
        alpha = jnp.exp(m_old - m_new)
        pr = jnp.exp(s_t - m_new)
        if mk is not None:
            pr = jnp.where(mk, pr, 0.0)
        l_ref[0] = alpha * l_ref[0] + jnp.sum(pr, axis=0, keepdims=True)
        acc_ref[0] = alpha * acc_ref[0] + pv_of(pr.astype(BF16))
        m_ref[0] = m_new

    @pl.when(p < n_steps)
    def _():
        local, s_raw = [], []
        for i in range(pps):
            lf_t = lf_refs[i][...]
            lf_exp = sum(lax.dot_general(part, onehot, (((0,), (0,)), ((), ())), preferred_element_type=F32)
                         for part in _split_bf16(lf_t, 3))
            local.append(_cumsum_rows(lf_exp, tri))
            s_raw.append(_dot(qzt_ref[...], k_refs[i][...].astype(BF16)).T)
        carry = carry_ref[...]
        s_list = []
        for i in range(pps):
            d = carry + local[i]
            s_list.append(s_raw[i] - d)
            carry = d[PAGE_SIZE - 1:PAGE_SIZE, :]
        carry_ref[...] = carry
        _joint_update(s_list, [None] * pps, [r[...] for r in v_refs], m_ref, l_ref, acc_ref)

    @pl.when(p == n_steps)
    def _():
        lf_new = _log_sigmoid(fn_ref[...] + bf_ref[...])
        lfn_ref[...] = lf_new
        pad = PAGE_SIZE - per_seq
        k_rows = jnp.concatenate([kn_ref[...], jnp.zeros((pad, D_MODEL), F32)], axis=0).astype(BF16)
        v_rows = jnp.concatenate([vn_ref[...], jnp.zeros((pad, D_MODEL), F32)], axis=0).astype(BF16)
        lf_rows = jnp.concatenate([lf_new, jnp.zeros((pad, N_HEADS), F32)], axis=0)
        d = _decay_rows(_expand_heads(lf_rows, onehot), carry_ref[...], tri)
        srow = lax.broadcasted_iota(jnp.int32, (PAGE_SIZE, HT_LANES), 0)
        tq = _mod_pow2(lax.broadcasted_iota(jnp.int32, (PAGE_SIZE, HT_LANES), 1), per_seq)
        s_t = _dot(k_rows, qz_ref[...]) - d
        update(s_t, srow <= tq, lambda pr: lax.dot_general(v_rows, pr, (((0,), (0,)), ((), ())),
                                                           preferred_element_type=F32))
        o_t = acc_ref[0] / jnp.maximum(l_ref[0], 1e-30)
        o_ref[...] = _diag_blocks_to_rows(o_t, N_HEADS, HEAD_DIM, per_seq)[:per_seq, :]


def fox_decode_fm(page_table, q_bd, kv_pool_t, lf_pool_t, layer, kv_new, f_new, b_f, row_block0, n_seq, per_seq):
    n_pages = page_table.shape[1]
    pps = PAGES_PER_STEP
    assert N_HEADS * per_seq == HT_LANES and n_pages % pps == 0
    n_steps = n_pages // pps
    kern = functools.partial(_fox_decode_fm_kernel, n_steps=n_steps, per_seq=per_seq, pps=pps)

    def page(i):
        return lambda b, p, pt: pt[b, jnp.minimum(p, n_steps - 1) * pps + i]

    def page_spec(rows, blk, i):
        return pl.BlockSpec((None, None, rows, PAGE_SIZE), lambda b, p, pt, i=i: (layer, page(i)(b, p, pt), blk, 0))

    grid_spec = pltpu.PrefetchScalarGridSpec(
        num_scalar_prefetch=1,
        grid=(n_seq, n_steps + 1),
        in_specs=[pl.BlockSpec((None, D_MODEL, HT_LANES), lambda b, p, pt: (b, 0, 0)),
                  pl.BlockSpec((None, HT_LANES, D_MODEL), lambda b, p, pt: (b, 0, 0))]
        + [page_spec(D_MODEL, 0, i) for i in range(pps)]
        + [page_spec(D_MODEL, 1, i) for i in range(pps)]
        + [page_spec(N_HEADS, 0, i) for i in range(pps)]
        + [
            pl.BlockSpec((per_seq, D_MODEL), lambda b, p, pt: (row_block0 + b, 0)),
            pl.BlockSpec((per_seq, D_MODEL), lambda b, p, pt: (row_block0 + b, 1)),
            pl.BlockSpec((per_seq, N_HEADS), lambda b, p, pt: (row_block0 + b, 0)),
            pl.BlockSpec((1, N_HEADS), lambda b, p, pt: (0, 0)),
        ],
        out_specs=[
            pl.BlockSpec((per_seq, D_MODEL), lambda b, p, pt: (b, 0)),
            pl.BlockSpec((per_seq, N_HEADS), lambda b, p, pt: (b, 0)),
        ],
        scratch_shapes=[
            pltpu.VMEM((D_MODEL, HT_LANES), BF16),
            pltpu.VMEM((HT_LANES, D_MODEL), BF16),
            pltpu.VMEM((1, 1, HT_LANES), F32),
            pltpu.VMEM((1, 1, HT_LANES), F32),
            pltpu.VMEM((1, D_MODEL, HT_LANES), F32),
            pltpu.VMEM((1, HT_LANES), F32),
        ],
    )
    return pl.pallas_call(
        kern,
        grid_spec=grid_spec,
        out_shape=[jax.ShapeDtypeStruct((n_seq * per_seq, D_MODEL), F32),
                   jax.ShapeDtypeStruct((n_seq * per_seq, N_HEADS), F32)],
        compiler_params=_cparams("parallel", "arbitrary"),
        name="fox_decode",
    )(page_table, q_bd, q_bd.transpose(0, 2, 1), *([kv_pool_t] * (2 * pps)), *([lf_pool_t] * pps),
      kv_new, kv_new, f_new, b_f.reshape(1, N_HEADS))


DEC_ROW_TILE = 512


def _group_rows_to_heads(acc_t):
    return jnp.concatenate([acc_t[(h // NSA_HPG) * HEAD_DIM:(h // NSA_HPG + 1) * HEAD_DIM] for h in range(N_HEADS)],
                           axis=0)


def group_block_diag_queries(q_rows, n_seq, per_seq):
    q5 = q_rows.reshape(n_seq, per_seq, NSA_KV_GROUPS, NSA_HPG, HEAD_DIM).transpose(0, 2, 4, 3, 1)
    eye = jnp.eye(NSA_KV_GROUPS, dtype=q_rows.dtype)
    q_bd = q5[:, :, :, None, :, :] * eye[None, :, None, :, None, None]
    return q_bd.reshape(n_seq, GROUP_COLS, N_HEADS * per_seq).astype(BF16)


def _nsa_dec_bias_kernel(tbl_ref, o_ref, *, pos0, per_seq):
    onehot = _lane_head_onehot(N_HEADS, per_seq)
    tab = _expand_heads(tbl_ref[...], onehot)
    row = lax.broadcasted_iota(jnp.int32, (PAGE_SIZE, HT_LANES), 0)
    t = _mod_pow2(lax.broadcasted_iota(jnp.int32, (PAGE_SIZE, HT_LANES), 1), per_seq)
    dists = (PAGE_SIZE + t - row, t - row, pos0 + t - (row * CMP_STRIDE + CMP_LEN - 1))
    for n, dist in enumerate(dists):
        bucket = _t5_bucket(dist)
        val = jnp.broadcast_to(tab[0:1, :], (PAGE_SIZE, HT_LANES))
        for k in range(1, N_BUCKETS):
            val = jnp.where(bucket == k, tab[k:k + 1, :], val)
        o_ref[n] = val
    o_ref[3] = jnp.broadcast_to(tab[N_BUCKETS - 1:N_BUCKETS, :], (PAGE_SIZE, HT_LANES))


def nsa_dec_bias(table, pos0, per_seq):
    assert PAGE_SIZE >= T5_SATURATION_DIST
    return pl.pallas_call(
        functools.partial(_nsa_dec_bias_kernel, pos0=pos0, per_seq=per_seq),
        out_shape=jax.ShapeDtypeStruct((4, PAGE_SIZE, HT_LANES), F32),
        name="nsa_dec_bias",
    )(table)


def _dec_gather_kernel(pt_ref, x_ref, o_ref):
    for l in range(CMP_STRIDE):
        o_ref[l] = x_ref[:, l, :]


def nsa_dec_gather(page_table, pool):
    n_seq, n_pages = page_table.shape
    n_pool = pool.shape[0]
    rpp = PAGE_SIZE // CMP_STRIDE
    pool4 = pool.reshape(n_pool, rpp, CMP_STRIDE, pool.shape[-1])
    grid_spec = pltpu.PrefetchScalarGridSpec(
        num_scalar_prefetch=1,
        grid=(n_seq, n_pages),
        in_specs=[pl.BlockSpec((None, rpp, CMP_STRIDE, 2 * GROUP_COLS), lambda b, p, pt: (pt[b, p], 0, 0, 0))],
        out_specs=pl.BlockSpec((CMP_STRIDE, rpp, 2 * GROUP_COLS), lambda b, p, pt: (0, b * n_pages + p, 0)),
    )
    return pl.pallas_call(
        _dec_gather_kernel,
        grid_spec=grid_spec,
        out_shape=jax.ShapeDtypeStruct((CMP_STRIDE, n_seq * n_pages * rpp, 2 * GROUP_COLS), F32),
        compiler_params=_cparams("parallel", "arbitrary"),
        name="nsa_dec_gather",
    )(page_table, pool4)


def nsa_dec_compress(xd, pe_t, w1_bd, w2_bd):
    r_total = xd.shape[1]
    rows = min(DEC_ROW_TILE, r_total)
    return pl.pallas_call(
        functools.partial(_compress_kernel, rows=rows, l_major=True),
        grid=(r_total // rows, 2),
        in_specs=[
            pl.BlockSpec((CMP_STRIDE, rows, GROUP_COLS), lambda i, j: (0, i, j)),
            pl.BlockSpec((None, CMP_LEN, 1, GROUP_COLS), lambda i, j: (j, 0, 0, 0)),
            pl.BlockSpec((None, CMP_LEN, GROUP_COLS, NSA_KV_GROUPS * CMP_HIDDEN), lambda i, j: (j, 0, 0, 0)),
            pl.BlockSpec((None, NSA_KV_GROUPS * CMP_HIDDEN, GROUP_COLS), lambda i, j: (j, 0, 0)),
        ],
        out_specs=pl.BlockSpec((None, rows, GROUP_COLS), lambda i, j: (j, i, 0)),
        out_shape=jax.ShapeDtypeStruct((2, r_total, GROUP_COLS), F32),
        compiler_params=_cparams("parallel", "arbitrary"),
        name="nsa_dec_compress",
    )(xd, pe_t, w1_bd, w2_bd)


def _dec_softmax_step(s_t, mk, v_rows, m_ref, l_ref, acc_ref):
    if mk is not None:
        s_t = jnp.where(mk, s_t, NEG)
    m_old = m_ref[0]
    m_new = jnp.maximum(m_old, jnp.max(s_t, axis=0, keepdims=True))
    alpha = jnp.exp(m_old - m_new)
    pr = jnp.exp(s_t - m_new)
    if mk is not None:
        pr = jnp.where(mk, pr, 0.0)
    l_ref[0] = alpha * l_ref[0] + jnp.sum(pr, axis=0, keepdims=True)
    pv = lax.dot_general(v_rows, pr.astype(BF16), (((0,), (0,)), ((), ())), preferred_element_type=F32)
    acc_ref[0] = alpha * acc_ref[0] + pv
    m_ref[0] = m_new


def _dec_finish(m_ref, l_ref, acc_ref, gate_row, per_seq):
    o_t = acc_ref[0] / jnp.maximum(l_ref[0], 1e-30) * _sigmoid(gate_row)
    return _diag_blocks_to_rows(_group_rows_to_heads(o_t), N_HEADS, HEAD_DIM, per_seq)[:per_seq, :]


def _nsa_dec_cmp_kernel(qbd_ref, kc_ref, vc_ref, g_ref, bias_ref, o_ref, sel_ref, score_ref, *,
                        n_cmp, n_sel, pos0, per_seq):
    n_c = kc_ref.shape[0]
    c_idx = lax.broadcasted_iota(jnp.int32, (n_c, HT_LANES), 0)
    lane = lax.broadcasted_iota(jnp.int32, (n_c, HT_LANES), 1)
    t = _mod_pow2(lane, per_seq)
    dist = pos0 + t - (c_idx * CMP_STRIDE + CMP_LEN - 1)
    valid = (dist >= 0) & (c_idx < n_cmp)
    qz = (qbd_ref[...].astype(F32) * HEAD_DIM ** -0.5).astype(BF16)
    s = _dot(kc_ref[...], qz) + bias_ref[2]
    s = jnp.where(valid, s, NEG)
    m = jnp.max(s, axis=0, keepdims=True)
    e = jnp.where(valid, jnp.exp(s - m), 0.0)
    p = e / jnp.maximum(jnp.sum(e, axis=0, keepdims=True), 1e-30)
    o_t = lax.dot_general(vc_ref[...], p.astype(BF16), (((0,), (0,)), ((), ())), preferred_element_type=F32)
    o_t = o_t * _sigmoid(g_ref[0])
    o_ref[...] = _diag_blocks_to_rows(_group_rows_to_heads(o_t), N_HEADS, HEAD_DIM, per_seq)[:per_seq, :]
    src = lax.broadcasted_iota(jnp.int32, (HT_LANES, HT_LANES), 0)
    dst = lax.broadcasted_iota(jnp.int32, (HT_LANES, HT_LANES), 1)
    same_t = _mod_pow2(src, per_seq) == _mod_pow2(dst, per_seq)
    g_of_src = _div_pow2(src, per_seq * NSA_HPG)
    n_gt = NSA_KV_GROUPS * per_seq
    head_sum = jnp.where(same_t & (g_of_src == _div_pow2(dst, per_seq)) & (dst < n_gt), 1.0, 0.0).astype(BF16)
    psum = sum(_dot(part, head_sum) for part in _split_bf16(p, 2))
    j_i = lax.broadcasted_iota(jnp.int32, (SEL_PAD, n_c), 0) * SEL_LEN
    c_i = lax.broadcasted_iota(jnp.int32, (SEL_PAD, n_c), 1) * CMP_STRIDE
    inc_t = jnp.where((c_i < j_i + SEL_LEN) & (c_i + CMP_LEN > j_i) & (j_i < n_sel * SEL_LEN)
                      & (c_i < n_cmp * CMP_STRIDE), 1.0, 0.0).astype(BF16)
    imp_t = sum(_dot(inc_t, part) for part in _split_bf16(psum, 2))
    j_t = lax.broadcasted_iota(jnp.int32, (SEL_PAD, HT_LANES), 0)
    q_t = pos0 + _mod_pow2(lax.broadcasted_iota(jnp.int32, (SEL_PAD, HT_LANES), 1), per_seq)
    cur = jnp.right_shift(q_t, 6)
    forced = (j_t == 0) | (j_t == cur) | (j_t == cur - 1)
    avail = j_t <= cur
    score_t = jnp.where(forced, BIG, jnp.where(avail, imp_t, -BIG))
    score_ref[...] = score_t
    n_rank = -(-n_sel // 8) * 8
    sc = score_t[:n_rank]
    j_s = lax.broadcasted_iota(jnp.int32, (n_rank, HT_LANES), 0)

    def rank_step(r, rank):
        row = jnp.broadcast_to(score_ref[pl.ds(r, 1), :], (n_rank, HT_LANES))
        ahead = jnp.where(row > sc, 1.0, jnp.where((row == sc) & (j_s > r), 1.0, 0.0))
        return rank + ahead

    rank = lax.fori_loop(0, n_sel, rank_step, jnp.zeros((n_rank, HT_LANES), F32))
    chosen = jnp.where((rank < float(N_SELECT)) & (j_s < n_sel), 1.0, 0.0)
    chosen = jnp.concatenate([chosen, jnp.zeros((SEL_PAD - n_rank, HT_LANES), F32)], axis=0).astype(BF16)
    spread = jnp.where(same_t & (src == _div_pow2(dst, per_seq * NSA_HPG) * per_seq + _mod_pow2(dst, per_seq)),
                       1.0, 0.0).astype(BF16)
    sel_ref[...] = _dot(chosen, spread).astype(BF16)


def nsa_dec_cmp(q_bd, k_cmp, v_cmp, gate_rows, bias, n_cmp, n_sel, pos0, per_seq):
    n_seq, n_c = k_cmp.shape[:2]
    kern = functools.partial(_nsa_dec_cmp_kernel, n_cmp=n_cmp, n_sel=n_sel, pos0=pos0, per_seq=per_seq)
    return pl.pallas_call(
        kern,
        grid=(n_seq,),
        in_specs=[
            pl.BlockSpec((None, GROUP_COLS, HT_LANES), lambda b: (b, 0, 0)),
            pl.BlockSpec((None, n_c, GROUP_COLS), lambda b: (b, 0, 0)),
            pl.BlockSpec((None, n_c, GROUP_COLS), lambda b: (b, 0, 0)),
            pl.BlockSpec((None, 3, 1, HT_LANES), lambda b: (b, 0, 0, 0)),
            pl.BlockSpec((4, PAGE_SIZE, HT_LANES), lambda b: (0, 0, 0)),
        ],
        out_specs=[
            pl.BlockSpec((per_seq, D_MODEL), lambda b: (b, 0)),
            pl.BlockSpec((None, SEL_PAD, HT_LANES), lambda b: (b, 0, 0)),
        ],
        out_shape=[
            jax.ShapeDtypeStruct((n_seq * per_seq, D_MODEL), F32),
            jax.ShapeDtypeStruct((n_seq, SEL_PAD, HT_LANES), BF16),
        ],
        scratch_shapes=[pltpu.VMEM((SEL_PAD, HT_LANES), F32)],
        compiler_params=_cparams("parallel"),
        name="nsa_dec_cmp",
    )(q_bd, k_cmp, v_cmp, gate_rows, bias)


def _nsa_dec_sel_kernel(pt_ref, qbd_ref, k_ref, v_ref, kn_ref, vn_ref, sel_ref, g_ref, bias_ref, o_ref,
                        qz_ref, m_ref, l_ref, acc_ref, *, n_pages, per_seq):
    p = pl.program_id(1)

    @pl.when(p == 0)
    def _():
        qz_ref[...] = (qbd_ref[...].astype(F32) * HEAD_DIM ** -0.5).astype(BF16)
        _reset_softmax(m_ref, l_ref, acc_ref)

    srow = lax.broadcasted_iota(jnp.int32, (PAGE_SIZE, SEL_PAD), 0)
    jcol = lax.broadcasted_iota(jnp.int32, (PAGE_SIZE, SEL_PAD), 1)
    blocks_per_page = PAGE_SIZE // SEL_LEN

    def member(first_block):
        expand = jnp.where(jcol == first_block + jnp.right_shift(srow, 6), 1.0, 0.0).astype(BF16)
        return _dot(expand, sel_ref[...]) > 0.5

    @pl.when(p < n_pages)
    def _():
        bias = jnp.where(p == n_pages - 1, bias_ref[0], bias_ref[3])
        s_t = _dot(k_ref[...].astype(BF16), qz_ref[...]) + bias
        _dec_softmax_step(s_t, member(p * blocks_per_page), v_ref[...].astype(BF16), m_ref, l_ref, acc_ref)

    @pl.when(p == n_pages)
    def _():
        pad = PAGE_SIZE - per_seq
        k_rows = jnp.concatenate([kn_ref[...], jnp.zeros((pad, GROUP_COLS), F32)], axis=0).astype(BF16)
        v_rows = jnp.concatenate([vn_ref[...], jnp.zeros((pad, GROUP_COLS), F32)], axis=0).astype(BF16)
        krow = lax.broadcasted_iota(jnp.int32, (PAGE_SIZE, HT_LANES), 0)
        tq = _mod_pow2(lax.broadcasted_iota(jnp.int32, (PAGE_SIZE, HT_LANES), 1), per_seq)
        mk = member(n_pages * blocks_per_page) & (krow <= tq)
        s_t = _dot(k_rows, qz_ref[...]) + bias_ref[1]
        _dec_softmax_step(s_t, mk, v_rows, m_ref, l_ref, acc_ref)
        o_ref[...] = _dec_finish(m_ref, l_ref, acc_ref, g_ref[1], per_seq)


def nsa_dec_sel(page_table, q_bd, pool, kv_new, sel, gate_rows, bias, row_block0, per_seq):
    n_seq, n_pages = page_table.shape
    kern = functools.partial(_nsa_dec_sel_kernel, n_pages=n_pages, per_seq=per_seq)
    page = lambda b, p, pt: pt[b, jnp.minimum(p, n_pages - 1)]
    grid_spec = pltpu.PrefetchScalarGridSpec(
        num_scalar_prefetch=1,
        grid=(n_seq, n_pages + 1),
        in_specs=[
            pl.BlockSpec((None, GROUP_COLS, HT_LANES), lambda b, p, pt: (b, 0, 0)),
            pl.BlockSpec((None, PAGE_SIZE, GROUP_COLS), lambda b, p, pt: (page(b, p, pt), 0, 2)),
            pl.BlockSpec((None, PAGE_SIZE, GROUP_COLS), lambda b, p, pt: (page(b, p, pt), 0, 3)),
            pl.BlockSpec((per_seq, GROUP_COLS), lambda b, p, pt: (row_block0 + b, 2)),
            pl.BlockSpec((per_seq, GROUP_COLS), lambda b, p, pt: (row_block0 + b, 3)),
            pl.BlockSpec((None, SEL_PAD, HT_LANES), lambda b, p, pt: (b, 0, 0)),
            pl.BlockSpec((None, 3, 1, HT_LANES), lambda b, p, pt: (b, 0, 0, 0)),
            pl.BlockSpec((4, PAGE_SIZE, HT_LANES), lambda b, p, pt: (0, 0, 0)),
        ],
        out_specs=pl.BlockSpec((per_seq, D_MODEL), lambda b, p, pt: (b, 0)),
        scratch_shapes=[
            pltpu.VMEM((GROUP_COLS, HT_LANES), BF16),
            pltpu.VMEM((1, 1, HT_LANES), F32),
            pltpu.VMEM((1, 1, HT_LANES), F32),
            pltpu.VMEM((1, GROUP_COLS, HT_LANES), F32),
        ],
    )
    return pl.pallas_call(
        kern,
        grid_spec=grid_spec,
        out_shape=jax.ShapeDtypeStruct((n_seq * per_seq, D_MODEL), F32),
        compiler_params=_cparams("parallel", "arbitrary"),
        name="nsa_dec_sel",
    )(page_table, q_bd, pool, pool, kv_new, kv_new, sel, gate_rows, bias)


def _nsa_dec_win_kernel(qbd_ref, w_ref, wn_ref, g_ref, bias_ref, o_ref, ws_ref, m_ref, l_ref, acc_ref, *, per_seq):
    n_past = w_ref.shape[0]
    n_tiles = n_past // PAGE_SIZE
    qz = (qbd_ref[...].astype(F32) * HEAD_DIM ** -0.5).astype(BF16)
    _reset_softmax(m_ref, l_ref, acc_ref)
    krow = lax.broadcasted_iota(jnp.int32, (PAGE_SIZE, HT_LANES), 0)
    tq = _mod_pow2(lax.broadcasted_iota(jnp.int32, (PAGE_SIZE, HT_LANES), 1), per_seq)
    for w in range(n_tiles):
        rows = w_ref[w * PAGE_SIZE:(w + 1) * PAGE_SIZE, :]
        mk = (krow > tq) if w == 0 else None
        bias = bias_ref[0] if w == n_tiles - 1 else bias_ref[3]
        s_t = _dot(rows[:, :GROUP_COLS].astype(BF16), qz) + bias
        _dec_softmax_step(s_t, mk, rows[:, GROUP_COLS:].astype(BF16), m_ref, l_ref, acc_ref)
    pad = PAGE_SIZE - per_seq
    new = jnp.concatenate([wn_ref[...], jnp.zeros((pad, 2 * GROUP_COLS), F32)], axis=0)
    s_t = _dot(new[:, :GROUP_COLS].astype(BF16), qz) + bias_ref[1]
    _dec_softmax_step(s_t, krow <= tq, new[:, GROUP_COLS:].astype(BF16), m_ref, l_ref, acc_ref)
    o_ref[...] = _dec_finish(m_ref, l_ref, acc_ref, g_ref[2], per_seq)
    ws_ref[0:n_past - per_seq, :] = w_ref[per_seq:n_past, :]
    ws_ref[n_past - per_seq:n_past, :] = wn_ref[...]


def nsa_dec_win(q_bd, win_past, win_new, gate_rows, bias, row_block0, per_seq):
    n_seq, n_past = win_past.shape[:2]
    assert n_past == WINDOW and n_past % PAGE_SIZE == 0 and PAGE_SIZE + 1 >= T5_SATURATION_DIST
    kern = functools.partial(_nsa_dec_win_kernel, per_seq=per_seq)
    return pl.pallas_call(
        kern,
        grid=(n_seq,),
        in_specs=[
            pl.BlockSpec((None, GROUP_COLS, HT_LANES), lambda b: (b, 0, 0)),
            pl.BlockSpec((None, n_past, 2 * GROUP_COLS), lambda b: (b, 0, 0)),
            pl.BlockSpec((per_seq, 2 * GROUP_COLS), lambda b: (row_block0 + b, 0)),
            pl.BlockSpec((None, 3, 1, HT_LANES), lambda b: (b, 0, 0, 0)),
            pl.BlockSpec((4, PAGE_SIZE, HT_LANES), lambda b: (0, 0, 0)),
        ],
        out_specs=[
            pl.BlockSpec((per_seq, D_MODEL), lambda b: (b, 0)),
            pl.BlockSpec((None, n_past, 2 * GROUP_COLS), lambda b: (b, 0, 0)),
        ],
        out_shape=[
            jax.ShapeDtypeStruct((n_seq * per_seq, D_MODEL), F32),
            jax.ShapeDtypeStruct((n_seq, n_past, 2 * GROUP_COLS), F32),
        ],
        scratch_shapes=[
            pltpu.VMEM((1, 1, HT_LANES), F32),
            pltpu.VMEM((1, 1, HT_LANES), F32),
            pltpu.VMEM((1, GROUP_COLS, HT_LANES), F32),
        ],
        compiler_params=_cparams("parallel"),
        name="nsa_dec_win",
    )(q_bd, win_past, win_new, gate_rows, bias)


def nsa_decode(table, page_table, pool, win_past, q_rows, kv_new, win_new, gates_rows, cw, row_block0, pos0, per_seq):
    n_seq, n_pages = page_table.shape
    pe_t, w1_bd, w2_bd = cw
    past = n_pages * PAGE_SIZE
    s_total = past + per_seq
    n_cmp = (s_total - CMP_LEN) // CMP_STRIDE + 1
    n_sel = -(-s_total // SEL_LEN)
    assert n_cmp == past // CMP_STRIDE - 1 and n_sel == past // SEL_LEN + 1 and per_seq <= CMP_STRIDE
    assert pos0 == past and pos0 % SEL_LEN == 0 and pos0 >= WINDOW and past // CMP_STRIDE == PAGE_SIZE
    bias = nsa_dec_bias(table, pos0, per_seq)
    ctok = nsa_dec_compress(nsa_dec_gather(page_table, pool), pe_t, w1_bd, w2_bd).astype(BF16)
    rows_per_seq = past // CMP_STRIDE
    k_cmp = ctok[0].reshape(n_seq, rows_per_seq, GROUP_COLS)
    v_cmp = ctok[1].reshape(n_seq, rows_per_seq, GROUP_COLS)
    q_bd = group_block_diag_queries(q_rows, n_seq, per_seq)
    gate_rows = gates_rows.reshape(n_seq, per_seq, 3, N_HEADS).transpose(0, 2, 3, 1).reshape(n_seq, 3, 1, HT_LANES)
    o_cmp, sel = nsa_dec_cmp(q_bd, k_cmp, v_cmp, gate_rows, bias, n_cmp, n_sel, pos0, per_seq)
    o_slc = nsa_dec_sel(page_table, q_bd, pool, kv_new, sel, gate_rows, bias, row_block0, per_seq)
    o_win, win_state = nsa_dec_win(q_bd, win_past, win_new, gate_rows, bias, row_block0, per_seq)
    return o_cmp, o_slc, o_win, win_state


def _dec_gather_fm_kernel(pt_ref, *refs, pps):
    o_ref = refs[pps]
    rpp = PAGE_SIZE // CMP_STRIDE
    for i in range(pps):
        x3 = refs[i][...].T.reshape(rpp, CMP_STRIDE, 2 * GROUP_COLS)
        for l in range(CMP_STRIDE):
            o_ref[l, i * rpp:(i + 1) * rpp, :] = x3[:, l, :]


def nsa_dec_gather_fm(page_table, pool_t, layer):
    n_seq, n_pages = page_table.shape
    pps = PAGES_PER_STEP
    assert n_pages % pps == 0
    rpp = PAGE_SIZE // CMP_STRIDE
    grid_spec = pltpu.PrefetchScalarGridSpec(
        num_scalar_prefetch=1,
        grid=(n_seq, n_pages // pps),
        in_specs=[pl.BlockSpec((None, None, 2 * GROUP_COLS, PAGE_SIZE),
                               lambda b, p, pt, i=i: (layer, pt[b, p * pps + i], 0, 0)) for i in range(pps)],
        out_specs=pl.BlockSpec((CMP_STRIDE, rpp * pps, 2 * GROUP_COLS),
                               lambda b, p, pt: (0, b * (n_pages // pps) + p, 0)),
    )
    return pl.pallas_call(
        functools.partial(_dec_gather_fm_kernel, pps=pps),
        grid_spec=grid_spec,
        out_shape=jax.ShapeDtypeStruct((CMP_STRIDE, n_seq * n_pages * rpp, 2 * GROUP_COLS), F32),
        compiler_params=_cparams("parallel", "arbitrary"),
        name="nsa_dec_gather",
    )(page_table, *([pool_t] * pps))


def _dec_update(s_t, mk, pv_of, m_ref, l_ref, acc_ref):
    if mk is not None:
        s_t = jnp.where(mk, s_t, NEG)
    m_old = m_ref[0]
    m_new = jnp.maximum(m_old, jnp.max(s_t, axis=0, keepdims=True))
    alpha = jnp.exp(m_old - m_new)
    pr = jnp.exp(s_t - m_new)
    if mk is not None:
        pr = jnp.where(mk, pr, 0.0)
    l_ref[0] = alpha * l_ref[0] + jnp.sum(pr, axis=0, keepdims=True)
    acc_ref[0] = alpha * acc_ref[0] + pv_of(pr.astype(BF16))
    m_ref[0] = m_new


def _pv_key_major(v_rows):
    return lambda pr: lax.dot_general(v_rows, pr, (((0,), (0,)), ((), ())), preferred_element_type=F32)


def _nsa_dec_sel_fm_kernel(pt_ref, qbd_ref, qbdt_ref, *refs, n_steps, per_seq, pps):
    k_refs = refs[:pps]
    v_refs = refs[pps:2 * pps]
    kn_ref, vn_ref, sel_ref, g_ref, bias_ref, o_ref, qz_ref, qzt_ref, m_ref, l_ref, acc_ref = refs[2 * pps:]
    p = pl.program_id(1)

    @pl.when(p == 0)
    def _():
        qz_ref[...] = (qbd_ref[...].astype(F32) * HEAD_DIM ** -0.5).astype(BF16)
        qzt_ref[...] = (qbdt_ref[...].astype(F32) * HEAD_DIM ** -0.5).astype(BF16)
        _reset_softmax(m_ref, l_ref, acc_ref)

    srow = lax.broadcasted_iota(jnp.int32, (PAGE_SIZE, SEL_PAD), 0)
    jcol = lax.broadcasted_iota(jnp.int32, (PAGE_SIZE, SEL_PAD), 1)
    blocks_per_page = PAGE_SIZE // SEL_LEN

    def member(first_block):
        expand = jnp.where(jcol == first_block + jnp.right_shift(srow, 6), 1.0, 0.0).astype(BF16)
        return _dot(expand, sel_ref[...]) > 0.5

    @pl.when(p < n_steps)
    def _():
        s_list, mk_list = [], []
        for i in range(pps):
            page_no = p * pps + i
            bias = jnp.where(page_no == n_steps * pps - 1, bias_ref[0], bias_ref[3])
            s_list.append(_dot(qzt_ref[...], k_refs[i][...].astype(BF16)).T + bias)
            mk_list.append(member(page_no * blocks_per_page))
        _joint_update(s_list, mk_list, [r[...] for r in v_refs], m_ref, l_ref, acc_ref)

    @pl.when(p == n_steps)
    def _():
        pad = PAGE_SIZE - per_seq
        k_rows = jnp.concatenate([kn_ref[...], jnp.zeros((pad, GROUP_COLS), F32)], axis=0).astype(BF16)
        v_rows = jnp.concatenate([vn_ref[...], jnp.zeros((pad, GROUP_COLS), F32)], axis=0).astype(BF16)
        krow = lax.broadcasted_iota(jnp.int32, (PAGE_SIZE, HT_LANES), 0)
        tq = _mod_pow2(lax.broadcasted_iota(jnp.int32, (PAGE_SIZE, HT_LANES), 1), per_seq)
        mk = member(n_steps * pps * blocks_per_page) & (krow <= tq)
        s_t = _dot(k_rows, qz_ref[...]) + bias_ref[1]
        _dec_update(s_t, mk, _pv_key_major(v_rows), m_ref, l_ref, acc_ref)
        o_ref[...] = _dec_finish(m_ref, l_ref, acc_ref, g_ref[1], per_seq)


def nsa_dec_sel_fm(page_table, q_bd, pool_t, layer, kv_new, sel, gate_rows, bias, row_block0, per_seq):
    n_seq, n_pages = page_table.shape
    pps = PAGES_PER_STEP
    assert n_pages % pps == 0
    n_steps = n_pages // pps
    kern = functools.partial(_nsa_dec_sel_fm_kernel, n_steps=n_steps, per_seq=per_seq, pps=pps)

    def page_spec(blk, i):
        return pl.BlockSpec((None, None, GROUP_COLS, PAGE_SIZE),
                            lambda b, p, pt, i=i: (layer, pt[b, jnp.minimum(p, n_steps - 1) * pps + i], blk, 0))

    grid_spec = pltpu.PrefetchScalarGridSpec(
        num_scalar_prefetch=1,
        grid=(n_seq, n_steps + 1),
        in_specs=[pl.BlockSpec((None, GROUP_COLS, HT_LANES), lambda b, p, pt: (b, 0, 0)),
                  pl.BlockSpec((None, HT_LANES, GROUP_COLS), lambda b, p, pt: (b, 0, 0))]
        + [page_spec(2, i) for i in range(pps)] + [page_spec(3, i) for i in range(pps)]
        + [
            pl.BlockSpec((per_seq, GROUP_COLS), lambda b, p, pt: (row_block0 + b, 2)),
            pl.BlockSpec((per_seq, GROUP_COLS), lambda b, p, pt: (row_block0 + b, 3)),
            pl.BlockSpec((None, SEL_PAD, HT_LANES), lambda b, p, pt: (b, 0, 0)),
            pl.BlockSpec((None, 3, 1, HT_LANES), lambda b, p, pt: (b, 0, 0, 0)),
            pl.BlockSpec((4, PAGE_SIZE, HT_LANES), lambda b, p, pt: (0, 0, 0)),
        ],
        out_specs=pl.BlockSpec((per_seq, D_MODEL), lambda b, p, pt: (b, 0)),
        scratch_shapes=[
            pltpu.VMEM((GROUP_COLS, HT_LANES), BF16),
            pltpu.VMEM((HT_LANES, GROUP_COLS), BF16),
            pltpu.VMEM((1, 1, HT_LANES), F32),
            pltpu.VMEM((1, 1, HT_LANES), F32),
            pltpu.VMEM((1, GROUP_COLS, HT_LANES), F32),
        ],
    )
    return pl.pallas_call(
        kern,
        grid_spec=grid_spec,
        out_shape=jax.ShapeDtypeStruct((n_seq * per_seq, D_MODEL), F32),
        compiler_params=_cparams("parallel", "arbitrary"),
        name="nsa_dec_sel",
    )(page_table, q_bd, q_bd.transpose(0, 2, 1), *([pool_t] * (2 * pps)), kv_new, kv_new, sel, gate_rows, bias)


def _nsa_dec_win_fm_kernel(qbd_ref, qbdt_ref, wt_ref, wn_ref, g_ref, bias_ref, o_ref, ws_ref, m_ref, l_ref, acc_ref, *,
                           per_seq):
    n_past = wt_ref.shape[1]
    n_tiles = n_past // PAGE_SIZE
    qz = (qbd_ref[...].astype(F32) * HEAD_DIM ** -0.5).astype(BF16)
    qzt = (qbdt_ref[...].astype(F32) * HEAD_DIM ** -0.5).astype(BF16)
    _reset_softmax(m_ref, l_ref, acc_ref)
    krow = lax.broadcasted_iota(jnp.int32, (PAGE_SIZE, HT_LANES), 0)
    tq = _mod_pow2(lax.broadcasted_iota(jnp.int32, (PAGE_SIZE, HT_LANES), 1), per_seq)
    s_qm = _dot(qzt, wt_ref[:GROUP_COLS, :].astype(BF16))
    for w in range(n_tiles):
        keys = slice(w * PAGE_SIZE, (w + 1) * PAGE_SIZE)
        mk = (krow > tq) if w == 0 else None
        bias = bias_ref[0] if w == n_tiles - 1 else bias_ref[3]
        v_t = wt_ref[GROUP_COLS:, keys].astype(BF16)
        _dec_update(s_qm[:, keys].T + bias, mk, lambda pr, v_t=v_t: _dot(v_t, pr), m_ref, l_ref, acc_ref)
    pad = PAGE_SIZE - per_seq
    new = jnp.concatenate([wn_ref[...], jnp.zeros((pad, 2 * GROUP_COLS), F32)], axis=0)
    s_t = _dot(new[:, :GROUP_COLS].astype(BF16), qz) + bias_ref[1]
    _dec_update(s_t, krow <= tq, _pv_key_major(new[:, GROUP_COLS:].astype(BF16)), m_ref, l_ref, acc_ref)
    o_ref[...] = _dec_finish(m_ref, l_ref, acc_ref, g_ref[2], per_seq)
    shifted = pltpu.roll(wt_ref[...], n_past - per_seq, 1)
    ws_ref[...] = shifted
    lane = lax.broadcasted_iota(jnp.int32, (2 * GROUP_COLS, PAGE_SIZE), 1)
    new_t = pltpu.roll(new.T, PAGE_SIZE - per_seq, 1)
    ws_ref[:, n_past - PAGE_SIZE:] = jnp.where(lane >= PAGE_SIZE - per_seq, new_t, shifted[:, n_past - PAGE_SIZE:])


def nsa_dec_win_fm(q_bd, win_t, layer, win_new, gate_rows, bias, row_block0, per_seq):
    _, n_seq, n_feat, n_past = win_t.shape
    assert n_past == WINDOW and n_past % PAGE_SIZE == 0 and PAGE_SIZE + 1 >= T5_SATURATION_DIST
    kern = functools.partial(_nsa_dec_win_fm_kernel, per_seq=per_seq)
    return pl.pallas_call(
        kern,
        grid=(n_seq,),
        in_specs=[
            pl.BlockSpec((None, GROUP_COLS, HT_LANES), lambda b: (b, 0, 0)),
            pl.BlockSpec((None, HT_LANES, GROUP_COLS), lambda b: (b, 0, 0)),
            pl.BlockSpec((None, None, n_feat, n_past), lambda b: (layer, b, 0, 0)),
            pl.BlockSpec((per_seq, n_feat), lambda b: (row_block0 + b, 0)),
            pl.BlockSpec((None, 3, 1, HT_LANES), lambda b: (b, 0, 0, 0)),
            pl.BlockSpec((4, PAGE_SIZE, HT_LANES), lambda b: (0, 0, 0)),
        ],
        out_specs=[
            pl.BlockSpec((per_seq, D_MODEL), lambda b: (b, 0)),
            pl.BlockSpec((None, n_feat, n_past), lambda b: (b, 0, 0)),
        ],
        out_shape=[
            jax.ShapeDtypeStruct((n_seq * per_seq, D_MODEL), F32),
            jax.ShapeDtypeStruct((n_seq, n_feat, n_past), F32),
        ],
        scratch_shapes=[
            pltpu.VMEM((1, 1, HT_LANES), F32),
            pltpu.VMEM((1, 1, HT_LANES), F32),
            pltpu.VMEM((1, GROUP_COLS, HT_LANES), F32),
        ],
        compiler_params=_cparams("parallel"),
        name="nsa_dec_win",
    )(q_bd, q_bd.transpose(0, 2, 1), win_t, win_new, gate_rows, bias)


def nsa_decode_fm(table, page_table, pool_t, win_t, layer, q_rows, kv_new, win_new, gates_rows, cw, row_block0, pos0,
                  per_seq):
    n_seq, n_pages = page_table.shape
    pe_t, w1_bd, w2_bd = cw
    past = n_pages * PAGE_SIZE
    s_total = past + per_seq
    n_cmp = (s_total - CMP_LEN) // CMP_STRIDE + 1
    n_sel = -(-s_total // SEL_LEN)
    assert n_cmp == past // CMP_STRIDE - 1 and n_sel == past // SEL_LEN + 1 and per_seq <= CMP_STRIDE
    assert pos0 == past and pos0 % SEL_LEN == 0 and pos0 >= WINDOW and past // CMP_STRIDE == PAGE_SIZE
    bias = nsa_dec_bias(table, pos0, per_seq)
    ctok = nsa_dec_compress(nsa_dec_gather_fm(page_table, pool_t, layer), pe_t, w1_bd, w2_bd).astype(BF16)
    rows_per_seq = past // CMP_STRIDE
    k_cmp = ctok[0].reshape(n_seq, rows_per_seq, GROUP_COLS)
    v_cmp = ctok[1].reshape(n_seq, rows_per_seq, GROUP_COLS)
    q_bd = group_block_diag_queries(q_rows, n_seq, per_seq)
    gate_rows = gates_rows.reshape(n_seq, per_seq, 3, N_HEADS).transpose(0, 2, 3, 1).reshape(n_seq, 3, 1, HT_LANES)
    o_cmp, sel = nsa_dec_cmp(q_bd, k_cmp, v_cmp, gate_rows, bias, n_cmp, n_sel, pos0, per_seq)
    o_slc = nsa_dec_sel_fm(page_table, q_bd, pool_t, layer, kv_new, sel, gate_rows, bias, row_block0, per_seq)
    o_win, win_state_t = nsa_dec_win_fm(q_bd, win_t, layer, win_new, gate_rows, bias, row_block0, per_seq)
    return o_cmp, o_slc, o_win, win_state_t


GLA_SUB = 16
GLA_BLOCK = 512


def _gla_kernel(q_ref, k_ref, v_ref, r_ref, a_ref, wa_ref, ba_ref, ng_ref, s0_ref, o_ref, sfin_ref, s_ref, *,
                n_chunks, chunk):
    t_blk = pl.program_id(2)

    @pl.when(t_blk == 0)
    def _():
        s_ref[...] = s0_ref[...]

    tri = _tril_bf16(chunk)
    it = lax.broadcasted_iota(jnp.int32, (chunk, chunk), 0)
    js = lax.broadcasted_iota(jnp.int32, (chunk, chunk), 1)
    same_sub = jnp.right_shift(it, 4) == jnp.right_shift(js, 4)
    n_sub = chunk // GLA_SUB

    def chunk_step(c, carry):
        r0 = pl.multiple_of(c * chunk, chunk)
        rows = pl.ds(r0, chunk)
        q = q_ref[rows, :] * GLA_DK ** -0.5
        k = k_ref[rows, :]
        v = v_ref[rows, :].astype(BF16)
        g = _log_sigmoid(_dot(a_ref[rows, :].astype(BF16), wa_ref[...]) + ba_ref[...]) / GLA_TAU
        b = _cumsum_rows(g, tri)
        s_old = s_ref[...]
        o = _dot((q * jnp.exp(b)).astype(BF16), s_old.astype(BF16))
        a_mat = jnp.zeros((chunk, chunk), F32)
        for d in range(GLA_SUB):
            if d == 0:
                val = jnp.sum(q * k, axis=-1, keepdims=True)
            else:
                k_sh = pltpu.roll(k, d, 0)
                b_sh = pltpu.roll(b, d, 0)
                val = jnp.sum(q * k_sh * jnp.exp(jnp.minimum(b - b_sh, 0.0)), axis=-1, keepdims=True)
            a_mat = a_mat + jnp.where((it - js == d) & same_sub, val, 0.0)
        if n_sub > 1:
            blocks = [jnp.zeros((GLA_SUB, chunk), F32)]
            for a in range(1, n_sub):
                ref_b = b[a * GLA_SUB - 1:a * GLA_SUB, :]
                qa = q[a * GLA_SUB:(a + 1) * GLA_SUB] * jnp.exp(b[a * GLA_SUB:(a + 1) * GLA_SUB] - ref_b)
                kk = k * jnp.exp(jnp.minimum(ref_b - b, 0.0))
                blocks.append(_dot_nt(qa.astype(BF16), kk.astype(BF16)))
            off = jnp.concatenate(blocks, axis=0)
            a_mat = a_mat + jnp.where(jnp.right_shift(js, 4) < jnp.right_shift(it, 4), off, 0.0)
        o = o + _dot(a_mat.astype(BF16), v)
        b_last = b[chunk - 1:chunk, :]
        kd = k * jnp.exp(b_last - b)
        pad_k = jnp.zeros((GLA_DK - chunk, GLA_DK), F32)
        pad_v = jnp.zeros((GLA_DK - chunk, GLA_DV), BF16)
        kd_t = jnp.concatenate([kd, pad_k], axis=0).T
        dec = jnp.broadcast_to(jnp.exp(b_last), (GLA_DK, GLA_DK)).T
        s_ref[...] = (s_old * jnp.concatenate([dec] * (GLA_DV // GLA_DK), axis=1)
                      + _dot(kd_t.astype(BF16), jnp.concatenate([v, pad_v], axis=0)))
        y = o * lax.rsqrt(jnp.mean(o * o, axis=-1, keepdims=True) + EPS) * ng_ref[...]
        rr = r_ref[rows, :]
        o_ref[rows, :] = y * (rr * _sigmoid(rr))
        return carry

    lax.fori_loop(0, n_chunks, chunk_step, 0)

    @pl.when(t_blk == pl.num_programs(2) - 1)
    def _():
        sfin_ref[...] = s_ref[...]


def gla_prompt(q, k, v, r, a_lr, w_a2, b_a, norm_g, s0, n_batch, seq, row_block0=0):
    chunk = GLA_CHUNK if seq % GLA_CHUNK == 0 else seq
    blk = min(GLA_BLOCK, seq)
    nt = seq // blk
    m = n_batch * seq
    kern = functools.partial(_gla_kernel, n_chunks=blk // chunk, chunk=chunk)
    row = lambda b, h, t: row_block0 + b * nt + t
    return pl.pallas_call(
        kern,
        grid=(n_batch, GLA_HEADS, nt),
        in_specs=[
            pl.BlockSpec((blk, GLA_DK), lambda b, h, t: (row(b, h, t), h)),
            pl.BlockSpec((blk, GLA_DK), lambda b, h, t: (row(b, h, t), h)),
            pl.BlockSpec((blk, GLA_DV), lambda b, h, t: (row(b, h, t), h)),
            pl.BlockSpec((blk, GLA_DV), lambda b, h, t: (row(b, h, t), h)),
            pl.BlockSpec((blk, GLA_RANK), lambda b, h, t: (row(b, h, t), 0)),
            pl.BlockSpec((GLA_RANK, GLA_DK), lambda b, h, t: (0, h)),
            pl.BlockSpec((1, GLA_DK), lambda b, h, t: (0, h)),
            pl.BlockSpec((1, GLA_DV), lambda b, h, t: (0, 0)),
            pl.BlockSpec((None, None, GLA_DK, GLA_DV), lambda b, h, t: (b, h, 0, 0)),
        ],
        out_specs=[
            pl.BlockSpec((blk, GLA_DV), lambda b, h, t: (b * nt + t, h)),
            pl.BlockSpec((None, None, GLA_DK, GLA_DV), lambda b, h, t: (b, h, 0, 0)),
        ],
        out_shape=[
            jax.ShapeDtypeStruct((m, D_MODEL), F32),
            jax.ShapeDtypeStruct((n_batch, GLA_HEADS, GLA_DK, GLA_DV), F32),
        ],
        scratch_shapes=[pltpu.VMEM((GLA_DK, GLA_DV), F32)],
        compiler_params=_cparams("parallel", "parallel", "arbitrary"),
        name="gla",
    )(q, k, v, r, a_lr, w_a2.astype(BF16), b_a.reshape(1, GLA_QK), norm_g.reshape(1, GLA_DV), s0)


def _gla_decode_kernel(q_ref, k_ref, v_ref, r_ref, a_ref, wa_ref, ba_ref, ng_ref, s0_ref, o_ref, s_ref, *, n_rows):
    pad_rows = 2 * n_rows

    def pad_to(x, rows):
        return jnp.concatenate([x, jnp.zeros((rows - x.shape[0], x.shape[1]), x.dtype)], axis=0)

    row = lax.broadcasted_iota(jnp.int32, (n_rows, GLA_DK), 0)
    row_v = lax.broadcasted_iota(jnp.int32, (n_rows, GLA_DV), 0)
    q = q_ref[...] * GLA_DK ** -0.5
    k = k_ref[...]
    v = v_ref[...]
    g = _dot(pad_to(a_ref[...], pad_rows).astype(BF16), wa_ref[...])[:n_rows]
    g = _log_sigmoid(g + ba_ref[...]) / GLA_TAU
    b = g
    shift = 1
    while shift < n_rows:
        b = b + jnp.where(row >= shift, pltpu.roll(b, shift, 0), 0.0)
        shift *= 2
    s_old = s0_ref[...]
    o = _dot(pad_to(q * jnp.exp(b), pad_rows).astype(BF16), s_old.astype(BF16))[:n_rows]
    for d in range(n_rows):
        if d == 0:
            val = jnp.sum(q * k, axis=-1, keepdims=True)
            o = o + val * v
        else:
            k_sh = pltpu.roll(k, d, 0)
            b_sh = pltpu.roll(b, d, 0)
            val = jnp.sum(q * k_sh * jnp.exp(jnp.minimum(b - b_sh, 0.0)), axis=-1, keepdims=True)
            o = o + jnp.where(row_v >= d, val * pltpu.roll(v, d, 0), 0.0)
    b_last = b[n_rows - 1:n_rows, :]
    kd_t = pad_to(k * jnp.exp(b_last - b), GLA_DK).T
    dec = jnp.broadcast_to(jnp.exp(b_last), (GLA_DK, GLA_DK)).T
    s_ref[...] = (s_old * jnp.concatenate([dec] * (GLA_DV // GLA_DK), axis=1)
                  + _dot(kd_t.astype(BF16), pad_to(v, GLA_DK).astype(BF16)))
    y = o * lax.rsqrt(jnp.mean(o * o, axis=-1, keepdims=True) + EPS) * ng_ref[...]
    rr = r_ref[...]
    o_ref[...] = y * (rr * _sigmoid(rr))


def gla_decode(q, k, v, r, a_lr, w_a2, b_a, norm_g, s0, n_seq, per_seq, row_block0):
    assert per_seq % GLA_CHUNK != 0 and per_seq % 8 == 0
    kern = functools.partial(_gla_decode_kernel, n_rows=per_seq)
    row = lambda b, h: (row_block0 + b, h)
    return pl.pallas_call(
        kern,
        grid=(n_seq, GLA_HEADS),
        in_specs=[
            pl.BlockSpec((per_seq, GLA_DK), row),
            pl.BlockSpec((per_seq, GLA_DK), row),
            pl.BlockSpec((per_seq, GLA_DV), row),
            pl.BlockSpec((per_seq, GLA_DV), row),
            pl.BlockSpec((per_seq, GLA_RANK), lambda b, h: (row_block0 + b, 0)),
            pl.BlockSpec((GLA_RANK, GLA_DK), lambda b, h: (0, h)),
            pl.BlockSpec((1, GLA_DK), lambda b, h: (0, h)),
            pl.BlockSpec((1, GLA_DV), lambda b, h: (0, 0)),
            pl.BlockSpec((None, None, GLA_DK, GLA_DV), lambda b, h: (b, h, 0, 0)),
        ],
        out_specs=[
            pl.BlockSpec((per_seq, GLA_DV), lambda b, h: (b, h)),
            pl.BlockSpec((None, None, GLA_DK, GLA_DV), lambda b, h: (b, h, 0, 0)),
        ],
        out_shape=[
            jax.ShapeDtypeStruct((n_seq * per_seq, D_MODEL), F32),
            jax.ShapeDtypeStruct((n_seq, GLA_HEADS, GLA_DK, GLA_DV), F32),
        ],
        compiler_params=_cparams("parallel", "parallel"),
        name="gla_decode",
    )(q, k, v, r, a_lr, w_a2.astype(BF16), b_a.reshape(1, GLA_QK), norm_g.reshape(1, GLA_DV), s0)


def masked_softmax(logits, mask, axis):
    logits = jnp.where(mask, logits.astype(jnp.float32), NEG)
    m = jnp.max(logits, axis=axis, keepdims=True)
    e = jnp.exp(logits - m) * mask
    return e / jnp.maximum(jnp.sum(e, axis=axis, keepdims=True), 1e-30)


def t5_bucket(dist):
    n = jnp.maximum(dist, 0)
    max_exact = N_BUCKETS // 2
    nf = jnp.maximum(n, 1).astype(jnp.float32)
    log_part = max_exact + (jnp.log(nf / max_exact) / math.log(MAX_DISTANCE / max_exact)
                            * (N_BUCKETS - max_exact)).astype(jnp.int32)
    return jnp.where(n < max_exact, n, jnp.minimum(log_part, N_BUCKETS - 1))


def head_bias(dist, table):
    b = table.astype(jnp.float32)[t5_bucket(dist)]
    return b.reshape(*dist.shape, NSA_KV_GROUPS, NSA_HPG).transpose(2, 3, 0, 1)


def gather_pages(pool, page_table):
    rows = pool[page_table]
    return rows.reshape(page_table.shape[0], -1, *pool.shape[2:])


def query_blocks(fn, args, axes):
    T = args[0].shape[axes[0]]
    if T % Q_BLOCK != 0 or T <= Q_BLOCK:
        return fn(*args)
    nb = T // Q_BLOCK

    def split(a, ax):
        a = a.reshape(a.shape[:ax] + (nb, Q_BLOCK) + a.shape[ax + 1:])
        return jnp.moveaxis(a, ax, 0)

    out = lax.map(lambda blk: fn(*blk), tuple(split(a, ax) for a, ax in zip(args, axes)))
    out = jnp.moveaxis(out, 0, 1)
    return out.reshape(out.shape[:1] + (T,) + out.shape[3:])


def compress_tokens(kv, pe, w1, w2):
    B, S = kv.shape[:2]
    n_cmp = (S - CMP_LEN) // CMP_STRIDE + 1
    idx = jnp.arange(n_cmp)[:, None] * CMP_STRIDE + jnp.arange(CMP_LEN)[None, :]
    blocks = kv[:, idx] + pe[None, None, :, None, :]
    z = blocks.transpose(0, 1, 3, 2, 4).reshape(B, n_cmp, NSA_KV_GROUPS, CMP_LEN * HEAD_DIM)
    return jax.nn.gelu(z @ w1) @ w2


def gqa_window_attend(q, k, v, q_pos, k_pos, table):
    B, Tq = q.shape[:2]
    qg = q.reshape(B, Tq, NSA_KV_GROUPS, NSA_HPG, HEAD_DIM)
    logits = jnp.einsum('bqghd,bkgd->bghqk', qg, k).astype(jnp.float32) * HEAD_DIM ** -0.5
    dist = q_pos[:, None] - k_pos[None, :]
    mask = (dist >= 0) & (dist < WINDOW) & (k_pos[None, :] >= 0)
    p = masked_softmax(logits + head_bias(dist, table), mask, -1)
    o = jnp.einsum('bghqk,bkgd->bqghd', p, v)
    return o.reshape(B, Tq, N_HEADS, HEAD_DIM)


def nsa_core_sample(q, kv_new, win_new, gates, phi_pe, phi_w1, phi_w2, table, past_kv, past_win, pos0):
    B, T = q.shape[:2]
    G = NSA_KV_GROUPS
    gates = jax.nn.sigmoid(gates).reshape(B, T, 3, N_HEADS)
    kv = jnp.concatenate([past_kv, kv_new], axis=1)
    S = kv.shape[1]
    q_pos = pos0 + jnp.arange(T)
    qg = q.reshape(B, T, G, NSA_HPG, HEAD_DIM)
    k_cmp = compress_tokens(kv[:, :, 0], phi_pe[0], phi_w1[0], phi_w2[0])
    v_cmp = compress_tokens(kv[:, :, 1], phi_pe[1], phi_w1[1], phi_w2[1])
    n_cmp = k_cmp.shape[1]
    c_start = jnp.arange(n_cmp) * CMP_STRIDE
    dist_c = q_pos[:, None] - (c_start + CMP_LEN - 1)[None, :]
    logits_c = jnp.einsum('bqghd,bcgd->bghqc', qg, k_cmp).astype(jnp.float32) * HEAD_DIM ** -0.5
    p_cmp = masked_softmax(logits_c + head_bias(dist_c, table), dist_c >= 0, -1)
    o_cmp = jnp.einsum('bghqc,bcgd->bqghd', p_cmp, v_cmp).reshape(B, T, N_HEADS, HEAD_DIM)
    n_sel = -(-S // SEL_LEN)
    s_start = jnp.arange(n_sel) * SEL_LEN
    incidence = ((c_start[:, None] < s_start[None, :] + SEL_LEN)
                 & (c_start[:, None] + CMP_LEN > s_start[None, :])).astype(jnp.float32)
    imp = jnp.einsum('bghqc,cj->bgqj', p_cmp, incidence)
    cur = q_pos // SEL_LEN
    j = jnp.arange(n_sel)
    forced = (j[None, :] == 0) | (j[None, :] == cur[:, None]) | (j[None, :] == cur[:, None] - 1)
    avail = j[None, :] <= cur[:, None]
    score = jnp.where(forced, BIG, jnp.where(avail, imp, -BIG))
    _, sel = lax.top_k(score, min(N_SELECT, n_sel))
    member = jnp.sum(jax.nn.one_hot(sel, n_sel, dtype=jnp.float32), axis=-2) > 0.5
    k_pos = jnp.arange(S)
    key_member = jnp.take(member, k_pos // SEL_LEN, axis=-1)
    dist = q_pos[:, None] - k_pos[None, :]
    mask = key_member[:, :, None] & (dist >= 0)[None, None, None]
    logits = jnp.einsum('bqghd,bkgd->bghqk', qg, kv[:, :, 2]).astype(jnp.float32) * HEAD_DIM ** -0.5
    p = masked_softmax(logits + head_bias(dist, table), mask, -1)
    o_slc = jnp.einsum('bghqk,bkgd->bqghd', p, kv[:, :, 3]).reshape(B, T, N_HEADS, HEAD_DIM)
    win_all = jnp.concatenate([past_win, win_new], axis=1)
    lw = win_all.shape[1]
    kw_pos = pos0 + T - lw + jnp.arange(lw)
    o_win = gqa_window_attend(q, win_all[:, :, 0], win_all[:, :, 1], q_pos, kw_pos, table)
    win_state = win_all[:, win_all.shape[1] - min(WINDOW, win_all.shape[1]):]
    o = (gates[:, :, 0, :, None] * o_cmp + gates[:, :, 1, :, None] * o_slc + gates[:, :, 2, :, None] * o_win)
    return o.reshape(B * T, D_MODEL), win_state


def fox_attend(q, dq, q_pos, k, v, dk, k_pos):
    logits = jnp.einsum('bqhd,bkhd->bhqk', q, k).astype(jnp.float32) * HEAD_DIM ** -0.5
    logits = logits + dq.transpose(0, 2, 1)[..., None] - dk.transpose(0, 2, 1)[:, :, None, :]
    p = masked_softmax(logits, q_pos[:, None] >= k_pos[None, :], -1)
    return jnp.einsum('bhqk,bkhd->bqhd', p, v)


def fox_core(q, kv_new, f_raw, b_f, past_kv, past_logf, pos0):
    B, T = q.shape[:2]
    logf_new = jax.nn.log_sigmoid(f_raw + b_f.astype(jnp.float32))
    if past_kv is None:
        kv, logf = kv_new, logf_new
    else:
        kv = jnp.concatenate([past_kv, kv_new], axis=1)
        logf = jnp.concatenate([past_logf.astype(jnp.float32), logf_new], axis=1)
    S = kv.shape[1]
    cum = jnp.cumsum(logf, axis=1)
    k_pos = jnp.arange(S)
    q_pos = pos0 + jnp.arange(T)
    k, v = kv[:, :, 0], kv[:, :, 1]
    o = query_blocks(lambda q_, dq_, p_: fox_attend(q_, dq_, p_, k, v, cum, k_pos), [q, cum[:, S - T:], q_pos], [1, 1, 0])
    return o.reshape(B * T, D_MODEL), logf_new


def gla_chunk(S, q, k, v, g):
    C = q.shape[1]
    b = jnp.cumsum(g, axis=1)
    causal = jnp.tril(jnp.ones((C, C), dtype=bool))
    diff = b[:, :, None] - b[:, None, :]
    decay = jnp.exp(jnp.where(causal[None, :, :, None, None], diff, NEG))
    A = jnp.einsum('bthd,bshd,btshd->bhts', q, k, decay)
    o = jnp.einsum('bhts,bshv->bthv', A, v) + jnp.einsum('bthd,bhdv->bthv', q * jnp.exp(b), S)
    b_last = b[:, -1]
    S_new = jnp.exp(b_last)[..., None] * S + jnp.einsum('bshd,bshv->bhdv', k * jnp.exp(b_last[:, None] - b), v)
    return o, S_new


def gla_scan(S0, q, k, v, g):
    B, T = q.shape[:2]
    c = GLA_CHUNK if T % GLA_CHUNK == 0 else T
    nc = T // c

    def to_chunks(a):
        return jnp.moveaxis(a.reshape(B, nc, c, *a.shape[2:]), 1, 0)

    def step(S, blk):
        o, S = gla_chunk(S, *blk)
        return S, o

    S_fin, o = lax.scan(step, S0, tuple(to_chunks(a) for a in (q, k, v, g)))
    return jnp.moveaxis(o, 0, 1).reshape(B, T, GLA_HEADS, GLA_DV), S_fin


def gla_core(q, k, v, r, a_lr, w_a2, b_a, norm_g, S0):
    B, T = q.shape[:2]
    q = q.reshape(B, T, GLA_HEADS, GLA_DK) * GLA_DK ** -0.5
    k = k.reshape(B, T, GLA_HEADS, GLA_DK)
    v = v.reshape(B, T, GLA_HEADS, GLA_DV)
    g = jax.nn.log_sigmoid(a_lr @ w_a2.astype(jnp.float32) + b_a.astype(jnp.float32)) / GLA_TAU
    g = g.reshape(B, T, GLA_HEADS, GLA_DK)
    if S0 is None:
        S0 = jnp.zeros((B, GLA_HEADS, GLA_DK, GLA_DV), jnp.float32)
    o, S_fin = gla_scan(S0.astype(jnp.float32), q, k, v, g)
    xf = o
    y = xf * lax.rsqrt(jnp.mean(xf * xf, axis=-1, keepdims=True) + EPS) * norm_g
    o = y.reshape(B, T, D_MODEL) * jax.nn.silu(r)
    return o.reshape(B * T, D_MODEL), S_fin


_C1 = D_MODEL + NSA_KV_COLS
_C2 = _C1 + NSA_WIN_COLS
NSA_SPLITS = ((D_MODEL, _C1, F32), (D_MODEL, _C1, BF16), (_C1, _C2, F32), (_C1, _C2, BF16))
NSA_SPLITS_T = ((0, D_MODEL, BF16), (_C1 - GROUP_COLS, _C1, BF16), (_C2 - GROUP_COLS, _C2, BF16),
                (_C2, _C2 + 3 * N_HEADS, F32))
FOX_SPLITS = ((D_MODEL, 3 * D_MODEL, F32), (D_MODEL, 2 * D_MODEL, BF16), (3 * D_MODEL, 3 * D_MODEL + N_HEADS, F32))
FOX_SPLITS_T = ((0, D_MODEL, BF16), (2 * D_MODEL, 3 * D_MODEL, BF16))
GLA_SPLITS = ((0, GLA_QK, F32), (GLA_QK, 2 * GLA_QK, F32), (2 * GLA_QK, 2 * GLA_QK + D_MODEL, F32),
              (2 * GLA_QK + D_MODEL, 2 * GLA_QK + 2 * D_MODEL, F32),
              (2 * GLA_QK + 2 * D_MODEL, 2 * GLA_QK + 2 * D_MODEL + GLA_RANK, F32))


def kernel(x_prompt, x_sample, cache_nsa_kv, state_nsa_win, cache_fox_kv, cache_fox_logf, state_gla, page_table, rel_bias, norm_g, final_g, nsa_w_in, nsa_phi_pe, nsa_phi_w1, nsa_phi_w2, nsa_w_out, fox_w_in, fox_b_f, fox_w_out, gla_w_in, gla_w_a2, gla_b_a, gla_norm_g, gla_w_out, mlp_w1, mlp_w2):
    bp, tp, _ = x_prompt.shape
    bs, ts, _ = x_sample.shape
    mp, ms = bp * tp, bs * ts
    pos_s = page_table.shape[1] * PAGE_SIZE
    x = jnp.concatenate([x_prompt.reshape(mp, D_MODEL), x_sample.reshape(ms, D_MODEL)], axis=0)
    toep = t5_tiles(rel_bias, ATT_TILE)
    nsa_pool_t = feature_major_pages(cache_nsa_kv)
    nsa_win_t = feature_major_pages(state_nsa_win)
    fox_pool_t = feature_major_pages(cache_fox_kv)
    fox_lf_t = feature_major_pages(cache_fox_logf)

    def both(a, shape_p, shape_s):
        return a[:mp].reshape(shape_p), a[mp:].reshape(shape_s)

    nsa_kv_p, nsa_kv_s, nsa_win_p, nsa_win_s = [], [], [], []
    fox_kv_p, fox_kv_s, fox_lf_p, fox_lf_s, gla_p, gla_s = [], [], [], [], [], []
    G = NSA_KV_GROUPS
    for i in range(DEPTH):
        kind, li = i % N_MIXERS, i // N_MIXERS
        if kind == 0:
            kvn, kv_bf, winn, win_bf, q_t, vslc_t, vwin_t, gates_t = norm_proj(
                x, norm_g[i, 0], nsa_w_in[li], NSA_SPLITS, NSA_SPLITS_T)
            cw = compress_weights(nsa_phi_pe[li], nsa_phi_w1[li], nsa_phi_w2[li])
            o_parts_p = nsa_prompt_t(rel_bias, toep, q_t, kvn, kv_bf, win_bf, vslc_t, vwin_t, gates_t, cw, bp, tp)
            kv_p, kv_s = both(kvn, (bp, tp, 4, G, HEAD_DIM), (bs, ts, 4, G, HEAD_DIM))
            w_p, w_s = both(winn, (bp, tp, 2, G, HEAD_DIM), (bs, ts, 2, G, HEAD_DIM))
            *o_parts_s, ws_t = nsa_decode_fm(rel_bias, page_table, nsa_pool_t, nsa_win_t, li, q_t[:, mp:].T, kvn, winn,
                                             gates_t[:, mp:].T, cw, mp // ts, pos_s, ts)
            ws_p = w_p[:, tp - min(WINDOW, tp):]
            ws_s = ws_t.reshape(bs, 2, G, HEAD_DIM, -1).transpose(0, 4, 1, 2, 3)
            nsa_kv_p.append(kv_p); nsa_kv_s.append(kv_s); nsa_win_p.append(ws_p); nsa_win_s.append(ws_s)
            o_parts_p = list(o_parts_p)
            w_out = nsa_w_out[li]
        elif kind == 1:
            kvn, k_bf, fr, q_t, v_t = norm_proj(x, norm_g[i, 0], fox_w_in[li], FOX_SPLITS, FOX_SPLITS_T)
            kv_p, kv_s = both(kvn, (bp, tp, 2, N_HEADS, HEAD_DIM), (bs, ts, 2, N_HEADS, HEAD_DIM))
            lf_all, d_all = fox_prep_prompt(fr, fox_b_f[li], bp, tp)
            o_p = fox_attn_t(q_t, k_bf, v_t, d_all, bp, tp)
            lf_p = lf_all.reshape(bp, tp, N_HEADS)
            q_bd = block_diag_queries(q_t[:, mp:].T, bs, ts, N_HEADS)
            o_s, lf_s = fox_decode_fm(page_table, q_bd, fox_pool_t, fox_lf_t, li, kvn, fr, fox_b_f[li], mp // ts,
                                      bs, ts)
            lf_s = lf_s.reshape(bs, ts, N_HEADS)
            fox_kv_p.append(kv_p); fox_kv_s.append(kv_s); fox_lf_p.append(lf_p); fox_lf_s.append(lf_s)
            o_parts_p, o_parts_s = [o_p], [o_s]
            w_out = fox_w_out[li]
        else:
            q, k, v, r, a = norm_proj(x, norm_g[i, 0], gla_w_in[li], GLA_SPLITS)
            s_zero = jnp.zeros((bp, GLA_HEADS, GLA_DK, GLA_DV), F32)
            o_p, s_p = gla_prompt(q, k, v, r, a, gla_w_a2[li], gla_b_a[li], gla_norm_g[li], s_zero, bp, tp)
            o_s, s_s = gla_decode(q, k, v, r, a, gla_w_a2[li], gla_b_a[li], gla_norm_g[li], state_gla[li], bs, ts,
                                  mp // ts)
            gla_p.append(s_p); gla_s.append(s_s)
            o_parts_p, o_parts_s = [o_p], [o_s]
            w_out = gla_w_out[li]
        x = out_mlp(x, o_parts_p, o_parts_s, w_out.astype(BF16), norm_g[i, 1], mlp_w1[i].astype(BF16),
                    mlp_w2[i].astype(BF16))
    y = final_norm(x, final_g)
    y_p, y_s = both(y, (bp, tp, D_MODEL), (bs, ts, D_MODEL))
    return (y_p, y_s, jnp.stack(nsa_kv_p), jnp.stack(nsa_kv_s), jnp.stack(nsa_win_p), jnp.stack(nsa_win_s),
            jnp.stack(fox_kv_p), jnp.stack(fox_kv_s), jnp.stack(fox_lf_p), jnp.stack(fox_lf_s),
            jnp.stack(gla_p), jnp.stack(gla_s))
```

```python
import functools
import math

import jax
import jax.numpy as jnp
from jax import lax
from jax.experimental import pallas as pl
from jax.experimental.pallas import tpu as pltpu

D_MODEL = 1024
DEPTH = 4
PAGE_SIZE = 128
N_MIXERS = 3
N_HEADS = 16
HEAD_DIM = D_MODEL // N_HEADS
NSA_KV_GROUPS = 4
NSA_HPG = N_HEADS // NSA_KV_GROUPS
CMP_LEN = 32
CMP_STRIDE = 16
CMP_HIDDEN = 2 * HEAD_DIM
SEL_LEN = 64
N_SELECT = 16
WINDOW = 512
N_BUCKETS = 32
MAX_DISTANCE = 128
GLA_HEADS = 4
GLA_DK = D_MODEL // 2 // GLA_HEADS
GLA_DV = D_MODEL // GLA_HEADS
GLA_QK = GLA_HEADS * GLA_DK
GLA_RANK = 16
GLA_TAU = 16.0
GLA_CHUNK = 64
D_FF = 4 * D_MODEL
Q_BLOCK = 128
EPS = 1e-6
NEG = -1e30
BIG = 1e9
TAKEN = -3e38
NSA_KV_COLS = 4 * NSA_KV_GROUPS * HEAD_DIM
NSA_WIN_COLS = 2 * NSA_KV_GROUPS * HEAD_DIM
GROUP_COLS = NSA_KV_GROUPS * HEAD_DIM

F32 = jnp.float32
BF16 = jnp.bfloat16

V7X_VMEM_LIMIT_BYTES = 48 * 1024 * 1024
V7X_LANES = 128
TOKEN_TILE = 512
FF_TILE = 1024
ATT_TILE = 256
GROUPS_PER_PASS = 4
SEL_PAD = V7X_LANES
T5_SATURATION_DIST = 113


def _cparams(*sem):
    return pltpu.CompilerParams(dimension_semantics=sem, vmem_limit_bytes=V7X_VMEM_LIMIT_BYTES)


def _rms(x, g):
    return x * lax.rsqrt(jnp.mean(x * x, axis=-1, keepdims=True) + EPS) * g


def _sigmoid(x):
    return 1.0 / (1.0 + jnp.exp(-x))


def _log_sigmoid(x):
    return jnp.minimum(x, 0.0) - jnp.log(1.0 + jnp.exp(-jnp.abs(x)))


def _gelu_tanh(x):
    return 0.5 * x * (1.0 + jnp.tanh(math.sqrt(2.0 / math.pi) * (x + 0.044715 * (x * x * x))))


def _dot(a, b):
    return jnp.dot(a, b, preferred_element_type=F32)


def _dot_nt(a, b):
    return lax.dot_general(a, b, (((1,), (1,)), ((), ())), preferred_element_type=F32)


def _split_bf16(x, parts):
    out = []
    for _ in range(parts):
        hi = x.astype(BF16)
        out.append(hi)
        x = x - hi.astype(F32)
    return out


def _t5_bucket(dist):
    n = jnp.maximum(dist, 0)
    max_exact = N_BUCKETS // 2
    nf = jnp.maximum(n, 1).astype(F32)
    log_part = max_exact + (jnp.log(nf / max_exact) / math.log(MAX_DISTANCE / max_exact)
                            * (N_BUCKETS - max_exact)).astype(jnp.int32)
    return jnp.where(n < max_exact, n, jnp.minimum(log_part, N_BUCKETS - 1))


def _bias_from_bucket(bucket, tbl_ref, h):
    val = jnp.full(bucket.shape, tbl_ref[0, h], F32)
    for k in range(1, N_BUCKETS):
        val = jnp.where(bucket == k, tbl_ref[k, h], val)
    return val


def _div_pow2(x, d):
    assert d & (d - 1) == 0
    return jnp.right_shift(x, d.bit_length() - 1)


def _mod_pow2(x, d):
    assert d & (d - 1) == 0
    return jnp.bitwise_and(x, d - 1)


def _smem_spec():
    return pl.BlockSpec(memory_space=pltpu.SMEM)


def _norm_proj_kernel(x_ref, g_ref, w_ref, wt_ref, *out_refs, splits, splits_t):
    h = _rms(x_ref[...], g_ref[...]).astype(BF16)
    for (lo, hi, _), o_ref in zip(splits, out_refs):
        o_ref[...] = _dot(h, w_ref[:, lo:hi]).astype(o_ref.dtype)
    for (lo, hi, _), o_ref in zip(splits_t, out_refs[len(splits):]):
        o_ref[...] = _dot_nt(wt_ref[lo:hi, :], h).astype(o_ref.dtype)


def norm_proj(x, g, w, splits, splits_t=()):
    m = x.shape[0]
    n = w.shape[1]
    w_bf = w.astype(BF16)
    kern = functools.partial(_norm_proj_kernel, splits=splits, splits_t=splits_t)
    return pl.pallas_call(
        kern,
        grid=(m // TOKEN_TILE,),
        in_specs=[
            pl.BlockSpec((TOKEN_TILE, D_MODEL), lambda i: (i, 0)),
            pl.BlockSpec((1, D_MODEL), lambda i: (0, 0)),
            pl.BlockSpec((D_MODEL, n), lambda i: (0, 0)),
            pl.BlockSpec((n, D_MODEL), lambda i: (0, 0)),
        ],
        out_specs=[pl.BlockSpec((TOKEN_TILE, hi - lo), lambda i: (i, 0)) for lo, hi, _ in splits]
        + [pl.BlockSpec((hi - lo, TOKEN_TILE), lambda i: (0, i)) for lo, hi, _ in splits_t],
        out_shape=[jax.ShapeDtypeStruct((m, hi - lo), dt) for lo, hi, dt in splits]
        + [jax.ShapeDtypeStruct((hi - lo, m), dt) for lo, hi, dt in splits_t],
        compiler_params=_cparams("parallel"),
        name="norm_proj",
    )(x, g.reshape(1, D_MODEL), w_bf, w_bf.T)


def _out_mlp_kernel(x_ref, *refs, n_o, n_first):
    oa_refs = refs[:n_o]
    ob_refs = refs[n_o:2 * n_o]
    wo_ref, g_ref, w1_ref, w2_ref, y_ref, x1_ref, h_ref, acc_ref = refs[2 * n_o:]
    i = pl.program_id(0)
    j = pl.program_id(1)

    @pl.when(j == 0)
    def _():
        first = i < n_first
        o = jnp.where(first, oa_refs[0][...], ob_refs[0][...])
        for ra, rb in zip(oa_refs[1:], ob_refs[1:]):
            o = o + jnp.where(first, ra[...], rb[...])
        x1 = x_ref[...] + _dot(o.astype(BF16), wo_ref[...])
        x1_ref[...] = x1
        h_ref[...] = _rms(x1, g_ref[...]).astype(BF16)
        acc_ref[...] = jnp.zeros_like(acc_ref)

    a = _dot(h_ref[...], w1_ref[...])
    a = jnp.square(jnp.maximum(a, 0.0)).astype(BF16)
    acc_ref[...] += _dot(a, w2_ref[...])

    @pl.when(j == pl.num_programs(1) - 1)
    def _():
        y_ref[...] = x1_ref[...] + acc_ref[...]


def out_mlp(x, o_first, o_rest, w_out, g, w1, w2):
    m = x.shape[0]
    n_o = len(o_first)
    n_first = o_first[0].shape[0] // TOKEN_TILE
    n_rest = o_rest[0].shape[0] // TOKEN_TILE
    assert o_first[0].shape[0] % TOKEN_TILE == 0 and n_first + n_rest == m // TOKEN_TILE
    kern = functools.partial(_out_mlp_kernel, n_o=n_o, n_first=n_first)
    tok = pl.BlockSpec((TOKEN_TILE, D_MODEL), lambda i, j: (i, 0))
    tok_a = pl.BlockSpec((TOKEN_TILE, D_MODEL), lambda i, j: (jnp.minimum(i, n_first - 1), 0))
    tok_b = pl.BlockSpec((TOKEN_TILE, D_MODEL), lambda i, j: (jnp.maximum(i - n_first, 0), 0))
    return pl.pallas_call(
        kern,
        grid=(m // TOKEN_TILE, D_FF // FF_TILE),
        in_specs=[tok] + [tok_a] * n_o + [tok_b] * n_o + [
            pl.BlockSpec((D_MODEL, D_MODEL), lambda i, j: (0, 0)),
            pl.BlockSpec((1, D_MODEL), lambda i, j: (0, 0)),
            pl.BlockSpec((D_MODEL, FF_TILE), lambda i, j: (0, j)),
            pl.BlockSpec((FF_TILE, D_MODEL), lambda i, j: (j, 0)),
        ],
        out_specs=tok,
        out_shape=jax.ShapeDtypeStruct((m, D_MODEL), F32),
        scratch_shapes=[
            pltpu.VMEM((TOKEN_TILE, D_MODEL), F32),
            pltpu.VMEM((TOKEN_TILE, D_MODEL), BF16),
            pltpu.VMEM((TOKEN_TILE, D_MODEL), F32),
        ],
        compiler_params=_cparams("parallel", "arbitrary"),
        name="out_mlp",
    )(x, *o_first, *o_rest, w_out, g.reshape(1, D_MODEL), w1, w2)


def _final_norm_kernel(x_ref, g_ref, y_ref):
    y_ref[...] = _rms(x_ref[...], g_ref[...])


def final_norm(x, g):
    m = x.shape[0]
    return pl.pallas_call(
        _final_norm_kernel,
        grid=(m // TOKEN_TILE,),
        in_specs=[pl.BlockSpec((TOKEN_TILE, D_MODEL), lambda i: (i, 0)),
                  pl.BlockSpec((1, D_MODEL), lambda i: (0, 0))],
        out_specs=pl.BlockSpec((TOKEN_TILE, D_MODEL), lambda i: (i, 0)),
        out_shape=jax.ShapeDtypeStruct((m, D_MODEL), F32),
        compiler_params=_cparams("parallel"),
        name="final_norm",
    )(x, g.reshape(1, D_MODEL))


def _t5_tiles_kernel(tbl_ref, o_ref, *, tile):
    h = pl.program_id(0)
    ik = lax.broadcasted_iota(jnp.int32, (tile, tile), 0)
    iq = lax.broadcasted_iota(jnp.int32, (tile, tile), 1)
    for dd in range(2):
        o_ref[dd] = _bias_from_bucket(_t5_bucket(dd * tile + iq - ik), tbl_ref, h) - tbl_ref[N_BUCKETS - 1, h]


def t5_tiles(table, tile):
    assert 2 * tile - (tile - 1) >= T5_SATURATION_DIST
    return pl.pallas_call(
        functools.partial(_t5_tiles_kernel, tile=tile),
        grid=(N_HEADS,),
        in_specs=[_smem_spec()],
        out_specs=pl.BlockSpec((None, 2, tile, tile), lambda h: (h, 0, 0, 0)),
        out_shape=jax.ShapeDtypeStruct((N_HEADS, 2, tile, tile), F32),
        compiler_params=_cparams("arbitrary"),
        name="t5_tiles",
    )(table)


def _compress_kernel(x_ref, pe_ref, w1_ref, w2_ref, o_ref, *, rows, l_major=False):
    half = CMP_LEN // 2
    top = jnp.zeros((rows, NSA_KV_GROUPS * CMP_HIDDEN), F32)
    bot = jnp.zeros((rows, NSA_KV_GROUPS * CMP_HIDDEN), F32)
    for l in range(half):
        a = x_ref[l] if l_major else x_ref[:, l, :]
        top = top + _dot((a + pe_ref[l]).astype(BF16), w1_ref[l])
        bot = bot + _dot((a + pe_ref[half + l]).astype(BF16), w1_ref[half + l])
    hid = top + pltpu.roll(bot, rows - 1, 0)
    o_ref[...] = _dot(_gelu_tanh(hid).astype(BF16), w2_ref[...])


def compress_prompt(kvn, n_batch, seq, pe_t, w1_bd, w2_bd):
    rows = seq // CMP_STRIDE
    return pl.pallas_call(
        functools.partial(_compress_kernel, rows=rows),
        grid=(n_batch, 2),
        in_specs=[
            pl.BlockSpec((rows, CMP_STRIDE, GROUP_COLS), lambda b, j: (b, 0, j)),
            pl.BlockSpec((None, CMP_LEN, 1, GROUP_COLS), lambda b, j: (j, 0, 0, 0)),
            pl.BlockSpec((None, CMP_LEN, GROUP_COLS, NSA_KV_GROUPS * CMP_HIDDEN), lambda b, j: (j, 0, 0, 0)),
            pl.BlockSpec((None, NSA_KV_GROUPS * CMP_HIDDEN, GROUP_COLS), lambda b, j: (j, 0, 0)),
        ],
        out_specs=pl.BlockSpec((None, None, rows, GROUP_COLS), lambda b, j: (b, j, 0, 0)),
        out_shape=jax.ShapeDtypeStruct((n_batch, 2, rows, GROUP_COLS), F32),
        compiler_params=_cparams("parallel", "arbitrary"),
        name="nsa_compress",
    )(kvn.reshape(-1, CMP_STRIDE, kvn.shape[-1]), pe_t, w1_bd, w2_bd)


def compress_weights(phi_pe, phi_w1, phi_w2):
    eye = jnp.eye(NSA_KV_GROUPS, dtype=F32)
    w1 = phi_w1.reshape(2, CMP_LEN, HEAD_DIM, CMP_HIDDEN)
    w1_bd = jnp.einsum('gh,jldc->jlgdhc', eye, w1).reshape(2, CMP_LEN, GROUP_COLS, NSA_KV_GROUPS * CMP_HIDDEN)
    w2_bd = jnp.einsum('gh,jcd->jgchd', eye, phi_w2).reshape(2, NSA_KV_GROUPS * CMP_HIDDEN, GROUP_COLS)
    pe_t = jnp.tile(phi_pe, (1, 1, NSA_KV_GROUPS)).reshape(2, CMP_LEN, 1, GROUP_COLS)
    return pe_t, w1_bd.astype(BF16), w2_bd.astype(BF16)


def _fill_qz(qz_ref, qt_rows, row0, n_heads, tq):
    qz_ref[...] = jnp.zeros(qz_ref.shape, BF16)
    for h in range(n_heads):
        r = row0(h)
        qz_ref[r:r + HEAD_DIM, h * tq:(h + 1) * tq] = (qt_rows(h).astype(F32) * HEAD_DIM ** -0.5).astype(BF16)


def _softmax_probs(score_cols, tq, mk, m_ref, l_ref, p_ref, h, hp=None):
    hp = h if hp is None else hp
    m_all = m_ref[h]
    l_all = l_ref[h]
    alphas, m_news, l_news = [], [], []
    for c0 in range(0, tq, V7X_LANES):
        cols = slice(c0, c0 + V7X_LANES)
        s = score_cols(cols)
        if mk is not None:
            s = jnp.where(mk[:, cols], s, NEG)
        m_old = m_all[:, cols]
        m_new = jnp.maximum(m_old, jnp.max(s, axis=0, keepdims=True))
        alpha = jnp.exp(m_old - m_new)
        p = jnp.exp(s - m_new)
        l_news.append(alpha * l_all[:, cols] + jnp.sum(p, axis=0, keepdims=True))
        p_ref[:, hp * tq + c0:hp * tq + c0 + V7X_LANES] = p.astype(BF16)
        m_news.append(m_new)
        alphas.append(alpha)
    m_ref[h] = jnp.concatenate(m_news, axis=1)
    l_ref[h] = jnp.concatenate(l_news, axis=1)
    return jnp.concatenate(alphas, axis=1)


def _reset_softmax(m_ref, l_ref, acc_ref):
    m_ref[...] = jnp.full(m_ref.shape, NEG, F32)
    l_ref[...] = jnp.zeros(l_ref.shape, F32)
    acc_ref[...] = jnp.zeros(acc_ref.shape, F32)


def _cmp_select_t_kernel(tbl_ref, qt_ref, kc_ref, vct_ref, gt_ref, o_ref, selt_ref, bias_ref, qz_ref, *,
                         tq, n_cmp_pad, n_cmp, n_sel, pos0):
    i = pl.program_id(0)
    b = pl.program_id(1)
    q0 = pos0 + i * tq
    c_idx = lax.broadcasted_iota(jnp.int32, (n_cmp_pad, tq), 0)
    qpos = q0 + lax.broadcasted_iota(jnp.int32, (n_cmp_pad, tq), 1)
    dist = qpos - (c_idx * CMP_STRIDE + CMP_LEN - 1)
    valid = (dist >= 0) & (c_idx < n_cmp)

    @pl.when(b == 0)
    def _():
        bucket = _t5_bucket(dist)
        for h in range(N_HEADS):
            bias_ref[h] = _bias_from_bucket(bucket, tbl_ref, h)

    j_t = lax.broadcasted_iota(jnp.int32, (SEL_PAD, tq), 0)
    q_t = q0 + lax.broadcasted_iota(jnp.int32, (SEL_PAD, tq), 1)
    cur = jnp.right_shift(q_t, 6)
    forced = (j_t == 0) | (j_t == cur) | (j_t == cur - 1)
    avail = j_t <= cur
    j_i = lax.broadcasted_iota(jnp.int32, (SEL_PAD, n_cmp_pad), 0) * SEL_LEN
    c_i = lax.broadcasted_iota(jnp.int32, (SEL_PAD, n_cmp_pad), 1) * CMP_STRIDE
    inc_t = jnp.where((c_i < j_i + SEL_LEN) & (c_i + CMP_LEN > j_i) & (j_i < n_sel * SEL_LEN)
                      & (c_i < n_cmp * CMP_STRIDE), 1.0, 0.0).astype(BF16)
    j_s = lax.broadcasted_iota(jnp.int32, (n_sel, tq), 0)
    kc = kc_ref[...]

    for g in range(NSA_KV_GROUPS):
        _fill_qz(qz_ref, lambda h: qt_ref[(g * NSA_HPG + h) * HEAD_DIM:(g * NSA_HPG + h + 1) * HEAD_DIM, :],
                 lambda h: g * HEAD_DIM, NSA_HPG, tq)
        s_all = _dot(kc, qz_ref[...])
        psum = jnp.zeros((n_cmp_pad, tq), F32)
        outs = []
        for h in range(NSA_HPG):
            hh = g * NSA_HPG + h
            s = s_all[:, h * tq:(h + 1) * tq] + bias_ref[hh]
            s = jnp.where(valid, s, NEG)
            m = jnp.max(s, axis=0, keepdims=True)
            e = jnp.where(valid, jnp.exp(s - m), 0.0)
            p = e / jnp.maximum(jnp.sum(e, axis=0, keepdims=True), 1e-30)
            gate = _sigmoid(gt_ref[hh:hh + 1, :])
            outs.append(_dot(vct_ref[g * HEAD_DIM:(g + 1) * HEAD_DIM, :], p.astype(BF16)) * gate)
            psum = psum + p
        o_ref[:, g * GROUP_COLS:(g + 1) * GROUP_COLS] = jnp.concatenate(outs, axis=0).T
        p_hi, p_lo = _split_bf16(psum, 2)
        imp_t = _dot(inc_t, p_hi) + _dot(inc_t, p_lo)
        score_t = jnp.where(forced, BIG, jnp.where(avail, imp_t, -BIG))

        def pick_step(_, carry):
            remaining, chosen = carry
            top = jnp.max(remaining, axis=0, keepdims=True)
            first = jnp.min(jnp.where(remaining == top, j_s, n_sel), axis=0, keepdims=True)
            pick = j_s == first
            return jnp.where(pick, TAKEN, remaining), jnp.where(pick, 1.0, chosen)

        _, chosen = lax.fori_loop(0, N_SELECT, pick_step, (score_t[:n_sel], jnp.zeros((n_sel, tq), F32)))
        chosen = jnp.concatenate([chosen, jnp.zeros((SEL_PAD - n_sel, tq), F32)], axis=0)
        selt_ref[g * SEL_PAD:(g + 1) * SEL_PAD, :] = chosen.astype(BF16)


def nsa_cmp_select_t(table, q_t, k_cmp, v_cmp_t, gates_t, n_batch, seq, n_cmp, n_sel):
    tq = ATT_TILE
    nq = seq // tq
    m = n_batch * seq
    n_cmp_pad = k_cmp.shape[1]
    kern = functools.partial(_cmp_select_t_kernel, tq=tq, n_cmp_pad=n_cmp_pad, n_cmp=n_cmp, n_sel=n_sel, pos0=0)
    return pl.pallas_call(
        kern,
        grid=(nq, n_batch),
        in_specs=[
            _smem_spec(),
            pl.BlockSpec((D_MODEL, tq), lambda i, b: (0, b * nq + i)),
            pl.BlockSpec((None, n_cmp_pad, GROUP_COLS), lambda i, b: (b, 0, 0)),
            pl.BlockSpec((None, GROUP_COLS, n_cmp_pad), lambda i, b: (b, 0, 0)),
            pl.BlockSpec((3 * N_HEADS, tq), lambda i, b: (0, b * nq + i)),
        ],
        out_specs=[
            pl.BlockSpec((tq, D_MODEL), lambda i, b: (b * nq + i, 0)),
            pl.BlockSpec((NSA_KV_GROUPS * SEL_PAD, tq), lambda i, b: (0, b * nq + i)),
        ],
        out_shape=[
            jax.ShapeDtypeStruct((m, D_MODEL), F32),
            jax.ShapeDtypeStruct((NSA_KV_GROUPS * SEL_PAD, m), BF16),
        ],
        scratch_shapes=[
            pltpu.VMEM((N_HEADS, n_cmp_pad, tq), F32),
            pltpu.VMEM((GROUP_COLS, NSA_HPG * tq), BF16),
        ],
        compiler_params=_cparams("arbitrary", "arbitrary"),
        name="nsa_cmp_select",
    )(table, q_t, k_cmp, v_cmp_t, gates_t)


def _nsa_attn_t_kernel(qt_ref, k_ref, vt_ref, gt_ref, toep_ref, *rest, mode, tile, branch):
    if mode == "sel":
        selt_ref, o_ref, qz_ref, m_ref, l_ref, acc_ref, p_ref = rest
    else:
        o_ref, qz_ref, m_ref, l_ref, acc_ref, p_ref = rest
    i = pl.program_id(1)
    ik = lax.broadcasted_iota(jnp.int32, (tile, tile), 0)
    iq = lax.broadcasted_iota(jnp.int32, (tile, tile), 1)
    causal = iq >= ik
    above = ik > iq

    for g0 in range(0, NSA_KV_GROUPS, GROUPS_PER_PASS):
        groups = range(g0, g0 + GROUPS_PER_PASS)
        for gi, g in enumerate(groups):
            _fill_qz(qz_ref.at[gi],
                     lambda h, g=g: qt_ref[(g * NSA_HPG + h) * HEAD_DIM:(g * NSA_HPG + h + 1) * HEAD_DIM, :],
                     lambda h, g=g: g * HEAD_DIM, NSA_HPG, tile)
        _reset_softmax(m_ref, l_ref, acc_ref)

        def attend(kt, dd, groups=groups):
            r0 = pl.multiple_of(kt * tile, tile)
            k_blk = k_ref[pl.ds(r0, tile), :]
            s_alls = [_dot(k_blk, qz_ref[gi]) for gi in range(GROUPS_PER_PASS)]
            if mode == "sel":
                krow = lax.broadcasted_iota(jnp.int32, (tile, SEL_PAD), 0)
                jcol = lax.broadcasted_iota(jnp.int32, (tile, SEL_PAD), 1)
                expand = jnp.where(jcol == jnp.right_shift(r0 + krow, 6), 1.0, 0.0).astype(BF16)
                mks = [_dot(expand, selt_ref[g * SEL_PAD:(g + 1) * SEL_PAD, :]) > 0.5 for g in groups]
                if dd == 0:
                    mks = [mk & causal for mk in mks]
            else:
                mks = [causal if dd == 0 else (above if dd == 2 else None)] * GROUPS_PER_PASS
            alphas = []
            for gi, g in enumerate(groups):
                for h in range(NSA_HPG):
                    hh = g * NSA_HPG + h

                    def score_cols(cols, gi=gi, h=h, hh=hh):
                        s = s_alls[gi][:, h * tile + cols.start:h * tile + cols.stop]
                        return s + toep_ref[hh, dd, :, cols] if dd < 2 else s

                    alphas.append(_softmax_probs(score_cols, tile, mks[gi], m_ref, l_ref, p_ref.at[gi],
                                                 gi * NSA_HPG + h, h))
            for gi, g in enumerate(groups):
                v_t = vt_ref[g * HEAD_DIM:(g + 1) * HEAD_DIM, pl.ds(r0, tile)]
                pv_all = _dot(v_t, p_ref[gi])
                for h in range(NSA_HPG):
                    a = gi * NSA_HPG + h
                    acc_ref[a] = alphas[a] * acc_ref[a] + pv_all[:, h * tile:(h + 1) * tile]

        attend(i, 0)

        @pl.when(i >= 1)
        def _():
            attend(i - 1, 1)

        if mode == "sel":
            def far_body(kt, carry):
                attend(kt, 2)
                return carry
            lax.fori_loop(0, jnp.maximum(i - 1, 0), far_body, 0)
        else:
            @pl.when(i >= 2)
            def _():
                attend(i - 2, 2)

        for gi, g in enumerate(groups):
            outs = []
            for h in range(NSA_HPG):
                hh = g * NSA_HPG + h
                a = gi * NSA_HPG + h
                gate = _sigmoid(gt_ref[branch * N_HEADS + hh:branch * N_HEADS + hh + 1, :])
                outs.append(acc_ref[a] / jnp.maximum(l_ref[a], 1e-30) * gate)
            o_ref[:, g * GROUP_COLS:(g + 1) * GROUP_COLS] = jnp.concatenate(outs, axis=0).T


def nsa_attn_t(mode, toep, q_t, k_src, k_col, v_t, gates_t, sel_t, n_batch, seq):
    tile = ATT_TILE
    assert WINDOW == 2 * tile
    nq = seq // tile
    m = n_batch * seq
    branch = 1 if mode == "sel" else 2
    kern = functools.partial(_nsa_attn_t_kernel, mode=mode, tile=tile, branch=branch)
    in_specs = [
        pl.BlockSpec((D_MODEL, tile), lambda b, i: (0, b * nq + i)),
        pl.BlockSpec((seq, GROUP_COLS), lambda b, i: (b, k_col)),
        pl.BlockSpec((GROUP_COLS, seq), lambda b, i: (0, b)),
        pl.BlockSpec((3 * N_HEADS, tile), lambda b, i: (0, b * nq + i)),
        pl.BlockSpec((N_HEADS, 2, tile, tile), lambda b, i: (0, 0, 0, 0)),
    ]
    args = [q_t, k_src, v_t, gates_t, toep]
    if mode == "sel":
        in_specs.append(pl.BlockSpec((NSA_KV_GROUPS * SEL_PAD, tile), lambda b, i: (0, b * nq + i)))
        args.append(sel_t)
    return pl.pallas_call(
        kern,
        grid=(n_batch, nq),
        in_specs=in_specs,
        out_specs=pl.BlockSpec((tile, D_MODEL), lambda b, i: (b * nq + i, 0)),
        out_shape=jax.ShapeDtypeStruct((m, D_MODEL), F32),
        scratch_shapes=[
            pltpu.VMEM((GROUPS_PER_PASS, GROUP_COLS, NSA_HPG * tile), BF16),
            pltpu.VMEM((GROUPS_PER_PASS * NSA_HPG, 1, tile), F32),
            pltpu.VMEM((GROUPS_PER_PASS * NSA_HPG, 1, tile), F32),
            pltpu.VMEM((GROUPS_PER_PASS * NSA_HPG, HEAD_DIM, tile), F32),
            pltpu.VMEM((GROUPS_PER_PASS, tile, NSA_HPG * tile), BF16),
        ],
        compiler_params=_cparams("parallel", "arbitrary"),
        name="nsa_attn_" + mode,
    )(*args)


def nsa_prompt_t(table, toep, q_t, kvn, kv_bf, win_bf, vslc_t, vwin_t, gates_t, cw, n_batch, seq):
    pe_t, w1_bd, w2_bd = cw
    n_cmp = (seq - CMP_LEN) // CMP_STRIDE + 1
    n_sel = -(-seq // SEL_LEN)
    assert n_sel <= SEL_PAD and n_sel % 8 == 0
    ctok = compress_prompt(kvn, n_batch, seq, pe_t, w1_bd, w2_bd).astype(BF16)
    k_cmp = ctok[:, 0]
    v_cmp_t = ctok[:, 1].transpose(0, 2, 1)
    o_cmp, sel_t = nsa_cmp_select_t(table, q_t, k_cmp, v_cmp_t, gates_t, n_batch, seq, n_cmp, n_sel)
    o_slc = nsa_attn_t("sel", toep, q_t, kv_bf, 2, vslc_t, gates_t, sel_t, n_batch, seq)
    o_win = nsa_attn_t("win", toep, q_t, win_bf, 0, vwin_t, gates_t, None, n_batch, seq)
    return o_cmp, o_slc, o_win


def _cmp_select_kernel(tbl_ref, q_ref, kt_ref, v_ref, gate_ref, o_ref, sel_ref, bias_ref, score_ref, *,
                       tq, n_cmp_pad, n_cmp, n_sel, pos0):
    i = pl.program_id(0)
    b = pl.program_id(1)
    q0 = pos0 + i * tq
    qpos = q0 + lax.broadcasted_iota(jnp.int32, (tq, n_cmp_pad), 0)
    c_idx = lax.broadcasted_iota(jnp.int32, (tq, n_cmp_pad), 1)
    dist = qpos - (c_idx * CMP_STRIDE + CMP_LEN - 1)
    valid = (dist >= 0) & (c_idx < n_cmp)

    @pl.when(b == 0)
    def _():
        bucket = _t5_bucket(dist)
        for h in range(N_HEADS):
            bias_ref[h] = _bias_from_bucket(bucket, tbl_ref, h)

    j_t = lax.broadcasted_iota(jnp.int32, (SEL_PAD, tq), 0)
    q_t = q0 + lax.broadcasted_iota(jnp.int32, (SEL_PAD, tq), 1)
    cur = jnp.right_shift(q_t, 6)
    forced = (j_t == 0) | (j_t == cur) | (j_t == cur - 1)
    avail = j_t <= cur
    j_i = lax.broadcasted_iota(jnp.int32, (SEL_PAD, n_cmp_pad), 0) * SEL_LEN
    c_i = lax.broadcasted_iota(jnp.int32, (SEL_PAD, n_cmp_pad), 1) * CMP_STRIDE
    inc_t = jnp.where((c_i < j_i + SEL_LEN) & (c_i + CMP_LEN > j_i) & (j_i < n_sel * SEL_LEN)
                      & (c_i < n_cmp * CMP_STRIDE), 1.0, 0.0).astype(BF16)
    j_s = lax.broadcasted_iota(jnp.int32, (n_sel, tq), 0)

    for g in range(NSA_KV_GROUPS):
        psum = jnp.zeros((tq, n_cmp_pad), F32)
        for h in range(NSA_HPG):
            hh = g * NSA_HPG + h
            qh = (q_ref[:, hh * HEAD_DIM:(hh + 1) * HEAD_DIM].astype(F32) * HEAD_DIM ** -0.5).astype(BF16)
            s = _dot(qh, kt_ref[g]) + bias_ref[hh]
            s = jnp.where(valid, s, NEG)
            m = jnp.max(s, axis=-1, keepdims=True)
            e = jnp.where(valid, jnp.exp(s - m), 0.0)
            p = e / jnp.maximum(jnp.sum(e, axis=-1, keepdims=True), 1e-30)
            gate = _sigmoid(gate_ref[:, hh:hh + 1])
            o_ref[:, hh * HEAD_DIM:(hh + 1) * HEAD_DIM] = _dot(p.astype(BF16), v_ref[g]) * gate
            psum = psum + p
        p_hi, p_lo = _split_bf16(psum, 2)
        imp_t = _dot_nt(inc_t, p_hi) + _dot_nt(inc_t, p_lo)
        score_t = jnp.where(forced, BIG, jnp.where(avail, imp_t, -BIG))
        score_ref[...] = score_t
        sc = score_t[:n_sel]

        def rank_step(r, rank):
            row = jnp.broadcast_to(score_ref[pl.ds(r, 1), :], (n_sel, tq))
            ahead = jnp.where(row > sc, 1.0, jnp.where((row == sc) & (j_s > r), 1.0, 0.0))
            return rank + ahead

        rank = lax.fori_loop(0, n_sel, rank_step, jnp.zeros((n_sel, tq), F32))
        chosen = jnp.where(rank < float(N_SELECT), 1.0, 0.0)
        chosen = jnp.concatenate([chosen, jnp.zeros((SEL_PAD - n_sel, tq), F32)], axis=0)
        sel_ref[:, g * SEL_PAD:(g + 1) * SEL_PAD] = chosen.T.astype(BF16)


def cmp_select_prompt(table, q_bf, kt_cmp, v_cmp, gates, n_batch, seq, n_cmp, n_sel):
    tq = CMP_Q_TILE
    nq = seq // tq
    m = n_batch * seq
    n_cmp_pad = kt_cmp.shape[-1]
    kern = functools.partial(_cmp_select_kernel, tq=tq, n_cmp_pad=n_cmp_pad, n_cmp=n_cmp, n_sel=n_sel, pos0=0)
    return pl.pallas_call(
        kern,
        grid=(nq, n_batch),
        in_specs=[
            _smem_spec(),
            pl.BlockSpec((tq, D_MODEL), lambda i, b: (b * nq + i, 0)),
            pl.BlockSpec((None, NSA_KV_GROUPS, HEAD_DIM, n_cmp_pad), lambda i, b: (b, 0, 0, 0)),
            pl.BlockSpec((None, NSA_KV_GROUPS, n_cmp_pad, HEAD_DIM), lambda i, b: (b, 0, 0, 0)),
            pl.BlockSpec((tq, 3 * N_HEADS), lambda i, b: (b * nq + i, 0)),
        ],
        out_specs=[
            pl.BlockSpec((tq, D_MODEL), lambda i, b: (b * nq + i, 0)),
            pl.BlockSpec((tq, NSA_KV_GROUPS * SEL_PAD), lambda i, b: (b * nq + i, 0)),
        ],
        out_shape=[
            jax.ShapeDtypeStruct((m, D_MODEL), F32),
            jax.ShapeDtypeStruct((m, NSA_KV_GROUPS * SEL_PAD), BF16),
        ],
        scratch_shapes=[
            pltpu.VMEM((N_HEADS, tq, n_cmp_pad), F32),
            pltpu.VMEM((SEL_PAD, tq), F32),
        ],
        compiler_params=_cparams("arbitrary", "arbitrary"),
        name="nsa_cmp_select",
    )(table, q_bf, kt_cmp, v_cmp, gates)


def _nsa_attn_kernel(tbl_ref, q_ref, k_ref, v_ref, gate_ref, toep_ref, *rest, mode, tile, branch):
    if mode == "sel":
        sel_ref, o_ref, qs_ref, m_ref, l_ref, acc_ref = rest
    else:
        o_ref, qs_ref, m_ref, l_ref, acc_ref = rest
    i = pl.program_id(1)
    iq = lax.broadcasted_iota(jnp.int32, (tile, tile), 0)
    ik = lax.broadcasted_iota(jnp.int32, (tile, tile), 1)
    causal = iq >= ik
    above = ik > iq

    for hh in range(N_HEADS):
        qs_ref[hh] = (q_ref[:, hh * HEAD_DIM:(hh + 1) * HEAD_DIM].astype(F32) * HEAD_DIM ** -0.5).astype(BF16)

    for g in range(NSA_KV_GROUPS):
        m_ref[...] = jnp.full(m_ref.shape, NEG, F32)
        l_ref[...] = jnp.zeros(l_ref.shape, F32)
        acc_ref[...] = jnp.zeros(acc_ref.shape, F32)

        def attend(kt, dd, g=g):
            r0 = pl.multiple_of(kt * tile, tile)
            kg = k_ref[pl.ds(r0, tile), :][:, g * HEAD_DIM:(g + 1) * HEAD_DIM]
            vg = v_ref[pl.ds(r0, tile), :][:, g * HEAD_DIM:(g + 1) * HEAD_DIM]
            if mode == "sel":
                jrow = lax.broadcasted_iota(jnp.int32, (SEL_PAD, tile), 0)
                kcol = lax.broadcasted_iota(jnp.int32, (SEL_PAD, tile), 1)
                expand = jnp.where(jrow == jnp.right_shift(r0 + kcol, 6), 1.0, 0.0).astype(BF16)
                mk = _dot(sel_ref[:, g * SEL_PAD:(g + 1) * SEL_PAD], expand) > 0.5
                if dd == 0:
                    mk = mk & causal
            else:
                mk = causal if dd == 0 else (above if dd == 2 else None)
            for h in range(NSA_HPG):
                hh = g * NSA_HPG + h
                s = _dot_nt(qs_ref[hh], kg)
                if dd == 2:
                    s = s + tbl_ref[N_BUCKETS - 1, hh]
                else:
                    s = s + toep_ref[hh, dd]
                if mk is not None:
                    s = jnp.where(mk, s, NEG)
                m_old = m_ref[h]
                m_new = jnp.maximum(m_old, jnp.max(s, axis=-1, keepdims=True))
                alpha = jnp.exp(m_old - m_new)
                p = jnp.exp(s - m_new)
                if mk is not None:
                    p = jnp.where(mk, p, 0.0)
                l_ref[h] = alpha * l_ref[h] + jnp.sum(p, axis=-1, keepdims=True)
                acc_ref[h] = alpha * acc_ref[h] + _dot(p.astype(BF16), vg)
                m_ref[h] = m_new

        if mode == "sel":
            def far_body(kt, carry):
                attend(kt, 2)
                return carry
            lax.fori_loop(0, jnp.maximum(i - 1, 0), far_body, 0)
        else:
            @pl.when(i >= 2)
            def _():
                attend(i - 2, 2)

        @pl.when(i >= 1)
        def _():
            attend(i - 1, 1)

        attend(i, 0)

        for h in range(NSA_HPG):
            hh = g * NSA_HPG + h
            gate = _sigmoid(gate_ref[:, branch * N_HEADS + hh:branch * N_HEADS + hh + 1])
            o_ref[:, hh * HEAD_DIM:(hh + 1) * HEAD_DIM] = acc_ref[h] / jnp.maximum(l_ref[h], 1e-30) * gate


def nsa_attn_prompt(mode, table, toep, q_bf, kv_src, k_col, v_col, gates, sel, n_batch, seq):
    tile = ATT_TILE
    assert WINDOW == 2 * tile
    nq = seq // tile
    m = n_batch * seq
    branch = 1 if mode == "sel" else 2
    kern = functools.partial(_nsa_attn_kernel, mode=mode, tile=tile, branch=branch)
    in_specs = [
        _smem_spec(),
        pl.BlockSpec((tile, D_MODEL), lambda b, i: (b * nq + i, 0)),
        pl.BlockSpec((seq, GROUP_COLS), lambda b, i: (b, k_col)),
        pl.BlockSpec((seq, GROUP_COLS), lambda b, i: (b, v_col)),
        pl.BlockSpec((tile, 3 * N_HEADS), lambda b, i: (b * nq + i, 0)),
        pl.BlockSpec((N_HEADS, 2, tile, tile), lambda b, i: (0, 0, 0, 0)),
    ]
    args = [table, q_bf, kv_src, kv_src, gates, toep]
    if mode == "sel":
        in_specs.append(pl.BlockSpec((tile, NSA_KV_GROUPS * SEL_PAD), lambda b, i: (b * nq + i, 0)))
        args.append(sel)
    return pl.pallas_call(
        kern,
        grid=(n_batch, nq),
        in_specs=in_specs,
        out_specs=pl.BlockSpec((tile, D_MODEL), lambda b, i: (b * nq + i, 0)),
        out_shape=jax.ShapeDtypeStruct((m, D_MODEL), F32),
        scratch_shapes=[
            pltpu.VMEM((N_HEADS, tile, HEAD_DIM), BF16),
            pltpu.VMEM((NSA_HPG, tile, 1), F32),
            pltpu.VMEM((NSA_HPG, tile, 1), F32),
            pltpu.VMEM((NSA_HPG, tile, HEAD_DIM), F32),
        ],
        compiler_params=_cparams("parallel", "arbitrary"),
        name="nsa_attn_" + mode,
    )(*args)


def nsa_prompt(table, toep, q_bf, kvn, kv_bf, win_bf, gates, cw, n_batch, seq):
    pe_t, w1_bd, w2_bd = cw
    n_cmp = (seq - CMP_LEN) // CMP_STRIDE + 1
    n_sel = -(-seq // SEL_LEN)
    assert n_sel <= SEL_PAD and n_sel % 8 == 0
    ctok = compress_prompt(kvn, n_batch, seq, pe_t, w1_bd, w2_bd)
    rows = ctok.shape[2]
    ctok = ctok.reshape(n_batch, 2, rows, NSA_KV_GROUPS, HEAD_DIM).astype(BF16)
    kt_cmp = ctok[:, 0].transpose(0, 2, 3, 1)
    v_cmp = ctok[:, 1].transpose(0, 2, 1, 3)
    o_cmp, sel = cmp_select_prompt(table, q_bf, kt_cmp, v_cmp, gates, n_batch, seq, n_cmp, n_sel)
    o_slc = nsa_attn_prompt("sel", table, toep, q_bf, kv_bf, 2, 3, gates, sel, n_batch, seq)
    o_win = nsa_attn_prompt("win", table, toep, q_bf, win_bf, 0, 1, gates, None, n_batch, seq)
    return o_cmp, o_slc, o_win


CUMSUM_CHUNK = 256
HEAD_BLOCK = 4
FOX_BLOCKS_PER_STEP = 2


def _tril_bf16(n):
    r = lax.broadcasted_iota(jnp.int32, (n, n), 0)
    c = lax.broadcasted_iota(jnp.int32, (n, n), 1)
    return jnp.where(r >= c, 1.0, 0.0).astype(BF16)


def _cumsum_rows(x, tri):
    return sum(_dot(tri, part) for part in _split_bf16(x, 3))


def _fox_prep_kernel(f_ref, b_ref, lf_ref, d_ref, *, seq):
    tri = _tril_bf16(CUMSUM_CHUNK)
    carry = jnp.zeros((1, N_HEADS), F32)
    for c in range(seq // CUMSUM_CHUNK):
        rows = slice(c * CUMSUM_CHUNK, (c + 1) * CUMSUM_CHUNK)
        lf = _log_sigmoid(f_ref[rows, :] + b_ref[...])
        lf_ref[rows, :] = lf
        d = carry + _cumsum_rows(lf, tri)
        d_ref[rows, :] = d
        carry = d[CUMSUM_CHUNK - 1:CUMSUM_CHUNK, :]


def fox_prep_prompt(f_raw, b_f, n_batch, seq):
    m = n_batch * seq
    blk = pl.BlockSpec((seq, N_HEADS), lambda b: (b, 0))
    return pl.pallas_call(
        functools.partial(_fox_prep_kernel, seq=seq),
        grid=(n_batch,),
        in_specs=[blk, pl.BlockSpec((1, N_HEADS), lambda b: (0, 0))],
        out_specs=[blk, blk],
        out_shape=[jax.ShapeDtypeStruct((m, N_HEADS), F32)] * 2,
        compiler_params=_cparams("parallel"),
        name="fox_prep",
    )(f_raw, b_f.reshape(1, N_HEADS))


def _fox_attn_t_kernel(qt_ref, k_ref, vt_ref, drow_ref, dcol_ref, o_ref, qz_ref, m_ref, l_ref, acc_ref, p_ref, *,
                       tile):
    i = pl.program_id(2)
    ik = lax.broadcasted_iota(jnp.int32, (tile, tile), 0)
    iq = lax.broadcasted_iota(jnp.int32, (tile, tile), 1)
    causal = iq >= ik
    cols_blk = HEAD_BLOCK * HEAD_DIM
    for bi in range(FOX_BLOCKS_PER_STEP):
        _fill_qz(qz_ref.at[bi], lambda h, bi=bi: qt_ref[(bi * HEAD_BLOCK + h) * HEAD_DIM:(bi * HEAD_BLOCK + h + 1) * HEAD_DIM, :],
                 lambda h: h * HEAD_DIM, HEAD_BLOCK, tile)
    _reset_softmax(m_ref, l_ref, acc_ref)

    def attend(kt, diag):
        r0 = pl.multiple_of(kt * tile, tile)
        k_rows = k_ref[pl.ds(r0, tile), :]
        s_alls = [_dot(k_rows[:, bi * cols_blk:(bi + 1) * cols_blk], qz_ref[bi])
                  for bi in range(FOX_BLOCKS_PER_STEP)]
        alphas = []
        for bi in range(FOX_BLOCKS_PER_STEP):
            for h in range(HEAD_BLOCK):
                a = bi * HEAD_BLOCK + h
                d_keys = dcol_ref[pl.ds(r0, tile), a:a + 1]

                def score_cols(cols, bi=bi, h=h, a=a, d_keys=d_keys):
                    s = s_alls[bi][:, h * tile + cols.start:h * tile + cols.stop]
                    return s + drow_ref[a:a + 1, cols] - d_keys

                alphas.append(_softmax_probs(score_cols, tile, causal if diag else None, m_ref, l_ref,
                                             p_ref.at[bi], a, h))
        for bi in range(FOX_BLOCKS_PER_STEP):
            for h in range(HEAD_BLOCK):
                a = bi * HEAD_BLOCK + h
                v_t = vt_ref[a * HEAD_DIM:(a + 1) * HEAD_DIM, pl.ds(r0, tile)]
                acc_ref[a] = alphas[a] * acc_ref[a] + _dot(v_t, p_ref[bi, :, h * tile:(h + 1) * tile])

    attend(i, True)

    def body(kt, carry):
        attend(kt, False)
        return carry

    lax.fori_loop(0, i, body, 0)
    outs = [acc_ref[a] / jnp.maximum(l_ref[a], 1e-30) for a in range(FOX_BLOCKS_PER_STEP * HEAD_BLOCK)]
    o_ref[...] = jnp.concatenate(outs, axis=0).T


def fox_attn_t(q_t, k_bf, v_t, d_cum, n_batch, seq):
    tile = ATT_TILE
    nq = seq // tile
    m = n_batch * seq
    hps = FOX_BLOCKS_PER_STEP * HEAD_BLOCK
    nhb = N_HEADS // hps
    d4 = d_cum[:m].reshape(n_batch, seq, nhb, hps)
    d_col = d4.transpose(0, 2, 1, 3)
    d_row = d4.transpose(0, 2, 3, 1)
    cols = hps * HEAD_DIM
    return pl.pallas_call(
        functools.partial(_fox_attn_t_kernel, tile=tile),
        grid=(n_batch, nhb, nq),
        in_specs=[
            pl.BlockSpec((cols, tile), lambda b, g, i: (g, b * nq + i)),
            pl.BlockSpec((seq, cols), lambda b, g, i: (b, g)),
            pl.BlockSpec((cols, seq), lambda b, g, i: (g, b)),
            pl.BlockSpec((None, None, hps, tile), lambda b, g, i: (b, g, 0, i)),
            pl.BlockSpec((None, None, seq, hps), lambda b, g, i: (b, g, 0, 0)),
        ],
        out_specs=pl.BlockSpec((tile, cols), lambda b, g, i: (b * nq + i, g)),
        out_shape=jax.ShapeDtypeStruct((m, D_MODEL), F32),
        scratch_shapes=[
            pltpu.VMEM((FOX_BLOCKS_PER_STEP, HEAD_BLOCK * HEAD_DIM, HEAD_BLOCK * tile), BF16),
            pltpu.VMEM((hps, 1, tile), F32),
            pltpu.VMEM((hps, 1, tile), F32),
            pltpu.VMEM((hps, HEAD_DIM, tile), F32),
            pltpu.VMEM((FOX_BLOCKS_PER_STEP, tile, HEAD_BLOCK * tile), BF16),
        ],
        compiler_params=_cparams("parallel", "parallel", "arbitrary"),
        name="fox_attn",
    )(q_t, k_bf, v_t, d_row, d_col)


def _fox_attn_kernel(q_ref, k_ref, v_ref, dq_ref, dk_ref, o_ref, qs_ref, m_ref, l_ref, acc_ref, *, tile):
    i = pl.program_id(2)
    iq = lax.broadcasted_iota(jnp.int32, (tile, tile), 0)
    ik = lax.broadcasted_iota(jnp.int32, (tile, tile), 1)
    causal = iq >= ik
    for h in range(HEAD_BLOCK):
        qs_ref[h] = (q_ref[:, h * HEAD_DIM:(h + 1) * HEAD_DIM].astype(F32) * HEAD_DIM ** -0.5).astype(BF16)
    m_ref[...] = jnp.full(m_ref.shape, NEG, F32)
    l_ref[...] = jnp.zeros(l_ref.shape, F32)
    acc_ref[...] = jnp.zeros(acc_ref.shape, F32)

    def attend(kt, diag):
        r0 = pl.multiple_of(kt * tile, tile)
        kb = k_ref[pl.ds(r0, tile), :]
        vb = v_ref[pl.ds(r0, tile), :]
        dk = dk_ref[kt]
        for h in range(HEAD_BLOCK):
            lanes = slice(h * HEAD_DIM, (h + 1) * HEAD_DIM)
            s = _dot_nt(qs_ref[h], kb[:, lanes]) + dq_ref[:, h:h + 1] - dk[h:h + 1, :]
            if diag:
                s = jnp.where(causal, s, NEG)
            m_old = m_ref[h]
            m_new = jnp.maximum(m_old, jnp.max(s, axis=-1, keepdims=True))
            alpha = jnp.exp(m_old - m_new)
            p = jnp.exp(s - m_new)
            if diag:
                p = jnp.where(causal, p, 0.0)
            l_ref[h] = alpha * l_ref[h] + jnp.sum(p, axis=-1, keepdims=True)
            acc_ref[h] = alpha * acc_ref[h] + _dot(p.astype(BF16), vb[:, lanes])
            m_ref[h] = m_new

    def body(kt, carry):
        attend(kt, False)
        return carry

    lax.fori_loop(0, i, body, 0)
    attend(i, True)
    for h in range(HEAD_BLOCK):
        o_ref[:, h * HEAD_DIM:(h + 1) * HEAD_DIM] = acc_ref[h] / jnp.maximum(l_ref[h], 1e-30)


def fox_attn_prompt(q_bf, kv_bf, d_cum, n_batch, seq):
    tile = ATT_TILE
    nq = seq // tile
    m = n_batch * seq
    nhb = N_HEADS // HEAD_BLOCK
    d4 = d_cum[:m].reshape(n_batch, seq, nhb, HEAD_BLOCK)
    dq = d4.transpose(0, 2, 1, 3)
    dk = d4.reshape(n_batch, nq, tile, nhb, HEAD_BLOCK).transpose(0, 3, 1, 4, 2)
    cols = HEAD_BLOCK * HEAD_DIM
    return pl.pallas_call(
        functools.partial(_fox_attn_kernel, tile=tile),
        grid=(n_batch, nhb, nq),
        in_specs=[
            pl.BlockSpec((tile, cols), lambda b, g, i: (b * nq + i, g)),
            pl.BlockSpec((seq, cols), lambda b, g, i: (b, g)),
            pl.BlockSpec((seq, cols), lambda b, g, i: (b, nhb + g)),
            pl.BlockSpec((None, None, tile, HEAD_BLOCK), lambda b, g, i: (b, g, i, 0)),
            pl.BlockSpec((None, None, nq, HEAD_BLOCK, tile), lambda b, g, i: (b, g, 0, 0, 0)),
        ],
        out_specs=pl.BlockSpec((tile, cols), lambda b, g, i: (b * nq + i, g)),
        out_shape=jax.ShapeDtypeStruct((m, D_MODEL), F32),
        scratch_shapes=[
            pltpu.VMEM((HEAD_BLOCK, tile, HEAD_DIM), BF16),
            pltpu.VMEM((HEAD_BLOCK, tile, 1), F32),
            pltpu.VMEM((HEAD_BLOCK, tile, 1), F32),
            pltpu.VMEM((HEAD_BLOCK, tile, HEAD_DIM), F32),
        ],
        compiler_params=_cparams("parallel", "parallel", "arbitrary"),
        name="fox_attn",
    )(q_bf, kv_bf, kv_bf, dq, dk)


HT_LANES = V7X_LANES


def _lane_head_onehot(n_heads, per_head):
    r = lax.broadcasted_iota(jnp.int32, (n_heads, n_heads * per_head), 0)
    c = lax.broadcasted_iota(jnp.int32, (n_heads, n_heads * per_head), 1)
    return jnp.where(_div_pow2(c, per_head) == r, 1.0, 0.0).astype(BF16)


def _expand_heads(x, onehot):
    return sum(_dot(part, onehot) for part in _split_bf16(x, 3))


def _diag_blocks_to_rows(acc_t, n_heads, feat, per_head):
    rows, lanes = acc_t.shape
    r = lax.broadcasted_iota(jnp.int32, (rows, lanes), 0)
    c = lax.broadcasted_iota(jnp.int32, (rows, lanes), 1)
    z = jnp.where(_div_pow2(r, feat) == _div_pow2(c, per_head), acc_t, 0.0)
    cc = lax.broadcasted_iota(jnp.int32, (lanes, lanes), 0)
    tt = lax.broadcasted_iota(jnp.int32, (lanes, lanes), 1)
    fold = jnp.where(_mod_pow2(cc, per_head) == tt, 1.0, 0.0).astype(BF16)
    out_t = sum(_dot(part, fold) for part in _split_bf16(z, 3))
    return out_t.T


def block_diag_queries(q_rows, n_seq, per_seq, n_heads):
    q4 = q_rows.reshape(n_seq, per_seq, n_heads, HEAD_DIM).transpose(0, 2, 3, 1)
    eye = jnp.eye(n_heads, dtype=q_rows.dtype)
    q_bd = q4[:, :, :, None, :] * eye[None, :, None, :, None]
    return q_bd.reshape(n_seq, n_heads * HEAD_DIM, n_heads * per_seq).astype(BF16)


def _fox_decode_kernel(pt_ref, qbd_ref, k_ref, v_ref, lf_ref, kn_ref, vn_ref, fn_ref, bf_ref, o_ref, lfn_ref,
                       qz_ref, m_ref, l_ref, acc_ref, carry_ref, *, n_pages, per_seq):
    p = pl.program_id(1)

    @pl.when(p == 0)
    def _():
        qz_ref[...] = (qbd_ref[...].astype(F32) * HEAD_DIM ** -0.5).astype(BF16)
        _reset_softmax(m_ref, l_ref, acc_ref)
        carry_ref[...] = jnp.zeros(carry_ref.shape, F32)

    tri = _tril_bf16(PAGE_SIZE)
    onehot = _lane_head_onehot(N_HEADS, per_seq)

    def attend(k_rows, v_rows, lf_rows, mk):
        d = carry_ref[...] + _cumsum_rows(lf_rows, tri)
        s_t = _dot(k_rows.astype(BF16), qz_ref[...]) - _expand_heads(d, onehot)
        if mk is not None:
            s_t = jnp.where(mk, s_t, NEG)
        m_old = m_ref[0]
        m_new = jnp.maximum(m_old, jnp.max(s_t, axis=0, keepdims=True))
        alpha = jnp.exp(m_old - m_new)
        pr = jnp.exp(s_t - m_new)
        if mk is not None:
            pr = jnp.where(mk, pr, 0.0)
        l_ref[0] = alpha * l_ref[0] + jnp.sum(pr, axis=0, keepdims=True)
        pv = lax.dot_general(v_rows.astype(BF16), pr.astype(BF16), (((0,), (0,)), ((), ())),
                             preferred_element_type=F32)
        acc_ref[0] = alpha * acc_ref[0] + pv
        m_ref[0] = m_new
        return d

    @pl.when(p < n_pages)
    def _():
        d = attend(k_ref[...], v_ref[...], lf_ref[...], None)
        carry_ref[...] = d[PAGE_SIZE - 1:PAGE_SIZE, :]

    @pl.when(p == n_pages)
    def _():
        lf_new = _log_sigmoid(fn_ref[...] + bf_ref[...])
        lfn_ref[...] = lf_new
        pad = PAGE_SIZE - per_seq
        k_rows = jnp.concatenate([kn_ref[...], jnp.zeros((pad, D_MODEL), F32)], axis=0)
        v_rows = jnp.concatenate([vn_ref[...], jnp.zeros((pad, D_MODEL), F32)], axis=0)
        lf_rows = jnp.concatenate([lf_new, jnp.zeros((pad, N_HEADS), F32)], axis=0)
        srow = lax.broadcasted_iota(jnp.int32, (PAGE_SIZE, HT_LANES), 0)
        tq = _mod_pow2(lax.broadcasted_iota(jnp.int32, (PAGE_SIZE, HT_LANES), 1), per_seq)
        attend(k_rows, v_rows, lf_rows, srow <= tq)
        o_t = acc_ref[0] / jnp.maximum(l_ref[0], 1e-30)
        o_ref[...] = _diag_blocks_to_rows(o_t, N_HEADS, HEAD_DIM, per_seq)[:per_seq, :]


def fox_decode(page_table, q_bd, kv_pool, lf_pool, kv_new, f_new, b_f, row_block0, n_seq, per_seq):
    n_pages = page_table.shape[1]
    assert N_HEADS * per_seq == HT_LANES
    kern = functools.partial(_fox_decode_kernel, n_pages=n_pages, per_seq=per_seq)
    page = lambda b, p, pt: pt[b, jnp.minimum(p, n_pages - 1)]
    grid_spec = pltpu.PrefetchScalarGridSpec(
        num_scalar_prefetch=1,
        grid=(n_seq, n_pages + 1),
        in_specs=[
            pl.BlockSpec((None, D_MODEL, HT_LANES), lambda b, p, pt: (b, 0, 0)),
            pl.BlockSpec((None, PAGE_SIZE, D_MODEL), lambda b, p, pt: (page(b, p, pt), 0, 0)),
            pl.BlockSpec((None, PAGE_SIZE, D_MODEL), lambda b, p, pt: (page(b, p, pt), 0, 1)),
            pl.BlockSpec((None, PAGE_SIZE, N_HEADS), lambda b, p, pt: (page(b, p, pt), 0, 0)),
            pl.BlockSpec((per_seq, D_MODEL), lambda b, p, pt: (row_block0 + b, 0)),
            pl.BlockSpec((per_seq, D_MODEL), lambda b, p, pt: (row_block0 + b, 1)),
            pl.BlockSpec((per_seq, N_HEADS), lambda b, p, pt: (row_block0 + b, 0)),
            pl.BlockSpec((1, N_HEADS), lambda b, p, pt: (0, 0)),
        ],
        out_specs=[
            pl.BlockSpec((per_seq, D_MODEL), lambda b, p, pt: (b, 0)),
            pl.BlockSpec((per_seq, N_HEADS), lambda b, p, pt: (b, 0)),
        ],
        scratch_shapes=[
            pltpu.VMEM((D_MODEL, HT_LANES), BF16),
            pltpu.VMEM((1, 1, HT_LANES), F32),
            pltpu.VMEM((1, 1, HT_LANES), F32),
            pltpu.VMEM((1, D_MODEL, HT_LANES), F32),
            pltpu.VMEM((1, N_HEADS), F32),
        ],
    )
    return pl.pallas_call(
        kern,
        grid_spec=grid_spec,
        out_shape=[jax.ShapeDtypeStruct((n_seq * per_seq, D_MODEL), F32),
                   jax.ShapeDtypeStruct((n_seq * per_seq, N_HEADS), F32)],
        compiler_params=_cparams("parallel", "arbitrary"),
        name="fox_decode",
    )(page_table, q_bd, kv_pool, kv_pool, lf_pool, kv_new, kv_new, f_new, b_f.reshape(1, N_HEADS))


PAGES_PER_STEP = 4


def feature_major_pages(pool):
    nd = pool.ndim
    return pool.transpose(0, 1, *range(3, nd), 2).reshape(pool.shape[0], pool.shape[1], -1, pool.shape[2])


def _joint_update(s_list, mk_list, vt_list, m_ref, l_ref, acc_ref):
    s_list = [s if mk is None else jnp.where(mk, s, NEG) for s, mk in zip(s_list, mk_list)]
    m_old = m_ref[0]
    m_new = m_old
    for s in s_list:
        m_new = jnp.maximum(m_new, jnp.max(s, axis=0, keepdims=True))
    alpha = jnp.exp(m_old - m_new)
    l_new = alpha * l_ref[0]
    pv = None
    for s, mk, v_t in zip(s_list, mk_list, vt_list):
        pr = jnp.exp(s - m_new)
        if mk is not None:
            pr = jnp.where(mk, pr, 0.0)
        l_new = l_new + jnp.sum(pr, axis=0, keepdims=True)
        term = _dot(v_t.astype(BF16), pr.astype(BF16))
        pv = term if pv is None else pv + term
    l_ref[0] = l_new
    acc_ref[0] = alpha * acc_ref[0] + pv
    m_ref[0] = m_new


def _decay_rows(lf_rows_exp, carry_row, tri):
    return carry_row + _cumsum_rows(lf_rows_exp, tri)


def _fox_decode_fm_kernel(pt_ref, qbd_ref, qbdt_ref, *refs, n_steps, per_seq, pps):
    k_refs = refs[:pps]
    v_refs = refs[pps:2 * pps]
    lf_refs = refs[2 * pps:3 * pps]
    kn_ref, vn_ref, fn_ref, bf_ref, o_ref, lfn_ref, qz_ref, qzt_ref, m_ref, l_ref, acc_ref, carry_ref = refs[3 * pps:]
    p = pl.program_id(1)

    @pl.when(p == 0)
    def _():
        qz_ref[...] = (qbd_ref[...].astype(F32) * HEAD_DIM ** -0.5).astype(BF16)
        qzt_ref[...] = (qbdt_ref[...].astype(F32) * HEAD_DIM ** -0.5).astype(BF16)
        _reset_softmax(m_ref, l_ref, acc_ref)
        carry_ref[...] = jnp.zeros(carry_ref.shape, F32)

    tri = _tril_bf16(PAGE_SIZE)
    onehot = _lane_head_onehot(N_HEADS, per_seq)

    def update(s_t, mk, pv_of):
        if mk is not None:
            s_t = jnp.where(mk, s_t, NEG)
        m_old = m_ref[0]
        m_new = jnp.maximum(m_old, jnp.max(s_t, axis=0, keepdims=True))
        alpha = jnp.exp(m_old - m_new)
        pr = jnp.exp(s_t - m_new)
        if mk is not None:
            pr = jnp.where(mk, pr, 0.0)
        l_ref[0] = alpha * l_ref[0] + jnp.sum(pr, axis=0, keepdims=True)
        acc_ref[0] = alpha * acc_ref[0] + pv_of(pr.astype(BF16))
        m_ref[0] = m_new

    @pl.when(p < n_steps)
    def _():
        local, s_raw = [], []
        for i in range(pps):
            lf_t = lf_refs[i][...]
            lf_exp = sum(lax.dot_general(part, onehot, (((0,), (0,)), ((), ())), preferred_element_type=F32)
                         for part in _split_bf16(lf_t, 3))
            local.append(_cumsum_rows(lf_exp, tri))
            s_raw.append(_dot(qzt_ref[...], k_refs[i][...].astype(BF16)).T)
        carry = carry_ref[...]
        s_list = []
        for i in range(pps):
            d = carry + local[i]
            s_list.append(s_raw[i] - d)
            carry = d[PAGE_SIZE - 1:PAGE_SIZE, :]
        carry_ref[...] = carry
        _joint_update(s_list, [None] * pps, [r[...] for r in v_refs], m_ref, l_ref, acc_ref)

    @pl.when(p == n_steps)
    def _():
        lf_new = _log_sigmoid(fn_ref[...] + bf_ref[...])
        lfn_ref[...] = lf_new
        pad = PAGE_SIZE - per_seq
        k_rows = jnp.concatenate([kn_ref[...], jnp.zeros((pad, D_MODEL), F32)], axis=0).astype(BF16)
        v_rows = jnp.concatenate([vn_ref[...], jnp.zeros((pad, D_MODEL), F32)], axis=0).astype(BF16)
        lf_rows = jnp.concatenate([lf_new, jnp.zeros((pad, N_HEADS), F32)], axis=0)
        d = _decay_rows(_expand_heads(lf_rows, onehot), carry_ref[...], tri)
        srow = lax.broadcasted_iota(jnp.int32, (PAGE_SIZE, HT_LANES), 0)
        tq = _mod_pow2(lax.broadcasted_iota(jnp.int32, (PAGE_SIZE, HT_LANES), 1), per_seq)
        s_t = _dot(k_rows, qz_ref[...]) - d
        update(s_t, srow <= tq, lambda pr: lax.dot_general(v_rows, pr, (((0,), (0,)), ((), ())),
                                                           preferred_element_type=F32))
        o_t = acc_ref[0] / jnp.maximum(l_ref[0], 1e-30)
        o_ref[...] = _diag_blocks_to_rows(o_t, N_HEADS, HEAD_DIM, per_seq)[:per_seq, :]


def fox_decode_fm(page_table, q_bd, kv_pool_t, lf_pool_t, layer, kv_new, f_new, b_f, row_block0, n_seq, per_seq):
    n_pages = page_table.shape[1]
    pps = PAGES_PER_STEP
    assert N_HEADS * per_seq == HT_LANES and n_pages % pps == 0
    n_steps = n_pages // pps
    kern = functools.partial(_fox_decode_fm_kernel, n_steps=n_steps, per_seq=per_seq, pps=pps)

    def page(i):
        return lambda b, p, pt: pt[b, jnp.minimum(p, n_steps - 1) * pps + i]

    def page_spec(rows, blk, i):
        return pl.BlockSpec((None, None, rows, PAGE_SIZE), lambda b, p, pt, i=i: (layer, page(i)(b, p, pt), blk, 0))

    grid_spec = pltpu.PrefetchScalarGridSpec(
        num_scalar_prefetch=1,
        grid=(n_seq, n_steps + 1),
        in_specs=[pl.BlockSpec((None, D_MODEL, HT_LANES), lambda b, p, pt: (b, 0, 0)),
                  pl.BlockSpec((None, HT_LANES, D_MODEL), lambda b, p, pt: (b, 0, 0))]
        + [page_spec(D_MODEL, 0, i) for i in range(pps)]
        + [page_spec(D_MODEL, 1, i) for i in range(pps)]
        + [page_spec(N_HEADS, 0, i) for i in range(pps)]
        + [
            pl.BlockSpec((per_seq, D_MODEL), lambda b, p, pt: (row_block0 + b, 0)),
            pl.BlockSpec((per_seq, D_MODEL), lambda b, p, pt: (row_block0 + b, 1)),
            pl.BlockSpec((per_seq, N_HEADS), lambda b, p, pt: (row_block0 + b, 0)),
            pl.BlockSpec((1, N_HEADS), lambda b, p, pt: (0, 0)),
        ],
        out_specs=[
            pl.BlockSpec((per_seq, D_MODEL), lambda b, p, pt: (b, 0)),
            pl.BlockSpec((per_seq, N_HEADS), lambda b, p, pt: (b, 0)),
        ],
        scratch_shapes=[
            pltpu.VMEM((D_MODEL, HT_LANES), BF16),
            pltpu.VMEM((HT_LANES, D_MODEL), BF16),
            pltpu.VMEM((1, 1, HT_LANES), F32),
            pltpu.VMEM((1, 1, HT_LANES), F32),
            pltpu.VMEM((1, D_MODEL, HT_LANES), F32),
            pltpu.VMEM((1, HT_LANES), F32),
        ],
    )
    return pl.pallas_call(
        kern,
        grid_spec=grid_spec,
        out_shape=[jax.ShapeDtypeStruct((n_seq * per_seq, D_MODEL), F32),
                   jax.ShapeDtypeStruct((n_seq * per_seq, N_HEADS), F32)],
        compiler_params=_cparams("parallel", "arbitrary"),
        name="fox_decode",
    )(page_table, q_bd, q_bd.transpose(0, 2, 1), *([kv_pool_t] * (2 * pps)), *([lf_pool_t] * pps),
      kv_new, kv_new, f_new, b_f.reshape(1, N_HEADS))


DEC_ROW_TILE = 512


def _group_rows_to_heads(acc_t):
    return jnp.concatenate([acc_t[(h // NSA_HPG) * HEAD_DIM:(h // NSA_HPG + 1) * HEAD_DIM] for h in range(N_HEADS)],
                           axis=0)


def group_block_diag_queries(q_rows, n_seq, per_seq):
    q5 = q_rows.reshape(n_seq, per_seq, NSA_KV_GROUPS, NSA_HPG, HEAD_DIM).transpose(0, 2, 4, 3, 1)
    eye = jnp.eye(NSA_KV_GROUPS, dtype=q_rows.dtype)
    q_bd = q5[:, :, :, None, :, :] * eye[None, :, None, :, None, None]
    return q_bd.reshape(n_seq, GROUP_COLS, N_HEADS * per_seq).astype(BF16)


def _nsa_dec_bias_kernel(tbl_ref, o_ref, *, pos0, per_seq):
    onehot = _lane_head_onehot(N_HEADS, per_seq)
    tab = _expand_heads(tbl_ref[...], onehot)
    row = lax.broadcasted_iota(jnp.int32, (PAGE_SIZE, HT_LANES), 0)
    t = _mod_pow2(lax.broadcasted_iota(jnp.int32, (PAGE_SIZE, HT_LANES), 1), per_seq)
    dists = (PAGE_SIZE + t - row, t - row, pos0 + t - (row * CMP_STRIDE + CMP_LEN - 1))
    for n, dist in enumerate(dists):
        bucket = _t5_bucket(dist)
        val = jnp.broadcast_to(tab[0:1, :], (PAGE_SIZE, HT_LANES))
        for k in range(1, N_BUCKETS):
            val = jnp.where(bucket == k, tab[k:k + 1, :], val)
        o_ref[n] = val
    o_ref[3] = jnp.broadcast_to(tab[N_BUCKETS - 1:N_BUCKETS, :], (PAGE_SIZE, HT_LANES))


def nsa_dec_bias(table, pos0, per_seq):
    assert PAGE_SIZE >= T5_SATURATION_DIST
    return pl.pallas_call(
        functools.partial(_nsa_dec_bias_kernel, pos0=pos0, per_seq=per_seq),
        out_shape=jax.ShapeDtypeStruct((4, PAGE_SIZE, HT_LANES), F32),
        name="nsa_dec_bias",
    )(table)


def _dec_gather_kernel(pt_ref, x_ref, o_ref):
    for l in range(CMP_STRIDE):
        o_ref[l] = x_ref[:, l, :]


def nsa_dec_gather(page_table, pool):
    n_seq, n_pages = page_table.shape
    n_pool = pool.shape[0]
    rpp = PAGE_SIZE // CMP_STRIDE
    pool4 = pool.reshape(n_pool, rpp, CMP_STRIDE, pool.shape[-1])
    grid_spec = pltpu.PrefetchScalarGridSpec(
        num_scalar_prefetch=1,
        grid=(n_seq, n_pages),
        in_specs=[pl.BlockSpec((None, rpp, CMP_STRIDE, 2 * GROUP_COLS), lambda b, p, pt: (pt[b, p], 0, 0, 0))],
        out_specs=pl.BlockSpec((CMP_STRIDE, rpp, 2 * GROUP_COLS), lambda b, p, pt: (0, b * n_pages + p, 0)),
    )
    return pl.pallas_call(
        _dec_gather_kernel,
        grid_spec=grid_spec,
        out_shape=jax.ShapeDtypeStruct((CMP_STRIDE, n_seq * n_pages * rpp, 2 * GROUP_COLS), F32),
        compiler_params=_cparams("parallel", "arbitrary"),
        name="nsa_dec_gather",
    )(page_table, pool4)


def nsa_dec_compress(xd, pe_t, w1_bd, w2_bd):
    r_total = xd.shape[1]
    rows = min(DEC_ROW_TILE, r_total)
    return pl.pallas_call(
        functools.partial(_compress_kernel, rows=rows, l_major=True),
        grid=(r_total // rows, 2),
        in_specs=[
            pl.BlockSpec((CMP_STRIDE, rows, GROUP_COLS), lambda i, j: (0, i, j)),
            pl.BlockSpec((None, CMP_LEN, 1, GROUP_COLS), lambda i, j: (j, 0, 0, 0)),
            pl.BlockSpec((None, CMP_LEN, GROUP_COLS, NSA_KV_GROUPS * CMP_HIDDEN), lambda i, j: (j, 0, 0, 0)),
            pl.BlockSpec((None, NSA_KV_GROUPS * CMP_HIDDEN, GROUP_COLS), lambda i, j: (j, 0, 0)),
        ],
        out_specs=pl.BlockSpec((None, rows, GROUP_COLS), lambda i, j: (j, i, 0)),
        out_shape=jax.ShapeDtypeStruct((2, r_total, GROUP_COLS), F32),
        compiler_params=_cparams("parallel", "arbitrary"),
        name="nsa_dec_compress",
    )(xd, pe_t, w1_bd, w2_bd)


def _dec_softmax_step(s_t, mk, v_rows, m_ref, l_ref, acc_ref):
    if mk is not None:
        s_t = jnp.where(mk, s_t, NEG)
    m_old = m_ref[0]
    m_new = jnp.maximum(m_old, jnp.max(s_t, axis=0, keepdims=True))
    alpha = jnp.exp(m_old - m_new)
    pr = jnp.exp(s_t - m_new)
    if mk is not None:
        pr = jnp.where(mk, pr, 0.0)
    l_ref[0] = alpha * l_ref[0] + jnp.sum(pr, axis=0, keepdims=True)
    pv = lax.dot_general(v_rows, pr.astype(BF16), (((0,), (0,)), ((), ())), preferred_element_type=F32)
    acc_ref[0] = alpha * acc_ref[0] + pv
    m_ref[0] = m_new


def _dec_finish(m_ref, l_ref, acc_ref, gate_row, per_seq):
    o_t = acc_ref[0] / jnp.maximum(l_ref[0], 1e-30) * _sigmoid(gate_row)
    return _diag_blocks_to_rows(_group_rows_to_heads(o_t), N_HEADS, HEAD_DIM, per_seq)[:per_seq, :]


def _nsa_dec_cmp_kernel(qbd_ref, kc_ref, vc_ref, g_ref, bias_ref, o_ref, sel_ref, score_ref, *,
                        n_cmp, n_sel, pos0, per_seq):
    n_c = kc_ref.shape[0]
    c_idx = lax.broadcasted_iota(jnp.int32, (n_c, HT_LANES), 0)
    lane = lax.broadcasted_iota(jnp.int32, (n_c, HT_LANES), 1)
    t = _mod_pow2(lane, per_seq)
    dist = pos0 + t - (c_idx * CMP_STRIDE + CMP_LEN - 1)
    valid = (dist >= 0) & (c_idx < n_cmp)
    qz = (qbd_ref[...].astype(F32) * HEAD_DIM ** -0.5).astype(BF16)
    s = _dot(kc_ref[...], qz) + bias_ref[2]
    s = jnp.where(valid, s, NEG)
    m = jnp.max(s, axis=0, keepdims=True)
    e = jnp.where(valid, jnp.exp(s - m), 0.0)
    p = e / jnp.maximum(jnp.sum(e, axis=0, keepdims=True), 1e-30)
    o_t = lax.dot_general(vc_ref[...], p.astype(BF16), (((0,), (0,)), ((), ())), preferred_element_type=F32)
    o_t = o_t * _sigmoid(g_ref[0])
    o_ref[...] = _diag_blocks_to_rows(_group_rows_to_heads(o_t), N_HEADS, HEAD_DIM, per_seq)[:per_seq, :]
    src = lax.broadcasted_iota(jnp.int32, (HT_LANES, HT_LANES), 0)
    dst = lax.broadcasted_iota(jnp.int32, (HT_LANES, HT_LANES), 1)
    same_t = _mod_pow2(src, per_seq) == _mod_pow2(dst, per_seq)
    g_of_src = _div_pow2(src, per_seq * NSA_HPG)
    n_gt = NSA_KV_GROUPS * per_seq
    head_sum = jnp.where(same_t & (g_of_src == _div_pow2(dst, per_seq)) & (dst < n_gt), 1.0, 0.0).astype(BF16)
    psum = sum(_dot(part, head_sum) for part in _split_bf16(p, 2))
    j_i = lax.broadcasted_iota(jnp.int32, (SEL_PAD, n_c), 0) * SEL_LEN
    c_i = lax.broadcasted_iota(jnp.int32, (SEL_PAD, n_c), 1) * CMP_STRIDE
    inc_t = jnp.where((c_i < j_i + SEL_LEN) & (c_i + CMP_LEN > j_i) & (j_i < n_sel * SEL_LEN)
                      & (c_i < n_cmp * CMP_STRIDE), 1.0, 0.0).astype(BF16)
    imp_t = sum(_dot(inc_t, part) for part in _split_bf16(psum, 2))
    j_t = lax.broadcasted_iota(jnp.int32, (SEL_PAD, HT_LANES), 0)
    q_t = pos0 + _mod_pow2(lax.broadcasted_iota(jnp.int32, (SEL_PAD, HT_LANES), 1), per_seq)
    cur = jnp.right_shift(q_t, 6)
    forced = (j_t == 0) | (j_t == cur) | (j_t == cur - 1)
    avail = j_t <= cur
    score_t = jnp.where(forced, BIG, jnp.where(avail, imp_t, -BIG))
    score_ref[...] = score_t
    n_rank = -(-n_sel // 8) * 8
    sc = score_t[:n_rank]
    j_s = lax.broadcasted_iota(jnp.int32, (n_rank, HT_LANES), 0)

    def rank_step(r, rank):
        row = jnp.broadcast_to(score_ref[pl.ds(r, 1), :], (n_rank, HT_LANES))
        ahead = jnp.where(row > sc, 1.0, jnp.where((row == sc) & (j_s > r), 1.0, 0.0))
        return rank + ahead

    rank = lax.fori_loop(0, n_sel, rank_step, jnp.zeros((n_rank, HT_LANES), F32))
    chosen = jnp.where((rank < float(N_SELECT)) & (j_s < n_sel), 1.0, 0.0)
    chosen = jnp.concatenate([chosen, jnp.zeros((SEL_PAD - n_rank, HT_LANES), F32)], axis=0).astype(BF16)
    spread = jnp.where(same_t & (src == _div_pow2(dst, per_seq * NSA_HPG) * per_seq + _mod_pow2(dst, per_seq)),
                       1.0, 0.0).astype(BF16)
    sel_ref[...] = _dot(chosen, spread).astype(BF16)


def nsa_dec_cmp(q_bd, k_cmp, v_cmp, gate_rows, bias, n_cmp, n_sel, pos0, per_seq):
    n_seq, n_c = k_cmp.shape[:2]
    kern = functools.partial(_nsa_dec_cmp_kernel, n_cmp=n_cmp, n_sel=n_sel, pos0=pos0, per_seq=per_seq)
    return pl.pallas_call(
        kern,
        grid=(n_seq,),
        in_specs=[
            pl.BlockSpec((None, GROUP_COLS, HT_LANES), lambda b: (b, 0, 0)),
            pl.BlockSpec((None, n_c, GROUP_COLS), lambda b: (b, 0, 0)),
            pl.BlockSpec((None, n_c, GROUP_COLS), lambda b: (b, 0, 0)),
            pl.BlockSpec((None, 3, 1, HT_LANES), lambda b: (b, 0, 0, 0)),
            pl.BlockSpec((4, PAGE_SIZE, HT_LANES), lambda b: (0, 0, 0)),
        ],
        out_specs=[
            pl.BlockSpec((per_seq, D_MODEL), lambda b: (b, 0)),
            pl.BlockSpec((None, SEL_PAD, HT_LANES), lambda b: (b, 0, 0)),
        ],
        out_shape=[
            jax.ShapeDtypeStruct((n_seq * per_seq, D_MODEL), F32),
            jax.ShapeDtypeStruct((n_seq, SEL_PAD, HT_LANES), BF16),
        ],
        scratch_shapes=[pltpu.VMEM((SEL_PAD, HT_LANES), F32)],
        compiler_params=_cparams("parallel"),
        name="nsa_dec_cmp",
    )(q_bd, k_cmp, v_cmp, gate_rows, bias)


def _nsa_dec_sel_kernel(pt_ref, qbd_ref, k_ref, v_ref, kn_ref, vn_ref, sel_ref, g_ref, bias_ref, o_ref,
                        qz_ref, m_ref, l_ref, acc_ref, *, n_pages, per_seq):
    p = pl.program_id(1)

    @pl.when(p == 0)
    def _():
        qz_ref[...] = (qbd_ref[...].astype(F32) * HEAD_DIM ** -0.5).astype(BF16)
        _reset_softmax(m_ref, l_ref, acc_ref)

    srow = lax.broadcasted_iota(jnp.int32, (PAGE_SIZE, SEL_PAD), 0)
    jcol = lax.broadcasted_iota(jnp.int32, (PAGE_SIZE, SEL_PAD), 1)
    blocks_per_page = PAGE_SIZE // SEL_LEN

    def member(first_block):
        expand = jnp.where(jcol == first_block + jnp.right_shift(srow, 6), 1.0, 0.0).astype(BF16)
        return _dot(expand, sel_ref[...]) > 0.5

    @pl.when(p < n_pages)
    def _():
        bias = jnp.where(p == n_pages - 1, bias_ref[0], bias_ref[3])
        s_t = _dot(k_ref[...].astype(BF16), qz_ref[...]) + bias
        _dec_softmax_step(s_t, member(p * blocks_per_page), v_ref[...].astype(BF16), m_ref, l_ref, acc_ref)

    @pl.when(p == n_pages)
    def _():
        pad = PAGE_SIZE - per_seq
        k_rows = jnp.concatenate([kn_ref[...], jnp.zeros((pad, GROUP_COLS), F32)], axis=0).astype(BF16)
        v_rows = jnp.concatenate([vn_ref[...], jnp.zeros((pad, GROUP_COLS), F32)], axis=0).astype(BF16)
        krow = lax.broadcasted_iota(jnp.int32, (PAGE_SIZE, HT_LANES), 0)
        tq = _mod_pow2(lax.broadcasted_iota(jnp.int32, (PAGE_SIZE, HT_LANES), 1), per_seq)
        mk = member(n_pages * blocks_per_page) & (krow <= tq)
        s_t = _dot(k_rows, qz_ref[...]) + bias_ref[1]
        _dec_softmax_step(s_t, mk, v_rows, m_ref, l_ref, acc_ref)
        o_ref[...] = _dec_finish(m_ref, l_ref, acc_ref, g_ref[1], per_seq)


def nsa_dec_sel(page_table, q_bd, pool, kv_new, sel, gate_rows, bias, row_block0, per_seq):
    n_seq, n_pages = page_table.shape
    kern = functools.partial(_nsa_dec_sel_kernel, n_pages=n_pages, per_seq=per_seq)
    page = lambda b, p, pt: pt[b, jnp.minimum(p, n_pages - 1)]
    grid_spec = pltpu.PrefetchScalarGridSpec(
        num_scalar_prefetch=1,
        grid=(n_seq, n_pages + 1),
        in_specs=[
            pl.BlockSpec((None, GROUP_COLS, HT_LANES), lambda b, p, pt: (b, 0, 0)),
            pl.BlockSpec((None, PAGE_SIZE, GROUP_COLS), lambda b, p, pt: (page(b, p, pt), 0, 2)),
            pl.BlockSpec((None, PAGE_SIZE, GROUP_COLS), lambda b, p, pt: (page(b, p, pt), 0, 3)),
            pl.BlockSpec((per_seq, GROUP_COLS), lambda b, p, pt: (row_block0 + b, 2)),
            pl.BlockSpec((per_seq, GROUP_COLS), lambda b, p, pt: (row_block0 + b, 3)),
            pl.BlockSpec((None, SEL_PAD, HT_LANES), lambda b, p, pt: (b, 0, 0)),
            pl.BlockSpec((None, 3, 1, HT_LANES), lambda b, p, pt: (b, 0, 0, 0)),
            pl.BlockSpec((4, PAGE_SIZE, HT_LANES), lambda b, p, pt: (0, 0, 0)),
        ],
        out_specs=pl.BlockSpec((per_seq, D_MODEL), lambda b, p, pt: (b, 0)),
        scratch_shapes=[
            pltpu.VMEM((GROUP_COLS, HT_LANES), BF16),
            pltpu.VMEM((1, 1, HT_LANES), F32),
            pltpu.VMEM((1, 1, HT_LANES), F32),
            pltpu.VMEM((1, GROUP_COLS, HT_LANES), F32),
        ],
    )
    return pl.pallas_call(
        kern,
        grid_spec=grid_spec,
        out_shape=jax.ShapeDtypeStruct((n_seq * per_seq, D_MODEL), F32),
        compiler_params=_cparams("parallel", "arbitrary"),
        name="nsa_dec_sel",
    )(page_table, q_bd, pool, pool, kv_new, kv_new, sel, gate_rows, bias)


def _nsa_dec_win_kernel(qbd_ref, w_ref, wn_ref, g_ref, bias_ref, o_ref, ws_ref, m_ref, l_ref, acc_ref, *, per_seq):
    n_past = w_ref.shape[0]
    n_tiles = n_past // PAGE_SIZE
    qz = (qbd_ref[...].astype(F32) * HEAD_DIM ** -0.5).astype(BF16)
    _reset_softmax(m_ref, l_ref, acc_ref)
    krow = lax.broadcasted_iota(jnp.int32, (PAGE_SIZE, HT_LANES), 0)
    tq = _mod_pow2(lax.broadcasted_iota(jnp.int32, (PAGE_SIZE, HT_LANES), 1), per_seq)
    for w in range(n_tiles):
        rows = w_ref[w * PAGE_SIZE:(w + 1) * PAGE_SIZE, :]
        mk = (krow > tq) if w == 0 else None
        bias = bias_ref[0] if w == n_tiles - 1 else bias_ref[3]
        s_t = _dot(rows[:, :GROUP_COLS].astype(BF16), qz) + bias
        _dec_softmax_step(s_t, mk, rows[:, GROUP_COLS:].astype(BF16), m_ref, l_ref, acc_ref)
    pad = PAGE_SIZE - per_seq
    new = jnp.concatenate([wn_ref[...], jnp.zeros((pad, 2 * GROUP_COLS), F32)], axis=0)
    s_t = _dot(new[:, :GROUP_COLS].astype(BF16), qz) + bias_ref[1]
    _dec_softmax_step(s_t, krow <= tq, new[:, GROUP_COLS:].astype(BF16), m_ref, l_ref, acc_ref)
    o_ref[...] = _dec_finish(m_ref, l_ref, acc_ref, g_ref[2], per_seq)
    ws_ref[0:n_past - per_seq, :] = w_ref[per_seq:n_past, :]
    ws_ref[n_past - per_seq:n_past, :] = wn_ref[...]


def nsa_dec_win(q_bd, win_past, win_new, gate_rows, bias, row_block0, per_seq):
    n_seq, n_past = win_past.shape[:2]
    assert n_past == WINDOW and n_past % PAGE_SIZE == 0 and PAGE_SIZE + 1 >= T5_SATURATION_DIST
    kern = functools.partial(_nsa_dec_win_kernel, per_seq=per_seq)
    return pl.pallas_call(
        kern,
        grid=(n_seq,),
        in_specs=[
            pl.BlockSpec((None, GROUP_COLS, HT_LANES), lambda b: (b, 0, 0)),
            pl.BlockSpec((None, n_past, 2 * GROUP_COLS), lambda b: (b, 0, 0)),
            pl.BlockSpec((per_seq, 2 * GROUP_COLS), lambda b: (row_block0 + b, 0)),
            pl.BlockSpec((None, 3, 1, HT_LANES), lambda b: (b, 0, 0, 0)),
            pl.BlockSpec((4, PAGE_SIZE, HT_LANES), lambda b: (0, 0, 0)),
        ],
        out_specs=[
            pl.BlockSpec((per_seq, D_MODEL), lambda b: (b, 0)),
            pl.BlockSpec((None, n_past, 2 * GROUP_COLS), lambda b: (b, 0, 0)),
        ],
        out_shape=[
            jax.ShapeDtypeStruct((n_seq * per_seq, D_MODEL), F32),
            jax.ShapeDtypeStruct((n_seq, n_past, 2 * GROUP_COLS), F32),
        ],
        scratch_shapes=[
            pltpu.VMEM((1, 1, HT_LANES), F32),
            pltpu.VMEM((1, 1, HT_LANES), F32),
            pltpu.VMEM((1, GROUP_COLS, HT_LANES), F32),
        ],
        compiler_params=_cparams("parallel"),
        name="nsa_dec_win",
    )(q_bd, win_past, win_new, gate_rows, bias)


def nsa_decode(table, page_table, pool, win_past, q_rows, kv_new, win_new, gates_rows, cw, row_block0, pos0, per_seq):
    n_seq, n_pages = page_table.shape
    pe_t, w1_bd, w2_bd = cw
    past = n_pages * PAGE_SIZE
    s_total = past + per_seq
    n_cmp = (s_total - CMP_LEN) // CMP_STRIDE + 1
    n_sel = -(-s_total // SEL_LEN)
    assert n_cmp == past // CMP_STRIDE - 1 and n_sel == past // SEL_LEN + 1 and per_seq <= CMP_STRIDE
    assert pos0 == past and pos0 % SEL_LEN == 0 and pos0 >= WINDOW and past // CMP_STRIDE == PAGE_SIZE
    bias = nsa_dec_bias(table, pos0, per_seq)
    ctok = nsa_dec_compress(nsa_dec_gather(page_table, pool), pe_t, w1_bd, w2_bd).astype(BF16)
    rows_per_seq = past // CMP_STRIDE
    k_cmp = ctok[0].reshape(n_seq, rows_per_seq, GROUP_COLS)
    v_cmp = ctok[1].reshape(n_seq, rows_per_seq, GROUP_COLS)
    q_bd = group_block_diag_queries(q_rows, n_seq, per_seq)
    gate_rows = gates_rows.reshape(n_seq, per_seq, 3, N_HEADS).transpose(0, 2, 3, 1).reshape(n_seq, 3, 1, HT_LANES)
    o_cmp, sel = nsa_dec_cmp(q_bd, k_cmp, v_cmp, gate_rows, bias, n_cmp, n_sel, pos0, per_seq)
    o_slc = nsa_dec_sel(page_table, q_bd, pool, kv_new, sel, gate_rows, bias, row_block0, per_seq)
    o_win, win_state = nsa_dec_win(q_bd, win_past, win_new, gate_rows, bias, row_block0, per_seq)
    return o_cmp, o_slc, o_win, win_state


def _dec_gather_fm_kernel(pt_ref, *refs, pps):
    o_ref = refs[pps]
    rpp = PAGE_SIZE // CMP_STRIDE
    for i in range(pps):
        x3 = refs[i][...].T.reshape(rpp, CMP_STRIDE, 2 * GROUP_COLS)
        for l in range(CMP_STRIDE):
            o_ref[l, i * rpp:(i + 1) * rpp, :] = x3[:, l, :]


def nsa_dec_gather_fm(page_table, pool_t, layer):
    n_seq, n_pages = page_table.shape
    pps = PAGES_PER_STEP
    assert n_pages % pps == 0
    rpp = PAGE_SIZE // CMP_STRIDE
    grid_spec = pltpu.PrefetchScalarGridSpec(
        num_scalar_prefetch=1,
        grid=(n_seq, n_pages // pps),
        in_specs=[pl.BlockSpec((None, None, 2 * GROUP_COLS, PAGE_SIZE),
                               lambda b, p, pt, i=i: (layer, pt[b, p * pps + i], 0, 0)) for i in range(pps)],
        out_specs=pl.BlockSpec((CMP_STRIDE, rpp * pps, 2 * GROUP_COLS),
                               lambda b, p, pt: (0, b * (n_pages // pps) + p, 0)),
    )
    return pl.pallas_call(
        functools.partial(_dec_gather_fm_kernel, pps=pps),
        grid_spec=grid_spec,
        out_shape=jax.ShapeDtypeStruct((CMP_STRIDE, n_seq * n_pages * rpp, 2 * GROUP_COLS), F32),
        compiler_params=_cparams("parallel", "arbitrary"),
        name="nsa_dec_gather",
    )(page_table, *([pool_t] * pps))


def _dec_update(s_t, mk, pv_of, m_ref, l_ref, acc_ref):
    if mk is not None:
        s_t = jnp.where(mk, s_t, NEG)
    m_old = m_ref[0]
    m_new = jnp.maximum(m_old, jnp.max(s_t, axis=0, keepdims=True))
    alpha = jnp.exp(m_old - m_new)
    pr = jnp.exp(s_t - m_new)
    if mk is not None:
        pr = jnp.where(mk, pr, 0.0)
    l_ref[0] = alpha * l_ref[0] + jnp.sum(pr, axis=0, keepdims=True)
    acc_ref[0] = alpha * acc_ref[0] + pv_of(pr.astype(BF16))
    m_ref[0] = m_new


def _pv_key_major(v_rows):
    return lambda pr: lax.dot_general(v_rows, pr, (((0,), (0,)), ((), ())), preferred_element_type=F32)


def _nsa_dec_sel_fm_kernel(pt_ref, qbd_ref, qbdt_ref, *refs, n_steps, per_seq, pps):
    k_refs = refs[:pps]
    v_refs = refs[pps:2 * pps]
    kn_ref, vn_ref, sel_ref, g_ref, bias_ref, o_ref, qz_ref, qzt_ref, m_ref, l_ref, acc_ref = refs[2 * pps:]
    p = pl.program_id(1)

    @pl.when(p == 0)
    def _():
        qz_ref[...] = (qbd_ref[...].astype(F32) * HEAD_DIM ** -0.5).astype(BF16)
        qzt_ref[...] = (qbdt_ref[...].astype(F32) * HEAD_DIM ** -0.5).astype(BF16)
        _reset_softmax(m_ref, l_ref, acc_ref)

    srow = lax.broadcasted_iota(jnp.int32, (PAGE_SIZE, SEL_PAD), 0)
    jcol = lax.broadcasted_iota(jnp.int32, (PAGE_SIZE, SEL_PAD), 1)
    blocks_per_page = PAGE_SIZE // SEL_LEN

    def member(first_block):
        expand = jnp.where(jcol == first_block + jnp.right_shift(srow, 6), 1.0, 0.0).astype(BF16)
        return _dot(expand, sel_ref[...]) > 0.5

    @pl.when(p < n_steps)
    def _():
        s_list, mk_list = [], []
        for i in range(pps):
            page_no = p * pps + i
            bias = jnp.where(page_no == n_steps * pps - 1, bias_ref[0], bias_ref[3])
            s_list.append(_dot(qzt_ref[...], k_refs[i][...].astype(BF16)).T + bias)
            mk_list.append(member(page_no * blocks_per_page))
        _joint_update(s_list, mk_list, [r[...] for r in v_refs], m_ref, l_ref, acc_ref)

    @pl.when(p == n_steps)
    def _():
        pad = PAGE_SIZE - per_seq
        k_rows = jnp.concatenate([kn_ref[...], jnp.zeros((pad, GROUP_COLS), F32)], axis=0).astype(BF16)
        v_rows = jnp.concatenate([vn_ref[...], jnp.zeros((pad, GROUP_COLS), F32)], axis=0).astype(BF16)
        krow = lax.broadcasted_iota(jnp.int32, (PAGE_SIZE, HT_LANES), 0)
        tq = _mod_pow2(lax.broadcasted_iota(jnp.int32, (PAGE_SIZE, HT_LANES), 1), per_seq)
        mk = member(n_steps * pps * blocks_per_page) & (krow <= tq)
        s_t = _dot(k_rows, qz_ref[...]) + bias_ref[1]
        _dec_update(s_t, mk, _pv_key_major(v_rows), m_ref, l_ref, acc_ref)
        o_ref[...] = _dec_finish(m_ref, l_ref, acc_ref, g_ref[1], per_seq)


def nsa_dec_sel_fm(page_table, q_bd, pool_t, layer, kv_new, sel, gate_rows, bias, row_block0, per_seq):
    n_seq, n_pages = page_table.shape
    pps = PAGES_PER_STEP
    assert n_pages % pps == 0
    n_steps = n_pages // pps
    kern = functools.partial(_nsa_dec_sel_fm_kernel, n_steps=n_steps, per_seq=per_seq, pps=pps)

    def page_spec(blk, i):
        return pl.BlockSpec((None, None, GROUP_COLS, PAGE_SIZE),
                            lambda b, p, pt, i=i: (layer, pt[b, jnp.minimum(p, n_steps - 1) * pps + i], blk, 0))

    grid_spec = pltpu.PrefetchScalarGridSpec(
        num_scalar_prefetch=1,
        grid=(n_seq, n_steps + 1),
        in_specs=[pl.BlockSpec((None, GROUP_COLS, HT_LANES), lambda b, p, pt: (b, 0, 0)),
                  pl.BlockSpec((None, HT_LANES, GROUP_COLS), lambda b, p, pt: (b, 0, 0))]
        + [page_spec(2, i) for i in range(pps)] + [page_spec(3, i) for i in range(pps)]
        + [
            pl.BlockSpec((per_seq, GROUP_COLS), lambda b, p, pt: (row_block0 + b, 2)),
            pl.BlockSpec((per_seq, GROUP_COLS), lambda b, p, pt: (row_block0 + b, 3)),
            pl.BlockSpec((None, SEL_PAD, HT_LANES), lambda b, p, pt: (b, 0, 0)),
            pl.BlockSpec((None, 3, 1, HT_LANES), lambda b, p, pt: (b, 0, 0, 0)),
            pl.BlockSpec((4, PAGE_SIZE, HT_LANES), lambda b, p, pt: (0, 0, 0)),
        ],
        out_specs=pl.BlockSpec((per_seq, D_MODEL), lambda b, p, pt: (b, 0)),
        scratch_shapes=[
            pltpu.VMEM((GROUP_COLS, HT_LANES), BF16),
            pltpu.VMEM((HT_LANES, GROUP_COLS), BF16),
            pltpu.VMEM((1, 1, HT_LANES), F32),
            pltpu.VMEM((1, 1, HT_LANES), F32),
            pltpu.VMEM((1, GROUP_COLS, HT_LANES), F32),
        ],
    )
    return pl.pallas_call(
        kern,
        grid_spec=grid_spec,
        out_shape=jax.ShapeDtypeStruct((n_seq * per_seq, D_MODEL), F32),
        compiler_params=_cparams("parallel", "arbitrary"),
        name="nsa_dec_sel",
    )(page_table, q_bd, q_bd.transpose(0, 2, 1), *([pool_t] * (2 * pps)), kv_new, kv_new, sel, gate_rows, bias)


def _nsa_dec_win_fm_kernel(qbd_ref, qbdt_ref, wt_ref, wn_ref, g_ref, bias_ref, o_ref, ws_ref, m_ref, l_ref, acc_ref, *,
                           per_seq):
    n_past = wt_ref.shape[1]
    n_tiles = n_past // PAGE_SIZE
    qz = (qbd_ref[...].astype(F32) * HEAD_DIM ** -0.5).astype(BF16)
    qzt = (qbdt_ref[...].astype(F32) * HEAD_DIM ** -0.5).astype(BF16)
    _reset_softmax(m_ref, l_ref, acc_ref)
    krow = lax.broadcasted_iota(jnp.int32, (PAGE_SIZE, HT_LANES), 0)
    tq = _mod_pow2(lax.broadcasted_iota(jnp.int32, (PAGE_SIZE, HT_LANES), 1), per_seq)
    s_qm = _dot(qzt, wt_ref[:GROUP_COLS, :].astype(BF16))
    for w in range(n_tiles):
        keys = slice(w * PAGE_SIZE, (w + 1) * PAGE_SIZE)
        mk = (krow > tq) if w == 0 else None
        bias = bias_ref[0] if w == n_tiles - 1 else bias_ref[3]
        v_t = wt_ref[GROUP_COLS:, keys].astype(BF16)
        _dec_update(s_qm[:, keys].T + bias, mk, lambda pr, v_t=v_t: _dot(v_t, pr), m_ref, l_ref, acc_ref)
    pad = PAGE_SIZE - per_seq
    new = jnp.concatenate([wn_ref[...], jnp.zeros((pad, 2 * GROUP_COLS), F32)], axis=0)
    s_t = _dot(new[:, :GROUP_COLS].astype(BF16), qz) + bias_ref[1]
    _dec_update(s_t, krow <= tq, _pv_key_major(new[:, GROUP_COLS:].astype(BF16)), m_ref, l_ref, acc_ref)
    o_ref[...] = _dec_finish(m_ref, l_ref, acc_ref, g_ref[2], per_seq)
    shifted = pltpu.roll(wt_ref[...], n_past - per_seq, 1)
    ws_ref[...] = shifted
    lane = lax.broadcasted_iota(jnp.int32, (2 * GROUP_COLS, PAGE_SIZE), 1)
    new_t = pltpu.roll(new.T, PAGE_SIZE - per_seq, 1)
    ws_ref[:, n_past - PAGE_SIZE:] = jnp.where(lane >= PAGE_SIZE - per_seq, new_t, shifted[:, n_past - PAGE_SIZE:])


def nsa_dec_win_fm(q_bd, win_t, layer, win_new, gate_rows, bias, row_block0, per_seq):
    _, n_seq, n_feat, n_past = win_t.shape
    assert n_past == WINDOW and n_past % PAGE_SIZE == 0 and PAGE_SIZE + 1 >= T5_SATURATION_DIST
    kern = functools.partial(_nsa_dec_win_fm_kernel, per_seq=per_seq)
    return pl.pallas_call(
        kern,
        grid=(n_seq,),
        in_specs=[
            pl.BlockSpec((None, GROUP_COLS, HT_LANES), lambda b: (b, 0, 0)),
            pl.BlockSpec((None, HT_LANES, GROUP_COLS), lambda b: (b, 0, 0)),
            pl.BlockSpec((None, None, n_feat, n_past), lambda b: (layer, b, 0, 0)),
            pl.BlockSpec((per_seq, n_feat), lambda b: (row_block0 + b, 0)),
            pl.BlockSpec((None, 3, 1, HT_LANES), lambda b: (b, 0, 0, 0)),
            pl.BlockSpec((4, PAGE_SIZE, HT_LANES), lambda b: (0, 0, 0)),
        ],
        out_specs=[
            pl.BlockSpec((per_seq, D_MODEL), lambda b: (b, 0)),
            pl.BlockSpec((None, n_feat, n_past), lambda b: (b, 0, 0)),
        ],
        out_shape=[
            jax.ShapeDtypeStruct((n_seq * per_seq, D_MODEL), F32),
            jax.ShapeDtypeStruct((n_seq, n_feat, n_past), F32),
        ],
        scratch_shapes=[
            pltpu.VMEM((1, 1, HT_LANES), F32),
            pltpu.VMEM((1, 1, HT_LANES), F32),
            pltpu.VMEM((1, GROUP_COLS, HT_LANES), F32),
        ],
        compiler_params=_cparams("parallel"),
        name="nsa_dec_win",
    )(q_bd, q_bd.transpose(0, 2, 1), win_t, win_new, gate_rows, bias)


def nsa_decode_fm(table, page_table, pool_t, win_t, layer, q_rows, kv_new, win_new, gates_rows, cw, row_block0, pos0,
                  per_seq):
    n_seq, n_pages = page_table.shape
    pe_t, w1_bd, w2_bd = cw
    past = n_pages * PAGE_SIZE
    s_total = past + per_seq
    n_cmp = (s_total - CMP_LEN) // CMP_STRIDE + 1
    n_sel = -(-s_total // SEL_LEN)
    assert n_cmp == past // CMP_STRIDE - 1 and n_sel == past // SEL_LEN + 1 and per_seq <= CMP_STRIDE
    assert pos0 == past and pos0 % SEL_LEN == 0 and pos0 >= WINDOW and past // CMP_STRIDE == PAGE_SIZE
    bias = nsa_dec_bias(table, pos0, per_seq)
    ctok = nsa_dec_compress(nsa_dec_gather_fm(page_table, pool_t, layer), pe_t, w1_bd, w2_bd).astype(BF16)
    rows_per_seq = past // CMP_STRIDE
    k_cmp = ctok[0].reshape(n_seq, rows_per_seq, GROUP_COLS)
    v_cmp = ctok[1].reshape(n_seq, rows_per_seq, GROUP_COLS)
    q_bd = group_block_diag_queries(q_rows, n_seq, per_seq)
    gate_rows = gates_rows.reshape(n_seq, per_seq, 3, N_HEADS).transpose(0, 2, 3, 1).reshape(n_seq, 3, 1, HT_LANES)
    o_cmp, sel = nsa_dec_cmp(q_bd, k_cmp, v_cmp, gate_rows, bias, n_cmp, n_sel, pos0, per_seq)
    o_slc = nsa_dec_sel_fm(page_table, q_bd, pool_t, layer, kv_new, sel, gate_rows, bias, row_block0, per_seq)
    o_win, win_state_t = nsa_dec_win_fm(q_bd, win_t, layer, win_new, gate_rows, bias, row_block0, per_seq)
    return o_cmp, o_slc, o_win, win_state_t


GLA_SUB = 16
GLA_BLOCK = 512


def _gla_kernel(q_ref, k_ref, v_ref, r_ref, a_ref, wa_ref, ba_ref, ng_ref, s0_ref, o_ref, sfin_ref, s_ref, *,
                n_chunks, chunk):
    t_blk = pl.program_id(2)

    @pl.when(t_blk == 0)
    def _():
        s_ref[...] = s0_ref[...]

    tri = _tril_bf16(chunk)
    it = lax.broadcasted_iota(jnp.int32, (chunk, chunk), 0)
    js = lax.broadcasted_iota(jnp.int32, (chunk, chunk), 1)
    same_sub = jnp.right_shift(it, 4) == jnp.right_shift(js, 4)
    n_sub = chunk // GLA_SUB

    def chunk_step(c, carry):
        r0 = pl.multiple_of(c * chunk, chunk)
        rows = pl.ds(r0, chunk)
        q = q_ref[rows, :] * GLA_DK ** -0.5
        k = k_ref[rows, :]
        v = v_ref[rows, :].astype(BF16)
        g = _log_sigmoid(_dot(a_ref[rows, :].astype(BF16), wa_ref[...]) + ba_ref[...]) / GLA_TAU
        b = _cumsum_rows(g, tri)
        s_old = s_ref[...]
        o = _dot((q * jnp.exp(b)).astype(BF16), s_old.astype(BF16))
        a_mat = jnp.zeros((chunk, chunk), F32)
        for d in range(GLA_SUB):
            if d == 0:
                val = jnp.sum(q * k, axis=-1, keepdims=True)
            else:
                k_sh = pltpu.roll(k, d, 0)
                b_sh = pltpu.roll(b, d, 0)
                val = jnp.sum(q * k_sh * jnp.exp(jnp.minimum(b - b_sh, 0.0)), axis=-1, keepdims=True)
            a_mat = a_mat + jnp.where((it - js == d) & same_sub, val, 0.0)
        if n_sub > 1:
            blocks = [jnp.zeros((GLA_SUB, chunk), F32)]
            for a in range(1, n_sub):
                ref_b = b[a * GLA_SUB - 1:a * GLA_SUB, :]
                qa = q[a * GLA_SUB:(a + 1) * GLA_SUB] * jnp.exp(b[a * GLA_SUB:(a + 1) * GLA_SUB] - ref_b)
                kk = k * jnp.exp(jnp.minimum(ref_b - b, 0.0))
                blocks.append(_dot_nt(qa.astype(BF16), kk.astype(BF16)))
            off = jnp.concatenate(blocks, axis=0)
            a_mat = a_mat + jnp.where(jnp.right_shift(js, 4) < jnp.right_shift(it, 4), off, 0.0)
        o = o + _dot(a_mat.astype(BF16), v)
        b_last = b[chunk - 1:chunk, :]
        kd = k * jnp.exp(b_last - b)
        pad_k = jnp.zeros((GLA_DK - chunk, GLA_DK), F32)
        pad_v = jnp.zeros((GLA_DK - chunk, GLA_DV), BF16)
        kd_t = jnp.concatenate([kd, pad_k], axis=0).T
        dec = jnp.broadcast_to(jnp.exp(b_last), (GLA_DK, GLA_DK)).T
        s_ref[...] = (s_old * jnp.concatenate([dec] * (GLA_DV // GLA_DK), axis=1)
                      + _dot(kd_t.astype(BF16), jnp.concatenate([v, pad_v], axis=0)))
        y = o * lax.rsqrt(jnp.mean(o * o, axis=-1, keepdims=True) + EPS) * ng_ref[...]
        rr = r_ref[rows, :]
        o_ref[rows, :] = y * (rr * _sigmoid(rr))
        return carry

    lax.fori_loop(0, n_chunks, chunk_step, 0)

    @pl.when(t_blk == pl.num_programs(2) - 1)
    def _():
        sfin_ref[...] = s_ref[...]


def gla_prompt(q, k, v, r, a_lr, w_a2, b_a, norm_g, s0, n_batch, seq, row_block0=0):
    chunk = GLA_CHUNK if seq % GLA_CHUNK == 0 else seq
    blk = min(GLA_BLOCK, seq)
    nt = seq // blk
    m = n_batch * seq
    kern = functools.partial(_gla_kernel, n_chunks=blk // chunk, chunk=chunk)
    row = lambda b, h, t: row_block0 + b * nt + t
    return pl.pallas_call(
        kern,
        grid=(n_batch, GLA_HEADS, nt),
        in_specs=[
            pl.BlockSpec((blk, GLA_DK), lambda b, h, t: (row(b, h, t), h)),
            pl.BlockSpec((blk, GLA_DK), lambda b, h, t: (row(b, h, t), h)),
            pl.BlockSpec((blk, GLA_DV), lambda b, h, t: (row(b, h, t), h)),
            pl.BlockSpec((blk, GLA_DV), lambda b, h, t: (row(b, h, t), h)),
            pl.BlockSpec((blk, GLA_RANK), lambda b, h, t: (row(b, h, t), 0)),
            pl.BlockSpec((GLA_RANK, GLA_DK), lambda b, h, t: (0, h)),
            pl.BlockSpec((1, GLA_DK), lambda b, h, t: (0, h)),
            pl.BlockSpec((1, GLA_DV), lambda b, h, t: (0, 0)),
            pl.BlockSpec((None, None, GLA_DK, GLA_DV), lambda b, h, t: (b, h, 0, 0)),
        ],
        out_specs=[
            pl.BlockSpec((blk, GLA_DV), lambda b, h, t: (b * nt + t, h)),
            pl.BlockSpec((None, None, GLA_DK, GLA_DV), lambda b, h, t: (b, h, 0, 0)),
        ],
        out_shape=[
            jax.ShapeDtypeStruct((m, D_MODEL), F32),
            jax.ShapeDtypeStruct((n_batch, GLA_HEADS, GLA_DK, GLA_DV), F32),
        ],
        scratch_shapes=[pltpu.VMEM((GLA_DK, GLA_DV), F32)],
        compiler_params=_cparams("parallel", "parallel", "arbitrary"),
        name="gla",
    )(q, k, v, r, a_lr, w_a2.astype(BF16), b_a.reshape(1, GLA_QK), norm_g.reshape(1, GLA_DV), s0)


def _gla_decode_kernel(q_ref, k_ref, v_ref, r_ref, a_ref, wa_ref, ba_ref, ng_ref, s0_ref, o_ref, s_ref, *, n_rows):
    pad_rows = 2 * n_rows

    def pad_to(x, rows):
        return jnp.concatenate([x, jnp.zeros((rows - x.shape[0], x.shape[1]), x.dtype)], axis=0)

    row = lax.broadcasted_iota(jnp.int32, (n_rows, GLA_DK), 0)
    row_v = lax.broadcasted_iota(jnp.int32, (n_rows, GLA_DV), 0)
    q = q_ref[...] * GLA_DK ** -0.5
    k = k_ref[...]
    v = v_ref[...]
    g = _dot(pad_to(a_ref[...], pad_rows).astype(BF16), wa_ref[...])[:n_rows]
    g = _log_sigmoid(g + ba_ref[...]) / GLA_TAU
    b = g
    shift = 1
    while shift < n_rows:
        b = b + jnp.where(row >= shift, pltpu.roll(b, shift, 0), 0.0)
        shift *= 2
    s_old = s0_ref[...]
    o = _dot(pad_to(q * jnp.exp(b), pad_rows).astype(BF16), s_old.astype(BF16))[:n_rows]
    for d in range(n_rows):
        if d == 0:
            val = jnp.sum(q * k, axis=-1, keepdims=True)
            o = o + val * v
        else:
            k_sh = pltpu.roll(k, d, 0)
            b_sh = pltpu.roll(b, d, 0)
            val = jnp.sum(q * k_sh * jnp.exp(jnp.minimum(b - b_sh, 0.0)), axis=-1, keepdims=True)
            o = o + jnp.where(row_v >= d, val * pltpu.roll(v, d, 0), 0.0)
    b_last = b[n_rows - 1:n_rows, :]
    kd_t = pad_to(k * jnp.exp(b_last - b), GLA_DK).T
    dec = jnp.broadcast_to(jnp.exp(b_last), (GLA_DK, GLA_DK)).T
    s_ref[...] = (s_old * jnp.concatenate([dec] * (GLA_DV // GLA_DK), axis=1)
                  + _dot(kd_t.astype(BF16), pad_to(v, GLA_DK).astype(BF16)))
    y = o * lax.rsqrt(jnp.mean(o * o, axis=-1, keepdims=True) + EPS) * ng_ref[...]
    rr = r_ref[...]
    o_ref[...] = y * (rr * _sigmoid(rr))


def gla_decode(q, k, v, r, a_lr, w_a2, b_a, norm_g, s0, n_seq, per_seq, row_block0):
    assert per_seq % GLA_CHUNK != 0 and per_seq % 8 == 0
    kern = functools.partial(_gla_decode_kernel, n_rows=per_seq)
    row = lambda b, h: (row_block0 + b, h)
    return pl.pallas_call(
        kern,
        grid=(n_seq, GLA_HEADS),
        in_specs=[
            pl.BlockSpec((per_seq, GLA_DK), row),
            pl.BlockSpec((per_seq, GLA_DK), row),
            pl.BlockSpec((per_seq, GLA_DV), row),
            pl.BlockSpec((per_seq, GLA_DV), row),
            pl.BlockSpec((per_seq, GLA_RANK), lambda b, h: (row_block0 + b, 0)),
            pl.BlockSpec((GLA_RANK, GLA_DK), lambda b, h: (0, h)),
            pl.BlockSpec((1, GLA_DK), lambda b, h: (0, h)),
            pl.BlockSpec((1, GLA_DV), lambda b, h: (0, 0)),
            pl.BlockSpec((None, None, GLA_DK, GLA_DV), lambda b, h: (b, h, 0, 0)),
        ],
        out_specs=[
            pl.BlockSpec((per_seq, GLA_DV), lambda b, h: (b, h)),
            pl.BlockSpec((None, None, GLA_DK, GLA_DV), lambda b, h: (b, h, 0, 0)),
        ],
        out_shape=[
            jax.ShapeDtypeStruct((n_seq * per_seq, D_MODEL), F32),
            jax.ShapeDtypeStruct((n_seq, GLA_HEADS, GLA_DK, GLA_DV), F32),
        ],
        compiler_params=_cparams("parallel", "parallel"),
        name="gla_decode",
    )(q, k, v, r, a_lr, w_a2.astype(BF16), b_a.reshape(1, GLA_QK), norm_g.reshape(1, GLA_DV), s0)


def masked_softmax(logits, mask, axis):
    logits = jnp.where(mask, logits.astype(jnp.float32), NEG)
    m = jnp.max(logits, axis=axis, keepdims=True)
    e = jnp.exp(logits - m) * mask
    return e / jnp.maximum(jnp.sum(e, axis=axis, keepdims=True), 1e-30)


def t5_bucket(dist):
    n = jnp.maximum(dist, 0)
    max_exact = N_BUCKETS // 2
    nf = jnp.maximum(n, 1).astype(jnp.float32)
    log_part = max_exact + (jnp.log(nf / max_exact) / math.log(MAX_DISTANCE / max_exact)
                            * (N_BUCKETS - max_exact)).astype(jnp.int32)
    return jnp.where(n < max_exact, n, jnp.minimum(log_part, N_BUCKETS - 1))


def head_bias(dist, table):
    b = table.astype(jnp.float32)[t5_bucket(dist)]
    return b.reshape(*dist.shape, NSA_KV_GROUPS, NSA_HPG).transpose(2, 3, 0, 1)


def gather_pages(pool, page_table):
    rows = pool[page_table]
    return rows.reshape(page_table.shape[0], -1, *pool.shape[2:])


def query_blocks(fn, args, axes):
    T = args[0].shape[axes[0]]
    if T % Q_BLOCK != 0 or T <= Q_BLOCK:
        return fn(*args)
    nb = T // Q_BLOCK

    def split(a, ax):
        a = a.reshape(a.shape[:ax] + (nb, Q_BLOCK) + a.shape[ax + 1:])
        return jnp.moveaxis(a, ax, 0)

    out = lax.map(lambda blk: fn(*blk), tuple(split(a, ax) for a, ax in zip(args, axes)))
    out = jnp.moveaxis(out, 0, 1)
    return out.reshape(out.shape[:1] + (T,) + out.shape[3:])


def compress_tokens(kv, pe, w1, w2):
    B, S = kv.shape[:2]
    n_cmp = (S - CMP_LEN) // CMP_STRIDE + 1
    idx = jnp.arange(n_cmp)[:, None] * CMP_STRIDE + jnp.arange(CMP_LEN)[None, :]
    blocks = kv[:, idx] + pe[None, None, :, None, :]
    z = blocks.transpose(0, 1, 3, 2, 4).reshape(B, n_cmp, NSA_KV_GROUPS, CMP_LEN * HEAD_DIM)
    return jax.nn.gelu(z @ w1) @ w2


def gqa_window_attend(q, k, v, q_pos, k_pos, table):
    B, Tq = q.shape[:2]
    qg = q.reshape(B, Tq, NSA_KV_GROUPS, NSA_HPG, HEAD_DIM)
    logits = jnp.einsum('bqghd,bkgd->bghqk', qg, k).astype(jnp.float32) * HEAD_DIM ** -0.5
    dist = q_pos[:, None] - k_pos[None, :]
    mask = (dist >= 0) & (dist < WINDOW) & (k_pos[None, :] >= 0)
    p = masked_softmax(logits + head_bias(dist, table), mask, -1)
    o = jnp.einsum('bghqk,bkgd->bqghd', p, v)
    return o.reshape(B, Tq, N_HEADS, HEAD_DIM)


def nsa_core_sample(q, kv_new, win_new, gates, phi_pe, phi_w1, phi_w2, table, past_kv, past_win, pos0):
    B, T = q.shape[:2]
    G = NSA_KV_GROUPS
    gates = jax.nn.sigmoid(gates).reshape(B, T, 3, N_HEADS)
    kv = jnp.concatenate([past_kv, kv_new], axis=1)
    S = kv.shape[1]
    q_pos = pos0 + jnp.arange(T)
    qg = q.reshape(B, T, G, NSA_HPG, HEAD_DIM)
    k_cmp = compress_tokens(kv[:, :, 0], phi_pe[0], phi_w1[0], phi_w2[0])
    v_cmp = compress_tokens(kv[:, :, 1], phi_pe[1], phi_w1[1], phi_w2[1])
    n_cmp = k_cmp.shape[1]
    c_start = jnp.arange(n_cmp) * CMP_STRIDE
    dist_c = q_pos[:, None] - (c_start + CMP_LEN - 1)[None, :]
    logits_c = jnp.einsum('bqghd,bcgd->bghqc', qg, k_cmp).astype(jnp.float32) * HEAD_DIM ** -0.5
    p_cmp = masked_softmax(logits_c + head_bias(dist_c, table), dist_c >= 0, -1)
    o_cmp = jnp.einsum('bghqc,bcgd->bqghd', p_cmp, v_cmp).reshape(B, T, N_HEADS, HEAD_DIM)
    n_sel = -(-S // SEL_LEN)
    s_start = jnp.arange(n_sel) * SEL_LEN
    incidence = ((c_start[:, None] < s_start[None, :] + SEL_LEN)
                 & (c_start[:, None] + CMP_LEN > s_start[None, :])).astype(jnp.float32)
    imp = jnp.einsum('bghqc,cj->bgqj', p_cmp, incidence)
    cur = q_pos // SEL_LEN
    j = jnp.arange(n_sel)
    forced = (j[None, :] == 0) | (j[None, :] == cur[:, None]) | (j[None, :] == cur[:, None] - 1)
    avail = j[None, :] <= cur[:, None]
    score = jnp.where(forced, BIG, jnp.where(avail, imp, -BIG))
    _, sel = lax.top_k(score, min(N_SELECT, n_sel))
    member = jnp.sum(jax.nn.one_hot(sel, n_sel, dtype=jnp.float32), axis=-2) > 0.5
    k_pos = jnp.arange(S)
    key_member = jnp.take(member, k_pos // SEL_LEN, axis=-1)
    dist = q_pos[:, None] - k_pos[None, :]
    mask = key_member[:, :, None] & (dist >= 0)[None, None, None]
    logits = jnp.einsum('bqghd,bkgd->bghqk', qg, kv[:, :, 2]).astype(jnp.float32) * HEAD_DIM ** -0.5
    p = masked_softmax(logits + head_bias(dist, table), mask, -1)
    o_slc = jnp.einsum('bghqk,bkgd->bqghd', p, kv[:, :, 3]).reshape(B, T, N_HEADS, HEAD_DIM)
    win_all = jnp.concatenate([past_win, win_new], axis=1)
    lw = win_all.shape[1]
    kw_pos = pos0 + T - lw + jnp.arange(lw)
    o_win = gqa_window_attend(q, win_all[:, :, 0], win_all[:, :, 1], q_pos, kw_pos, table)
    win_state = win_all[:, win_all.shape[1] - min(WINDOW, win_all.shape[1]):]
    o = (gates[:, :, 0, :, None] * o_cmp + gates[:, :, 1, :, None] * o_slc + gates[:, :, 2, :, None] * o_win)
    return o.reshape(B * T, D_MODEL), win_state


def fox_attend(q, dq, q_pos, k, v, dk, k_pos):
    logits = jnp.einsum('bqhd,bkhd->bhqk', q, k).astype(jnp.float32) * HEAD_DIM ** -0.5
    logits = logits + dq.transpose(0, 2, 1)[..., None] - dk.transpose(0, 2, 1)[:, :, None, :]
    p = masked_softmax(logits, q_pos[:, None] >= k_pos[None, :], -1)
    return jnp.einsum('bhqk,bkhd->bqhd', p, v)


def fox_core(q, kv_new, f_raw, b_f, past_kv, past_logf, pos0):
    B, T = q.shape[:2]
    logf_new = jax.nn.log_sigmoid(f_raw + b_f.astype(jnp.float32))
    if past_kv is None:
        kv, logf = kv_new, logf_new
    else:
        kv = jnp.concatenate([past_kv, kv_new], axis=1)
        logf = jnp.concatenate([past_logf.astype(jnp.float32), logf_new], axis=1)
    S = kv.shape[1]
    cum = jnp.cumsum(logf, axis=1)
    k_pos = jnp.arange(S)
    q_pos = pos0 + jnp.arange(T)
    k, v = kv[:, :, 0], kv[:, :, 1]
    o = query_blocks(lambda q_, dq_, p_: fox_attend(q_, dq_, p_, k, v, cum, k_pos), [q, cum[:, S - T:], q_pos], [1, 1, 0])
    return o.reshape(B * T, D_MODEL), logf_new


def gla_chunk(S, q, k, v, g):
    C = q.shape[1]
    b = jnp.cumsum(g, axis=1)
    causal = jnp.tril(jnp.ones((C, C), dtype=bool))
    diff = b[:, :, None] - b[:, None, :]
    decay = jnp.exp(jnp.where(causal[None, :, :, None, None], diff, NEG))
    A = jnp.einsum('bthd,bshd,btshd->bhts', q, k, decay)
    o = jnp.einsum('bhts,bshv->bthv', A, v) + jnp.einsum('bthd,bhdv->bthv', q * jnp.exp(b), S)
    b_last = b[:, -1]
    S_new = jnp.exp(b_last)[..., None] * S + jnp.einsum('bshd,bshv->bhdv', k * jnp.exp(b_last[:, None] - b), v)
    return o, S_new


def gla_scan(S0, q, k, v, g):
    B, T = q.shape[:2]
    c = GLA_CHUNK if T % GLA_CHUNK == 0 else T
    nc = T // c

    def to_chunks(a):
        return jnp.moveaxis(a.reshape(B, nc, c, *a.shape[2:]), 1, 0)

    def step(S, blk):
        o, S = gla_chunk(S, *blk)
        return S, o

    S_fin, o = lax.scan(step, S0, tuple(to_chunks(a) for a in (q, k, v, g)))
    return jnp.moveaxis(o, 0, 1).reshape(B, T, GLA_HEADS, GLA_DV), S_fin


def gla_core(q, k, v, r, a_lr, w_a2, b_a, norm_g, S0):
    B, T = q.shape[:2]
    q = q.reshape(B, T, GLA_HEADS, GLA_DK) * GLA_DK ** -0.5
    k = k.reshape(B, T, GLA_HEADS, GLA_DK)
    v = v.reshape(B, T, GLA_HEADS, GLA_DV)
    g = jax.nn.log_sigmoid(a_lr @ w_a2.astype(jnp.float32) + b_a.astype(jnp.float32)) / GLA_TAU
    g = g.reshape(B, T, GLA_HEADS, GLA_DK)
    if S0 is None:
        S0 = jnp.zeros((B, GLA_HEADS, GLA_DK, GLA_DV), jnp.float32)
    o, S_fin = gla_scan(S0.astype(jnp.float32), q, k, v, g)
    xf = o
    y = xf * lax.rsqrt(jnp.mean(xf * xf, axis=-1, keepdims=True) + EPS) * norm_g
    o = y.reshape(B, T, D_MODEL) * jax.nn.silu(r)
    return o.reshape(B * T, D_MODEL), S_fin


_C1 = D_MODEL + NSA_KV_COLS
_C2 = _C1 + NSA_WIN_COLS
NSA_SPLITS = ((D_MODEL, _C1, F32), (D_MODEL, _C1, BF16), (_C1, _C2, F32), (_C1, _C2, BF16))
NSA_SPLITS_T = ((0, D_MODEL, BF16), (_C1 - GROUP_COLS, _C1, BF16), (_C2 - GROUP_COLS, _C2, BF16),
                (_C2, _C2 + 3 * N_HEADS, F32))
FOX_SPLITS = ((D_MODEL, 3 * D_MODEL, F32), (D_MODEL, 2 * D_MODEL, BF16), (3 * D_MODEL, 3 * D_MODEL + N_HEADS, F32))
FOX_SPLITS_T = ((0, D_MODEL, BF16), (2 * D_MODEL, 3 * D_MODEL, BF16))
GLA_SPLITS = ((0, GLA_QK, F32), (GLA_QK, 2 * GLA_QK, F32), (2 * GLA_QK, 2 * GLA_QK + D_MODEL, F32),
              (2 * GLA_QK + D_MODEL, 2 * GLA_QK + 2 * D_MODEL, F32),
              (2 * GLA_QK + 2 * D_MODEL, 2 * GLA_QK + 2 * D_MODEL + GLA_RANK, F32))


def kernel(x_prompt, x_sample, cache_nsa_kv, state_nsa_win, cache_fox_kv, cache_fox_logf, state_gla, page_table, rel_bias, norm_g, final_g, nsa_w_in, nsa_phi_pe, nsa_phi_w1, nsa_phi_w2, nsa_w_out, fox_w_in, fox_b_f, fox_w_out, gla_w_in, gla_w_a2, gla_b_a, gla_norm_g, gla_w_out, mlp_w1, mlp_w2):
    bp, tp, _ = x_prompt.shape
    bs, ts, _ = x_sample.shape
    mp, ms = bp * tp, bs * ts
    pos_s = page_table.shape[1] * PAGE_SIZE
    x = jnp.concatenate([x_prompt.reshape(mp, D_MODEL), x_sample.reshape(ms, D_MODEL)], axis=0)
    toep = t5_tiles(rel_bias, ATT_TILE)
    nsa_pool_t = feature_major_pages(cache_nsa_kv)
    nsa_win_t = feature_major_pages(state_nsa_win)
    fox_pool_t = feature_major_pages(cache_fox_kv)
    fox_lf_t = feature_major_pages(cache_fox_logf)

    def both(a, shape_p, shape_s):
        return a[:mp].reshape(shape_p), a[mp:].reshape(shape_s)

    nsa_kv_p, nsa_kv_s, nsa_win_p, nsa_win_s = [], [], [], []
    fox_kv_p, fox_kv_s, fox_lf_p, fox_lf_s, gla_p, gla_s = [], [], [], [], [], []
    G = NSA_KV_GROUPS
    for i in range(DEPTH):
        kind, li = i % N_MIXERS, i // N_MIXERS
        if kind == 0:
            kvn, kv_bf, winn, win_bf, q_t, vslc_t, vwin_t, gates_t = norm_proj(
                x, norm_g[i, 0], nsa_w_in[li], NSA_SPLITS, NSA_SPLITS_T)
            cw = compress_weights(nsa_phi_pe[li], nsa_phi_w1[li], nsa_phi_w2[li])
            o_parts_p = nsa_prompt_t(rel_bias, toep, q_t, kvn, kv_bf, win_bf, vslc_t, vwin_t, gates_t, cw, bp, tp)
            kv_p, kv_s = both(kvn, (bp, tp, 4, G, HEAD_DIM), (bs, ts, 4, G, HEAD_DIM))
            w_p, w_s = both(winn, (bp, tp, 2, G, HEAD_DIM), (bs, ts, 2, G, HEAD_DIM))
            *o_parts_s, ws_t = nsa_decode_fm(rel_bias, page_table, nsa_pool_t, nsa_win_t, li, q_t[:, mp:].T, kvn, winn,
                                             gates_t[:, mp:].T, cw, mp // ts, pos_s, ts)
            ws_p = w_p[:, tp - min(WINDOW, tp):]
            ws_s = ws_t.reshape(bs, 2, G, HEAD_DIM, -1).transpose(0, 4, 1, 2, 3)
            nsa_kv_p.append(kv_p); nsa_kv_s.append(kv_s); nsa_win_p.append(ws_p); nsa_win_s.append(ws_s)
            o_parts_p = list(o_parts_p)
            w_out = nsa_w_out[li]
        elif kind == 1:
            kvn, k_bf, fr, q_t, v_t = norm_proj(x, norm_g[i, 0], fox_w_in[li], FOX_SPLITS, FOX_SPLITS_T)
            kv_p, kv_s = both(kvn, (bp, tp, 2, N_HEADS, HEAD_DIM), (bs, ts, 2, N_HEADS, HEAD_DIM))
            lf_all, d_all = fox_prep_prompt(fr, fox_b_f[li], bp, tp)
            o_p = fox_attn_t(q_t, k_bf, v_t, d_all, bp, tp)
            lf_p = lf_all.reshape(bp, tp, N_HEADS)
            q_bd = block_diag_queries(q_t[:, mp:].T, bs, ts, N_HEADS)
            o_s, lf_s = fox_decode_fm(page_table, q_bd, fox_pool_t, fox_lf_t, li, kvn, fr, fox_b_f[li], mp // ts,
                                      bs, ts)
            lf_s = lf_s.reshape(bs, ts, N_HEADS)
            fox_kv_p.append(kv_p); fox_kv_s.append(kv_s); fox_lf_p.append(lf_p); fox_lf_s.append(lf_s)
            o_parts_p, o_parts_s = [o_p], [o_s]
            w_out = fox_w_out[li]
        else:
            q, k, v, r, a = norm_proj(x, norm_g[i, 0], gla_w_in[li], GLA_SPLITS)
            s_zero = jnp.zeros((bp, GLA_HEADS, GLA_DK, GLA_DV), F32)
            o_p, s_p = gla_prompt(q, k, v, r, a, gla_w_a2[li], gla_b_a[li], gla_norm_g[li], s_zero, bp, tp)
            o_s, s_s = gla_decode(q, k, v, r, a, gla_w_a2[li], gla_b_a[li], gla_norm_g[li], state_gla[li], bs, ts,
                                  mp // ts)
            gla_p.append(s_p); gla_s.append(s_s)
            o_parts_p, o_parts_s = [o_p], [o_s]
            w_out = gla_w_out[li]
        x = out_mlp(x, o_parts_p, o_parts_s, w_out.astype(BF16), norm_g[i, 1], mlp_w1[i].astype(BF16),
                    mlp_w2[i].astype(BF16))
    y = final_norm(x, final_g)
    y_p, y_s = both(y, (bp, tp, D_MODEL), (bs, ts, D_MODEL))
    return (y_p, y_s, jnp.stack(nsa_kv_p), jnp.stack(nsa_kv_s), jnp.stack(nsa_win_p), jnp.stack(nsa_win_s),
            jnp.stack(fox_kv_p), jnp.stack(fox_kv_s), jnp.stack(fox_lf_p), jnp.stack(fox_lf_s),
            jnp.stack(gla_p), jnp.stack(gla_s))
```

```python
import functools
import math

import jax
import jax.numpy as jnp
from jax import lax
from jax.experimental import pallas as pl
from jax.experimental.pallas import tpu as pltpu

D_MODEL = 1024
DEPTH = 4
PAGE_SIZE = 128
N_MIXERS = 3
N_HEADS = 16
HEAD_DIM = D_MODEL // N_HEADS
NSA_KV_GROUPS = 4
NSA_HPG = N_HEADS // NSA_KV_GROUPS
CMP_LEN = 32
CMP_STRIDE = 16
CMP_HIDDEN = 2 * HEAD_DIM
SEL_LEN = 64
N_SELECT = 16
WINDOW = 512
N_BUCKETS = 32
MAX_DISTANCE = 128
GLA_HEADS = 4
GLA_DK = D_MODEL // 2 // GLA_HEADS
GLA_DV = D_MODEL // GLA_HEADS
GLA_QK = GLA_HEADS * GLA_DK
GLA_RANK = 16
GLA_TAU = 16.0
GLA_CHUNK = 64
D_FF = 4 * D_MODEL
EPS = 1e-6
NEG = -1e30
BIG = 1e9
TAKEN = -3e38
NSA_KV_COLS = 4 * NSA_KV_GROUPS * HEAD_DIM
NSA_WIN_COLS = 2 * NSA_KV_GROUPS * HEAD_DIM
GROUP_COLS = NSA_KV_GROUPS * HEAD_DIM

F32 = jnp.float32
BF16 = jnp.bfloat16

V7X_VMEM_LIMIT_BYTES = 48 * 1024 * 1024
V7X_LANES = 128
TOKEN_TILE = 512
FF_TILE = 1024
ATT_TILE = 256
GROUPS_PER_PASS = 4
SEL_PAD = V7X_LANES
T5_SATURATION_DIST = 113


def _cparams(*sem):
    return pltpu.CompilerParams(dimension_semantics=sem, vmem_limit_bytes=V7X_VMEM_LIMIT_BYTES)


def _rms(x, g):
    return x * lax.rsqrt(jnp.mean(x * x, axis=-1, keepdims=True) + EPS) * g


def _sigmoid(x):
    return 1.0 / (1.0 + jnp.exp(-x))


def _log_sigmoid(x):
    return jnp.minimum(x, 0.0) - jnp.log(1.0 + jnp.exp(-jnp.abs(x)))


def _gelu_tanh(x):
    return 0.5 * x * (1.0 + jnp.tanh(math.sqrt(2.0 / math.pi) * (x + 0.044715 * (x * x * x))))


def _dot(a, b):
    return jnp.dot(a, b, preferred_element_type=F32)


def _dot_nt(a, b):
    return lax.dot_general(a, b, (((1,), (1,)), ((), ())), preferred_element_type=F32)


def _split_bf16(x, parts):
    out = []
    for _ in range(parts):
        hi = x.astype(BF16)
        out.append(hi)
        x = x - hi.astype(F32)
    return out


def _t5_bucket(dist):
    n = jnp.maximum(dist, 0)
    max_exact = N_BUCKETS // 2
    nf = jnp.maximum(n, 1).astype(F32)
    log_part = max_exact + (jnp.log(nf / max_exact) / math.log(MAX_DISTANCE / max_exact)
                            * (N_BUCKETS - max_exact)).astype(jnp.int32)
    return jnp.where(n < max_exact, n, jnp.minimum(log_part, N_BUCKETS - 1))


def _bias_from_bucket(bucket, tbl_ref, h):
    val = jnp.full(bucket.shape, tbl_ref[0, h], F32)
    for k in range(1, N_BUCKETS):
        val = jnp.where(bucket == k, tbl_ref[k, h], val)
    return val


def _div_pow2(x, d):
    assert d & (d - 1) == 0
    return jnp.right_shift(x, d.bit_length() - 1)


def _mod_pow2(x, d):
    assert d & (d - 1) == 0
    return jnp.bitwise_and(x, d - 1)


def _smem_spec():
    return pl.BlockSpec(memory_space=pltpu.SMEM)


def _norm_proj_kernel(x_ref, g_ref, w_ref, wt_ref, *out_refs, splits, splits_t):
    h = _rms(x_ref[...], g_ref[...]).astype(BF16)
    for (lo, hi, _), o_ref in zip(splits, out_refs):
        o_ref[...] = _dot(h, w_ref[:, lo:hi]).astype(o_ref.dtype)
    for (lo, hi, _), o_ref in zip(splits_t, out_refs[len(splits):]):
        o_ref[...] = _dot_nt(wt_ref[lo:hi, :], h).astype(o_ref.dtype)


def norm_proj(x, g, w, splits, splits_t=()):
    m = x.shape[0]
    n = w.shape[1]
    w_bf = w.astype(BF16)
    kern = functools.partial(_norm_proj_kernel, splits=splits, splits_t=splits_t)
    return pl.pallas_call(
        kern,
        grid=(m // TOKEN_TILE,),
        in_specs=[
            pl.BlockSpec((TOKEN_TILE, D_MODEL), lambda i: (i, 0)),
            pl.BlockSpec((1, D_MODEL), lambda i: (0, 0)),
            pl.BlockSpec((D_MODEL, n), lambda i: (0, 0)),
            pl.BlockSpec((n, D_MODEL), lambda i: (0, 0)),
        ],
        out_specs=[pl.BlockSpec((TOKEN_TILE, hi - lo), lambda i: (i, 0)) for lo, hi, _ in splits]
        + [pl.BlockSpec((hi - lo, TOKEN_TILE), lambda i: (0, i)) for lo, hi, _ in splits_t],
        out_shape=[jax.ShapeDtypeStruct((m, hi - lo), dt) for lo, hi, dt in splits]
        + [jax.ShapeDtypeStruct((hi - lo, m), dt) for lo, hi, dt in splits_t],
        compiler_params=_cparams("parallel"),
        name="norm_proj",
    )(x, g.reshape(1, D_MODEL), w_bf, w_bf.T)


def _out_mlp_kernel(x_ref, *refs, n_o, n_first):
    oa_refs = refs[:n_o]
    ob_refs = refs[n_o:2 * n_o]
    wo_ref, g_ref, w1_ref, w2_ref, y_ref, x1_ref, h_ref, acc_ref = refs[2 * n_o:]
    i = pl.program_id(0)
    j = pl.program_id(1)

    @pl.when(j == 0)
    def _():
        first = i < n_first
        o = jnp.where(first, oa_refs[0][...], ob_refs[0][...])
        for ra, rb in zip(oa_refs[1:], ob_refs[1:]):
            o = o + jnp.where(first, ra[...], rb[...])
        x1 = x_ref[...] + _dot(o.astype(BF16), wo_ref[...])
        x1_ref[...] = x1
        h_ref[...] = _rms(x1, g_ref[...]).astype(BF16)
        acc_ref[...] = jnp.zeros_like(acc_ref)

    a = _dot(h_ref[...], w1_ref[...])
    a = jnp.square(jnp.maximum(a, 0.0)).astype(BF16)
    acc_ref[...] += _dot(a, w2_ref[...])

    @pl.when(j == pl.num_programs(1) - 1)
    def _():
        y_ref[...] = x1_ref[...] + acc_ref[...]


def out_mlp(x, o_first, o_rest, w_out, g, w1, w2):
    m = x.shape[0]
    n_o = len(o_first)
    n_first = o_first[0].shape[0] // TOKEN_TILE
    n_rest = o_rest[0].shape[0] // TOKEN_TILE
    assert o_first[0].shape[0] % TOKEN_TILE == 0 and n_first + n_rest == m // TOKEN_TILE
    kern = functools.partial(_out_mlp_kernel, n_o=n_o, n_first=n_first)
    tok = pl.BlockSpec((TOKEN_TILE, D_MODEL), lambda i, j: (i, 0))
    tok_a = pl.BlockSpec((TOKEN_TILE, D_MODEL), lambda i, j: (jnp.minimum(i, n_first - 1), 0))
    tok_b = pl.BlockSpec((TOKEN_TILE, D_MODEL), lambda i, j: (jnp.maximum(i - n_first, 0), 0))
    return pl.pallas_call(
        kern,
        grid=(m // TOKEN_TILE, D_FF // FF_TILE),
        in_specs=[tok] + [tok_a] * n_o + [tok_b] * n_o + [
            pl.BlockSpec((D_MODEL, D_MODEL), lambda i, j: (0, 0)),
            pl.BlockSpec((1, D_MODEL), lambda i, j: (0, 0)),
            pl.BlockSpec((D_MODEL, FF_TILE), lambda i, j: (0, j)),
            pl.BlockSpec((FF_TILE, D_MODEL), lambda i, j: (j, 0)),
        ],
        out_specs=tok,
        out_shape=jax.ShapeDtypeStruct((m, D_MODEL), F32),
        scratch_shapes=[
            pltpu.VMEM((TOKEN_TILE, D_MODEL), F32),
            pltpu.VMEM((TOKEN_TILE, D_MODEL), BF16),
            pltpu.VMEM((TOKEN_TILE, D_MODEL), F32),
        ],
        compiler_params=_cparams("parallel", "arbitrary"),
        name="out_mlp",
    )(x, *o_first, *o_rest, w_out, g.reshape(1, D_MODEL), w1, w2)


def _final_norm_kernel(x_ref, g_ref, y_ref):
    y_ref[...] = _rms(x_ref[...], g_ref[...])


def final_norm(x, g):
    m = x.shape[0]
    return pl.pallas_call(
        _final_norm_kernel,
        grid=(m // TOKEN_TILE,),
        in_specs=[pl.BlockSpec((TOKEN_TILE, D_MODEL), lambda i: (i, 0)),
                  pl.BlockSpec((1, D_MODEL), lambda i: (0, 0))],
        out_specs=pl.BlockSpec((TOKEN_TILE, D_MODEL), lambda i: (i, 0)),
        out_shape=jax.ShapeDtypeStruct((m, D_MODEL), F32),
        compiler_params=_cparams("parallel"),
        name="final_norm",
    )(x, g.reshape(1, D_MODEL))


def _t5_tiles_kernel(tbl_ref, o_ref, *, tile):
    h = pl.program_id(0)
    ik = lax.broadcasted_iota(jnp.int32, (tile, tile), 0)
    iq = lax.broadcasted_iota(jnp.int32, (tile, tile), 1)
    for dd in range(2):
        o_ref[dd] = _bias_from_bucket(_t5_bucket(dd * tile + iq - ik), tbl_ref, h) - tbl_ref[N_BUCKETS - 1, h]


def t5_tiles(table, tile):
    assert 2 * tile - (tile - 1) >= T5_SATURATION_DIST
    return pl.pallas_call(
        functools.partial(_t5_tiles_kernel, tile=tile),
        grid=(N_HEADS,),
        in_specs=[_smem_spec()],
        out_specs=pl.BlockSpec((None, 2, tile, tile), lambda h: (h, 0, 0, 0)),
        out_shape=jax.ShapeDtypeStruct((N_HEADS, 2, tile, tile), F32),
        compiler_params=_cparams("arbitrary"),
        name="t5_tiles",
    )(table)


def _compress_kernel(x_ref, pe_ref, w1_ref, w2_ref, o_ref, *, rows):
    half = CMP_LEN // 2
    top = jnp.zeros((rows, NSA_KV_GROUPS * CMP_HIDDEN), F32)
    bot = jnp.zeros((rows, NSA_KV_GROUPS * CMP_HIDDEN), F32)
    for l in range(half):
        a = x_ref[:, l, :]
        top = top + _dot((a + pe_ref[l]).astype(BF16), w1_ref[l])
        bot = bot + _dot((a + pe_ref[half + l]).astype(BF16), w1_ref[half + l])
    hid = top + pltpu.roll(bot, rows - 1, 0)
    o_ref[...] = _dot(_gelu_tanh(hid).astype(BF16), w2_ref[...])


def compress_prompt(kvn, n_batch, seq, pe_t, w1_bd, w2_bd):
    rows = seq // CMP_STRIDE
    return pl.pallas_call(
        functools.partial(_compress_kernel, rows=rows),
        grid=(n_batch, 2),
        in_specs=[
            pl.BlockSpec((rows, CMP_STRIDE, GROUP_COLS), lambda b, j: (b, 0, j)),
            pl.BlockSpec((None, CMP_LEN, 1, GROUP_COLS), lambda b, j: (j, 0, 0, 0)),
            pl.BlockSpec((None, CMP_LEN, GROUP_COLS, NSA_KV_GROUPS * CMP_HIDDEN), lambda b, j: (j, 0, 0, 0)),
            pl.BlockSpec((None, NSA_KV_GROUPS * CMP_HIDDEN, GROUP_COLS), lambda b, j: (j, 0, 0)),
        ],
        out_specs=pl.BlockSpec((None, None, rows, GROUP_COLS), lambda b, j: (b, j, 0, 0)),
        out_shape=jax.ShapeDtypeStruct((n_batch, 2, rows, GROUP_COLS), F32),
        compiler_params=_cparams("parallel", "arbitrary"),
        name="nsa_compress",
    )(kvn.reshape(-1, CMP_STRIDE, kvn.shape[-1]), pe_t, w1_bd, w2_bd)


def compress_weights(phi_pe, phi_w1, phi_w2):
    eye = jnp.eye(NSA_KV_GROUPS, dtype=F32)
    w1 = phi_w1.reshape(2, CMP_LEN, HEAD_DIM, CMP_HIDDEN)
    w1_bd = jnp.einsum('gh,jldc->jlgdhc', eye, w1).reshape(2, CMP_LEN, GROUP_COLS, NSA_KV_GROUPS * CMP_HIDDEN)
    w2_bd = jnp.einsum('gh,jcd->jgchd', eye, phi_w2).reshape(2, NSA_KV_GROUPS * CMP_HIDDEN, GROUP_COLS)
    pe_t = jnp.tile(phi_pe, (1, 1, NSA_KV_GROUPS)).reshape(2, CMP_LEN, 1, GROUP_COLS)
    return pe_t, w1_bd.astype(BF16), w2_bd.astype(BF16)


def _fill_qz(qz_ref, qt_rows, row0, n_heads, tq):
    qz_ref[...] = jnp.zeros(qz_ref.shape, BF16)
    for h in range(n_heads):
        r = row0(h)
        qz_ref[r:r + HEAD_DIM, h * tq:(h + 1) * tq] = (qt_rows(h).astype(F32) * HEAD_DIM ** -0.5).astype(BF16)


def _softmax_probs(score_cols, tq, mk, m_ref, l_ref, p_ref, h, hp=None):
    hp = h if hp is None else hp
    m_all = m_ref[h]
    l_all = l_ref[h]
    alphas, m_news, l_news = [], [], []
    for c0 in range(0, tq, V7X_LANES):
        cols = slice(c0, c0 + V7X_LANES)
        s = score_cols(cols)
        if mk is not None:
            s = jnp.where(mk[:, cols], s, NEG)
        m_old = m_all[:, cols]
        m_new = jnp.maximum(m_old, jnp.max(s, axis=0, keepdims=True))
        alpha = jnp.exp(m_old - m_new)
        p = jnp.exp(s - m_new)
        l_news.append(alpha * l_all[:, cols] + jnp.sum(p, axis=0, keepdims=True))
        p_ref[:, hp * tq + c0:hp * tq + c0 + V7X_LANES] = p.astype(BF16)
        m_news.append(m_new)
        alphas.append(alpha)
    m_ref[h] = jnp.concatenate(m_news, axis=1)
    l_ref[h] = jnp.concatenate(l_news, axis=1)
    return jnp.concatenate(alphas, axis=1)


def _reset_softmax(m_ref, l_ref, acc_ref):
    m_ref[...] = jnp.full(m_ref.shape, NEG, F32)
    l_ref[...] = jnp.zeros(l_ref.shape, F32)
    acc_ref[...] = jnp.zeros(acc_ref.shape, F32)


def _cmp_select_t_kernel(tbl_ref, qt_ref, kc_ref, vct_ref, gt_ref, o_ref, selt_ref, bias_ref, qz_ref, *,
                         tq, n_cmp_pad, n_cmp, n_sel, pos0):
    i = pl.program_id(0)
    b = pl.program_id(1)
    q0 = pos0 + i * tq
    c_idx = lax.broadcasted_iota(jnp.int32, (n_cmp_pad, tq), 0)
    qpos = q0 + lax.broadcasted_iota(jnp.int32, (n_cmp_pad, tq), 1)
    dist = qpos - (c_idx * CMP_STRIDE + CMP_LEN - 1)
    valid = (dist >= 0) & (c_idx < n_cmp)

    @pl.when(b == 0)
    def _():
        bucket = _t5_bucket(dist)
        for h in range(N_HEADS):
            bias_ref[h] = _bias_from_bucket(bucket, tbl_ref, h)

    j_t = lax.broadcasted_iota(jnp.int32, (SEL_PAD, tq), 0)
    q_t = q0 + lax.broadcasted_iota(jnp.int32, (SEL_PAD, tq), 1)
    cur = jnp.right_shift(q_t, 6)
    forced = (j_t == 0) | (j_t == cur) | (j_t == cur - 1)
    avail = j_t <= cur
    j_i = lax.broadcasted_iota(jnp.int32, (SEL_PAD, n_cmp_pad), 0) * SEL_LEN
    c_i = lax.broadcasted_iota(jnp.int32, (SEL_PAD, n_cmp_pad), 1) * CMP_STRIDE
    inc_t = jnp.where((c_i < j_i + SEL_LEN) & (c_i + CMP_LEN > j_i) & (j_i < n_sel * SEL_LEN)
                      & (c_i < n_cmp * CMP_STRIDE), 1.0, 0.0).astype(BF16)
    j_s = lax.broadcasted_iota(jnp.int32, (n_sel, tq), 0)
    kc = kc_ref[...]

    for g in range(NSA_KV_GROUPS):
        _fill_qz(qz_ref, lambda h: qt_ref[(g * NSA_HPG + h) * HEAD_DIM:(g * NSA_HPG + h + 1) * HEAD_DIM, :],
                 lambda h: g * HEAD_DIM, NSA_HPG, tq)
        s_all = _dot(kc, qz_ref[...])
        psum = jnp.zeros((n_cmp_pad, tq), F32)
        outs = []
        for h in range(NSA_HPG):
            hh = g * NSA_HPG + h
            s = s_all[:, h * tq:(h + 1) * tq] + bias_ref[hh]
            s = jnp.where(valid, s, NEG)
            m = jnp.max(s, axis=0, keepdims=True)
            e = jnp.where(valid, jnp.exp(s - m), 0.0)
            p = e / jnp.maximum(jnp.sum(e, axis=0, keepdims=True), 1e-30)
            gate = _sigmoid(gt_ref[hh:hh + 1, :])
            outs.append(_dot(vct_ref[g * HEAD_DIM:(g + 1) * HEAD_DIM, :], p.astype(BF16)) * gate)
            psum = psum + p
        o_ref[:, g * GROUP_COLS:(g + 1) * GROUP_COLS] = jnp.concatenate(outs, axis=0).T
        p_hi, p_lo = _split_bf16(psum, 2)
        imp_t = _dot(inc_t, p_hi) + _dot(inc_t, p_lo)
        score_t = jnp.where(forced, BIG, jnp.where(avail, imp_t, -BIG))

        def pick_step(_, carry):
            remaining, chosen = carry
            top = jnp.max(remaining, axis=0, keepdims=True)
            first = jnp.min(jnp.where(remaining == top, j_s, n_sel), axis=0, keepdims=True)
            pick = j_s == first
            return jnp.where(pick, TAKEN, remaining), jnp.where(pick, 1.0, chosen)

        _, chosen = lax.fori_loop(0, N_SELECT, pick_step, (score_t[:n_sel], jnp.zeros((n_sel, tq), F32)))
        chosen = jnp.concatenate([chosen, jnp.zeros((SEL_PAD - n_sel, tq), F32)], axis=0)
        selt_ref[g * SEL_PAD:(g + 1) * SEL_PAD, :] = chosen.astype(BF16)


def nsa_cmp_select_t(table, q_t, k_cmp, v_cmp_t, gates_t, n_batch, seq, n_cmp, n_sel):
    tq = ATT_TILE
    nq = seq // tq
    m = n_batch * seq
    n_cmp_pad = k_cmp.shape[1]
    kern = functools.partial(_cmp_select_t_kernel, tq=tq, n_cmp_pad=n_cmp_pad, n_cmp=n_cmp, n_sel=n_sel, pos0=0)
    return pl.pallas_call(
        kern,
        grid=(nq, n_batch),
        in_specs=[
            _smem_spec(),
            pl.BlockSpec((D_MODEL, tq), lambda i, b: (0, b * nq + i)),
            pl.BlockSpec((None, n_cmp_pad, GROUP_COLS), lambda i, b: (b, 0, 0)),
            pl.BlockSpec((None, GROUP_COLS, n_cmp_pad), lambda i, b: (b, 0, 0)),
            pl.BlockSpec((3 * N_HEADS, tq), lambda i, b: (0, b * nq + i)),
        ],
        out_specs=[
            pl.BlockSpec((tq, D_MODEL), lambda i, b: (b * nq + i, 0)),
            pl.BlockSpec((NSA_KV_GROUPS * SEL_PAD, tq), lambda i, b: (0, b * nq + i)),
        ],
        out_shape=[
            jax.ShapeDtypeStruct((m, D_MODEL), F32),
            jax.ShapeDtypeStruct((NSA_KV_GROUPS * SEL_PAD, m), BF16),
        ],
        scratch_shapes=[
            pltpu.VMEM((N_HEADS, n_cmp_pad, tq), F32),
            pltpu.VMEM((GROUP_COLS, NSA_HPG * tq), BF16),
        ],
        compiler_params=_cparams("arbitrary", "arbitrary"),
        name="nsa_cmp_select",
    )(table, q_t, k_cmp, v_cmp_t, gates_t)


def _nsa_attn_t_kernel(qt_ref, k_ref, vt_ref, gt_ref, toep_ref, *rest, mode, tile, branch):
    if mode == "sel":
        selt_ref, o_ref, qz_ref, m_ref, l_ref, acc_ref, p_ref = rest
    else:
        o_ref, qz_ref, m_ref, l_ref, acc_ref, p_ref = rest
    i = pl.program_id(1)
    ik = lax.broadcasted_iota(jnp.int32, (tile, tile), 0)
    iq = lax.broadcasted_iota(jnp.int32, (tile, tile), 1)
    causal = iq >= ik
    above = ik > iq

    for g0 in range(0, NSA_KV_GROUPS, GROUPS_PER_PASS):
        groups = range(g0, g0 + GROUPS_PER_PASS)
        for gi, g in enumerate(groups):
            _fill_qz(qz_ref.at[gi],
                     lambda h, g=g: qt_ref[(g * NSA_HPG + h) * HEAD_DIM:(g * NSA_HPG + h + 1) * HEAD_DIM, :],
                     lambda h, g=g: g * HEAD_DIM, NSA_HPG, tile)
        _reset_softmax(m_ref, l_ref, acc_ref)

        def attend(kt, dd, groups=groups):
            r0 = pl.multiple_of(kt * tile, tile)
            k_blk = k_ref[pl.ds(r0, tile), :]
            s_alls = [_dot(k_blk, qz_ref[gi]) for gi in range(GROUPS_PER_PASS)]
            if mode == "sel":
                krow = lax.broadcasted_iota(jnp.int32, (tile, SEL_PAD), 0)
                jcol = lax.broadcasted_iota(jnp.int32, (tile, SEL_PAD), 1)
                expand = jnp.where(jcol == jnp.right_shift(r0 + krow, 6), 1.0, 0.0).astype(BF16)
                mks = [_dot(expand, selt_ref[g * SEL_PAD:(g + 1) * SEL_PAD, :]) > 0.5 for g in groups]
                if dd == 0:
                    mks = [mk & causal for mk in mks]
            else:
                mks = [causal if dd == 0 else (above if dd == 2 else None)] * GROUPS_PER_PASS
            alphas = []
            for gi, g in enumerate(groups):
                for h in range(NSA_HPG):
                    hh = g * NSA_HPG + h

                    def score_cols(cols, gi=gi, h=h, hh=hh):
                        s = s_alls[gi][:, h * tile + cols.start:h * tile + cols.stop]
                        return s + toep_ref[hh, dd, :, cols] if dd < 2 else s

                    alphas.append(_softmax_probs(score_cols, tile, mks[gi], m_ref, l_ref, p_ref.at[gi],
                                                 gi * NSA_HPG + h, h))
            for gi, g in enumerate(groups):
                v_t = vt_ref[g * HEAD_DIM:(g + 1) * HEAD_DIM, pl.ds(r0, tile)]
                pv_all = _dot(v_t, p_ref[gi])
                for h in range(NSA_HPG):
                    a = gi * NSA_HPG + h
                    acc_ref[a] = alphas[a] * acc_ref[a] + pv_all[:, h * tile:(h + 1) * tile]

        attend(i, 0)

        @pl.when(i >= 1)
        def _():
            attend(i - 1, 1)

        if mode == "sel":
            def far_body(kt, carry):
                attend(kt, 2)
                return carry
            lax.fori_loop(0, jnp.maximum(i - 1, 0), far_body, 0)
        else:
            @pl.when(i >= 2)
            def _():
                attend(i - 2, 2)

        for gi, g in enumerate(groups):
            outs = []
            for h in range(NSA_HPG):
                hh = g * NSA_HPG + h
                a = gi * NSA_HPG + h
                gate = _sigmoid(gt_ref[branch * N_HEADS + hh:branch * N_HEADS + hh + 1, :])
                outs.append(acc_ref[a] / jnp.maximum(l_ref[a], 1e-30) * gate)
            o_ref[:, g * GROUP_COLS:(g + 1) * GROUP_COLS] = jnp.concatenate(outs, axis=0).T


def nsa_attn_t(mode, toep, q_t, k_src, k_col, v_t, gates_t, sel_t, n_batch, seq):
    tile = ATT_TILE
    assert WINDOW == 2 * tile
    nq = seq // tile
    m = n_batch * seq
    branch = 1 if mode == "sel" else 2
    kern = functools.partial(_nsa_attn_t_kernel, mode=mode, tile=tile, branch=branch)
    in_specs = [
        pl.BlockSpec((D_MODEL, tile), lambda b, i: (0, b * nq + i)),
        pl.BlockSpec((seq, GROUP_COLS), lambda b, i: (b, k_col)),
        pl.BlockSpec((GROUP_COLS, seq), lambda b, i: (0, b)),
        pl.BlockSpec((3 * N_HEADS, tile), lambda b, i: (0, b * nq + i)),
        pl.BlockSpec((N_HEADS, 2, tile, tile), lambda b, i: (0, 0, 0, 0)),
    ]
    args = [q_t, k_src, v_t, gates_t, toep]
    if mode == "sel":
        in_specs.append(pl.BlockSpec((NSA_KV_GROUPS * SEL_PAD, tile), lambda b, i: (0, b * nq + i)))
        args.append(sel_t)
    return pl.pallas_call(
        kern,
        grid=(n_batch, nq),
        in_specs=in_specs,
        out_specs=pl.BlockSpec((tile, D_MODEL), lambda b, i: (b * nq + i, 0)),
        out_shape=jax.ShapeDtypeStruct((m, D_MODEL), F32),
        scratch_shapes=[
            pltpu.VMEM((GROUPS_PER_PASS, GROUP_COLS, NSA_HPG * tile), BF16),
            pltpu.VMEM((GROUPS_PER_PASS * NSA_HPG, 1, tile), F32),
            pltpu.VMEM((GROUPS_PER_PASS * NSA_HPG, 1, tile), F32),
            pltpu.VMEM((GROUPS_PER_PASS * NSA_HPG, HEAD_DIM, tile), F32),
            pltpu.VMEM((GROUPS_PER_PASS, tile, NSA_HPG * tile), BF16),
        ],
        compiler_params=_cparams("parallel", "arbitrary"),
        name="nsa_attn_" + mode,
    )(*args)


def nsa_prompt_t(table, toep, q_t, kvn, kv_bf, win_bf, vslc_t, vwin_t, gates_t, cw, n_batch, seq):
    pe_t, w1_bd, w2_bd = cw
    n_cmp = (seq - CMP_LEN) // CMP_STRIDE + 1
    n_sel = -(-seq // SEL_LEN)
    assert n_sel <= SEL_PAD and n_sel % 8 == 0
    ctok = compress_prompt(kvn, n_batch, seq, pe_t, w1_bd, w2_bd).astype(BF16)
    k_cmp = ctok[:, 0]
    v_cmp_t = ctok[:, 1].transpose(0, 2, 1)
    o_cmp, sel_t = nsa_cmp_select_t(table, q_t, k_cmp, v_cmp_t, gates_t, n_batch, seq, n_cmp, n_sel)
    o_slc = nsa_attn_t("sel", toep, q_t, kv_bf, 2, vslc_t, gates_t, sel_t, n_batch, seq)
    o_win = nsa_attn_t("win", toep, q_t, win_bf, 0, vwin_t, gates_t, None, n_batch, seq)
    return o_cmp, o_slc, o_win


CUMSUM_CHUNK = 256
HEAD_BLOCK = 4
FOX_BLOCKS_PER_STEP = 2


def _tril_bf16(n):
    r = lax.broadcasted_iota(jnp.int32, (n, n), 0)
    c = lax.broadcasted_iota(jnp.int32, (n, n), 1)
    return jnp.where(r >= c, 1.0, 0.0).astype(BF16)


def _cumsum_rows(x, tri):
    return sum(_dot(tri, part) for part in _split_bf16(x, 3))


def _fox_prep_kernel(f_ref, b_ref, lf_ref, d_ref, *, seq):
    tri = _tril_bf16(CUMSUM_CHUNK)
    carry = jnp.zeros((1, N_HEADS), F32)
    for c in range(seq // CUMSUM_CHUNK):
        rows = slice(c * CUMSUM_CHUNK, (c + 1) * CUMSUM_CHUNK)
        lf = _log_sigmoid(f_ref[rows, :] + b_ref[...])
        lf_ref[rows, :] = lf
        d = carry + _cumsum_rows(lf, tri)
        d_ref[rows, :] = d
        carry = d[CUMSUM_CHUNK - 1:CUMSUM_CHUNK, :]


def fox_prep_prompt(f_raw, b_f, n_batch, seq):
    m = n_batch * seq
    blk = pl.BlockSpec((seq, N_HEADS), lambda b: (b, 0))
    return pl.pallas_call(
        functools.partial(_fox_prep_kernel, seq=seq),
        grid=(n_batch,),
        in_specs=[blk, pl.BlockSpec((1, N_HEADS), lambda b: (0, 0))],
        out_specs=[blk, blk],
        out_shape=[jax.ShapeDtypeStruct((m, N_HEADS), F32)] * 2,
        compiler_params=_cparams("parallel"),
        name="fox_prep",
    )(f_raw, b_f.reshape(1, N_HEADS))


def _fox_attn_t_kernel(qt_ref, k_ref, vt_ref, drow_ref, dcol_ref, o_ref, qz_ref, m_ref, l_ref, acc_ref, p_ref, *,
                       tile):
    i = pl.program_id(2)
    ik = lax.broadcasted_iota(jnp.int32, (tile, tile), 0)
    iq = lax.broadcasted_iota(jnp.int32, (tile, tile), 1)
    causal = iq >= ik
    cols_blk = HEAD_BLOCK * HEAD_DIM
    for bi in range(FOX_BLOCKS_PER_STEP):
        _fill_qz(qz_ref.at[bi], lambda h, bi=bi: qt_ref[(bi * HEAD_BLOCK + h) * HEAD_DIM:(bi * HEAD_BLOCK + h + 1) * HEAD_DIM, :],
                 lambda h: h * HEAD_DIM, HEAD_BLOCK, tile)
    _reset_softmax(m_ref, l_ref, acc_ref)

    def attend(kt, diag):
        r0 = pl.multiple_of(kt * tile, tile)
        k_rows = k_ref[pl.ds(r0, tile), :]
        s_alls = [_dot(k_rows[:, bi * cols_blk:(bi + 1) * cols_blk], qz_ref[bi])
                  for bi in range(FOX_BLOCKS_PER_STEP)]
        alphas = []
        for bi in range(FOX_BLOCKS_PER_STEP):
            for h in range(HEAD_BLOCK):
                a = bi * HEAD_BLOCK + h
                d_keys = dcol_ref[pl.ds(r0, tile), a:a + 1]

                def score_cols(cols, bi=bi, h=h, a=a, d_keys=d_keys):
                    s = s_alls[bi][:, h * tile + cols.start:h * tile + cols.stop]
                    return s + drow_ref[a:a + 1, cols] - d_keys

                alphas.append(_softmax_probs(score_cols, tile, causal if diag else None, m_ref, l_ref,
                                             p_ref.at[bi], a, h))
        for bi in range(FOX_BLOCKS_PER_STEP):
            for h in range(HEAD_BLOCK):
                a = bi * HEAD_BLOCK + h
                v_t = vt_ref[a * HEAD_DIM:(a + 1) * HEAD_DIM, pl.ds(r0, tile)]
                acc_ref[a] = alphas[a] * acc_ref[a] + _dot(v_t, p_ref[bi, :, h * tile:(h + 1) * tile])

    attend(i, True)

    def body(kt, carry):
        attend(kt, False)
        return carry

    lax.fori_loop(0, i, body, 0)
    outs = [acc_ref[a] / jnp.maximum(l_ref[a], 1e-30) for a in range(FOX_BLOCKS_PER_STEP * HEAD_BLOCK)]
    o_ref[...] = jnp.concatenate(outs, axis=0).T


def fox_attn_t(q_t, k_bf, v_t, d_cum, n_batch, seq):
    tile = ATT_TILE
    nq = seq // tile
    m = n_batch * seq
    hps = FOX_BLOCKS_PER_STEP * HEAD_BLOCK
    nhb = N_HEADS // hps
    d4 = d_cum[:m].reshape(n_batch, seq, nhb, hps)
    d_col = d4.transpose(0, 2, 1, 3)
    d_row = d4.transpose(0, 2, 3, 1)
    cols = hps * HEAD_DIM
    return pl.pallas_call(
        functools.partial(_fox_attn_t_kernel, tile=tile),
        grid=(n_batch, nhb, nq),
        in_specs=[
            pl.BlockSpec((cols, tile), lambda b, g, i: (g, b * nq + i)),
            pl.BlockSpec((seq, cols), lambda b, g, i: (b, g)),
            pl.BlockSpec((cols, seq), lambda b, g, i: (g, b)),
            pl.BlockSpec((None, None, hps, tile), lambda b, g, i: (b, g, 0, i)),
            pl.BlockSpec((None, None, seq, hps), lambda b, g, i: (b, g, 0, 0)),
        ],
        out_specs=pl.BlockSpec((tile, cols), lambda b, g, i: (b * nq + i, g)),
        out_shape=jax.ShapeDtypeStruct((m, D_MODEL), F32),
        scratch_shapes=[
            pltpu.VMEM((FOX_BLOCKS_PER_STEP, HEAD_BLOCK * HEAD_DIM, HEAD_BLOCK * tile), BF16),
            pltpu.VMEM((hps, 1, tile), F32),
            pltpu.VMEM((hps, 1, tile), F32),
            pltpu.VMEM((hps, HEAD_DIM, tile), F32),
            pltpu.VMEM((FOX_BLOCKS_PER_STEP, tile, HEAD_BLOCK * tile), BF16),
        ],
        compiler_params=_cparams("parallel", "parallel", "arbitrary"),
        name="fox_attn",
    )(q_t, k_bf, v_t, d_row, d_col)


HT_LANES = V7X_LANES


def _lane_head_onehot(n_heads, per_head):
    r = lax.broadcasted_iota(jnp.int32, (n_heads, n_heads * per_head), 0)
    c = lax.broadcasted_iota(jnp.int32, (n_heads, n_heads * per_head), 1)
    return jnp.where(_div_pow2(c, per_head) == r, 1.0, 0.0).astype(BF16)


def _expand_heads(x, onehot):
    return sum(_dot(part, onehot) for part in _split_bf16(x, 3))


def _diag_blocks_to_rows(acc_t, n_heads, feat, per_head):
    rows, lanes = acc_t.shape
    r = lax.broadcasted_iota(jnp.int32, (rows, lanes), 0)
    c = lax.broadcasted_iota(jnp.int32, (rows, lanes), 1)
    z = jnp.where(_div_pow2(r, feat) == _div_pow2(c, per_head), acc_t, 0.0)
    cc = lax.broadcasted_iota(jnp.int32, (lanes, lanes), 0)
    tt = lax.broadcasted_iota(jnp.int32, (lanes, lanes), 1)
    fold = jnp.where(_mod_pow2(cc, per_head) == tt, 1.0, 0.0).astype(BF16)
    out_t = sum(_dot(part, fold) for part in _split_bf16(z, 3))
    return out_t.T


def block_diag_queries(q_rows, n_seq, per_seq, n_heads):
    q4 = q_rows.reshape(n_seq, per_seq, n_heads, HEAD_DIM).transpose(0, 2, 3, 1)
    eye = jnp.eye(n_heads, dtype=q_rows.dtype)
    q_bd = q4[:, :, :, None, :] * eye[None, :, None, :, None]
    return q_bd.reshape(n_seq, n_heads * HEAD_DIM, n_heads * per_seq).astype(BF16)


PAGES_PER_STEP = 4


def feature_major_pages(pool):
    nd = pool.ndim
    return pool.transpose(0, 1, *range(3, nd), 2).reshape(pool.shape[0], pool.shape[1], -1, pool.shape[2])


def _joint_update(s_list, mk_list, vt_list, m_ref, l_ref, acc_ref):
    s_list = [s if mk is None else jnp.where(mk, s, NEG) for s, mk in zip(s_list, mk_list)]
    m_old = m_ref[0]
    m_new = m_old
    for s in s_list:
        m_new = jnp.maximum(m_new, jnp.max(s, axis=0, keepdims=True))
    alpha = jnp.exp(m_old - m_new)
    l_new = alpha * l_ref[0]
    pv = None
    for s, mk, v_t in zip(s_list, mk_list, vt_list):
        pr = jnp.exp(s - m_new)
        if mk is not None:
            pr = jnp.where(mk, pr, 0.0)
        l_new = l_new + jnp.sum(pr, axis=0, keepdims=True)
        term = _dot(v_t.astype(BF16), pr.astype(BF16))
        pv = term if pv is None else pv + term
    l_ref[0] = l_new
    acc_ref[0] = alpha * acc_ref[0] + pv
    m_ref[0] = m_new


def _fox_decode_fm_kernel(pt_ref, qbd_ref, qbdt_ref, *refs, n_steps, per_seq, pps):
    k_refs = refs[:pps]
    v_refs = refs[pps:2 * pps]
    lf_refs = refs[2 * pps:3 * pps]
    kn_ref, vn_ref, fn_ref, bf_ref, o_ref, lfn_ref, qz_ref, qzt_ref, m_ref, l_ref, acc_ref, carry_ref = refs[3 * pps:]
    p = pl.program_id(1)

    @pl.when(p == 0)
    def _():
        qz_ref[...] = (qbd_ref[...].astype(F32) * HEAD_DIM ** -0.5).astype(BF16)
        qzt_ref[...] = (qbdt_ref[...].astype(F32) * HEAD_DIM ** -0.5).astype(BF16)
        _reset_softmax(m_ref, l_ref, acc_ref)
        carry_ref[...] = jnp.zeros(carry_ref.shape, F32)

    tri = _tril_bf16(PAGE_SIZE)
    onehot = _lane_head_onehot(N_HEADS, per_seq)

    def update(s_t, mk, pv_of):
        if mk is not None:
            s_t = jnp.where(mk, s_t, NEG)
        m_old = m_ref[0]
        m_new = jnp.maximum(m_old, jnp.max(s_t, axis=0, keepdims=True))
        alpha = jnp.exp(m_old - m_new)
        pr = jnp.exp(s_t - m_new)
        if mk is not None:
            pr = jnp.where(mk, pr, 0.0)
        l_ref[0] = alpha * l_ref[0] + jnp.sum(pr, axis=0, keepdims=True)
        acc_ref[0] = alpha * acc_ref[0] + pv_of(pr.astype(BF16))
        m_ref[0] = m_new

    @pl.when(p < n_steps)
    def _():
        local, s_raw = [], []
        for i in range(pps):
            lf_t = lf_refs[i][...]
            lf_exp = sum(lax.dot_general(part, onehot, (((0,), (0,)), ((), ())), preferred_element_type=F32)
                         for part in _split_bf16(lf_t, 3))
            local.append(_cumsum_rows(lf_exp, tri))
            s_raw.append(_dot(qzt_ref[...], k_refs[i][...].astype(BF16)).T)
        carry = carry_ref[...]
        s_list = []
        for i in range(pps):
            d = carry + local[i]
            s_list.append(s_raw[i] - d)
            carry = d[PAGE_SIZE - 1:PAGE_SIZE, :]
        carry_ref[...] = carry
        _joint_update(s_list, [None] * pps, [r[...] for r in v_refs], m_ref, l_ref, acc_ref)

    @pl.when(p == n_steps)
    def _():
        lf_new = _log_sigmoid(fn_ref[...] + bf_ref[...])
        lfn_ref[...] = lf_new
        pad = PAGE_SIZE - per_seq
        k_rows = jnp.concatenate([kn_ref[...], jnp.zeros((pad, D_MODEL), F32)], axis=0).astype(BF16)
        v_rows = jnp.concatenate([vn_ref[...], jnp.zeros((pad, D_MODEL), F32)], axis=0).astype(BF16)
        lf_rows = jnp.concatenate([lf_new, jnp.zeros((pad, N_HEADS), F32)], axis=0)
        d = carry_ref[...] + _cumsum_rows(_expand_heads(lf_rows, onehot), tri)
        srow = lax.broadcasted_iota(jnp.int32, (PAGE_SIZE, HT_LANES), 0)
        tq = _mod_pow2(lax.broadcasted_iota(jnp.int32, (PAGE_SIZE, HT_LANES), 1), per_seq)
        s_t = _dot(k_rows, qz_ref[...]) - d
        update(s_t, srow <= tq, lambda pr: lax.dot_general(v_rows, pr, (((0,), (0,)), ((), ())),
                                                           preferred_element_type=F32))
        o_t = acc_ref[0] / jnp.maximum(l_ref[0], 1e-30)
        o_ref[...] = _diag_blocks_to_rows(o_t, N_HEADS, HEAD_DIM, per_seq)[:per_seq, :]


def fox_decode_fm(page_table, q_bd, kv_pool_t, lf_pool_t, layer, kv_new, f_new, b_f, row_block0, n_seq, per_seq):
    n_pages = page_table.shape[1]
    pps = PAGES_PER_STEP
    assert N_HEADS * per_seq == HT_LANES and n_pages % pps == 0
    n_steps = n_pages // pps
    kern = functools.partial(_fox_decode_fm_kernel, n_steps=n_steps, per_seq=per_seq, pps=pps)

    def page(i):
        return lambda b, p, pt: pt[b, jnp.minimum(p, n_steps - 1) * pps + i]

    def page_spec(rows, blk, i):
        return pl.BlockSpec((None, None, rows, PAGE_SIZE), lambda b, p, pt, i=i: (layer, page(i)(b, p, pt), blk, 0))

    grid_spec = pltpu.PrefetchScalarGridSpec(
        num_scalar_prefetch=1,
        grid=(n_seq, n_steps + 1),
        in_specs=[pl.BlockSpec((None, D_MODEL, HT_LANES), lambda b, p, pt: (b, 0, 0)),
                  pl.BlockSpec((None, HT_LANES, D_MODEL), lambda b, p, pt: (b, 0, 0))]
        + [page_spec(D_MODEL, 0, i) for i in range(pps)]
        + [page_spec(D_MODEL, 1, i) for i in range(pps)]
        + [page_spec(N_HEADS, 0, i) for i in range(pps)]
        + [
            pl.BlockSpec((per_seq, D_MODEL), lambda b, p, pt: (row_block0 + b, 0)),
            pl.BlockSpec((per_seq, D_MODEL), lambda b, p, pt: (row_block0 + b, 1)),
            pl.BlockSpec((per_seq, N_HEADS), lambda b, p, pt: (row_block0 + b, 0)),
            pl.BlockSpec((1, N_HEADS), lambda b, p, pt: (0, 0)),
        ],
        out_specs=[
            pl.BlockSpec((per_seq, D_MODEL), lambda b, p, pt: (b, 0)),
            pl.BlockSpec((per_seq, N_HEADS), lambda b, p, pt: (b, 0)),
        ],
        scratch_shapes=[
            pltpu.VMEM((D_MODEL, HT_LANES), BF16),
            pltpu.VMEM((HT_LANES, D_MODEL), BF16),
            pltpu.VMEM((1, 1, HT_LANES), F32),
            pltpu.VMEM((1, 1, HT_LANES), F32),
            pltpu.VMEM((1, D_MODEL, HT_LANES), F32),
            pltpu.VMEM((1, HT_LANES), F32),
        ],
    )
    return pl.pallas_call(
        kern,
        grid_spec=grid_spec,
        out_shape=[jax.ShapeDtypeStruct((n_seq * per_seq, D_MODEL), F32),
                   jax.ShapeDtypeStruct((n_seq * per_seq, N_HEADS), F32)],
        compiler_params=_cparams("parallel", "arbitrary"),
        name="fox_decode",
    )(page_table, q_bd, q_bd.transpose(0, 2, 1), *([kv_pool_t] * (2 * pps)), *([lf_pool_t] * pps),
      kv_new, kv_new, f_new, b_f.reshape(1, N_HEADS))


DEC_ROW_TILE = 512


def _group_rows_to_heads(acc_t):
    return jnp.concatenate([acc_t[(h // NSA_HPG) * HEAD_DIM:(h // NSA_HPG + 1) * HEAD_DIM] for h in range(N_HEADS)],
                           axis=0)


def group_block_diag_queries(q_rows, n_seq, per_seq):
    q5 = q_rows.reshape(n_seq, per_seq, NSA_KV_GROUPS, NSA_HPG, HEAD_DIM).transpose(0, 2, 4, 3, 1)
    eye = jnp.eye(NSA_KV_GROUPS, dtype=q_rows.dtype)
    q_bd = q5[:, :, :, None, :, :] * eye[None, :, None, :, None, None]
    return q_bd.reshape(n_seq, GROUP_COLS, N_HEADS * per_seq).astype(BF16)


def _nsa_dec_bias_kernel(tbl_ref, o_ref, *, pos0, per_seq):
    onehot = _lane_head_onehot(N_HEADS, per_seq)
    tab = _expand_heads(tbl_ref[...], onehot)
    row = lax.broadcasted_iota(jnp.int32, (PAGE_SIZE, HT_LANES), 0)
    t = _mod_pow2(lax.broadcasted_iota(jnp.int32, (PAGE_SIZE, HT_LANES), 1), per_seq)
    dists = (PAGE_SIZE + t - row, t - row, pos0 + t - (row * CMP_STRIDE + CMP_LEN - 1))
    for n, dist in enumerate(dists):
        bucket = _t5_bucket(dist)
        val = jnp.broadcast_to(tab[0:1, :], (PAGE_SIZE, HT_LANES))
        for k in range(1, N_BUCKETS):
            val = jnp.where(bucket == k, tab[k:k + 1, :], val)
        o_ref[n] = val
    o_ref[3] = jnp.broadcast_to(tab[N_BUCKETS - 1:N_BUCKETS, :], (PAGE_SIZE, HT_LANES))


def nsa_dec_bias(table, pos0, per_seq):
    assert PAGE_SIZE >= T5_SATURATION_DIST
    return pl.pallas_call(
        functools.partial(_nsa_dec_bias_kernel, pos0=pos0, per_seq=per_seq),
        out_shape=jax.ShapeDtypeStruct((4, PAGE_SIZE, HT_LANES), F32),
        name="nsa_dec_bias",
    )(table)


def nsa_dec_compress(xd, pe_t, w1_bd, w2_bd):
    r_total = xd.shape[0]
    rows = min(DEC_ROW_TILE, r_total)
    return pl.pallas_call(
        functools.partial(_compress_kernel, rows=rows),
        grid=(r_total // rows, 2),
        in_specs=[
            pl.BlockSpec((rows, CMP_STRIDE, GROUP_COLS), lambda i, j: (i, 0, j)),
            pl.BlockSpec((None, CMP_LEN, 1, GROUP_COLS), lambda i, j: (j, 0, 0, 0)),
            pl.BlockSpec((None, CMP_LEN, GROUP_COLS, NSA_KV_GROUPS * CMP_HIDDEN), lambda i, j: (j, 0, 0, 0)),
            pl.BlockSpec((None, NSA_KV_GROUPS * CMP_HIDDEN, GROUP_COLS), lambda i, j: (j, 0, 0)),
        ],
        out_specs=pl.BlockSpec((None, rows, GROUP_COLS), lambda i, j: (j, i, 0)),
        out_shape=jax.ShapeDtypeStruct((2, r_total, GROUP_COLS), F32),
        compiler_params=_cparams("parallel", "arbitrary"),
        name="nsa_dec_compress",
    )(xd, pe_t, w1_bd, w2_bd)


def _dec_finish(m_ref, l_ref, acc_ref, gate_row, per_seq):
    o_t = acc_ref[0] / jnp.maximum(l_ref[0], 1e-30) * _sigmoid(gate_row)
    return _diag_blocks_to_rows(_group_rows_to_heads(o_t), N_HEADS, HEAD_DIM, per_seq)[:per_seq, :]


def _nsa_dec_cmp_kernel(qbd_ref, kc_ref, vc_ref, g_ref, bias_ref, o_ref, sel_ref, score_ref, *,
                        n_cmp, n_sel, pos0, per_seq):
    n_c = kc_ref.shape[0]
    c_idx = lax.broadcasted_iota(jnp.int32, (n_c, HT_LANES), 0)
    lane = lax.broadcasted_iota(jnp.int32, (n_c, HT_LANES), 1)
    t = _mod_pow2(lane, per_seq)
    dist = pos0 + t - (c_idx * CMP_STRIDE + CMP_LEN - 1)
    valid = (dist >= 0) & (c_idx < n_cmp)
    qz = (qbd_ref[...].astype(F32) * HEAD_DIM ** -0.5).astype(BF16)
    s = _dot(kc_ref[...], qz) + bias_ref[2]
    s = jnp.where(valid, s, NEG)
    m = jnp.max(s, axis=0, keepdims=True)
    e = jnp.where(valid, jnp.exp(s - m), 0.0)
    p = e / jnp.maximum(jnp.sum(e, axis=0, keepdims=True), 1e-30)
    o_t = lax.dot_general(vc_ref[...], p.astype(BF16), (((0,), (0,)), ((), ())), preferred_element_type=F32)
    o_t = o_t * _sigmoid(g_ref[0])
    o_ref[...] = _diag_blocks_to_rows(_group_rows_to_heads(o_t), N_HEADS, HEAD_DIM, per_seq)[:per_seq, :]
    src = lax.broadcasted_iota(jnp.int32, (HT_LANES, HT_LANES), 0)
    dst = lax.broadcasted_iota(jnp.int32, (HT_LANES, HT_LANES), 1)
    same_t = _mod_pow2(src, per_seq) == _mod_pow2(dst, per_seq)
    g_of_src = _div_pow2(src, per_seq * NSA_HPG)
    n_gt = NSA_KV_GROUPS * per_seq
    head_sum = jnp.where(same_t & (g_of_src == _div_pow2(dst, per_seq)) & (dst < n_gt), 1.0, 0.0).astype(BF16)
    psum = sum(_dot(part, head_sum) for part in _split_bf16(p, 2))
    j_i = lax.broadcasted_iota(jnp.int32, (SEL_PAD, n_c), 0) * SEL_LEN
    c_i = lax.broadcasted_iota(jnp.int32, (SEL_PAD, n_c), 1) * CMP_STRIDE
    inc_t = jnp.where((c_i < j_i + SEL_LEN) & (c_i + CMP_LEN > j_i) & (j_i < n_sel * SEL_LEN)
                      & (c_i < n_cmp * CMP_STRIDE), 1.0, 0.0).astype(BF16)
    imp_t = sum(_dot(inc_t, part) for part in _split_bf16(psum, 2))
    j_t = lax.broadcasted_iota(jnp.int32, (SEL_PAD, HT_LANES), 0)
    q_t = pos0 + _mod_pow2(lax.broadcasted_iota(jnp.int32, (SEL_PAD, HT_LANES), 1), per_seq)
    cur = jnp.right_shift(q_t, 6)
    forced = (j_t == 0) | (j_t == cur) | (j_t == cur - 1)
    avail = j_t <= cur
    score_t = jnp.where(forced, BIG, jnp.where(avail, imp_t, -BIG))
    score_ref[...] = score_t
    n_rank = -(-n_sel // 8) * 8
    sc = score_t[:n_rank]
    j_s = lax.broadcasted_iota(jnp.int32, (n_rank, HT_LANES), 0)

    def rank_step(r, rank):
        row = jnp.broadcast_to(score_ref[pl.ds(r, 1), :], (n_rank, HT_LANES))
        ahead = jnp.where(row > sc, 1.0, jnp.where((row == sc) & (j_s > r), 1.0, 0.0))
        return rank + ahead

    rank = lax.fori_loop(0, n_sel, rank_step, jnp.zeros((n_rank, HT_LANES), F32))
    chosen = jnp.where((rank < float(N_SELECT)) & (j_s < n_sel), 1.0, 0.0)
    chosen = jnp.concatenate([chosen, jnp.zeros((SEL_PAD - n_rank, HT_LANES), F32)], axis=0).astype(BF16)
    spread = jnp.where(same_t & (src == _div_pow2(dst, per_seq * NSA_HPG) * per_seq + _mod_pow2(dst, per_seq)),
                       1.0, 0.0).astype(BF16)
    sel_ref[...] = _dot(chosen, spread).astype(BF16)


def nsa_dec_cmp(q_bd, k_cmp, v_cmp, gate_rows, bias, n_cmp, n_sel, pos0, per_seq):
    n_seq, n_c = k_cmp.shape[:2]
    kern = functools.partial(_nsa_dec_cmp_kernel, n_cmp=n_cmp, n_sel=n_sel, pos0=pos0, per_seq=per_seq)
    return pl.pallas_call(
        kern,
        grid=(n_seq,),
        in_specs=[
            pl.BlockSpec((None, GROUP_COLS, HT_LANES), lambda b: (b, 0, 0)),
            pl.BlockSpec((None, n_c, GROUP_COLS), lambda b: (b, 0, 0)),
            pl.BlockSpec((None, n_c, GROUP_COLS), lambda b: (b, 0, 0)),
            pl.BlockSpec((None, 3, 1, HT_LANES), lambda b: (b, 0, 0, 0)),
            pl.BlockSpec((4, PAGE_SIZE, HT_LANES), lambda b: (0, 0, 0)),
        ],
        out_specs=[
            pl.BlockSpec((per_seq, D_MODEL), lambda b: (b, 0)),
            pl.BlockSpec((None, SEL_PAD, HT_LANES), lambda b: (b, 0, 0)),
        ],
        out_shape=[
            jax.ShapeDtypeStruct((n_seq * per_seq, D_MODEL), F32),
            jax.ShapeDtypeStruct((n_seq, SEL_PAD, HT_LANES), BF16),
        ],
        scratch_shapes=[pltpu.VMEM((SEL_PAD, HT_LANES), F32)],
        compiler_params=_cparams("parallel"),
        name="nsa_dec_cmp",
    )(q_bd, k_cmp, v_cmp, gate_rows, bias)


def _dec_gather_fm_kernel(pt_ref, *refs, pps):
    o_ref = refs[pps]
    for i in range(pps):
        o_ref[i * PAGE_SIZE:(i + 1) * PAGE_SIZE, :] = refs[i][...].T


def nsa_dec_gather_fm(page_table, pool_t, layer):
    n_seq, n_pages = page_table.shape
    pps = PAGES_PER_STEP
    assert n_pages % pps == 0
    grid_spec = pltpu.PrefetchScalarGridSpec(
        num_scalar_prefetch=1,
        grid=(n_seq, n_pages // pps),
        in_specs=[pl.BlockSpec((None, None, 2 * GROUP_COLS, PAGE_SIZE),
                               lambda b, p, pt, i=i: (layer, pt[b, p * pps + i], 0, 0)) for i in range(pps)],
        out_specs=pl.BlockSpec((PAGE_SIZE * pps, 2 * GROUP_COLS), lambda b, p, pt: (b * (n_pages // pps) + p, 0)),
    )
    tokens = pl.pallas_call(
        functools.partial(_dec_gather_fm_kernel, pps=pps),
        grid_spec=grid_spec,
        out_shape=jax.ShapeDtypeStruct((n_seq * n_pages * PAGE_SIZE, 2 * GROUP_COLS), F32),
        compiler_params=_cparams("parallel", "arbitrary"),
        name="nsa_dec_gather",
    )(page_table, *([pool_t] * pps))
    return tokens.reshape(-1, CMP_STRIDE, 2 * GROUP_COLS)


def _dec_update(s_t, mk, pv_of, m_ref, l_ref, acc_ref):
    if mk is not None:
        s_t = jnp.where(mk, s_t, NEG)
    m_old = m_ref[0]
    m_new = jnp.maximum(m_old, jnp.max(s_t, axis=0, keepdims=True))
    alpha = jnp.exp(m_old - m_new)
    pr = jnp.exp(s_t - m_new)
    if mk is not None:
        pr = jnp.where(mk, pr, 0.0)
    l_ref[0] = alpha * l_ref[0] + jnp.sum(pr, axis=0, keepdims=True)
    acc_ref[0] = alpha * acc_ref[0] + pv_of(pr.astype(BF16))
    m_ref[0] = m_new


def _pv_key_major(v_rows):
    return lambda pr: lax.dot_general(v_rows, pr, (((0,), (0,)), ((), ())), preferred_element_type=F32)


def _nsa_dec_sel_fm_kernel(pt_ref, qbd_ref, qbdt_ref, *refs, n_steps, per_seq, pps):
    k_refs = refs[:pps]
    v_refs = refs[pps:2 * pps]
    kn_ref, vn_ref, sel_ref, g_ref, bias_ref, o_ref, qz_ref, qzt_ref, m_ref, l_ref, acc_ref = refs[2 * pps:]
    p = pl.program_id(1)

    @pl.when(p == 0)
    def _():
        qz_ref[...] = (qbd_ref[...].astype(F32) * HEAD_DIM ** -0.5).astype(BF16)
        qzt_ref[...] = (qbdt_ref[...].astype(F32) * HEAD_DIM ** -0.5).astype(BF16)
        _reset_softmax(m_ref, l_ref, acc_ref)

    srow = lax.broadcasted_iota(jnp.int32, (PAGE_SIZE, SEL_PAD), 0)
    jcol = lax.broadcasted_iota(jnp.int32, (PAGE_SIZE, SEL_PAD), 1)
    blocks_per_page = PAGE_SIZE // SEL_LEN

    def member(first_block):
        expand = jnp.where(jcol == first_block + jnp.right_shift(srow, 6), 1.0, 0.0).astype(BF16)
        return _dot(expand, sel_ref[...]) > 0.5

    @pl.when(p < n_steps)
    def _():
        s_list, mk_list = [], []
        for i in range(pps):
            page_no = p * pps + i
            bias = jnp.where(page_no == n_steps * pps - 1, bias_ref[0], bias_ref[3])
            s_list.append(_dot(qzt_ref[...], k_refs[i][...].astype(BF16)).T + bias)
            mk_list.append(member(page_no * blocks_per_page))
        _joint_update(s_list, mk_list, [r[...] for r in v_refs], m_ref, l_ref, acc_ref)

    @pl.when(p == n_steps)
    def _():
        pad = PAGE_SIZE - per_seq
        k_rows = jnp.concatenate([kn_ref[...], jnp.zeros((pad, GROUP_COLS), F32)], axis=0).astype(BF16)
        v_rows = jnp.concatenate([vn_ref[...], jnp.zeros((pad, GROUP_COLS), F32)], axis=0).astype(BF16)
        krow = lax.broadcasted_iota(jnp.int32, (PAGE_SIZE, HT_LANES), 0)
        tq = _mod_pow2(lax.broadcasted_iota(jnp.int32, (PAGE_SIZE, HT_LANES), 1), per_seq)
        mk = member(n_steps * pps * blocks_per_page) & (krow <= tq)
        s_t = _dot(k_rows, qz_ref[...]) + bias_ref[1]
        _dec_update(s_t, mk, _pv_key_major(v_rows), m_ref, l_ref, acc_ref)
        o_ref[...] = _dec_finish(m_ref, l_ref, acc_ref, g_ref[1], per_seq)


def nsa_dec_sel_fm(page_table, q_bd, pool_t, layer, kv_new, sel, gate_rows, bias, row_block0, per_seq):
    n_seq, n_pages = page_table.shape
    pps = PAGES_PER_STEP
    assert n_pages % pps == 0
    n_steps = n_pages // pps
    kern = functools.partial(_nsa_dec_sel_fm_kernel, n_steps=n_steps, per_seq=per_seq, pps=pps)

    def page_spec(blk, i):
        return pl.BlockSpec((None, None, GROUP_COLS, PAGE_SIZE),
                            lambda b, p, pt, i=i: (layer, pt[b, jnp.minimum(p, n_steps - 1) * pps + i], blk, 0))

    grid_spec = pltpu.PrefetchScalarGridSpec(
        num_scalar_prefetch=1,
        grid=(n_seq, n_steps + 1),
        in_specs=[pl.BlockSpec((None, GROUP_COLS, HT_LANES), lambda b, p, pt: (b, 0, 0)),
                  pl.BlockSpec((None, HT_LANES, GROUP_COLS), lambda b, p, pt: (b, 0, 0))]
        + [page_spec(2, i) for i in range(pps)] + [page_spec(3, i) for i in range(pps)]
        + [
            pl.BlockSpec((per_seq, GROUP_COLS), lambda b, p, pt: (row_block0 + b, 2)),
            pl.BlockSpec((per_seq, GROUP_COLS), lambda b, p, pt: (row_block0 + b, 3)),
            pl.BlockSpec((None, SEL_PAD, HT_LANES), lambda b, p, pt: (b, 0, 0)),
            pl.BlockSpec((None, 3, 1, HT_LANES), lambda b, p, pt: (b, 0, 0, 0)),
            pl.BlockSpec((4, PAGE_SIZE, HT_LANES), lambda b, p, pt: (0, 0, 0)),
        ],
        out_specs=pl.BlockSpec((per_seq, D_MODEL), lambda b, p, pt: (b, 0)),
        scratch_shapes=[
            pltpu.VMEM((GROUP_COLS, HT_LANES), BF16),
            pltpu.VMEM((HT_LANES, GROUP_COLS), BF16),
            pltpu.VMEM((1, 1, HT_LANES), F32),
            pltpu.VMEM((1, 1, HT_LANES), F32),
            pltpu.VMEM((1, GROUP_COLS, HT_LANES), F32),
        ],
    )
    return pl.pallas_call(
        kern,
        grid_spec=grid_spec,
        out_shape=jax.ShapeDtypeStruct((n_seq * per_seq, D_MODEL), F32),
        compiler_params=_cparams("parallel", "arbitrary"),
        name="nsa_dec_sel",
    )(page_table, q_bd, q_bd.transpose(0, 2, 1), *([pool_t] * (2 * pps)), kv_new, kv_new, sel, gate_rows, bias)


def _nsa_dec_win_fm_kernel(qbd_ref, qbdt_ref, wt_ref, wn_ref, g_ref, bias_ref, o_ref, ws_ref, m_ref, l_ref, acc_ref, *,
                           per_seq):
    n_past = wt_ref.shape[1]
    n_tiles = n_past // PAGE_SIZE
    qz = (qbd_ref[...].astype(F32) * HEAD_DIM ** -0.5).astype(BF16)
    qzt = (qbdt_ref[...].astype(F32) * HEAD_DIM ** -0.5).astype(BF16)
    _reset_softmax(m_ref, l_ref, acc_ref)
    krow = lax.broadcasted_iota(jnp.int32, (PAGE_SIZE, HT_LANES), 0)
    tq = _mod_pow2(lax.broadcasted_iota(jnp.int32, (PAGE_SIZE, HT_LANES), 1), per_seq)
    s_qm = _dot(qzt, wt_ref[:GROUP_COLS, :].astype(BF16))
    for w in range(n_tiles):
        keys = slice(w * PAGE_SIZE, (w + 1) * PAGE_SIZE)
        mk = (krow > tq) if w == 0 else None
        bias = bias_ref[0] if w == n_tiles - 1 else bias_ref[3]
        v_t = wt_ref[GROUP_COLS:, keys].astype(BF16)
        _dec_update(s_qm[:, keys].T + bias, mk, lambda pr, v_t=v_t: _dot(v_t, pr), m_ref, l_ref, acc_ref)
    pad = PAGE_SIZE - per_seq
    new = jnp.concatenate([wn_ref[...], jnp.zeros((pad, 2 * GROUP_COLS), F32)], axis=0)
    s_t = _dot(new[:, :GROUP_COLS].astype(BF16), qz) + bias_ref[1]
    _dec_update(s_t, krow <= tq, _pv_key_major(new[:, GROUP_COLS:].astype(BF16)), m_ref, l_ref, acc_ref)
    o_ref[...] = _dec_finish(m_ref, l_ref, acc_ref, g_ref[2], per_seq)
    shifted = pltpu.roll(wt_ref[...], n_past - per_seq, 1)
    ws_ref[...] = shifted
    lane = lax.broadcasted_iota(jnp.int32, (2 * GROUP_COLS, PAGE_SIZE), 1)
    new_t = pltpu.roll(new.T, PAGE_SIZE - per_seq, 1)
    ws_ref[:, n_past - PAGE_SIZE:] = jnp.where(lane >= PAGE_SIZE - per_seq, new_t, shifted[:, n_past - PAGE_SIZE:])


def nsa_dec_win_fm(q_bd, win_t, layer, win_new, gate_rows, bias, row_block0, per_seq):
    _, n_seq, n_feat, n_past = win_t.shape
    assert n_past == WINDOW and n_past % PAGE_SIZE == 0 and PAGE_SIZE + 1 >= T5_SATURATION_DIST
    kern = functools.partial(_nsa_dec_win_fm_kernel, per_seq=per_seq)
    return pl.pallas_call(
        kern,
        grid=(n_seq,),
        in_specs=[
            pl.BlockSpec((None, GROUP_COLS, HT_LANES), lambda b: (b, 0, 0)),
            pl.BlockSpec((None, HT_LANES, GROUP_COLS), lambda b: (b, 0, 0)),
            pl.BlockSpec((None, None, n_feat, n_past), lambda b: (layer, b, 0, 0)),
            pl.BlockSpec((per_seq, n_feat), lambda b: (row_block0 + b, 0)),
            pl.BlockSpec((None, 3, 1, HT_LANES), lambda b: (b, 0, 0, 0)),
            pl.BlockSpec((4, PAGE_SIZE, HT_LANES), lambda b: (0, 0, 0)),
        ],
        out_specs=[
            pl.BlockSpec((per_seq, D_MODEL), lambda b: (b, 0)),
            pl.BlockSpec((None, n_feat, n_past), lambda b: (b, 0, 0)),
        ],
        out_shape=[
            jax.ShapeDtypeStruct((n_seq * per_seq, D_MODEL), F32),
            jax.ShapeDtypeStruct((n_seq, n_feat, n_past), F32),
        ],
        scratch_shapes=[
            pltpu.VMEM((1, 1, HT_LANES), F32),
            pltpu.VMEM((1, 1, HT_LANES), F32),
            pltpu.VMEM((1, GROUP_COLS, HT_LANES), F32),
        ],
        compiler_params=_cparams("parallel"),
        name="nsa_dec_win",
    )(q_bd, q_bd.transpose(0, 2, 1), win_t, win_new, gate_rows, bias)


def nsa_decode_fm(table, page_table, pool_t, win_t, layer, q_rows, kv_new, win_new, gates_rows, cw, row_block0, pos0,
                  per_seq):
    n_seq, n_pages = page_table.shape
    pe_t, w1_bd, w2_bd = cw
    past = n_pages * PAGE_SIZE
    s_total = past + per_seq
    n_cmp = (s_total - CMP_LEN) // CMP_STRIDE + 1
    n_sel = -(-s_total // SEL_LEN)
    assert n_cmp == past // CMP_STRIDE - 1 and n_sel == past // SEL_LEN + 1 and per_seq <= CMP_STRIDE
    assert pos0 == past and pos0 % SEL_LEN == 0 and pos0 >= WINDOW and past // CMP_STRIDE == PAGE_SIZE
    bias = nsa_dec_bias(table, pos0, per_seq)
    ctok = nsa_dec_compress(nsa_dec_gather_fm(page_table, pool_t, layer), pe_t, w1_bd, w2_bd).astype(BF16)
    rows_per_seq = past // CMP_STRIDE
    k_cmp = ctok[0].reshape(n_seq, rows_per_seq, GROUP_COLS)
    v_cmp = ctok[1].reshape(n_seq, rows_per_seq, GROUP_COLS)
    q_bd = group_block_diag_queries(q_rows, n_seq, per_seq)
    gate_rows = gates_rows.reshape(n_seq, per_seq, 3, N_HEADS).transpose(0, 2, 3, 1).reshape(n_seq, 3, 1, HT_LANES)
    o_cmp, sel = nsa_dec_cmp(q_bd, k_cmp, v_cmp, gate_rows, bias, n_cmp, n_sel, pos0, per_seq)
    o_slc = nsa_dec_sel_fm(page_table, q_bd, pool_t, layer, kv_new, sel, gate_rows, bias, row_block0, per_seq)
    o_win, win_state_t = nsa_dec_win_fm(q_bd, win_t, layer, win_new, gate_rows, bias, row_block0, per_seq)
    return o_cmp, o_slc, o_win, win_state_t


GLA_SUB = 16
GLA_BLOCK = 512


def _gla_kernel(q_ref, k_ref, v_ref, r_ref, a_ref, wa_ref, ba_ref, ng_ref, s0_ref, o_ref, sfin_ref, s_ref, *,
                n_chunks, chunk):
    t_blk = pl.program_id(2)

    @pl.when(t_blk == 0)
    def _():
        s_ref[...] = s0_ref[...]

    tri = _tril_bf16(chunk)
    it = lax.broadcasted_iota(jnp.int32, (chunk, chunk), 0)
    js = lax.broadcasted_iota(jnp.int32, (chunk, chunk), 1)
    same_sub = jnp.right_shift(it, 4) == jnp.right_shift(js, 4)
    n_sub = chunk // GLA_SUB

    def chunk_step(c, carry):
        r0 = pl.multiple_of(c * chunk, chunk)
        rows = pl.ds(r0, chunk)
        q = q_ref[rows, :] * GLA_DK ** -0.5
        k = k_ref[rows, :]
        v = v_ref[rows, :].astype(BF16)
        g = _log_sigmoid(_dot(a_ref[rows, :].astype(BF16), wa_ref[...]) + ba_ref[...]) / GLA_TAU
        b = _cumsum_rows(g, tri)
        s_old = s_ref[...]
        o = _dot((q * jnp.exp(b)).astype(BF16), s_old.astype(BF16))
        a_mat = jnp.zeros((chunk, chunk), F32)
        for d in range(GLA_SUB):
            if d == 0:
                val = jnp.sum(q * k, axis=-1, keepdims=True)
            else:
                k_sh = pltpu.roll(k, d, 0)
                b_sh = pltpu.roll(b, d, 0)
                val = jnp.sum(q * k_sh * jnp.exp(jnp.minimum(b - b_sh, 0.0)), axis=-1, keepdims=True)
            a_mat = a_mat + jnp.where((it - js == d) & same_sub, val, 0.0)
        if n_sub > 1:
            blocks = [jnp.zeros((GLA_SUB, chunk), F32)]
            for a in range(1, n_sub):
                ref_b = b[a * GLA_SUB - 1:a * GLA_SUB, :]
                qa = q[a * GLA_SUB:(a + 1) * GLA_SUB] * jnp.exp(b[a * GLA_SUB:(a + 1) * GLA_SUB] - ref_b)
                kk = k * jnp.exp(jnp.minimum(ref_b - b, 0.0))
                blocks.append(_dot_nt(qa.astype(BF16), kk.astype(BF16)))
            off = jnp.concatenate(blocks, axis=0)
            a_mat = a_mat + jnp.where(jnp.right_shift(js, 4) < jnp.right_shift(it, 4), off, 0.0)
        o = o + _dot(a_mat.astype(BF16), v)
        b_last = b[chunk - 1:chunk, :]
        kd = k * jnp.exp(b_last - b)
        pad_k = jnp.zeros((GLA_DK - chunk, GLA_DK), F32)
        pad_v = jnp.zeros((GLA_DK - chunk, GLA_DV), BF16)
        kd_t = jnp.concatenate([kd, pad_k], axis=0).T
        dec = jnp.broadcast_to(jnp.exp(b_last), (GLA_DK, GLA_DK)).T
        s_ref[...] = (s_old * jnp.concatenate([dec] * (GLA_DV // GLA_DK), axis=1)
                      + _dot(kd_t.astype(BF16), jnp.concatenate([v, pad_v], axis=0)))
        y = o * lax.rsqrt(jnp.mean(o * o, axis=-1, keepdims=True) + EPS) * ng_ref[...]
        rr = r_ref[rows, :]
        o_ref[rows, :] = y * (rr * _sigmoid(rr))
        return carry

    lax.fori_loop(0, n_chunks, chunk_step, 0)

    @pl.when(t_blk == pl.num_programs(2) - 1)
    def _():
        sfin_ref[...] = s_ref[...]


def gla_prompt(q, k, v, r, a_lr, w_a2, b_a, norm_g, s0, n_batch, seq, row_block0=0):
    chunk = GLA_CHUNK if seq % GLA_CHUNK == 0 else seq
    blk = min(GLA_BLOCK, seq)
    nt = seq // blk
    m = n_batch * seq
    kern = functools.partial(_gla_kernel, n_chunks=blk // chunk, chunk=chunk)
    row = lambda b, h, t: row_block0 + b * nt + t
    return pl.pallas_call(
        kern,
        grid=(n_batch, GLA_HEADS, nt),
        in_specs=[
            pl.BlockSpec((blk, GLA_DK), lambda b, h, t: (row(b, h, t), h)),
            pl.BlockSpec((blk, GLA_DK), lambda b, h, t: (row(b, h, t), h)),
            pl.BlockSpec((blk, GLA_DV), lambda b, h, t: (row(b, h, t), h)),
            pl.BlockSpec((blk, GLA_DV), lambda b, h, t: (row(b, h, t), h)),
            pl.BlockSpec((blk, GLA_RANK), lambda b, h, t: (row(b, h, t), 0)),
            pl.BlockSpec((GLA_RANK, GLA_DK), lambda b, h, t: (0, h)),
            pl.BlockSpec((1, GLA_DK), lambda b, h, t: (0, h)),
            pl.BlockSpec((1, GLA_DV), lambda b, h, t: (0, 0)),
            pl.BlockSpec((None, None, GLA_DK, GLA_DV), lambda b, h, t: (b, h, 0, 0)),
        ],
        out_specs=[
            pl.BlockSpec((blk, GLA_DV), lambda b, h, t: (b * nt + t, h)),
            pl.BlockSpec((None, None, GLA_DK, GLA_DV), lambda b, h, t: (b, h, 0, 0)),
        ],
        out_shape=[
            jax.ShapeDtypeStruct((m, D_MODEL), F32),
            jax.ShapeDtypeStruct((n_batch, GLA_HEADS, GLA_DK, GLA_DV), F32),
        ],
        scratch_shapes=[pltpu.VMEM((GLA_DK, GLA_DV), F32)],
        compiler_params=_cparams("parallel", "parallel", "arbitrary"),
        name="gla",
    )(q, k, v, r, a_lr, w_a2.astype(BF16), b_a.reshape(1, GLA_QK), norm_g.reshape(1, GLA_DV), s0)


def _gla_decode_kernel(q_ref, k_ref, v_ref, r_ref, a_ref, wa_ref, ba_ref, ng_ref, s0_ref, o_ref, s_ref, *, n_rows):
    pad_rows = 2 * n_rows

    def pad_to(x, rows):
        return jnp.concatenate([x, jnp.zeros((rows - x.shape[0], x.shape[1]), x.dtype)], axis=0)

    row = lax.broadcasted_iota(jnp.int32, (n_rows, GLA_DK), 0)
    row_v = lax.broadcasted_iota(jnp.int32, (n_rows, GLA_DV), 0)
    q = q_ref[...] * GLA_DK ** -0.5
    k = k_ref[...]
    v = v_ref[...]
    g = _dot(pad_to(a_ref[...], pad_rows).astype(BF16), wa_ref[...])[:n_rows]
    g = _log_sigmoid(g + ba_ref[...]) / GLA_TAU
    b = g
    shift = 1
    while shift < n_rows:
        b = b + jnp.where(row >= shift, pltpu.roll(b, shift, 0), 0.0)
        shift *= 2
    s_old = s0_ref[...]
    o = _dot(pad_to(q * jnp.exp(b), pad_rows).astype(BF16), s_old.astype(BF16))[:n_rows]
    for d in range(n_rows):
        if d == 0:
            val = jnp.sum(q * k, axis=-1, keepdims=True)
            o = o + val * v
        else:
            k_sh = pltpu.roll(k, d, 0)
            b_sh = pltpu.roll(b, d, 0)
            val = jnp.sum(q * k_sh * jnp.exp(jnp.minimum(b - b_sh, 0.0)), axis=-1, keepdims=True)
            o = o + jnp.where(row_v >= d, val * pltpu.roll(v, d, 0), 0.0)
    b_last = b[n_rows - 1:n_rows, :]
    kd_t = pad_to(k * jnp.exp(b_last - b), GLA_DK).T
    dec = jnp.broadcast_to(jnp.exp(b_last), (GLA_DK, GLA_DK)).T
    s_ref[...] = (s_old * jnp.concatenate([dec] * (GLA_DV // GLA_DK), axis=1)
                  + _dot(kd_t.astype(BF16), pad_to(v, GLA_DK).astype(BF16)))
    y = o * lax.rsqrt(jnp.mean(o * o, axis=-1, keepdims=True) + EPS) * ng_ref[...]
    rr = r_ref[...]
    o_ref[...] = y * (rr * _sigmoid(rr))


def gla_decode(q, k, v, r, a_lr, w_a2, b_a, norm_g, s0, n_seq, per_seq, row_block0):
    assert per_seq % GLA_CHUNK != 0 and per_seq % 8 == 0
    kern = functools.partial(_gla_decode_kernel, n_rows=per_seq)
    row = lambda b, h: (row_block0 + b, h)
    return pl.pallas_call(
        kern,
        grid=(n_seq, GLA_HEADS),
        in_specs=[
            pl.BlockSpec((per_seq, GLA_DK), row),
            pl.BlockSpec((per_seq, GLA_DK), row),
            pl.BlockSpec((per_seq, GLA_DV), row),
            pl.BlockSpec((per_seq, GLA_DV), row),
            pl.BlockSpec((per_seq, GLA_RANK), lambda b, h: (row_block0 + b, 0)),
            pl.BlockSpec((GLA_RANK, GLA_DK), lambda b, h: (0, h)),
            pl.BlockSpec((1, GLA_DK), lambda b, h: (0, h)),
            pl.BlockSpec((1, GLA_DV), lambda b, h: (0, 0)),
            pl.BlockSpec((None, None, GLA_DK, GLA_DV), lambda b, h: (b, h, 0, 0)),
        ],
        out_specs=[
            pl.BlockSpec((per_seq, GLA_DV), lambda b, h: (b, h)),
            pl.BlockSpec((None, None, GLA_DK, GLA_DV), lambda b, h: (b, h, 0, 0)),
        ],
        out_shape=[
            jax.ShapeDtypeStruct((n_seq * per_seq, D_MODEL), F32),
            jax.ShapeDtypeStruct((n_seq, GLA_HEADS, GLA_DK, GLA_DV), F32),
        ],
        compiler_params=_cparams("parallel", "parallel"),
        name="gla_decode",
    )(q, k, v, r, a_lr, w_a2.astype(BF16), b_a.reshape(1, GLA_QK), norm_g.reshape(1, GLA_DV), s0)


_C1 = D_MODEL + NSA_KV_COLS
_C2 = _C1 + NSA_WIN_COLS
NSA_SPLITS = ((D_MODEL, _C1, F32), (D_MODEL, _C1, BF16), (_C1, _C2, F32), (_C1, _C2, BF16))
NSA_SPLITS_T = ((0, D_MODEL, BF16), (_C1 - GROUP_COLS, _C1, BF16), (_C2 - GROUP_COLS, _C2, BF16),
                (_C2, _C2 + 3 * N_HEADS, F32))
FOX_SPLITS = ((D_MODEL, 3 * D_MODEL, F32), (D_MODEL, 2 * D_MODEL, BF16), (3 * D_MODEL, 3 * D_MODEL + N_HEADS, F32))
FOX_SPLITS_T = ((0, D_MODEL, BF16), (2 * D_MODEL, 3 * D_MODEL, BF16))
GLA_SPLITS = ((0, GLA_QK, F32), (GLA_QK, 2 * GLA_QK, F32), (2 * GLA_QK, 2 * GLA_QK + D_MODEL, F32),
              (2 * GLA_QK + D_MODEL, 2 * GLA_QK + 2 * D_MODEL, F32),
              (2 * GLA_QK + 2 * D_MODEL, 2 * GLA_QK + 2 * D_MODEL + GLA_RANK, F32))


def kernel(x_prompt, x_sample, cache_nsa_kv, state_nsa_win, cache_fox_kv, cache_fox_logf, state_gla, page_table, rel_bias, norm_g, final_g, nsa_w_in, nsa_phi_pe, nsa_phi_w1, nsa_phi_w2, nsa_w_out, fox_w_in, fox_b_f, fox_w_out, gla_w_in, gla_w_a2, gla_b_a, gla_norm_g, gla_w_out, mlp_w1, mlp_w2):
    bp, tp, _ = x_prompt.shape
    bs, ts, _ = x_sample.shape
    mp, ms = bp * tp, bs * ts
    pos_s = page_table.shape[1] * PAGE_SIZE
    x = jnp.concatenate([x_prompt.reshape(mp, D_MODEL), x_sample.reshape(ms, D_MODEL)], axis=0)
    toep = t5_tiles(rel_bias, ATT_TILE)
    nsa_pool_t = feature_major_pages(cache_nsa_kv)
    nsa_win_t = feature_major_pages(state_nsa_win)
    fox_pool_t = feature_major_pages(cache_fox_kv)
    fox_lf_t = feature_major_pages(cache_fox_logf)

    def both(a, shape_p, shape_s):
        return a[:mp].reshape(shape_p), a[mp:].reshape(shape_s)

    nsa_kv_p, nsa_kv_s, nsa_win_p, nsa_win_s = [], [], [], []
    fox_kv_p, fox_kv_s, fox_lf_p, fox_lf_s, gla_p, gla_s = [], [], [], [], [], []
    G = NSA_KV_GROUPS
    for i in range(DEPTH):
        kind, li = i % N_MIXERS, i // N_MIXERS
        if kind == 0:
            kvn, kv_bf, winn, win_bf, q_t, vslc_t, vwin_t, gates_t = norm_proj(
                x, norm_g[i, 0], nsa_w_in[li], NSA_SPLITS, NSA_SPLITS_T)
            cw = compress_weights(nsa_phi_pe[li], nsa_phi_w1[li], nsa_phi_w2[li])
            o_parts_p = nsa_prompt_t(rel_bias, toep, q_t, kvn, kv_bf, win_bf, vslc_t, vwin_t, gates_t, cw, bp, tp)
            kv_p, kv_s = both(kvn, (bp, tp, 4, G, HEAD_DIM), (bs, ts, 4, G, HEAD_DIM))
            w_p, w_s = both(winn, (bp, tp, 2, G, HEAD_DIM), (bs, ts, 2, G, HEAD_DIM))
            *o_parts_s, ws_t = nsa_decode_fm(rel_bias, page_table, nsa_pool_t, nsa_win_t, li, q_t[:, mp:].T, kvn, winn,
                                             gates_t[:, mp:].T, cw, mp // ts, pos_s, ts)
            ws_p = w_p[:, tp - min(WINDOW, tp):]
            ws_s = ws_t.reshape(bs, 2, G, HEAD_DIM, -1).transpose(0, 4, 1, 2, 3)
            nsa_kv_p.append(kv_p); nsa_kv_s.append(kv_s); nsa_win_p.append(ws_p); nsa_win_s.append(ws_s)
            o_parts_p = list(o_parts_p)
            w_out = nsa_w_out[li]
        elif kind == 1:
            kvn, k_bf, fr, q_t, v_t = norm_proj(x, norm_g[i, 0], fox_w_in[li], FOX_SPLITS, FOX_SPLITS_T)
            kv_p, kv_s = both(kvn, (bp, tp, 2, N_HEADS, HEAD_DIM), (bs, ts, 2, N_HEADS, HEAD_DIM))
            lf_all, d_all = fox_prep_prompt(fr, fox_b_f[li], bp, tp)
            o_p = fox_attn_t(q_t, k_bf, v_t, d_all, bp, tp)
            lf_p = lf_all.reshape(bp, tp, N_HEADS)
            q_bd = block_diag_queries(q_t[:, mp:].T, bs, ts, N_HEADS)
            o_s, lf_s = fox_decode_fm(page_table, q_bd, fox_pool_t, fox_lf_t, li, kvn, fr, fox_b_f[li], mp // ts,
                                      bs, ts)
            lf_s = lf_s.reshape(bs, ts, N_HEADS)
            fox_kv_p.append(kv_p); fox_kv_s.append(kv_s); fox_lf_p.append(lf_p); fox_lf_s.append(lf_s)
            o_parts_p, o_parts_s = [o_p], [o_s]
            w_out = fox_w_out[li]
        else:
            q, k, v, r, a = norm_proj(x, norm_g[i, 0], gla_w_in[li], GLA_SPLITS)
            s_zero = jnp.zeros((bp, GLA_HEADS, GLA_DK, GLA_DV), F32)
            o_p, s_p = gla_prompt(q, k, v, r, a, gla_w_a2[li], gla_b_a[li], gla_norm_g[li], s_zero, bp, tp)
            o_s, s_s = gla_decode(q, k, v, r, a, gla_w_a2[li], gla_b_a[li], gla_norm_g[li], state_gla[li], bs, ts,
                                  mp // ts)
            gla_p.append(s_p); gla_s.append(s_s)
            o_parts_p, o_parts_s = [o_p], [o_s]
            w_out = gla_w_out[li]
        x = out_mlp(x, o_parts_p, o_parts_s, w_out.astype(BF16), norm_g[i, 1], mlp_w1[i].astype(BF16),
                    mlp_w2[i].astype(BF16))
    y = final_norm(x, final_g)
    y_p, y_s = both(y, (bp, tp, D_MODEL), (bs, ts, D_MODEL))
    return (y_p, y_s, jnp.stack(nsa_kv_p), jnp.stack(nsa_kv_s), jnp.stack(nsa_win_p), jnp.stack(nsa_win_s),
            jnp.stack(fox_kv_p), jnp.stack(fox_kv_s), jnp.stack(fox_lf_p), jnp.stack(fox_lf_s),
            jnp.stack(gla_p), jnp.stack(gla_s))
```

```python
import functools
import math

import jax
import jax.numpy as jnp
from jax import lax
from jax.experimental import pallas as pl
from jax.experimental.pallas import tpu as pltpu

D_MODEL = 1024
DEPTH = 4
PAGE_SIZE = 128
N_MIXERS = 3
N_HEADS = 16
HEAD_DIM = D_MODEL // N_HEADS
NSA_KV_GROUPS = 4
NSA_HPG = N_HEADS // NSA_KV_GROUPS
CMP_LEN = 32
CMP_STRIDE = 16
CMP_HIDDEN = 2 * HEAD_DIM
SEL_LEN = 64
N_SELECT = 16
WINDOW = 512
N_BUCKETS = 32
MAX_DISTANCE = 128
GLA_HEADS = 4
GLA_DK = D_MODEL // 2 // GLA_HEADS
GLA_DV = D_MODEL // GLA_HEADS
GLA_QK = GLA_HEADS * GLA_DK
GLA_RANK = 16
GLA_TAU = 16.0
GLA_CHUNK = 64
D_FF = 4 * D_MODEL
EPS = 1e-6
NEG = -1e30
BIG = 1e9
TAKEN = -3e38
NSA_KV_COLS = 4 * NSA_KV_GROUPS * HEAD_DIM
NSA_WIN_COLS = 2 * NSA_KV_GROUPS * HEAD_DIM
GROUP_COLS = NSA_KV_GROUPS * HEAD_DIM

F32 = jnp.float32
BF16 = jnp.bfloat16

V7X_VMEM_LIMIT_BYTES = 48 * 1024 * 1024
V7X_LANES = 128
TOKEN_TILE = 512
FF_TILE = 1024
ATT_TILE = 256
GROUPS_PER_PASS = 4
SEL_PAD = V7X_LANES
T5_SATURATION_DIST = 113


def _cparams(*sem):
    return pltpu.CompilerParams(dimension_semantics=sem, vmem_limit_bytes=V7X_VMEM_LIMIT_BYTES)


def _rms(x, g):
    return x * lax.rsqrt(jnp.mean(x * x, axis=-1, keepdims=True) + EPS) * g


def _sigmoid(x):
    return 1.0 / (1.0 + jnp.exp(-x))


def _log_sigmoid(x):
    return jnp.minimum(x, 0.0) - jnp.log(1.0 + jnp.exp(-jnp.abs(x)))


def _gelu_tanh(x):
    return 0.5 * x * (1.0 + jnp.tanh(math.sqrt(2.0 / math.pi) * (x + 0.044715 * (x * x * x))))


def _dot(a, b):
    return jnp.dot(a, b, preferred_element_type=F32)


def _dot_nt(a, b):
    return lax.dot_general(a, b, (((1,), (1,)), ((), ())), preferred_element_type=F32)


def _split_bf16(x, parts):
    out = []
    for _ in range(parts):
        hi = x.astype(BF16)
        out.append(hi)
        x = x - hi.astype(F32)
    return out


def _t5_bucket(dist):
    n = jnp.maximum(dist, 0)
    max_exact = N_BUCKETS // 2
    nf = jnp.maximum(n, 1).astype(F32)
    log_part = max_exact + (jnp.log(nf / max_exact) / math.log(MAX_DISTANCE / max_exact)
                            * (N_BUCKETS - max_exact)).astype(jnp.int32)
    return jnp.where(n < max_exact, n, jnp.minimum(log_part, N_BUCKETS - 1))


def _bias_from_bucket(bucket, tbl_ref, h):
    val = jnp.full(bucket.shape, tbl_ref[0, h], F32)
    for k in range(1, N_BUCKETS):
        val = jnp.where(bucket == k, tbl_ref[k, h], val)
    return val


def _div_pow2(x, d):
    assert d & (d - 1) == 0
    return jnp.right_shift(x, d.bit_length() - 1)


def _mod_pow2(x, d):
    assert d & (d - 1) == 0
    return jnp.bitwise_and(x, d - 1)


def _smem_spec():
    return pl.BlockSpec(memory_space=pltpu.SMEM)


def _norm_proj_kernel(x_ref, g_ref, w_ref, wt_ref, *out_refs, splits, splits_t):
    h = _rms(x_ref[...], g_ref[...]).astype(BF16)
    for (lo, hi, _), o_ref in zip(splits, out_refs):
        o_ref[...] = _dot(h, w_ref[:, lo:hi]).astype(o_ref.dtype)
    for (lo, hi, _), o_ref in zip(splits_t, out_refs[len(splits):]):
        o_ref[...] = _dot_nt(wt_ref[lo:hi, :], h).astype(o_ref.dtype)


def norm_proj(x, g, w, splits, splits_t=()):
    m = x.shape[0]
    n = w.shape[1]
    w_bf = w.astype(BF16)
    kern = functools.partial(_norm_proj_kernel, splits=splits, splits_t=splits_t)
    return pl.pallas_call(
        kern,
        grid=(m // TOKEN_TILE,),
        in_specs=[
            pl.BlockSpec((TOKEN_TILE, D_MODEL), lambda i: (i, 0)),
            pl.BlockSpec((1, D_MODEL), lambda i: (0, 0)),
            pl.BlockSpec((D_MODEL, n), lambda i: (0, 0)),
            pl.BlockSpec((n, D_MODEL), lambda i: (0, 0)),
        ],
        out_specs=[pl.BlockSpec((TOKEN_TILE, hi - lo), lambda i: (i, 0)) for lo, hi, _ in splits]
        + [pl.BlockSpec((hi - lo, TOKEN_TILE), lambda i: (0, i)) for lo, hi, _ in splits_t],
        out_shape=[jax.ShapeDtypeStruct((m, hi - lo), dt) for lo, hi, dt in splits]
        + [jax.ShapeDtypeStruct((hi - lo, m), dt) for lo, hi, dt in splits_t],
        compiler_params=_cparams("parallel"),
        name="norm_proj",
    )(x, g.reshape(1, D_MODEL), w_bf, w_bf.T)


def _out_mlp_kernel(x_ref, *refs, n_o, n_first):
    oa_refs = refs[:n_o]
    ob_refs = refs[n_o:2 * n_o]
    wo_ref, g_ref, w1_ref, w2_ref, y_ref, x1_ref, h_ref, acc_ref = refs[2 * n_o:]
    i = pl.program_id(0)
    j = pl.program_id(1)

    @pl.when(j == 0)
    def _():
        first = i < n_first
        o = jnp.where(first, oa_refs[0][...], ob_refs[0][...])
        for ra, rb in zip(oa_refs[1:], ob_refs[1:]):
            o = o + jnp.where(first, ra[...], rb[...])
        x1 = x_ref[...] + _dot(o.astype(BF16), wo_ref[...])
        x1_ref[...] = x1
        h_ref[...] = _rms(x1, g_ref[...]).astype(BF16)
        acc_ref[...] = jnp.zeros_like(acc_ref)

    a = _dot(h_ref[...], w1_ref[...])
    a = jnp.square(jnp.maximum(a, 0.0)).astype(BF16)
    acc_ref[...] += _dot(a, w2_ref[...])

    @pl.when(j == pl.num_programs(1) - 1)
    def _():
        y_ref[...] = x1_ref[...] + acc_ref[...]


def out_mlp(x, o_first, o_rest, w_out, g, w1, w2):
    m = x.shape[0]
    n_o = len(o_first)
    n_first = o_first[0].shape[0] // TOKEN_TILE
    n_rest = o_rest[0].shape[0] // TOKEN_TILE
    assert o_first[0].shape[0] % TOKEN_TILE == 0 and n_first + n_rest == m // TOKEN_TILE
    kern = functools.partial(_out_mlp_kernel, n_o=n_o, n_first=n_first)
    tok = pl.BlockSpec((TOKEN_TILE, D_MODEL), lambda i, j: (i, 0))
    tok_a = pl.BlockSpec((TOKEN_TILE, D_MODEL), lambda i, j: (jnp.minimum(i, n_first - 1), 0))
    tok_b = pl.BlockSpec((TOKEN_TILE, D_MODEL), lambda i, j: (jnp.maximum(i - n_first, 0), 0))
    return pl.pallas_call(
        kern,
        grid=(m // TOKEN_TILE, D_FF // FF_TILE),
        in_specs=[tok] + [tok_a] * n_o + [tok_b] * n_o + [
            pl.BlockSpec((D_MODEL, D_MODEL), lambda i, j: (0, 0)),
            pl.BlockSpec((1, D_MODEL), lambda i, j: (0, 0)),
            pl.BlockSpec((D_MODEL, FF_TILE), lambda i, j: (0, j)),
            pl.BlockSpec((FF_TILE, D_MODEL), lambda i, j: (j, 0)),
        ],
        out_specs=tok,
        out_shape=jax.ShapeDtypeStruct((m, D_MODEL), F32),
        scratch_shapes=[
            pltpu.VMEM((TOKEN_TILE, D_MODEL), F32),
            pltpu.VMEM((TOKEN_TILE, D_MODEL), BF16),
            pltpu.VMEM((TOKEN_TILE, D_MODEL), F32),
        ],
        compiler_params=_cparams("parallel", "arbitrary"),
        name="out_mlp",
    )(x, *o_first, *o_rest, w_out, g.reshape(1, D_MODEL), w1, w2)


def _final_norm_kernel(x_ref, g_ref, y_ref):
    y_ref[...] = _rms(x_ref[...], g_ref[...])


def final_norm(x, g):
    m = x.shape[0]
    return pl.pallas_call(
        _final_norm_kernel,
        grid=(m // TOKEN_TILE,),
        in_specs=[pl.BlockSpec((TOKEN_TILE, D_MODEL), lambda i: (i, 0)),
                  pl.BlockSpec((1, D_MODEL), lambda i: (0, 0))],
        out_specs=pl.BlockSpec((TOKEN_TILE, D_MODEL), lambda i: (i, 0)),
        out_shape=jax.ShapeDtypeStruct((m, D_MODEL), F32),
        compiler_params=_cparams("parallel"),
        name="final_norm",
    )(x, g.reshape(1, D_MODEL))


def _t5_tiles_kernel(tbl_ref, o_ref, *, tile):
    h = pl.program_id(0)
    ik = lax.broadcasted_iota(jnp.int32, (tile, tile), 0)
    iq = lax.broadcasted_iota(jnp.int32, (tile, tile), 1)
    for dd in range(2):
        o_ref[dd] = _bias_from_bucket(_t5_bucket(dd * tile + iq - ik), tbl_ref, h) - tbl_ref[N_BUCKETS - 1, h]


def t5_tiles(table, tile):
    assert 2 * tile - (tile - 1) >= T5_SATURATION_DIST
    return pl.pallas_call(
        functools.partial(_t5_tiles_kernel, tile=tile),
        grid=(N_HEADS,),
        in_specs=[_smem_spec()],
        out_specs=pl.BlockSpec((None, 2, tile, tile), lambda h: (h, 0, 0, 0)),
        out_shape=jax.ShapeDtypeStruct((N_HEADS, 2, tile, tile), F32),
        compiler_params=_cparams("arbitrary"),
        name="t5_tiles",
    )(table)


def _compress_kernel(x_ref, pe_ref, w1_ref, w2_ref, o_ref, *, rows, l_major=False):
    half = CMP_LEN // 2
    top = jnp.zeros((rows, NSA_KV_GROUPS * CMP_HIDDEN), F32)
    bot = jnp.zeros((rows, NSA_KV_GROUPS * CMP_HIDDEN), F32)
    for l in range(half):
        a = x_ref[l] if l_major else x_ref[:, l, :]
        top = top + _dot((a + pe_ref[l]).astype(BF16), w1_ref[l])
        bot = bot + _dot((a + pe_ref[half + l]).astype(BF16), w1_ref[half + l])
    hid = top + pltpu.roll(bot, rows - 1, 0)
    o_ref[...] = _dot(_gelu_tanh(hid).astype(BF16), w2_ref[...])


def compress_prompt(kvn, n_batch, seq, pe_t, w1_bd, w2_bd):
    rows = seq // CMP_STRIDE
    return pl.pallas_call(
        functools.partial(_compress_kernel, rows=rows),
        grid=(n_batch, 2),
        in_specs=[
            pl.BlockSpec((rows, CMP_STRIDE, GROUP_COLS), lambda b, j: (b, 0, j)),
            pl.BlockSpec((None, CMP_LEN, 1, GROUP_COLS), lambda b, j: (j, 0, 0, 0)),
            pl.BlockSpec((None, CMP_LEN, GROUP_COLS, NSA_KV_GROUPS * CMP_HIDDEN), lambda b, j: (j, 0, 0, 0)),
            pl.BlockSpec((None, NSA_KV_GROUPS * CMP_HIDDEN, GROUP_COLS), lambda b, j: (j, 0, 0)),
        ],
        out_specs=pl.BlockSpec((None, None, rows, GROUP_COLS), lambda b, j: (b, j, 0, 0)),
        out_shape=jax.ShapeDtypeStruct((n_batch, 2, rows, GROUP_COLS), F32),
        compiler_params=_cparams("parallel", "arbitrary"),
        name="nsa_compress",
    )(kvn.reshape(-1, CMP_STRIDE, kvn.shape[-1]), pe_t, w1_bd, w2_bd)


def compress_weights(phi_pe, phi_w1, phi_w2):
    eye = jnp.eye(NSA_KV_GROUPS, dtype=F32)
    w1 = phi_w1.reshape(2, CMP_LEN, HEAD_DIM, CMP_HIDDEN)
    w1_bd = jnp.einsum('gh,jldc->jlgdhc', eye, w1).reshape(2, CMP_LEN, GROUP_COLS, NSA_KV_GROUPS * CMP_HIDDEN)
    w2_bd = jnp.einsum('gh,jcd->jgchd', eye, phi_w2).reshape(2, NSA_KV_GROUPS * CMP_HIDDEN, GROUP_COLS)
    pe_t = jnp.tile(phi_pe, (1, 1, NSA_KV_GROUPS)).reshape(2, CMP_LEN, 1, GROUP_COLS)
    return pe_t, w1_bd.astype(BF16), w2_bd.astype(BF16)


def _fill_qz(qz_ref, qt_rows, row0, n_heads, tq):
    qz_ref[...] = jnp.zeros(qz_ref.shape, BF16)
    for h in range(n_heads):
        r = row0(h)
        qz_ref[r:r + HEAD_DIM, h * tq:(h + 1) * tq] = (qt_rows(h).astype(F32) * HEAD_DIM ** -0.5).astype(BF16)


def _softmax_probs(score_cols, tq, mk, m_ref, l_ref, p_ref, h, hp=None):
    hp = h if hp is None else hp
    m_all = m_ref[h]
    l_all = l_ref[h]
    alphas, m_news, l_news = [], [], []
    for c0 in range(0, tq, V7X_LANES):
        cols = slice(c0, c0 + V7X_LANES)
        s = score_cols(cols)
        if mk is not None:
            s = jnp.where(mk[:, cols], s, NEG)
        m_old = m_all[:, cols]
        m_new = jnp.maximum(m_old, jnp.max(s, axis=0, keepdims=True))
        alpha = jnp.exp(m_old - m_new)
        p = jnp.exp(s - m_new)
        l_news.append(alpha * l_all[:, cols] + jnp.sum(p, axis=0, keepdims=True))
        p_ref[:, hp * tq + c0:hp * tq + c0 + V7X_LANES] = p.astype(BF16)
        m_news.append(m_new)
        alphas.append(alpha)
    m_ref[h] = jnp.concatenate(m_news, axis=1)
    l_ref[h] = jnp.concatenate(l_news, axis=1)
    return jnp.concatenate(alphas, axis=1)


def _reset_softmax(m_ref, l_ref, acc_ref):
    m_ref[...] = jnp.full(m_ref.shape, NEG, F32)
    l_ref[...] = jnp.zeros(l_ref.shape, F32)
    acc_ref[...] = jnp.zeros(acc_ref.shape, F32)


def _cmp_select_t_kernel(tbl_ref, qt_ref, kc_ref, vct_ref, gt_ref, o_ref, selt_ref, bias_ref, qz_ref, *,
                         tq, n_cmp_pad, n_cmp, n_sel, pos0):
    i = pl.program_id(0)
    b = pl.program_id(1)
    q0 = pos0 + i * tq
    c_idx = lax.broadcasted_iota(jnp.int32, (n_cmp_pad, tq), 0)
    qpos = q0 + lax.broadcasted_iota(jnp.int32, (n_cmp_pad, tq), 1)
    dist = qpos - (c_idx * CMP_STRIDE + CMP_LEN - 1)
    valid = (dist >= 0) & (c_idx < n_cmp)

    @pl.when(b == 0)
    def _():
        bucket = _t5_bucket(dist)
        for h in range(N_HEADS):
            bias_ref[h] = _bias_from_bucket(bucket, tbl_ref, h)

    j_t = lax.broadcasted_iota(jnp.int32, (SEL_PAD, tq), 0)
    q_t = q0 + lax.broadcasted_iota(jnp.int32, (SEL_PAD, tq), 1)
    cur = jnp.right_shift(q_t, 6)
    forced = (j_t == 0) | (j_t == cur) | (j_t == cur - 1)
    avail = j_t <= cur
    j_i = lax.broadcasted_iota(jnp.int32, (SEL_PAD, n_cmp_pad), 0) * SEL_LEN
    c_i = lax.broadcasted_iota(jnp.int32, (SEL_PAD, n_cmp_pad), 1) * CMP_STRIDE
    inc_t = jnp.where((c_i < j_i + SEL_LEN) & (c_i + CMP_LEN > j_i) & (j_i < n_sel * SEL_LEN)
                      & (c_i < n_cmp * CMP_STRIDE), 1.0, 0.0).astype(BF16)
    j_s = lax.broadcasted_iota(jnp.int32, (n_sel, tq), 0)
    kc = kc_ref[...]

    for g in range(NSA_KV_GROUPS):
        _fill_qz(qz_ref, lambda h: qt_ref[(g * NSA_HPG + h) * HEAD_DIM:(g * NSA_HPG + h + 1) * HEAD_DIM, :],
                 lambda h: g * HEAD_DIM, NSA_HPG, tq)
        s_all = _dot(kc, qz_ref[...])
        psum = jnp.zeros((n_cmp_pad, tq), F32)
        outs = []
        for h in range(NSA_HPG):
            hh = g * NSA_HPG + h
            s = s_all[:, h * tq:(h + 1) * tq] + bias_ref[hh]
            s = jnp.where(valid, s, NEG)
            m = jnp.max(s, axis=0, keepdims=True)
            e = jnp.where(valid, jnp.exp(s - m), 0.0)
            p = e / jnp.maximum(jnp.sum(e, axis=0, keepdims=True), 1e-30)
            gate = _sigmoid(gt_ref[hh:hh + 1, :])
            outs.append(_dot(vct_ref[g * HEAD_DIM:(g + 1) * HEAD_DIM, :], p.astype(BF16)) * gate)
            psum = psum + p
        o_ref[:, g * GROUP_COLS:(g + 1) * GROUP_COLS] = jnp.concatenate(outs, axis=0).T
        p_hi, p_lo = _split_bf16(psum, 2)
        imp_t = _dot(inc_t, p_hi) + _dot(inc_t, p_lo)
        score_t = jnp.where(forced, BIG, jnp.where(avail, imp_t, -BIG))

        def pick_step(_, carry):
            remaining, chosen = carry
            top = jnp.max(remaining, axis=0, keepdims=True)
            first = jnp.min(jnp.where(remaining == top, j_s, n_sel), axis=0, keepdims=True)
            pick = j_s == first
            return jnp.where(pick, TAKEN, remaining), jnp.where(pick, 1.0, chosen)

        _, chosen = lax.fori_loop(0, N_SELECT, pick_step, (score_t[:n_sel], jnp.zeros((n_sel, tq), F32)))
        chosen = jnp.concatenate([chosen, jnp.zeros((SEL_PAD - n_sel, tq), F32)], axis=0)
        selt_ref[g * SEL_PAD:(g + 1) * SEL_PAD, :] = chosen.astype(BF16)


def nsa_cmp_select_t(table, q_t, k_cmp, v_cmp_t, gates_t, n_batch, seq, n_cmp, n_sel):
    tq = ATT_TILE
    nq = seq // tq
    m = n_batch * seq
    n_cmp_pad = k_cmp.shape[1]
    kern = functools.partial(_cmp_select_t_kernel, tq=tq, n_cmp_pad=n_cmp_pad, n_cmp=n_cmp, n_sel=n_sel, pos0=0)
    return pl.pallas_call(
        kern,
        grid=(nq, n_batch),
        in_specs=[
            _smem_spec(),
            pl.BlockSpec((D_MODEL, tq), lambda i, b: (0, b * nq + i)),
            pl.BlockSpec((None, n_cmp_pad, GROUP_COLS), lambda i, b: (b, 0, 0)),
            pl.BlockSpec((None, GROUP_COLS, n_cmp_pad), lambda i, b: (b, 0, 0)),
            pl.BlockSpec((3 * N_HEADS, tq), lambda i, b: (0, b * nq + i)),
        ],
        out_specs=[
            pl.BlockSpec((tq, D_MODEL), lambda i, b: (b * nq + i, 0)),
            pl.BlockSpec((NSA_KV_GROUPS * SEL_PAD, tq), lambda i, b: (0, b * nq + i)),
        ],
        out_shape=[
            jax.ShapeDtypeStruct((m, D_MODEL), F32),
            jax.ShapeDtypeStruct((NSA_KV_GROUPS * SEL_PAD, m), BF16),
        ],
        scratch_shapes=[
            pltpu.VMEM((N_HEADS, n_cmp_pad, tq), F32),
            pltpu.VMEM((GROUP_COLS, NSA_HPG * tq), BF16),
        ],
        compiler_params=_cparams("arbitrary", "arbitrary"),
        name="nsa_cmp_select",
    )(table, q_t, k_cmp, v_cmp_t, gates_t)


def _nsa_attn_t_kernel(qt_ref, k_ref, vt_ref, gt_ref, toep_ref, *rest, mode, tile, branch):
    if mode == "sel":
        selt_ref, o_ref, qz_ref, m_ref, l_ref, acc_ref, p_ref = rest
    else:
        o_ref, qz_ref, m_ref, l_ref, acc_ref, p_ref = rest
    i = pl.program_id(1)
    ik = lax.broadcasted_iota(jnp.int32, (tile, tile), 0)
    iq = lax.broadcasted_iota(jnp.int32, (tile, tile), 1)
    causal = iq >= ik
    above = ik > iq

    for g0 in range(0, NSA_KV_GROUPS, GROUPS_PER_PASS):
        groups = range(g0, g0 + GROUPS_PER_PASS)
        for gi, g in enumerate(groups):
            _fill_qz(qz_ref.at[gi],
                     lambda h, g=g: qt_ref[(g * NSA_HPG + h) * HEAD_DIM:(g * NSA_HPG + h + 1) * HEAD_DIM, :],
                     lambda h, g=g: g * HEAD_DIM, NSA_HPG, tile)
        _reset_softmax(m_ref, l_ref, acc_ref)

        def attend(kt, dd, groups=groups):
            r0 = pl.multiple_of(kt * tile, tile)
            k_blk = k_ref[pl.ds(r0, tile), :]
            s_alls = [_dot(k_blk, qz_ref[gi]) for gi in range(GROUPS_PER_PASS)]
            if mode == "sel":
                krow = lax.broadcasted_iota(jnp.int32, (tile, SEL_PAD), 0)
                jcol = lax.broadcasted_iota(jnp.int32, (tile, SEL_PAD), 1)
                expand = jnp.where(jcol == jnp.right_shift(r0 + krow, 6), 1.0, 0.0).astype(BF16)
                mks = [_dot(expand, selt_ref[g * SEL_PAD:(g + 1) * SEL_PAD, :]) > 0.5 for g in groups]
                if dd == 0:
                    mks = [mk & causal for mk in mks]
            else:
                mks = [causal if dd == 0 else (above if dd == 2 else None)] * GROUPS_PER_PASS
            alphas = []
            for gi, g in enumerate(groups):
                for h in range(NSA_HPG):
                    hh = g * NSA_HPG + h

                    def score_cols(cols, gi=gi, h=h, hh=hh):
                        s = s_alls[gi][:, h * tile + cols.start:h * tile + cols.stop]
                        return s + toep_ref[hh, dd, :, cols] if dd < 2 else s

                    alphas.append(_softmax_probs(score_cols, tile, mks[gi], m_ref, l_ref, p_ref.at[gi],
                                                 gi * NSA_HPG + h, h))
            for gi, g in enumerate(groups):
                v_t = vt_ref[g * HEAD_DIM:(g + 1) * HEAD_DIM, pl.ds(r0, tile)]
                pv_all = _dot(v_t, p_ref[gi])
                for h in range(NSA_HPG):
                    a = gi * NSA_HPG + h
                    acc_ref[a] = alphas[a] * acc_ref[a] + pv_all[:, h * tile:(h + 1) * tile]

        attend(i, 0)

        @pl.when(i >= 1)
        def _():
            attend(i - 1, 1)

        if mode == "sel":
            def far_body(kt, carry):
                attend(kt, 2)
                return carry
            lax.fori_loop(0, jnp.maximum(i - 1, 0), far_body, 0)
        else:
            @pl.when(i >= 2)
            def _():
                attend(i - 2, 2)

        for gi, g in enumerate(groups):
            outs = []
            for h in range(NSA_HPG):
                hh = g * NSA_HPG + h
                a = gi * NSA_HPG + h
                gate = _sigmoid(gt_ref[branch * N_HEADS + hh:branch * N_HEADS + hh + 1, :])
                outs.append(acc_ref[a] / jnp.maximum(l_ref[a], 1e-30) * gate)
            o_ref[:, g * GROUP_COLS:(g + 1) * GROUP_COLS] = jnp.concatenate(outs, axis=0).T


def nsa_attn_t(mode, toep, q_t, k_src, k_col, v_t, gates_t, sel_t, n_batch, seq):
    tile = ATT_TILE
    assert WINDOW == 2 * tile
    nq = seq // tile
    m = n_batch * seq
    branch = 1 if mode == "sel" else 2
    kern = functools.partial(_nsa_attn_t_kernel, mode=mode, tile=tile, branch=branch)
    in_specs = [
        pl.BlockSpec((D_MODEL, tile), lambda b, i: (0, b * nq + i)),
        pl.BlockSpec((seq, GROUP_COLS), lambda b, i: (b, k_col)),
        pl.BlockSpec((GROUP_COLS, seq), lambda b, i: (0, b)),
        pl.BlockSpec((3 * N_HEADS, tile), lambda b, i: (0, b * nq + i)),
        pl.BlockSpec((N_HEADS, 2, tile, tile), lambda b, i: (0, 0, 0, 0)),
    ]
    args = [q_t, k_src, v_t, gates_t, toep]
    if mode == "sel":
        in_specs.append(pl.BlockSpec((NSA_KV_GROUPS * SEL_PAD, tile), lambda b, i: (0, b * nq + i)))
        args.append(sel_t)
    return pl.pallas_call(
        kern,
        grid=(n_batch, nq),
        in_specs=in_specs,
        out_specs=pl.BlockSpec((tile, D_MODEL), lambda b, i: (b * nq + i, 0)),
        out_shape=jax.ShapeDtypeStruct((m, D_MODEL), F32),
        scratch_shapes=[
            pltpu.VMEM((GROUPS_PER_PASS, GROUP_COLS, NSA_HPG * tile), BF16),
            pltpu.VMEM((GROUPS_PER_PASS * NSA_HPG, 1, tile), F32),
            pltpu.VMEM((GROUPS_PER_PASS * NSA_HPG, 1, tile), F32),
            pltpu.VMEM((GROUPS_PER_PASS * NSA_HPG, HEAD_DIM, tile), F32),
            pltpu.VMEM((GROUPS_PER_PASS, tile, NSA_HPG * tile), BF16),
        ],
        compiler_params=_cparams("parallel", "arbitrary"),
        name="nsa_attn_" + mode,
    )(*args)


def nsa_prompt_t(table, toep, q_t, kvn, kv_bf, win_bf, vslc_t, vwin_t, gates_t, cw, n_batch, seq):
    pe_t, w1_bd, w2_bd = cw
    n_cmp = (seq - CMP_LEN) // CMP_STRIDE + 1
    n_sel = -(-seq // SEL_LEN)
    assert n_sel <= SEL_PAD and n_sel % 8 == 0
    ctok = compress_prompt(kvn, n_batch, seq, pe_t, w1_bd, w2_bd).astype(BF16)
    k_cmp = ctok[:, 0]
    v_cmp_t = ctok[:, 1].transpose(0, 2, 1)
    o_cmp, sel_t = nsa_cmp_select_t(table, q_t, k_cmp, v_cmp_t, gates_t, n_batch, seq, n_cmp, n_sel)
    o_slc = nsa_attn_t("sel", toep, q_t, kv_bf, 2, vslc_t, gates_t, sel_t, n_batch, seq)
    o_win = nsa_attn_t("win", toep, q_t, win_bf, 0, vwin_t, gates_t, None, n_batch, seq)
    return o_cmp, o_slc, o_win


CUMSUM_CHUNK = 256
HEAD_BLOCK = 4
FOX_BLOCKS_PER_STEP = 2


def _tril_bf16(n):
    r = lax.broadcasted_iota(jnp.int32, (n, n), 0)
    c = lax.broadcasted_iota(jnp.int32, (n, n), 1)
    return jnp.where(r >= c, 1.0, 0.0).astype(BF16)


def _cumsum_rows(x, tri):
    return sum(_dot(tri, part) for part in _split_bf16(x, 3))


def _fox_prep_kernel(f_ref, b_ref, lf_ref, d_ref, *, seq):
    tri = _tril_bf16(CUMSUM_CHUNK)
    carry = jnp.zeros((1, N_HEADS), F32)
    for c in range(seq // CUMSUM_CHUNK):
        rows = slice(c * CUMSUM_CHUNK, (c + 1) * CUMSUM_CHUNK)
        lf = _log_sigmoid(f_ref[rows, :] + b_ref[...])
        lf_ref[rows, :] = lf
        d = carry + _cumsum_rows(lf, tri)
        d_ref[rows, :] = d
        carry = d[CUMSUM_CHUNK - 1:CUMSUM_CHUNK, :]


def fox_prep_prompt(f_raw, b_f, n_batch, seq):
    m = n_batch * seq
    blk = pl.BlockSpec((seq, N_HEADS), lambda b: (b, 0))
    return pl.pallas_call(
        functools.partial(_fox_prep_kernel, seq=seq),
        grid=(n_batch,),
        in_specs=[blk, pl.BlockSpec((1, N_HEADS), lambda b: (0, 0))],
        out_specs=[blk, blk],
        out_shape=[jax.ShapeDtypeStruct((m, N_HEADS), F32)] * 2,
        compiler_params=_cparams("parallel"),
        name="fox_prep",
    )(f_raw, b_f.reshape(1, N_HEADS))


def _fox_attn_t_kernel(qt_ref, k_ref, vt_ref, drow_ref, dcol_ref, o_ref, qz_ref, m_ref, l_ref, acc_ref, p_ref, *,
                       tile):
    i = pl.program_id(2)
    ik = lax.broadcasted_iota(jnp.int32, (tile, tile), 0)
    iq = lax.broadcasted_iota(jnp.int32, (tile, tile), 1)
    causal = iq >= ik
    cols_blk = HEAD_BLOCK * HEAD_DIM
    for bi in range(FOX_BLOCKS_PER_STEP):
        _fill_qz(qz_ref.at[bi], lambda h, bi=bi: qt_ref[(bi * HEAD_BLOCK + h) * HEAD_DIM:(bi * HEAD_BLOCK + h + 1) * HEAD_DIM, :],
                 lambda h: h * HEAD_DIM, HEAD_BLOCK, tile)
    _reset_softmax(m_ref, l_ref, acc_ref)

    def attend(kt, diag):
        r0 = pl.multiple_of(kt * tile, tile)
        k_rows = k_ref[pl.ds(r0, tile), :]
        s_alls = [_dot(k_rows[:, bi * cols_blk:(bi + 1) * cols_blk], qz_ref[bi])
                  for bi in range(FOX_BLOCKS_PER_STEP)]
        alphas = []
        for bi in range(FOX_BLOCKS_PER_STEP):
            for h in range(HEAD_BLOCK):
                a = bi * HEAD_BLOCK + h
                d_keys = dcol_ref[pl.ds(r0, tile), a:a + 1]

                def score_cols(cols, bi=bi, h=h, a=a, d_keys=d_keys):
                    s = s_alls[bi][:, h * tile + cols.start:h * tile + cols.stop]
                    return s + drow_ref[a:a + 1, cols] - d_keys

                alphas.append(_softmax_probs(score_cols, tile, causal if diag else None, m_ref, l_ref,
                                             p_ref.at[bi], a, h))
        for bi in range(FOX_BLOCKS_PER_STEP):
            for h in range(HEAD_BLOCK):
                a = bi * HEAD_BLOCK + h
                v_t = vt_ref[a * HEAD_DIM:(a + 1) * HEAD_DIM, pl.ds(r0, tile)]
                acc_ref[a] = alphas[a] * acc_ref[a] + _dot(v_t, p_ref[bi, :, h * tile:(h + 1) * tile])

    attend(i, True)

    def body(kt, carry):
        attend(kt, False)
        return carry

    lax.fori_loop(0, i, body, 0)
    outs = [acc_ref[a] / jnp.maximum(l_ref[a], 1e-30) for a in range(FOX_BLOCKS_PER_STEP * HEAD_BLOCK)]
    o_ref[...] = jnp.concatenate(outs, axis=0).T


def fox_attn_t(q_t, k_bf, v_t, d_cum, n_batch, seq):
    tile = ATT_TILE
    nq = seq // tile
    m = n_batch * seq
    hps = FOX_BLOCKS_PER_STEP * HEAD_BLOCK
    nhb = N_HEADS // hps
    d4 = d_cum[:m].reshape(n_batch, seq, nhb, hps)
    d_col = d4.transpose(0, 2, 1, 3)
    d_row = d4.transpose(0, 2, 3, 1)
    cols = hps * HEAD_DIM
    return pl.pallas_call(
        functools.partial(_fox_attn_t_kernel, tile=tile),
        grid=(n_batch, nhb, nq),
        in_specs=[
            pl.BlockSpec((cols, tile), lambda b, g, i: (g, b * nq + i)),
            pl.BlockSpec((seq, cols), lambda b, g, i: (b, g)),
            pl.BlockSpec((cols, seq), lambda b, g, i: (g, b)),
            pl.BlockSpec((None, None, hps, tile), lambda b, g, i: (b, g, 0, i)),
            pl.BlockSpec((None, None, seq, hps), lambda b, g, i: (b, g, 0, 0)),
        ],
        out_specs=pl.BlockSpec((tile, cols), lambda b, g, i: (b * nq + i, g)),
        out_shape=jax.ShapeDtypeStruct((m, D_MODEL), F32),
        scratch_shapes=[
            pltpu.VMEM((FOX_BLOCKS_PER_STEP, HEAD_BLOCK * HEAD_DIM, HEAD_BLOCK * tile), BF16),
            pltpu.VMEM((hps, 1, tile), F32),
            pltpu.VMEM((hps, 1, tile), F32),
            pltpu.VMEM((hps, HEAD_DIM, tile), F32),
            pltpu.VMEM((FOX_BLOCKS_PER_STEP, tile, HEAD_BLOCK * tile), BF16),
        ],
        compiler_params=_cparams("parallel", "parallel", "arbitrary"),
        name="fox_attn",
    )(q_t, k_bf, v_t, d_row, d_col)


HT_LANES = V7X_LANES


def _lane_head_onehot(n_heads, per_head):
    r = lax.broadcasted_iota(jnp.int32, (n_heads, n_heads * per_head), 0)
    c = lax.broadcasted_iota(jnp.int32, (n_heads, n_heads * per_head), 1)
    return jnp.where(_div_pow2(c, per_head) == r, 1.0, 0.0).astype(BF16)


def _expand_heads(x, onehot):
    return sum(_dot(part, onehot) for part in _split_bf16(x, 3))


def _diag_blocks_to_rows(acc_t, n_heads, feat, per_head):
    rows, lanes = acc_t.shape
    r = lax.broadcasted_iota(jnp.int32, (rows, lanes), 0)
    c = lax.broadcasted_iota(jnp.int32, (rows, lanes), 1)
    z = jnp.where(_div_pow2(r, feat) == _div_pow2(c, per_head), acc_t, 0.0)
    cc = lax.broadcasted_iota(jnp.int32, (lanes, lanes), 0)
    tt = lax.broadcasted_iota(jnp.int32, (lanes, lanes), 1)
    fold = jnp.where(_mod_pow2(cc, per_head) == tt, 1.0, 0.0).astype(BF16)
    out_t = sum(_dot(part, fold) for part in _split_bf16(z, 3))
    return out_t.T


def block_diag_queries(q_rows, n_seq, per_seq, n_heads):
    q4 = q_rows.reshape(n_seq, per_seq, n_heads, HEAD_DIM).transpose(0, 2, 3, 1)
    eye = jnp.eye(n_heads, dtype=q_rows.dtype)
    q_bd = q4[:, :, :, None, :] * eye[None, :, None, :, None]
    return q_bd.reshape(n_seq, n_heads * HEAD_DIM, n_heads * per_seq).astype(BF16)


PAGES_PER_STEP = 4


def feature_major_pages(pool):
    nd = pool.ndim
    return pool.transpose(0, 1, *range(3, nd), 2).reshape(pool.shape[0], pool.shape[1], -1, pool.shape[2])


def _joint_update(s_list, mk_list, vt_list, m_ref, l_ref, acc_ref):
    s_list = [s if mk is None else jnp.where(mk, s, NEG) for s, mk in zip(s_list, mk_list)]
    m_old = m_ref[0]
    m_new = m_old
    for s in s_list:
        m_new = jnp.maximum(m_new, jnp.max(s, axis=0, keepdims=True))
    alpha = jnp.exp(m_old - m_new)
    l_new = alpha * l_ref[0]
    pv = None
    for s, mk, v_t in zip(s_list, mk_list, vt_list):
        pr = jnp.exp(s - m_new)
        if mk is not None:
            pr = jnp.where(mk, pr, 0.0)
        l_new = l_new + jnp.sum(pr, axis=0, keepdims=True)
        term = _dot(v_t.astype(BF16), pr.astype(BF16))
        pv = term if pv is None else pv + term
    l_ref[0] = l_new
    acc_ref[0] = alpha * acc_ref[0] + pv
    m_ref[0] = m_new


def _fox_decode_fm_kernel(pt_ref, qbd_ref, qbdt_ref, *refs, n_steps, per_seq, pps):
    k_refs = refs[:pps]
    v_refs = refs[pps:2 * pps]
    lf_refs = refs[2 * pps:3 * pps]
    kn_ref, vn_ref, fn_ref, bf_ref, o_ref, lfn_ref, qz_ref, qzt_ref, m_ref, l_ref, acc_ref, carry_ref = refs[3 * pps:]
    p = pl.program_id(1)

    @pl.when(p == 0)
    def _():
        qz_ref[...] = (qbd_ref[...].astype(F32) * HEAD_DIM ** -0.5).astype(BF16)
        qzt_ref[...] = (qbdt_ref[...].astype(F32) * HEAD_DIM ** -0.5).astype(BF16)
        _reset_softmax(m_ref, l_ref, acc_ref)
        carry_ref[...] = jnp.zeros(carry_ref.shape, F32)

    tri = _tril_bf16(PAGE_SIZE)
    onehot = _lane_head_onehot(N_HEADS, per_seq)

    def update(s_t, mk, pv_of):
        if mk is not None:
            s_t = jnp.where(mk, s_t, NEG)
        m_old = m_ref[0]
        m_new = jnp.maximum(m_old, jnp.max(s_t, axis=0, keepdims=True))
        alpha = jnp.exp(m_old - m_new)
        pr = jnp.exp(s_t - m_new)
        if mk is not None:
            pr = jnp.where(mk, pr, 0.0)
        l_ref[0] = alpha * l_ref[0] + jnp.sum(pr, axis=0, keepdims=True)
        acc_ref[0] = alpha * acc_ref[0] + pv_of(pr.astype(BF16))
        m_ref[0] = m_new

    @pl.when(p < n_steps)
    def _():
        local, s_raw = [], []
        for i in range(pps):
            lf_t = lf_refs[i][...]
            lf_exp = sum(lax.dot_general(part, onehot, (((0,), (0,)), ((), ())), preferred_element_type=F32)
                         for part in _split_bf16(lf_t, 3))
            local.append(_cumsum_rows(lf_exp, tri))
            s_raw.append(_dot(qzt_ref[...], k_refs[i][...].astype(BF16)).T)
        carry = carry_ref[...]
        s_list = []
        for i in range(pps):
            d = carry + local[i]
            s_list.append(s_raw[i] - d)
            carry = d[PAGE_SIZE - 1:PAGE_SIZE, :]
        carry_ref[...] = carry
        _joint_update(s_list, [None] * pps, [r[...] for r in v_refs], m_ref, l_ref, acc_ref)

    @pl.when(p == n_steps)
    def _():
        lf_new = _log_sigmoid(fn_ref[...] + bf_ref[...])
        lfn_ref[...] = lf_new
        pad = PAGE_SIZE - per_seq
        k_rows = jnp.concatenate([kn_ref[...], jnp.zeros((pad, D_MODEL), F32)], axis=0).astype(BF16)
        v_rows = jnp.concatenate([vn_ref[...], jnp.zeros((pad, D_MODEL), F32)], axis=0).astype(BF16)
        lf_rows = jnp.concatenate([lf_new, jnp.zeros((pad, N_HEADS), F32)], axis=0)
        d = carry_ref[...] + _cumsum_rows(_expand_heads(lf_rows, onehot), tri)
        srow = lax.broadcasted_iota(jnp.int32, (PAGE_SIZE, HT_LANES), 0)
        tq = _mod_pow2(lax.broadcasted_iota(jnp.int32, (PAGE_SIZE, HT_LANES), 1), per_seq)
        s_t = _dot(k_rows, qz_ref[...]) - d
        update(s_t, srow <= tq, lambda pr: lax.dot_general(v_rows, pr, (((0,), (0,)), ((), ())),
                                                           preferred_element_type=F32))
        o_t = acc_ref[0] / jnp.maximum(l_ref[0], 1e-30)
        o_ref[...] = _diag_blocks_to_rows(o_t, N_HEADS, HEAD_DIM, per_seq)[:per_seq, :]


def fox_decode_fm(page_table, q_bd, kv_pool_t, lf_pool_t, layer, kv_new, f_new, b_f, row_block0, n_seq, per_seq):
    n_pages = page_table.shape[1]
    pps = PAGES_PER_STEP
    assert N_HEADS * per_seq == HT_LANES and n_pages % pps == 0
    n_steps = n_pages // pps
    kern = functools.partial(_fox_decode_fm_kernel, n_steps=n_steps, per_seq=per_seq, pps=pps)

    def page(i):
        return lambda b, p, pt: pt[b, jnp.minimum(p, n_steps - 1) * pps + i]

    def page_spec(rows, blk, i):
        return pl.BlockSpec((None, None, rows, PAGE_SIZE), lambda b, p, pt, i=i: (layer, page(i)(b, p, pt), blk, 0))

    grid_spec = pltpu.PrefetchScalarGridSpec(
        num_scalar_prefetch=1,
        grid=(n_seq, n_steps + 1),
        in_specs=[pl.BlockSpec((None, D_MODEL, HT_LANES), lambda b, p, pt: (b, 0, 0)),
                  pl.BlockSpec((None, HT_LANES, D_MODEL), lambda b, p, pt: (b, 0, 0))]
        + [page_spec(D_MODEL, 0, i) for i in range(pps)]
        + [page_spec(D_MODEL, 1, i) for i in range(pps)]
        + [page_spec(N_HEADS, 0, i) for i in range(pps)]
        + [
            pl.BlockSpec((per_seq, D_MODEL), lambda b, p, pt: (row_block0 + b, 0)),
            pl.BlockSpec((per_seq, D_MODEL), lambda b, p, pt: (row_block0 + b, 1)),
            pl.BlockSpec((per_seq, N_HEADS), lambda b, p, pt: (row_block0 + b, 0)),
            pl.BlockSpec((1, N_HEADS), lambda b, p, pt: (0, 0)),
        ],
        out_specs=[
            pl.BlockSpec((per_seq, D_MODEL), lambda b, p, pt: (b, 0)),
            pl.BlockSpec((per_seq, N_HEADS), lambda b, p, pt: (b, 0)),
        ],
        scratch_shapes=[
            pltpu.VMEM((D_MODEL, HT_LANES), BF16),
            pltpu.VMEM((HT_LANES, D_MODEL), BF16),
            pltpu.VMEM((1, 1, HT_LANES), F32),
            pltpu.VMEM((1, 1, HT_LANES), F32),
            pltpu.VMEM((1, D_MODEL, HT_LANES), F32),
            pltpu.VMEM((1, HT_LANES), F32),
        ],
    )
    return pl.pallas_call(
        kern,
        grid_spec=grid_spec,
        out_shape=[jax.ShapeDtypeStruct((n_seq * per_seq, D_MODEL), F32),
                   jax.ShapeDtypeStruct((n_seq * per_seq, N_HEADS), F32)],
        compiler_params=_cparams("parallel", "arbitrary"),
        name="fox_decode",
    )(page_table, q_bd, q_bd.transpose(0, 2, 1), *([kv_pool_t] * (2 * pps)), *([lf_pool_t] * pps),
      kv_new, kv_new, f_new, b_f.reshape(1, N_HEADS))


DEC_ROW_TILE = 512


def _group_rows_to_heads(acc_t):
    return jnp.concatenate([acc_t[(h // NSA_HPG) * HEAD_DIM:(h // NSA_HPG + 1) * HEAD_DIM] for h in range(N_HEADS)],
                           axis=0)


def group_block_diag_queries(q_rows, n_seq, per_seq):
    q5 = q_rows.reshape(n_seq, per_seq, NSA_KV_GROUPS, NSA_HPG, HEAD_DIM).transpose(0, 2, 4, 3, 1)
    eye = jnp.eye(NSA_KV_GROUPS, dtype=q_rows.dtype)
    q_bd = q5[:, :, :, None, :, :] * eye[None, :, None, :, None, None]
    return q_bd.reshape(n_seq, GROUP_COLS, N_HEADS * per_seq).astype(BF16)


def _nsa_dec_bias_kernel(tbl_ref, o_ref, *, pos0, per_seq):
    onehot = _lane_head_onehot(N_HEADS, per_seq)
    tab = _expand_heads(tbl_ref[...], onehot)
    row = lax.broadcasted_iota(jnp.int32, (PAGE_SIZE, HT_LANES), 0)
    t = _mod_pow2(lax.broadcasted_iota(jnp.int32, (PAGE_SIZE, HT_LANES), 1), per_seq)
    dists = (PAGE_SIZE + t - row, t - row, pos0 + t - (row * CMP_STRIDE + CMP_LEN - 1))
    for n, dist in enumerate(dists):
        bucket = _t5_bucket(dist)
        val = jnp.broadcast_to(tab[0:1, :], (PAGE_SIZE, HT_LANES))
        for k in range(1, N_BUCKETS):
            val = jnp.where(bucket == k, tab[k:k + 1, :], val)
        o_ref[n] = val
    o_ref[3] = jnp.broadcast_to(tab[N_BUCKETS - 1:N_BUCKETS, :], (PAGE_SIZE, HT_LANES))


def nsa_dec_bias(table, pos0, per_seq):
    assert PAGE_SIZE >= T5_SATURATION_DIST
    return pl.pallas_call(
        functools.partial(_nsa_dec_bias_kernel, pos0=pos0, per_seq=per_seq),
        out_shape=jax.ShapeDtypeStruct((4, PAGE_SIZE, HT_LANES), F32),
        name="nsa_dec_bias",
    )(table)


def nsa_dec_compress(xd, pe_t, w1_bd, w2_bd):
    r_total = xd.shape[1]
    rows = min(DEC_ROW_TILE, r_total)
    return pl.pallas_call(
        functools.partial(_compress_kernel, rows=rows, l_major=True),
        grid=(r_total // rows, 2),
        in_specs=[
            pl.BlockSpec((CMP_STRIDE, rows, GROUP_COLS), lambda i, j: (0, i, j)),
            pl.BlockSpec((None, CMP_LEN, 1, GROUP_COLS), lambda i, j: (j, 0, 0, 0)),
            pl.BlockSpec((None, CMP_LEN, GROUP_COLS, NSA_KV_GROUPS * CMP_HIDDEN), lambda i, j: (j, 0, 0, 0)),
            pl.BlockSpec((None, NSA_KV_GROUPS * CMP_HIDDEN, GROUP_COLS), lambda i, j: (j, 0, 0)),
        ],
        out_specs=pl.BlockSpec((None, rows, GROUP_COLS), lambda i, j: (j, i, 0)),
        out_shape=jax.ShapeDtypeStruct((2, r_total, GROUP_COLS), F32),
        compiler_params=_cparams("parallel", "arbitrary"),
        name="nsa_dec_compress",
    )(xd, pe_t, w1_bd, w2_bd)


def _dec_finish(m_ref, l_ref, acc_ref, gate_row, per_seq):
    o_t = acc_ref[0] / jnp.maximum(l_ref[0], 1e-30) * _sigmoid(gate_row)
    return _diag_blocks_to_rows(_group_rows_to_heads(o_t), N_HEADS, HEAD_DIM, per_seq)[:per_seq, :]


def _nsa_dec_cmp_kernel(qbd_ref, kc_ref, vc_ref, g_ref, bias_ref, o_ref, sel_ref, score_ref, *,
                        n_cmp, n_sel, pos0, per_seq):
    n_c = kc_ref.shape[0]
    c_idx = lax.broadcasted_iota(jnp.int32, (n_c, HT_LANES), 0)
    lane = lax.broadcasted_iota(jnp.int32, (n_c, HT_LANES), 1)
    t = _mod_pow2(lane, per_seq)
    dist = pos0 + t - (c_idx * CMP_STRIDE + CMP_LEN - 1)
    valid = (dist >= 0) & (c_idx < n_cmp)
    qz = (qbd_ref[...].astype(F32) * HEAD_DIM ** -0.5).astype(BF16)
    s = _dot(kc_ref[...], qz) + bias_ref[2]
    s = jnp.where(valid, s, NEG)
    m = jnp.max(s, axis=0, keepdims=True)
    e = jnp.where(valid, jnp.exp(s - m), 0.0)
    p = e / jnp.maximum(jnp.sum(e, axis=0, keepdims=True), 1e-30)
    o_t = lax.dot_general(vc_ref[...], p.astype(BF16), (((0,), (0,)), ((), ())), preferred_element_type=F32)
    o_t = o_t * _sigmoid(g_ref[0])
    o_ref[...] = _diag_blocks_to_rows(_group_rows_to_heads(o_t), N_HEADS, HEAD_DIM, per_seq)[:per_seq, :]
    src = lax.broadcasted_iota(jnp.int32, (HT_LANES, HT_LANES), 0)
    dst = lax.broadcasted_iota(jnp.int32, (HT_LANES, HT_LANES), 1)
    same_t = _mod_pow2(src, per_seq) == _mod_pow2(dst, per_seq)
    g_of_src = _div_pow2(src, per_seq * NSA_HPG)
    n_gt = NSA_KV_GROUPS * per_seq
    head_sum = jnp.where(same_t & (g_of_src == _div_pow2(dst, per_seq)) & (dst < n_gt), 1.0, 0.0).astype(BF16)
    psum = sum(_dot(part, head_sum) for part in _split_bf16(p, 2))
    j_i = lax.broadcasted_iota(jnp.int32, (SEL_PAD, n_c), 0) * SEL_LEN
    c_i = lax.broadcasted_iota(jnp.int32, (SEL_PAD, n_c), 1) * CMP_STRIDE
    inc_t = jnp.where((c_i < j_i + SEL_LEN) & (c_i + CMP_LEN > j_i) & (j_i < n_sel * SEL_LEN)
                      & (c_i < n_cmp * CMP_STRIDE), 1.0, 0.0).astype(BF16)
    imp_t = sum(_dot(inc_t, part) for part in _split_bf16(psum, 2))
    j_t = lax.broadcasted_iota(jnp.int32, (SEL_PAD, HT_LANES), 0)
    q_t = pos0 + _mod_pow2(lax.broadcasted_iota(jnp.int32, (SEL_PAD, HT_LANES), 1), per_seq)
    cur = jnp.right_shift(q_t, 6)
    forced = (j_t == 0) | (j_t == cur) | (j_t == cur - 1)
    avail = j_t <= cur
    score_t = jnp.where(forced, BIG, jnp.where(avail, imp_t, -BIG))
    score_ref[...] = score_t
    n_rank = -(-n_sel // 8) * 8
    sc = score_t[:n_rank]
    j_s = lax.broadcasted_iota(jnp.int32, (n_rank, HT_LANES), 0)

    def rank_step(r, rank):
        row = jnp.broadcast_to(score_ref[pl.ds(r, 1), :], (n_rank, HT_LANES))
        ahead = jnp.where(row > sc, 1.0, jnp.where((row == sc) & (j_s > r), 1.0, 0.0))
        return rank + ahead

    rank = lax.fori_loop(0, n_sel, rank_step, jnp.zeros((n_rank, HT_LANES), F32))
    chosen = jnp.where((rank < float(N_SELECT)) & (j_s < n_sel), 1.0, 0.0)
    chosen = jnp.concatenate([chosen, jnp.zeros((SEL_PAD - n_rank, HT_LANES), F32)], axis=0).astype(BF16)
    spread = jnp.where(same_t & (src == _div_pow2(dst, per_seq * NSA_HPG) * per_seq + _mod_pow2(dst, per_seq)),
                       1.0, 0.0).astype(BF16)
    sel_ref[...] = _dot(chosen, spread).astype(BF16)


def nsa_dec_cmp(q_bd, k_cmp, v_cmp, gate_rows, bias, n_cmp, n_sel, pos0, per_seq):
    n_seq, n_c = k_cmp.shape[:2]
    kern = functools.partial(_nsa_dec_cmp_kernel, n_cmp=n_cmp, n_sel=n_sel, pos0=pos0, per_seq=per_seq)
    return pl.pallas_call(
        kern,
        grid=(n_seq,),
        in_specs=[
            pl.BlockSpec((None, GROUP_COLS, HT_LANES), lambda b: (b, 0, 0)),
            pl.BlockSpec((None, n_c, GROUP_COLS), lambda b: (b, 0, 0)),
            pl.BlockSpec((None, n_c, GROUP_COLS), lambda b: (b, 0, 0)),
            pl.BlockSpec((None, 3, 1, HT_LANES), lambda b: (b, 0, 0, 0)),
            pl.BlockSpec((4, PAGE_SIZE, HT_LANES), lambda b: (0, 0, 0)),
        ],
        out_specs=[
            pl.BlockSpec((per_seq, D_MODEL), lambda b: (b, 0)),
            pl.BlockSpec((None, SEL_PAD, HT_LANES), lambda b: (b, 0, 0)),
        ],
        out_shape=[
            jax.ShapeDtypeStruct((n_seq * per_seq, D_MODEL), F32),
            jax.ShapeDtypeStruct((n_seq, SEL_PAD, HT_LANES), BF16),
        ],
        scratch_shapes=[pltpu.VMEM((SEL_PAD, HT_LANES), F32)],
        compiler_params=_cparams("parallel"),
        name="nsa_dec_cmp",
    )(q_bd, k_cmp, v_cmp, gate_rows, bias)


def _dec_gather_fm_kernel(pt_ref, *refs, pps):
    o_ref = refs[pps]
    rpp = PAGE_SIZE // CMP_STRIDE
    for i in range(pps):
        x3 = refs[i][...].T.reshape(rpp, CMP_STRIDE, 2 * GROUP_COLS)
        for l in range(CMP_STRIDE):
            o_ref[l, i * rpp:(i + 1) * rpp, :] = x3[:, l, :]


def nsa_dec_gather_fm(page_table, pool_t, layer):
    n_seq, n_pages = page_table.shape
    pps = PAGES_PER_STEP
    assert n_pages % pps == 0
    rpp = PAGE_SIZE // CMP_STRIDE
    grid_spec = pltpu.PrefetchScalarGridSpec(
        num_scalar_prefetch=1,
        grid=(n_seq, n_pages // pps),
        in_specs=[pl.BlockSpec((None, None, 2 * GROUP_COLS, PAGE_SIZE),
                               lambda b, p, pt, i=i: (layer, pt[b, p * pps + i], 0, 0)) for i in range(pps)],
        out_specs=pl.BlockSpec((CMP_STRIDE, rpp * pps, 2 * GROUP_COLS),
                               lambda b, p, pt: (0, b * (n_pages // pps) + p, 0)),
    )
    return pl.pallas_call(
        functools.partial(_dec_gather_fm_kernel, pps=pps),
        grid_spec=grid_spec,
        out_shape=jax.ShapeDtypeStruct((CMP_STRIDE, n_seq * n_pages * rpp, 2 * GROUP_COLS), F32),
        compiler_params=_cparams("parallel", "arbitrary"),
        name="nsa_dec_gather",
    )(page_table, *([pool_t] * pps))


def _dec_update(s_t, mk, pv_of, m_ref, l_ref, acc_ref):
    if mk is not None:
        s_t = jnp.where(mk, s_t, NEG)
    m_old = m_ref[0]
    m_new = jnp.maximum(m_old, jnp.max(s_t, axis=0, keepdims=True))
    alpha = jnp.exp(m_old - m_new)
    pr = jnp.exp(s_t - m_new)
    if mk is not None:
        pr = jnp.where(mk, pr, 0.0)
    l_ref[0] = alpha * l_ref[0] + jnp.sum(pr, axis=0, keepdims=True)
    acc_ref[0] = alpha * acc_ref[0] + pv_of(pr.astype(BF16))
    m_ref[0] = m_new


def _pv_key_major(v_rows):
    return lambda pr: lax.dot_general(v_rows, pr, (((0,), (0,)), ((), ())), preferred_element_type=F32)


def _nsa_dec_sel_fm_kernel(pt_ref, qbd_ref, qbdt_ref, *refs, n_steps, per_seq, pps):
    k_refs = refs[:pps]
    v_refs = refs[pps:2 * pps]
    kn_ref, vn_ref, sel_ref, g_ref, bias_ref, o_ref, qz_ref, qzt_ref, m_ref, l_ref, acc_ref = refs[2 * pps:]
    p = pl.program_id(1)

    @pl.when(p == 0)
    def _():
        qz_ref[...] = (qbd_ref[...].astype(F32) * HEAD_DIM ** -0.5).astype(BF16)
        qzt_ref[...] = (qbdt_ref[...].astype(F32) * HEAD_DIM ** -0.5).astype(BF16)
        _reset_softmax(m_ref, l_ref, acc_ref)

    srow = lax.broadcasted_iota(jnp.int32, (PAGE_SIZE, SEL_PAD), 0)
    jcol = lax.broadcasted_iota(jnp.int32, (PAGE_SIZE, SEL_PAD), 1)
    blocks_per_page = PAGE_SIZE // SEL_LEN

    def member(first_block):
        expand = jnp.where(jcol == first_block + jnp.right_shift(srow, 6), 1.0, 0.0).astype(BF16)
        return _dot(expand, sel_ref[...]) > 0.5

    @pl.when(p < n_steps)
    def _():
        s_list, mk_list = [], []
        for i in range(pps):
            page_no = p * pps + i
            bias = jnp.where(page_no == n_steps * pps - 1, bias_ref[0], bias_ref[3])
            s_list.append(_dot(qzt_ref[...], k_refs[i][...].astype(BF16)).T + bias)
            mk_list.append(member(page_no * blocks_per_page))
        _joint_update(s_list, mk_list, [r[...] for r in v_refs], m_ref, l_ref, acc_ref)

    @pl.when(p == n_steps)
    def _():
        pad = PAGE_SIZE - per_seq
        k_rows = jnp.concatenate([kn_ref[...], jnp.zeros((pad, GROUP_COLS), F32)], axis=0).astype(BF16)
        v_rows = jnp.concatenate([vn_ref[...], jnp.zeros((pad, GROUP_COLS), F32)], axis=0).astype(BF16)
        krow = lax.broadcasted_iota(jnp.int32, (PAGE_SIZE, HT_LANES), 0)
        tq = _mod_pow2(lax.broadcasted_iota(jnp.int32, (PAGE_SIZE, HT_LANES), 1), per_seq)
        mk = member(n_steps * pps * blocks_per_page) & (krow <= tq)
        s_t = _dot(k_rows, qz_ref[...]) + bias_ref[1]
        _dec_update(s_t, mk, _pv_key_major(v_rows), m_ref, l_ref, acc_ref)
        o_ref[...] = _dec_finish(m_ref, l_ref, acc_ref, g_ref[1], per_seq)


def nsa_dec_sel_fm(page_table, q_bd, pool_t, layer, kv_new, sel, gate_rows, bias, row_block0, per_seq):
    n_seq, n_pages = page_table.shape
    pps = PAGES_PER_STEP
    assert n_pages % pps == 0
    n_steps = n_pages // pps
    kern = functools.partial(_nsa_dec_sel_fm_kernel, n_steps=n_steps, per_seq=per_seq, pps=pps)

    def page_spec(blk, i):
        return pl.BlockSpec((None, None, GROUP_COLS, PAGE_SIZE),
                            lambda b, p, pt, i=i: (layer, pt[b, jnp.minimum(p, n_steps - 1) * pps + i], blk, 0))

    grid_spec = pltpu.PrefetchScalarGridSpec(
        num_scalar_prefetch=1,
        grid=(n_seq, n_steps + 1),
        in_specs=[pl.BlockSpec((None, GROUP_COLS, HT_LANES), lambda b, p, pt: (b, 0, 0)),
                  pl.BlockSpec((None, HT_LANES, GROUP_COLS), lambda b, p, pt: (b, 0, 0))]
        + [page_spec(2, i) for i in range(pps)] + [page_spec(3, i) for i in range(pps)]
        + [
            pl.BlockSpec((per_seq, GROUP_COLS), lambda b, p, pt: (row_block0 + b, 2)),
            pl.BlockSpec((per_seq, GROUP_COLS), lambda b, p, pt: (row_block0 + b, 3)),
            pl.BlockSpec((None, SEL_PAD, HT_LANES), lambda b, p, pt: (b, 0, 0)),
            pl.BlockSpec((None, 3, 1, HT_LANES), lambda b, p, pt: (b, 0, 0, 0)),
            pl.BlockSpec((4, PAGE_SIZE, HT_LANES), lambda b, p, pt: (0, 0, 0)),
        ],
        out_specs=pl.BlockSpec((per_seq, D_MODEL), lambda b, p, pt: (b, 0)),
        scratch_shapes=[
            pltpu.VMEM((GROUP_COLS, HT_LANES), BF16),
            pltpu.VMEM((HT_LANES, GROUP_COLS), BF16),
            pltpu.VMEM((1, 1, HT_LANES), F32),
            pltpu.VMEM((1, 1, HT_LANES), F32),
            pltpu.VMEM((1, GROUP_COLS, HT_LANES), F32),
        ],
    )
    return pl.pallas_call(
        kern,
        grid_spec=grid_spec,
        out_shape=jax.ShapeDtypeStruct((n_seq * per_seq, D_MODEL), F32),
        compiler_params=_cparams("parallel", "arbitrary"),
        name="nsa_dec_sel",
    )(page_table, q_bd, q_bd.transpose(0, 2, 1), *([pool_t] * (2 * pps)), kv_new, kv_new, sel, gate_rows, bias)


def _nsa_dec_win_fm_kernel(qbd_ref, qbdt_ref, wt_ref, wn_ref, g_ref, bias_ref, o_ref, ws_ref, m_ref, l_ref, acc_ref, *,
                           per_seq):
    n_past = wt_ref.shape[1]
    n_tiles = n_past // PAGE_SIZE
    qz = (qbd_ref[...].astype(F32) * HEAD_DIM ** -0.5).astype(BF16)
    qzt = (qbdt_ref[...].astype(F32) * HEAD_DIM ** -0.5).astype(BF16)
    _reset_softmax(m_ref, l_ref, acc_ref)
    krow = lax.broadcasted_iota(jnp.int32, (PAGE_SIZE, HT_LANES), 0)
    tq = _mod_pow2(lax.broadcasted_iota(jnp.int32, (PAGE_SIZE, HT_LANES), 1), per_seq)
    s_qm = _dot(qzt, wt_ref[:GROUP_COLS, :].astype(BF16))
    for w in range(n_tiles):
        keys = slice(w * PAGE_SIZE, (w + 1) * PAGE_SIZE)
        mk = (krow > tq) if w == 0 else None
        bias = bias_ref[0] if w == n_tiles - 1 else bias_ref[3]
        v_t = wt_ref[GROUP_COLS:, keys].astype(BF16)
        _dec_update(s_qm[:, keys].T + bias, mk, lambda pr, v_t=v_t: _dot(v_t, pr), m_ref, l_ref, acc_ref)
    pad = PAGE_SIZE - per_seq
    new = jnp.concatenate([wn_ref[...], jnp.zeros((pad, 2 * GROUP_COLS), F32)], axis=0)
    s_t = _dot(new[:, :GROUP_COLS].astype(BF16), qz) + bias_ref[1]
    _dec_update(s_t, krow <= tq, _pv_key_major(new[:, GROUP_COLS:].astype(BF16)), m_ref, l_ref, acc_ref)
    o_ref[...] = _dec_finish(m_ref, l_ref, acc_ref, g_ref[2], per_seq)
    shifted = pltpu.roll(wt_ref[...], n_past - per_seq, 1)
    ws_ref[...] = shifted
    lane = lax.broadcasted_iota(jnp.int32, (2 * GROUP_COLS, PAGE_SIZE), 1)
    new_t = pltpu.roll(new.T, PAGE_SIZE - per_seq, 1)
    ws_ref[:, n_past - PAGE_SIZE:] = jnp.where(lane >= PAGE_SIZE - per_seq, new_t, shifted[:, n_past - PAGE_SIZE:])


def nsa_dec_win_fm(q_bd, win_t, layer, win_new, gate_rows, bias, row_block0, per_seq):
    _, n_seq, n_feat, n_past = win_t.shape
    assert n_past == WINDOW and n_past % PAGE_SIZE == 0 and PAGE_SIZE + 1 >= T5_SATURATION_DIST
    kern = functools.partial(_nsa_dec_win_fm_kernel, per_seq=per_seq)
    return pl.pallas_call(
        kern,
        grid=(n_seq,),
        in_specs=[
            pl.BlockSpec((None, GROUP_COLS, HT_LANES), lambda b: (b, 0, 0)),
            pl.BlockSpec((None, HT_LANES, GROUP_COLS), lambda b: (b, 0, 0)),
            pl.BlockSpec((None, None, n_feat, n_past), lambda b: (layer, b, 0, 0)),
            pl.BlockSpec((per_seq, n_feat), lambda b: (row_block0 + b, 0)),
            pl.BlockSpec((None, 3, 1, HT_LANES), lambda b: (b, 0, 0, 0)),
            pl.BlockSpec((4, PAGE_SIZE, HT_LANES), lambda b: (0, 0, 0)),
        ],
        out_specs=[
            pl.BlockSpec((per_seq, D_MODEL), lambda b: (b, 0)),
            pl.BlockSpec((None, n_feat, n_past), lambda b: (b, 0, 0)),
        ],
        out_shape=[
            jax.ShapeDtypeStruct((n_seq * per_seq, D_MODEL), F32),
            jax.ShapeDtypeStruct((n_seq, n_feat, n_past), F32),
        ],
        scratch_shapes=[
            pltpu.VMEM((1, 1, HT_LANES), F32),
            pltpu.VMEM((1, 1, HT_LANES), F32),
            pltpu.VMEM((1, GROUP_COLS, HT_LANES), F32),
        ],
        compiler_params=_cparams("parallel"),
        name="nsa_dec_win",
    )(q_bd, q_bd.transpose(0, 2, 1), win_t, win_new, gate_rows, bias)


def nsa_decode_fm(table, page_table, pool_t, win_t, layer, q_rows, kv_new, win_new, gates_rows, cw, row_block0, pos0,
                  per_seq):
    n_seq, n_pages = page_table.shape
    pe_t, w1_bd, w2_bd = cw
    past = n_pages * PAGE_SIZE
    s_total = past + per_seq
    n_cmp = (s_total - CMP_LEN) // CMP_STRIDE + 1
    n_sel = -(-s_total // SEL_LEN)
    assert n_cmp == past // CMP_STRIDE - 1 and n_sel == past // SEL_LEN + 1 and per_seq <= CMP_STRIDE
    assert pos0 == past and pos0 % SEL_LEN == 0 and pos0 >= WINDOW and past // CMP_STRIDE == PAGE_SIZE
    bias = nsa_dec_bias(table, pos0, per_seq)
    ctok = nsa_dec_compress(nsa_dec_gather_fm(page_table, pool_t, layer), pe_t, w1_bd, w2_bd).astype(BF16)
    rows_per_seq = past // CMP_STRIDE
    k_cmp = ctok[0].reshape(n_seq, rows_per_seq, GROUP_COLS)
    v_cmp = ctok[1].reshape(n_seq, rows_per_seq, GROUP_COLS)
    q_bd = group_block_diag_queries(q_rows, n_seq, per_seq)
    gate_rows = gates_rows.reshape(n_seq, per_seq, 3, N_HEADS).transpose(0, 2, 3, 1).reshape(n_seq, 3, 1, HT_LANES)
    o_cmp, sel = nsa_dec_cmp(q_bd, k_cmp, v_cmp, gate_rows, bias, n_cmp, n_sel, pos0, per_seq)
    o_slc = nsa_dec_sel_fm(page_table, q_bd, pool_t, layer, kv_new, sel, gate_rows, bias, row_block0, per_seq)
    o_win, win_state_t = nsa_dec_win_fm(q_bd, win_t, layer, win_new, gate_rows, bias, row_block0, per_seq)
    return o_cmp, o_slc, o_win, win_state_t


GLA_SUB = 16
GLA_BLOCK = 512


def _gla_kernel(q_ref, k_ref, v_ref, r_ref, a_ref, wa_ref, ba_ref, ng_ref, s0_ref, o_ref, sfin_ref, s_ref, *,
                n_chunks, chunk):
    t_blk = pl.program_id(2)

    @pl.when(t_blk == 0)
    def _():
        s_ref[...] = s0_ref[...]

    tri = _tril_bf16(chunk)
    it = lax.broadcasted_iota(jnp.int32, (chunk, chunk), 0)
    js = lax.broadcasted_iota(jnp.int32, (chunk, chunk), 1)
    same_sub = jnp.right_shift(it, 4) == jnp.right_shift(js, 4)
    n_sub = chunk // GLA_SUB

    def chunk_step(c, carry):
        r0 = pl.multiple_of(c * chunk, chunk)
        rows = pl.ds(r0, chunk)
        q = q_ref[rows, :] * GLA_DK ** -0.5
        k = k_ref[rows, :]
        v = v_ref[rows, :].astype(BF16)
        g = _log_sigmoid(_dot(a_ref[rows, :].astype(BF16), wa_ref[...]) + ba_ref[...]) / GLA_TAU
        b = _cumsum_rows(g, tri)
        s_old = s_ref[...]
        o = _dot((q * jnp.exp(b)).astype(BF16), s_old.astype(BF16))
        a_mat = jnp.zeros((chunk, chunk), F32)
        for d in range(GLA_SUB):
            if d == 0:
                val = jnp.sum(q * k, axis=-1, keepdims=True)
            else:
                k_sh = pltpu.roll(k, d, 0)
                b_sh = pltpu.roll(b, d, 0)
                val = jnp.sum(q * k_sh * jnp.exp(jnp.minimum(b - b_sh, 0.0)), axis=-1, keepdims=True)
            a_mat = a_mat + jnp.where((it - js == d) & same_sub, val, 0.0)
        if n_sub > 1:
            blocks = [jnp.zeros((GLA_SUB, chunk), F32)]
            for a in range(1, n_sub):
                ref_b = b[a * GLA_SUB - 1:a * GLA_SUB, :]
                qa = q[a * GLA_SUB:(a + 1) * GLA_SUB] * jnp.exp(b[a * GLA_SUB:(a + 1) * GLA_SUB] - ref_b)
                kk = k * jnp.exp(jnp.minimum(ref_b - b, 0.0))
                blocks.append(_dot_nt(qa.astype(BF16), kk.astype(BF16)))
            off = jnp.concatenate(blocks, axis=0)
            a_mat = a_mat + jnp.where(jnp.right_shift(js, 4) < jnp.right_shift(it, 4), off, 0.0)
        o = o + _dot(a_mat.astype(BF16), v)
        b_last = b[chunk - 1:chunk, :]
        kd = k * jnp.exp(b_last - b)
        pad_k = jnp.zeros((GLA_DK - chunk, GLA_DK), F32)
        pad_v = jnp.zeros((GLA_DK - chunk, GLA_DV), BF16)
        kd_t = jnp.concatenate([kd, pad_k], axis=0).T
        dec = jnp.broadcast_to(jnp.exp(b_last), (GLA_DK, GLA_DK)).T
        s_ref[...] = (s_old * jnp.concatenate([dec] * (GLA_DV // GLA_DK), axis=1)
                      + _dot(kd_t.astype(BF16), jnp.concatenate([v, pad_v], axis=0)))
        y = o * lax.rsqrt(jnp.mean(o * o, axis=-1, keepdims=True) + EPS) * ng_ref[...]
        rr = r_ref[rows, :]
        o_ref[rows, :] = y * (rr * _sigmoid(rr))
        return carry

    lax.fori_loop(0, n_chunks, chunk_step, 0)

    @pl.when(t_blk == pl.num_programs(2) - 1)
    def _():
        sfin_ref[...] = s_ref[...]


def gla_prompt(q, k, v, r, a_lr, w_a2, b_a, norm_g, s0, n_batch, seq, row_block0=0):
    chunk = GLA_CHUNK if seq % GLA_CHUNK == 0 else seq
    blk = min(GLA_BLOCK, seq)
    nt = seq // blk
    m = n_batch * seq
    kern = functools.partial(_gla_kernel, n_chunks=blk // chunk, chunk=chunk)
    row = lambda b, h, t: row_block0 + b * nt + t
    return pl.pallas_call(
        kern,
        grid=(n_batch, GLA_HEADS, nt),
        in_specs=[
            pl.BlockSpec((blk, GLA_DK), lambda b, h, t: (row(b, h, t), h)),
            pl.BlockSpec((blk, GLA_DK), lambda b, h, t: (row(b, h, t), h)),
            pl.BlockSpec((blk, GLA_DV), lambda b, h, t: (row(b, h, t), h)),
            pl.BlockSpec((blk, GLA_DV), lambda b, h, t: (row(b, h, t), h)),
            pl.BlockSpec((blk, GLA_RANK), lambda b, h, t: (row(b, h, t), 0)),
            pl.BlockSpec((GLA_RANK, GLA_DK), lambda b, h, t: (0, h)),
            pl.BlockSpec((1, GLA_DK), lambda b, h, t: (0, h)),
            pl.BlockSpec((1, GLA_DV), lambda b, h, t: (0, 0)),
            pl.BlockSpec((None, None, GLA_DK, GLA_DV), lambda b, h, t: (b, h, 0, 0)),
        ],
        out_specs=[
            pl.BlockSpec((blk, GLA_DV), lambda b, h, t: (b * nt + t, h)),
            pl.BlockSpec((None, None, GLA_DK, GLA_DV), lambda b, h, t: (b, h, 0, 0)),
        ],
        out_shape=[
            jax.ShapeDtypeStruct((m, D_MODEL), F32),
            jax.ShapeDtypeStruct((n_batch, GLA_HEADS, GLA_DK, GLA_DV), F32),
        ],
        scratch_shapes=[pltpu.VMEM((GLA_DK, GLA_DV), F32)],
        compiler_params=_cparams("parallel", "parallel", "arbitrary"),
        name="gla",
    )(q, k, v, r, a_lr, w_a2.astype(BF16), b_a.reshape(1, GLA_QK), norm_g.reshape(1, GLA_DV), s0)


def _gla_decode_kernel(q_ref, k_ref, v_ref, r_ref, a_ref, wa_ref, ba_ref, ng_ref, s0_ref, o_ref, s_ref, *, n_rows):
    pad_rows = 2 * n_rows

    def pad_to(x, rows):
        return jnp.concatenate([x, jnp.zeros((rows - x.shape[0], x.shape[1]), x.dtype)], axis=0)

    row = lax.broadcasted_iota(jnp.int32, (n_rows, GLA_DK), 0)
    row_v = lax.broadcasted_iota(jnp.int32, (n_rows, GLA_DV), 0)
    q = q_ref[...] * GLA_DK ** -0.5
    k = k_ref[...]
    v = v_ref[...]
    g = _dot(pad_to(a_ref[...], pad_rows).astype(BF16), wa_ref[...])[:n_rows]
    g = _log_sigmoid(g + ba_ref[...]) / GLA_TAU
    b = g
    shift = 1
    while shift < n_rows:
        b = b + jnp.where(row >= shift, pltpu.roll(b, shift, 0), 0.0)
        shift *= 2
    s_old = s0_ref[...]
    o = _dot(pad_to(q * jnp.exp(b), pad_rows).astype(BF16), s_old.astype(BF16))[:n_rows]
    for d in range(n_rows):
        if d == 0:
            val = jnp.sum(q * k, axis=-1, keepdims=True)
            o = o + val * v
        else:
            k_sh = pltpu.roll(k, d, 0)
            b_sh = pltpu.roll(b, d, 0)
            val = jnp.sum(q * k_sh * jnp.exp(jnp.minimum(b - b_sh, 0.0)), axis=-1, keepdims=True)
            o = o + jnp.where(row_v >= d, val * pltpu.roll(v, d, 0), 0.0)
    b_last = b[n_rows - 1:n_rows, :]
    kd_t = pad_to(k * jnp.exp(b_last - b), GLA_DK).T
    dec = jnp.broadcast_to(jnp.exp(b_last), (GLA_DK, GLA_DK)).T
    s_ref[...] = (s_old * jnp.concatenate([dec] * (GLA_DV // GLA_DK), axis=1)
                  + _dot(kd_t.astype(BF16), pad_to(v, GLA_DK).astype(BF16)))
    y = o * lax.rsqrt(jnp.mean(o * o, axis=-1, keepdims=True) + EPS) * ng_ref[...]
    rr = r_ref[...]
    o_ref[...] = y * (rr * _sigmoid(rr))


def gla_decode(q, k, v, r, a_lr, w_a2, b_a, norm_g, s0, n_seq, per_seq, row_block0):
    assert per_seq % GLA_CHUNK != 0 and per_seq % 8 == 0
    kern = functools.partial(_gla_decode_kernel, n_rows=per_seq)
    row = lambda b, h: (row_block0 + b, h)
    return pl.pallas_call(
        kern,
        grid=(n_seq, GLA_HEADS),
        in_specs=[
            pl.BlockSpec((per_seq, GLA_DK), row),
            pl.BlockSpec((per_seq, GLA_DK), row),
            pl.BlockSpec((per_seq, GLA_DV), row),
            pl.BlockSpec((per_seq, GLA_DV), row),
            pl.BlockSpec((per_seq, GLA_RANK), lambda b, h: (row_block0 + b, 0)),
            pl.BlockSpec((GLA_RANK, GLA_DK), lambda b, h: (0, h)),
            pl.BlockSpec((1, GLA_DK), lambda b, h: (0, h)),
            pl.BlockSpec((1, GLA_DV), lambda b, h: (0, 0)),
            pl.BlockSpec((None, None, GLA_DK, GLA_DV), lambda b, h: (b, h, 0, 0)),
        ],
        out_specs=[
            pl.BlockSpec((per_seq, GLA_DV), lambda b, h: (b, h)),
            pl.BlockSpec((None, None, GLA_DK, GLA_DV), lambda b, h: (b, h, 0, 0)),
        ],
        out_shape=[
            jax.ShapeDtypeStruct((n_seq * per_seq, D_MODEL), F32),
            jax.ShapeDtypeStruct((n_seq, GLA_HEADS, GLA_DK, GLA_DV), F32),
        ],
        compiler_params=_cparams("parallel", "parallel"),
        name="gla_decode",
    )(q, k, v, r, a_lr, w_a2.astype(BF16), b_a.reshape(1, GLA_QK), norm_g.reshape(1, GLA_DV), s0)


_C1 = D_MODEL + NSA_KV_COLS
_C2 = _C1 + NSA_WIN_COLS
NSA_SPLITS = ((D_MODEL, _C1, F32), (D_MODEL, _C1, BF16), (_C1, _C2, F32), (_C1, _C2, BF16))
NSA_SPLITS_T = ((0, D_MODEL, BF16), (_C1 - GROUP_COLS, _C1, BF16), (_C2 - GROUP_COLS, _C2, BF16),
                (_C2, _C2 + 3 * N_HEADS, F32))
FOX_SPLITS = ((D_MODEL, 3 * D_MODEL, F32), (D_MODEL, 2 * D_MODEL, BF16), (3 * D_MODEL, 3 * D_MODEL + N_HEADS, F32))
FOX_SPLITS_T = ((0, D_MODEL, BF16), (2 * D_MODEL, 3 * D_MODEL, BF16))
GLA_SPLITS = ((0, GLA_QK, F32), (GLA_QK, 2 * GLA_QK, F32), (2 * GLA_QK, 2 * GLA_QK + D_MODEL, F32),
              (2 * GLA_QK + D_MODEL, 2 * GLA_QK + 2 * D_MODEL, F32),
              (2 * GLA_QK + 2 * D_MODEL, 2 * GLA_QK + 2 * D_MODEL + GLA_RANK, F32))


def kernel(x_prompt, x_sample, cache_nsa_kv, state_nsa_win, cache_fox_kv, cache_fox_logf, state_gla, page_table, rel_bias, norm_g, final_g, nsa_w_in, nsa_phi_pe, nsa_phi_w1, nsa_phi_w2, nsa_w_out, fox_w_in, fox_b_f, fox_w_out, gla_w_in, gla_w_a2, gla_b_a, gla_norm_g, gla_w_out, mlp_w1, mlp_w2):
    bp, tp, _ = x_prompt.shape
    bs, ts, _ = x_sample.shape
    mp, ms = bp * tp, bs * ts
    pos_s = page_table.shape[1] * PAGE_SIZE
    x = jnp.concatenate([x_prompt.reshape(mp, D_MODEL), x_sample.reshape(ms, D_MODEL)], axis=0)
    toep = t5_tiles(rel_bias, ATT_TILE)
    nsa_pool_t = feature_major_pages(cache_nsa_kv)
    nsa_win_t = feature_major_pages(state_nsa_win)
    fox_pool_t = feature_major_pages(cache_fox_kv)
    fox_lf_t = feature_major_pages(cache_fox_logf)

    def both(a, shape_p, shape_s):
        return a[:mp].reshape(shape_p), a[mp:].reshape(shape_s)

    nsa_kv_p, nsa_kv_s, nsa_win_p, nsa_win_s = [], [], [], []
    fox_kv_p, fox_kv_s, fox_lf_p, fox_lf_s, gla_p, gla_s = [], [], [], [], [], []
    G = NSA_KV_GROUPS
    for i in range(DEPTH):
        kind, li = i % N_MIXERS, i // N_MIXERS
        if kind == 0:
            kvn, kv_bf, winn, win_bf, q_t, vslc_t, vwin_t, gates_t = norm_proj(
                x, norm_g[i, 0], nsa_w_in[li], NSA_SPLITS, NSA_SPLITS_T)
            cw = compress_weights(nsa_phi_pe[li], nsa_phi_w1[li], nsa_phi_w2[li])
            o_parts_p = nsa_prompt_t(rel_bias, toep, q_t, kvn, kv_bf, win_bf, vslc_t, vwin_t, gates_t, cw, bp, tp)
            kv_p, kv_s = both(kvn, (bp, tp, 4, G, HEAD_DIM), (bs, ts, 4, G, HEAD_DIM))
            w_p, w_s = both(winn, (bp, tp, 2, G, HEAD_DIM), (bs, ts, 2, G, HEAD_DIM))
            *o_parts_s, ws_t = nsa_decode_fm(rel_bias, page_table, nsa_pool_t, nsa_win_t, li, q_t[:, mp:].T, kvn, winn,
                                             gates_t[:, mp:].T, cw, mp // ts, pos_s, ts)
            ws_p = w_p[:, tp - min(WINDOW, tp):]
            ws_s = ws_t.reshape(bs, 2, G, HEAD_DIM, -1).transpose(0, 4, 1, 2, 3)
            nsa_kv_p.append(kv_p); nsa_kv_s.append(kv_s); nsa_win_p.append(ws_p); nsa_win_s.append(ws_s)
            o_parts_p = list(o_parts_p)
            w_out = nsa_w_out[li]
        elif kind == 1:
            kvn, k_bf, fr, q_t, v_t = norm_proj(x, norm_g[i, 0], fox_w_in[li], FOX_SPLITS, FOX_SPLITS_T)
            kv_p, kv_s = both(kvn, (bp, tp, 2, N_HEADS, HEAD_DIM), (bs, ts, 2, N_HEADS, HEAD_DIM))
            lf_all, d_all = fox_prep_prompt(fr, fox_b_f[li], bp, tp)
            o_p = fox_attn_t(q_t, k_bf, v_t, d_all, bp, tp)
            lf_p = lf_all.reshape(bp, tp, N_HEADS)
            q_bd = block_diag_queries(q_t[:, mp:].T, bs, ts, N_HEADS)
            o_s, lf_s = fox_decode_fm(page_table, q_bd, fox_pool_t, fox_lf_t, li, kvn, fr, fox_b_f[li], mp // ts,
                                      bs, ts)
            lf_s = lf_s.reshape(bs, ts, N_HEADS)
            fox_kv_p.append(kv_p); fox_kv_s.append(kv_s); fox_lf_p.append(lf_p); fox_lf_s.append(lf_s)
            o_parts_p, o_parts_s = [o_p], [o_s]
            w_out = fox_w_out[li]
        else:
            q, k, v, r, a = norm_proj(x, norm_g[i, 0], gla_w_in[li], GLA_SPLITS)
            s_zero = jnp.zeros((bp, GLA_HEADS, GLA_DK, GLA_DV), F32)
            o_p, s_p = gla_prompt(q, k, v, r, a, gla_w_a2[li], gla_b_a[li], gla_norm_g[li], s_zero, bp, tp)
            o_s, s_s = gla_decode(q, k, v, r, a, gla_w_a2[li], gla_b_a[li], gla_norm_g[li], state_gla[li], bs, ts,
                                  mp // ts)
            gla_p.append(s_p); gla_s.append(s_s)
            o_parts_p, o_parts_s = [o_p], [o_s]
            w_out = gla_w_out[li]
        x = out_mlp(x, o_parts_p, o_parts_s, w_out.astype(BF16), norm_g[i, 1], mlp_w1[i].astype(BF16),
                    mlp_w2[i].astype(BF16))
    y = final_norm(x, final_g)
    y_p, y_s = both(y, (bp, tp, D_MODEL), (bs, ts, D_MODEL))
    return (y_p, y_s, jnp.stack(nsa_kv_p), jnp.stack(nsa_kv_s), jnp.stack(nsa_win_p), jnp.stack(nsa_win_s),
            jnp.stack(fox_kv_p), jnp.stack(fox_kv_s), jnp.stack(fox_lf_p), jnp.stack(fox_lf_s),
            jnp.stack(gla_p), jnp.stack(gla_s))
```

```python
import functools
import math

import jax
import jax.numpy as jnp
from jax import lax
from jax.experimental import pallas as pl
from jax.experimental.pallas import tpu as pltpu

D_MODEL = 1024
DEPTH = 4
PAGE_SIZE = 128
N_MIXERS = 3
N_HEADS = 16
HEAD_DIM = D_MODEL // N_HEADS
NSA_KV_GROUPS = 4
NSA_HPG = N_HEADS // NSA_KV_GROUPS
CMP_LEN = 32
CMP_STRIDE = 16
CMP_HIDDEN = 2 * HEAD_DIM
SEL_LEN = 64
N_SELECT = 16
WINDOW = 512
N_BUCKETS = 32
MAX_DISTANCE = 128
GLA_HEADS = 4
GLA_DK = D_MODEL // 2 // GLA_HEADS
GLA_DV = D_MODEL // GLA_HEADS
GLA_QK = GLA_HEADS * GLA_DK
GLA_RANK = 16
GLA_TAU = 16.0
GLA_CHUNK = 64
D_FF = 4 * D_MODEL
EPS = 1e-6
NEG = -1e30
BIG = 1e9
TAKEN = -3e38
NSA_KV_COLS = 4 * NSA_KV_GROUPS * HEAD_DIM
NSA_WIN_COLS = 2 * NSA_KV_GROUPS * HEAD_DIM
GROUP_COLS = NSA_KV_GROUPS * HEAD_DIM

F32 = jnp.float32
BF16 = jnp.bfloat16

V7X_VMEM_LIMIT_BYTES = 48 * 1024 * 1024
V7X_LANES = 128
TOKEN_TILE = 512
FF_TILE = 1024
ATT_TILE = 256
GROUPS_PER_PASS = 4
SEL_PAD = V7X_LANES
T5_SATURATION_DIST = 113


def _cparams(*sem):
    return pltpu.CompilerParams(dimension_semantics=sem, vmem_limit_bytes=V7X_VMEM_LIMIT_BYTES)


def _rms(x, g):
    return x * lax.rsqrt(jnp.mean(x * x, axis=-1, keepdims=True) + EPS) * g


def _sigmoid(x):
    return 1.0 / (1.0 + jnp.exp(-x))


def _log_sigmoid(x):
    return jnp.minimum(x, 0.0) - jnp.log(1.0 + jnp.exp(-jnp.abs(x)))


def _gelu_tanh(x):
    return 0.5 * x * (1.0 + jnp.tanh(math.sqrt(2.0 / math.pi) * (x + 0.044715 * (x * x * x))))


def _dot(a, b):
    return jnp.dot(a, b, preferred_element_type=F32)


def _dot_nt(a, b):
    return lax.dot_general(a, b, (((1,), (1,)), ((), ())), preferred_element_type=F32)


def _split_bf16(x, parts):
    out = []
    for _ in range(parts):
        hi = x.astype(BF16)
        out.append(hi)
        x = x - hi.astype(F32)
    return out


def _t5_bucket(dist):
    n = jnp.maximum(dist, 0)
    max_exact = N_BUCKETS // 2
    nf = jnp.maximum(n, 1).astype(F32)
    log_part = max_exact + (jnp.log(nf / max_exact) / math.log(MAX_DISTANCE / max_exact)
                            * (N_BUCKETS - max_exact)).astype(jnp.int32)
    return jnp.where(n < max_exact, n, jnp.minimum(log_part, N_BUCKETS - 1))


def _bias_from_bucket(bucket, tbl_ref, h):
    val = jnp.full(bucket.shape, tbl_ref[0, h], F32)
    for k in range(1, N_BUCKETS):
        val = jnp.where(bucket == k, tbl_ref[k, h], val)
    return val


def _div_pow2(x, d):
    assert d & (d - 1) == 0
    return jnp.right_shift(x, d.bit_length() - 1)


def _mod_pow2(x, d):
    assert d & (d - 1) == 0
    return jnp.bitwise_and(x, d - 1)


def _smem_spec():
    return pl.BlockSpec(memory_space=pltpu.SMEM)


def _norm_proj_kernel(x_ref, g_ref, w_ref, wt_ref, *out_refs, splits, splits_t):
    h = _rms(x_ref[...], g_ref[...]).astype(BF16)
    for (lo, hi, _), o_ref in zip(splits, out_refs):
        o_ref[...] = _dot(h, w_ref[:, lo:hi]).astype(o_ref.dtype)
    for (lo, hi, _), o_ref in zip(splits_t, out_refs[len(splits):]):
        o_ref[...] = _dot_nt(wt_ref[lo:hi, :], h).astype(o_ref.dtype)


def norm_proj(x, g, w, splits, splits_t=()):
    m = x.shape[0]
    n = w.shape[1]
    w_bf = w.astype(BF16)
    kern = functools.partial(_norm_proj_kernel, splits=splits, splits_t=splits_t)
    return pl.pallas_call(
        kern,
        grid=(m // TOKEN_TILE,),
        in_specs=[
            pl.BlockSpec((TOKEN_TILE, D_MODEL), lambda i: (i, 0)),
            pl.BlockSpec((1, D_MODEL), lambda i: (0, 0)),
            pl.BlockSpec((D_MODEL, n), lambda i: (0, 0)),
            pl.BlockSpec((n, D_MODEL), lambda i: (0, 0)),
        ],
        out_specs=[pl.BlockSpec((TOKEN_TILE, hi - lo), lambda i: (i, 0)) for lo, hi, _ in splits]
        + [pl.BlockSpec((hi - lo, TOKEN_TILE), lambda i: (0, i)) for lo, hi, _ in splits_t],
        out_shape=[jax.ShapeDtypeStruct((m, hi - lo), dt) for lo, hi, dt in splits]
        + [jax.ShapeDtypeStruct((hi - lo, m), dt) for lo, hi, dt in splits_t],
        compiler_params=_cparams("parallel"),
        name="norm_proj",
    )(x, g.reshape(1, D_MODEL), w_bf, w_bf.T)


def _out_mlp_kernel(x_ref, *refs, n_o, n_first):
    oa_refs = refs[:n_o]
    ob_refs = refs[n_o:2 * n_o]
    wo_ref, g_ref, w1_ref, w2_ref, y_ref, x1_ref, h_ref, acc_ref = refs[2 * n_o:]
    i = pl.program_id(0)
    j = pl.program_id(1)

    @pl.when(j == 0)
    def _():
        first = i < n_first
        o = jnp.where(first, oa_refs[0][...], ob_refs[0][...])
        for ra, rb in zip(oa_refs[1:], ob_refs[1:]):
            o = o + jnp.where(first, ra[...], rb[...])
        x1 = x_ref[...] + _dot(o.astype(BF16), wo_ref[...])
        x1_ref[...] = x1
        h_ref[...] = _rms(x1, g_ref[...]).astype(BF16)
        acc_ref[...] = jnp.zeros_like(acc_ref)

    a = _dot(h_ref[...], w1_ref[...])
    a = jnp.square(jnp.maximum(a, 0.0)).astype(BF16)
    acc_ref[...] += _dot(a, w2_ref[...])

    @pl.when(j == pl.num_programs(1) - 1)
    def _():
        y_ref[...] = x1_ref[...] + acc_ref[...]


def out_mlp(x, o_first, o_rest, w_out, g, w1, w2):
    m = x.shape[0]
    n_o = len(o_first)
    n_first = o_first[0].shape[0] // TOKEN_TILE
    n_rest = o_rest[0].shape[0] // TOKEN_TILE
    assert o_first[0].shape[0] % TOKEN_TILE == 0 and n_first + n_rest == m // TOKEN_TILE
    kern = functools.partial(_out_mlp_kernel, n_o=n_o, n_first=n_first)
    tok = pl.BlockSpec((TOKEN_TILE, D_MODEL), lambda i, j: (i, 0))
    tok_a = pl.BlockSpec((TOKEN_TILE, D_MODEL), lambda i, j: (jnp.minimum(i, n_first - 1), 0))
    tok_b = pl.BlockSpec((TOKEN_TILE, D_MODEL), lambda i, j: (jnp.maximum(i - n_first, 0), 0))
    return pl.pallas_call(
        kern,
        grid=(m // TOKEN_TILE, D_FF // FF_TILE),
        in_specs=[tok] + [tok_a] * n_o + [tok_b] * n_o + [
            pl.BlockSpec((D_MODEL, D_MODEL), lambda i, j: (0, 0)),
            pl.BlockSpec((1, D_MODEL), lambda i, j: (0, 0)),
            pl.BlockSpec((D_MODEL, FF_TILE), lambda i, j: (0, j)),
            pl.BlockSpec((FF_TILE, D_MODEL), lambda i, j: (j, 0)),
        ],
        out_specs=tok,
        out_shape=jax.ShapeDtypeStruct((m, D_MODEL), F32),
        scratch_shapes=[
            pltpu.VMEM((TOKEN_TILE, D_MODEL), F32),
            pltpu.VMEM((TOKEN_TILE, D_MODEL), BF16),
            pltpu.VMEM((TOKEN_TILE, D_MODEL), F32),
        ],
        compiler_params=_cparams("parallel", "arbitrary"),
        name="out_mlp",
    )(x, *o_first, *o_rest, w_out, g.reshape(1, D_MODEL), w1, w2)


def _final_norm_kernel(x_ref, g_ref, y_ref):
    y_ref[...] = _rms(x_ref[...], g_ref[...])


def final_norm(x, g):
    m = x.shape[0]
    return pl.pallas_call(
        _final_norm_kernel,
        grid=(m // TOKEN_TILE,),
        in_specs=[pl.BlockSpec((TOKEN_TILE, D_MODEL), lambda i: (i, 0)),
                  pl.BlockSpec((1, D_MODEL), lambda i: (0, 0))],
        out_specs=pl.BlockSpec((TOKEN_TILE, D_MODEL), lambda i: (i, 0)),
        out_shape=jax.ShapeDtypeStruct((m, D_MODEL), F32),
        compiler_params=_cparams("parallel"),
        name="final_norm",
    )(x, g.reshape(1, D_MODEL))


def _t5_tiles_kernel(tbl_ref, o_ref, *, tile):
    h = pl.program_id(0)
    ik = lax.broadcasted_iota(jnp.int32, (tile, tile), 0)
    iq = lax.broadcasted_iota(jnp.int32, (tile, tile), 1)
    for dd in range(2):
        o_ref[dd] = _bias_from_bucket(_t5_bucket(dd * tile + iq - ik), tbl_ref, h) - tbl_ref[N_BUCKETS - 1, h]


def t5_tiles(table, tile):
    assert 2 * tile - (tile - 1) >= T5_SATURATION_DIST
    return pl.pallas_call(
        functools.partial(_t5_tiles_kernel, tile=tile),
        grid=(N_HEADS,),
        in_specs=[_smem_spec()],
        out_specs=pl.BlockSpec((None, 2, tile, tile), lambda h: (h, 0, 0, 0)),
        out_shape=jax.ShapeDtypeStruct((N_HEADS, 2, tile, tile), F32),
        compiler_params=_cparams("arbitrary"),
        name="t5_tiles",
    )(table)


def _compress_kernel(x_ref, pe_ref, w1_ref, w2_ref, o_ref, *, rows, l_major=False):
    half = CMP_LEN // 2
    top = jnp.zeros((rows, NSA_KV_GROUPS * CMP_HIDDEN), F32)
    bot = jnp.zeros((rows, NSA_KV_GROUPS * CMP_HIDDEN), F32)
    for l in range(half):
        a = x_ref[l] if l_major else x_ref[:, l, :]
        top = top + _dot((a + pe_ref[l]).astype(BF16), w1_ref[l])
        bot = bot + _dot((a + pe_ref[half + l]).astype(BF16), w1_ref[half + l])
    hid = top + pltpu.roll(bot, rows - 1, 0)
    o_ref[...] = _dot(_gelu_tanh(hid).astype(BF16), w2_ref[...])


def compress_prompt(kvn, n_batch, seq, pe_t, w1_bd, w2_bd):
    rows = seq // CMP_STRIDE
    return pl.pallas_call(
        functools.partial(_compress_kernel, rows=rows),
        grid=(n_batch, 2),
        in_specs=[
            pl.BlockSpec((rows, CMP_STRIDE, GROUP_COLS), lambda b, j: (b, 0, j)),
            pl.BlockSpec((None, CMP_LEN, 1, GROUP_COLS), lambda b, j: (j, 0, 0, 0)),
            pl.BlockSpec((None, CMP_LEN, GROUP_COLS, NSA_KV_GROUPS * CMP_HIDDEN), lambda b, j: (j, 0, 0, 0)),
            pl.BlockSpec((None, NSA_KV_GROUPS * CMP_HIDDEN, GROUP_COLS), lambda b, j: (j, 0, 0)),
        ],
        out_specs=pl.BlockSpec((None, None, rows, GROUP_COLS), lambda b, j: (b, j, 0, 0)),
        out_shape=jax.ShapeDtypeStruct((n_batch, 2, rows, GROUP_COLS), F32),
        compiler_params=_cparams("parallel", "arbitrary"),
        name="nsa_compress",
    )(kvn.reshape(-1, CMP_STRIDE, kvn.shape[-1]), pe_t, w1_bd, w2_bd)


def compress_weights(phi_pe, phi_w1, phi_w2):
    eye = jnp.eye(NSA_KV_GROUPS, dtype=F32)
    w1 = phi_w1.reshape(2, CMP_LEN, HEAD_DIM, CMP_HIDDEN)
    w1_bd = jnp.einsum('gh,jldc->jlgdhc', eye, w1).reshape(2, CMP_LEN, GROUP_COLS, NSA_KV_GROUPS * CMP_HIDDEN)
    w2_bd = jnp.einsum('gh,jcd->jgchd', eye, phi_w2).reshape(2, NSA_KV_GROUPS * CMP_HIDDEN, GROUP_COLS)
    pe_t = jnp.tile(phi_pe, (1, 1, NSA_KV_GROUPS)).reshape(2, CMP_LEN, 1, GROUP_COLS)
    return pe_t, w1_bd.astype(BF16), w2_bd.astype(BF16)


def _fill_qz(qz_ref, qt_rows, row0, n_heads, tq):
    qz_ref[...] = jnp.zeros(qz_ref.shape, BF16)
    for h in range(n_heads):
        r = row0(h)
        qz_ref[r:r + HEAD_DIM, h * tq:(h + 1) * tq] = (qt_rows(h).astype(F32) * HEAD_DIM ** -0.5).astype(BF16)


def _softmax_probs(score_cols, tq, mk, m_ref, l_ref, p_ref, h, hp=None):
    hp = h if hp is None else hp
    m_all = m_ref[h]
    l_all = l_ref[h]
    alphas, m_news, l_news = [], [], []
    for c0 in range(0, tq, V7X_LANES):
        cols = slice(c0, c0 + V7X_LANES)
        s = score_cols(cols)
        if mk is not None:
            s = jnp.where(mk[:, cols], s, NEG)
        m_old = m_all[:, cols]
        m_new = jnp.maximum(m_old, jnp.max(s, axis=0, keepdims=True))
        alpha = jnp.exp(m_old - m_new)
        p = jnp.exp(s - m_new)
        l_news.append(alpha * l_all[:, cols] + jnp.sum(p, axis=0, keepdims=True))
        p_ref[:, hp * tq + c0:hp * tq + c0 + V7X_LANES] = p.astype(BF16)
        m_news.append(m_new)
        alphas.append(alpha)
    m_ref[h] = jnp.concatenate(m_news, axis=1)
    l_ref[h] = jnp.concatenate(l_news, axis=1)
    return jnp.concatenate(alphas, axis=1)


def _reset_softmax(m_ref, l_ref, acc_ref):
    m_ref[...] = jnp.full(m_ref.shape, NEG, F32)
    l_ref[...] = jnp.zeros(l_ref.shape, F32)
    acc_ref[...] = jnp.zeros(acc_ref.shape, F32)


def _cmp_select_t_kernel(tbl_ref, qt_ref, kc_ref, vct_ref, gt_ref, o_ref, selt_ref, bias_ref, qz_ref, *,
                         tq, n_cmp_pad, n_cmp, n_sel, pos0):
    i = pl.program_id(0)
    b = pl.program_id(1)
    q0 = pos0 + i * tq
    c_idx = lax.broadcasted_iota(jnp.int32, (n_cmp_pad, tq), 0)
    qpos = q0 + lax.broadcasted_iota(jnp.int32, (n_cmp_pad, tq), 1)
    dist = qpos - (c_idx * CMP_STRIDE + CMP_LEN - 1)
    valid = (dist >= 0) & (c_idx < n_cmp)

    @pl.when(b == 0)
    def _():
        bucket = _t5_bucket(dist)
        for h in range(N_HEADS):
            bias_ref[h] = _bias_from_bucket(bucket, tbl_ref, h)

    j_t = lax.broadcasted_iota(jnp.int32, (SEL_PAD, tq), 0)
    q_t = q0 + lax.broadcasted_iota(jnp.int32, (SEL_PAD, tq), 1)
    cur = jnp.right_shift(q_t, 6)
    forced = (j_t == 0) | (j_t == cur) | (j_t == cur - 1)
    avail = j_t <= cur
    j_i = lax.broadcasted_iota(jnp.int32, (SEL_PAD, n_cmp_pad), 0) * SEL_LEN
    c_i = lax.broadcasted_iota(jnp.int32, (SEL_PAD, n_cmp_pad), 1) * CMP_STRIDE
    inc_t = jnp.where((c_i < j_i + SEL_LEN) & (c_i + CMP_LEN > j_i) & (j_i < n_sel * SEL_LEN)
                      & (c_i < n_cmp * CMP_STRIDE), 1.0, 0.0).astype(BF16)
    j_s = lax.broadcasted_iota(jnp.int32, (n_sel, tq), 0)
    kc = kc_ref[...]

    for g in range(NSA_KV_GROUPS):
        _fill_qz(qz_ref, lambda h: qt_ref[(g * NSA_HPG + h) * HEAD_DIM:(g * NSA_HPG + h + 1) * HEAD_DIM, :],
                 lambda h: g * HEAD_DIM, NSA_HPG, tq)
        s_all = _dot(kc, qz_ref[...])
        psum = jnp.zeros((n_cmp_pad, tq), F32)
        outs = []
        for h in range(NSA_HPG):
            hh = g * NSA_HPG + h
            s = s_all[:, h * tq:(h + 1) * tq] + bias_ref[hh]
            s = jnp.where(valid, s, NEG)
            m = jnp.max(s, axis=0, keepdims=True)
            e = jnp.where(valid, jnp.exp(s - m), 0.0)
            p = e / jnp.maximum(jnp.sum(e, axis=0, keepdims=True), 1e-30)
            gate = _sigmoid(gt_ref[hh:hh + 1, :])
            outs.append(_dot(vct_ref[g * HEAD_DIM:(g + 1) * HEAD_DIM, :], p.astype(BF16)) * gate)
            psum = psum + p
        o_ref[:, g * GROUP_COLS:(g + 1) * GROUP_COLS] = jnp.concatenate(outs, axis=0).T
        p_hi, p_lo = _split_bf16(psum, 2)
        imp_t = _dot(inc_t, p_hi) + _dot(inc_t, p_lo)
        score_t = jnp.where(forced, BIG, jnp.where(avail, imp_t, -BIG))

        def pick_step(_, carry):
            remaining, chosen = carry
            top = jnp.max(remaining, axis=0, keepdims=True)
            first = jnp.min(jnp.where(remaining == top, j_s, n_sel), axis=0, keepdims=True)
            pick = j_s == first
            return jnp.where(pick, TAKEN, remaining), jnp.where(pick, 1.0, chosen)

        _, chosen = lax.fori_loop(0, N_SELECT, pick_step, (score_t[:n_sel], jnp.zeros((n_sel, tq), F32)))
        chosen = jnp.concatenate([chosen, jnp.zeros((SEL_PAD - n_sel, tq), F32)], axis=0)
        selt_ref[g * SEL_PAD:(g + 1) * SEL_PAD, :] = chosen.astype(BF16)


def nsa_cmp_select_t(table, q_t, k_cmp, v_cmp_t, gates_t, n_batch, seq, n_cmp, n_sel):
    tq = ATT_TILE
    nq = seq // tq
    m = n_batch * seq
    n_cmp_pad = k_cmp.shape[1]
    kern = functools.partial(_cmp_select_t_kernel, tq=tq, n_cmp_pad=n_cmp_pad, n_cmp=n_cmp, n_sel=n_sel, pos0=0)
    return pl.pallas_call(
        kern,
        grid=(nq, n_batch),
        in_specs=[
            _smem_spec(),
            pl.BlockSpec((D_MODEL, tq), lambda i, b: (0, b * nq + i)),
            pl.BlockSpec((None, n_cmp_pad, GROUP_COLS), lambda i, b: (b, 0, 0)),
            pl.BlockSpec((None, GROUP_COLS, n_cmp_pad), lambda i, b: (b, 0, 0)),
            pl.BlockSpec((3 * N_HEADS, tq), lambda i, b: (0, b * nq + i)),
        ],
        out_specs=[
            pl.BlockSpec((tq, D_MODEL), lambda i, b: (b * nq + i, 0)),
            pl.BlockSpec((NSA_KV_GROUPS * SEL_PAD, tq), lambda i, b: (0, b * nq + i)),
        ],
        out_shape=[
            jax.ShapeDtypeStruct((m, D_MODEL), F32),
            jax.ShapeDtypeStruct((NSA_KV_GROUPS * SEL_PAD, m), BF16),
        ],
        scratch_shapes=[
            pltpu.VMEM((N_HEADS, n_cmp_pad, tq), F32),
            pltpu.VMEM((GROUP_COLS, NSA_HPG * tq), BF16),
        ],
        compiler_params=_cparams("arbitrary", "arbitrary"),
        name="nsa_cmp_select",
    )(table, q_t, k_cmp, v_cmp_t, gates_t)


def _nsa_attn_t_kernel(qt_ref, k_ref, vt_ref, gt_ref, toep_ref, *rest, mode, tile, branch):
    if mode == "sel":
        selt_ref, o_ref, qz_ref, m_ref, l_ref, acc_ref, p_ref = rest
    else:
        o_ref, qz_ref, m_ref, l_ref, acc_ref, p_ref = rest
    i = pl.program_id(1)
    ik = lax.broadcasted_iota(jnp.int32, (tile, tile), 0)
    iq = lax.broadcasted_iota(jnp.int32, (tile, tile), 1)
    causal = iq >= ik
    above = ik > iq

    for g0 in range(0, NSA_KV_GROUPS, GROUPS_PER_PASS):
        groups = range(g0, g0 + GROUPS_PER_PASS)
        for gi, g in enumerate(groups):
            _fill_qz(qz_ref.at[gi],
                     lambda h, g=g: qt_ref[(g * NSA_HPG + h) * HEAD_DIM:(g * NSA_HPG + h + 1) * HEAD_DIM, :],
                     lambda h, g=g: g * HEAD_DIM, NSA_HPG, tile)
        _reset_softmax(m_ref, l_ref, acc_ref)

        def attend(kt, dd, groups=groups):
            r0 = pl.multiple_of(kt * tile, tile)
            k_blk = k_ref[pl.ds(r0, tile), :]
            s_alls = [_dot(k_blk, qz_ref[gi]) for gi in range(GROUPS_PER_PASS)]
            if mode == "sel":
                krow = lax.broadcasted_iota(jnp.int32, (tile, SEL_PAD), 0)
                jcol = lax.broadcasted_iota(jnp.int32, (tile, SEL_PAD), 1)
                expand = jnp.where(jcol == jnp.right_shift(r0 + krow, 6), 1.0, 0.0).astype(BF16)
                mks = [_dot(expand, selt_ref[g * SEL_PAD:(g + 1) * SEL_PAD, :]) > 0.5 for g in groups]
                if dd == 0:
                    mks = [mk & causal for mk in mks]
            else:
                mks = [causal if dd == 0 else (above if dd == 2 else None)] * GROUPS_PER_PASS
            alphas = []
            for gi, g in enumerate(groups):
                for h in range(NSA_HPG):
                    hh = g * NSA_HPG + h

                    def score_cols(cols, gi=gi, h=h, hh=hh):
                        s = s_alls[gi][:, h * tile + cols.start:h * tile + cols.stop]
                        return s + toep_ref[hh, dd, :, cols] if dd < 2 else s

                    alphas.append(_softmax_probs(score_cols, tile, mks[gi], m_ref, l_ref, p_ref.at[gi],
                                                 gi * NSA_HPG + h, h))
            for gi, g in enumerate(groups):
                v_t = vt_ref[g * HEAD_DIM:(g + 1) * HEAD_DIM, pl.ds(r0, tile)]
                pv_all = _dot(v_t, p_ref[gi])
                for h in range(NSA_HPG):
                    a = gi * NSA_HPG + h
                    acc_ref[a] = alphas[a] * acc_ref[a] + pv_all[:, h * tile:(h + 1) * tile]

        attend(i, 0)

        @pl.when(i >= 1)
        def _():
            attend(i - 1, 1)

        if mode == "sel":
            def far_body(kt, carry):
                attend(kt, 2)
                return carry
            lax.fori_loop(0, jnp.maximum(i - 1, 0), far_body, 0)
        else:
            @pl.when(i >= 2)
            def _():
                attend(i - 2, 2)

        for gi, g in enumerate(groups):
            outs = []
            for h in range(NSA_HPG):
                hh = g * NSA_HPG + h
                a = gi * NSA_HPG + h
                gate = _sigmoid(gt_ref[branch * N_HEADS + hh:branch * N_HEADS + hh + 1, :])
                outs.append(acc_ref[a] / jnp.maximum(l_ref[a], 1e-30) * gate)
            o_ref[:, g * GROUP_COLS:(g + 1) * GROUP_COLS] = jnp.concatenate(outs, axis=0).T


def nsa_attn_t(mode, toep, q_t, k_src, k_col, v_t, gates_t, sel_t, n_batch, seq):
    tile = ATT_TILE
    assert WINDOW == 2 * tile
    nq = seq // tile
    m = n_batch * seq
    branch = 1 if mode == "sel" else 2
    kern = functools.partial(_nsa_attn_t_kernel, mode=mode, tile=tile, branch=branch)
    in_specs = [
        pl.BlockSpec((D_MODEL, tile), lambda b, i: (0, b * nq + i)),
        pl.BlockSpec((seq, GROUP_COLS), lambda b, i: (b, k_col)),
        pl.BlockSpec((GROUP_COLS, seq), lambda b, i: (0, b)),
        pl.BlockSpec((3 * N_HEADS, tile), lambda b, i: (0, b * nq + i)),
        pl.BlockSpec((N_HEADS, 2, tile, tile), lambda b, i: (0, 0, 0, 0)),
    ]
    args = [q_t, k_src, v_t, gates_t, toep]
    if mode == "sel":
        in_specs.append(pl.BlockSpec((NSA_KV_GROUPS * SEL_PAD, tile), lambda b, i: (0, b * nq + i)))
        args.append(sel_t)
    return pl.pallas_call(
        kern,
        grid=(n_batch, nq),
        in_specs=in_specs,
        out_specs=pl.BlockSpec((tile, D_MODEL), lambda b, i: (b * nq + i, 0)),
        out_shape=jax.ShapeDtypeStruct((m, D_MODEL), F32),
        scratch_shapes=[
            pltpu.VMEM((GROUPS_PER_PASS, GROUP_COLS, NSA_HPG * tile), BF16),
            pltpu.VMEM((GROUPS_PER_PASS * NSA_HPG, 1, tile), F32),
            pltpu.VMEM((GROUPS_PER_PASS * NSA_HPG, 1, tile), F32),
            pltpu.VMEM((GROUPS_PER_PASS * NSA_HPG, HEAD_DIM, tile), F32),
            pltpu.VMEM((GROUPS_PER_PASS, tile, NSA_HPG * tile), BF16),
        ],
        compiler_params=_cparams("parallel", "arbitrary"),
        name="nsa_attn_" + mode,
    )(*args)


def nsa_prompt_t(table, toep, q_t, kvn, kv_bf, win_bf, vslc_t, vwin_t, gates_t, cw, n_batch, seq):
    pe_t, w1_bd, w2_bd = cw
    n_cmp = (seq - CMP_LEN) // CMP_STRIDE + 1
    n_sel = -(-seq // SEL_LEN)
    assert n_sel <= SEL_PAD and n_sel % 8 == 0
    ctok = compress_prompt(kvn, n_batch, seq, pe_t, w1_bd, w2_bd).astype(BF16)
    k_cmp = ctok[:, 0]
    v_cmp_t = ctok[:, 1].transpose(0, 2, 1)
    o_cmp, sel_t = nsa_cmp_select_t(table, q_t, k_cmp, v_cmp_t, gates_t, n_batch, seq, n_cmp, n_sel)
    o_slc = nsa_attn_t("sel", toep, q_t, kv_bf, 2, vslc_t, gates_t, sel_t, n_batch, seq)
    o_win = nsa_attn_t("win", toep, q_t, win_bf, 0, vwin_t, gates_t, None, n_batch, seq)
    return o_cmp, o_slc, o_win


CUMSUM_CHUNK = 256
HEAD_BLOCK = 4
FOX_BLOCKS_PER_STEP = 2


def _tril_bf16(n):
    r = lax.broadcasted_iota(jnp.int32, (n, n), 0)
    c = lax.broadcasted_iota(jnp.int32, (n, n), 1)
    return jnp.where(r >= c, 1.0, 0.0).astype(BF16)


def _cumsum_rows(x, tri):
    return sum(_dot(tri, part) for part in _split_bf16(x, 3))


def _fox_prep_kernel(f_ref, b_ref, lf_ref, d_ref, *, seq):
    tri = _tril_bf16(CUMSUM_CHUNK)
    carry = jnp.zeros((1, N_HEADS), F32)
    for c in range(seq // CUMSUM_CHUNK):
        rows = slice(c * CUMSUM_CHUNK, (c + 1) * CUMSUM_CHUNK)
        lf = _log_sigmoid(f_ref[rows, :] + b_ref[...])
        lf_ref[rows, :] = lf
        d = carry + _cumsum_rows(lf, tri)
        d_ref[rows, :] = d
        carry = d[CUMSUM_CHUNK - 1:CUMSUM_CHUNK, :]


def fox_prep_prompt(f_raw, b_f, n_batch, seq):
    m = n_batch * seq
    blk = pl.BlockSpec((seq, N_HEADS), lambda b: (b, 0))
    return pl.pallas_call(
        functools.partial(_fox_prep_kernel, seq=seq),
        grid=(n_batch,),
        in_specs=[blk, pl.BlockSpec((1, N_HEADS), lambda b: (0, 0))],
        out_specs=[blk, blk],
        out_shape=[jax.ShapeDtypeStruct((m, N_HEADS), F32)] * 2,
        compiler_params=_cparams("parallel"),
        name="fox_prep",
    )(f_raw, b_f.reshape(1, N_HEADS))


def _fox_attn_t_kernel(qt_ref, k_ref, vt_ref, drow_ref, dcol_ref, o_ref, qz_ref, m_ref, l_ref, acc_ref, p_ref, *,
                       tile):
    i = pl.program_id(2)
    ik = lax.broadcasted_iota(jnp.int32, (tile, tile), 0)
    iq = lax.broadcasted_iota(jnp.int32, (tile, tile), 1)
    causal = iq >= ik
    cols_blk = HEAD_BLOCK * HEAD_DIM
    for bi in range(FOX_BLOCKS_PER_STEP):
        _fill_qz(qz_ref.at[bi], lambda h, bi=bi: qt_ref[(bi * HEAD_BLOCK + h) * HEAD_DIM:(bi * HEAD_BLOCK + h + 1) * HEAD_DIM, :],
                 lambda h: h * HEAD_DIM, HEAD_BLOCK, tile)
    _reset_softmax(m_ref, l_ref, acc_ref)

    def attend(kt, diag):
        r0 = pl.multiple_of(kt * tile, tile)
        k_rows = k_ref[pl.ds(r0, tile), :]
        s_alls = [_dot(k_rows[:, bi * cols_blk:(bi + 1) * cols_blk], qz_ref[bi])
                  for bi in range(FOX_BLOCKS_PER_STEP)]
        alphas = []
        for bi in range(FOX_BLOCKS_PER_STEP):
            for h in range(HEAD_BLOCK):
                a = bi * HEAD_BLOCK + h
                d_keys = dcol_ref[pl.ds(r0, tile), a:a + 1]

                def score_cols(cols, bi=bi, h=h, a=a, d_keys=d_keys):
                    s = s_alls[bi][:, h * tile + cols.start:h * tile + cols.stop]
                    return s + drow_ref[a:a + 1, cols] - d_keys

                alphas.append(_softmax_probs(score_cols, tile, causal if diag else None, m_ref, l_ref,
                                             p_ref.at[bi], a, h))
        for bi in range(FOX_BLOCKS_PER_STEP):
            for h in range(HEAD_BLOCK):
                a = bi * HEAD_BLOCK + h
                v_t = vt_ref[a * HEAD_DIM:(a + 1) * HEAD_DIM, pl.ds(r0, tile)]
                acc_ref[a] = alphas[a] * acc_ref[a] + _dot(v_t, p_ref[bi, :, h * tile:(h + 1) * tile])

    attend(i, True)

    def body(kt, carry):
        attend(kt, False)
        return carry

    lax.fori_loop(0, i, body, 0)
    outs = [acc_ref[a] / jnp.maximum(l_ref[a], 1e-30) for a in range(FOX_BLOCKS_PER_STEP * HEAD_BLOCK)]
    o_ref[...] = jnp.concatenate(outs, axis=0).T


def fox_attn_t(q_t, k_bf, v_t, d_cum, n_batch, seq):
    tile = ATT_TILE
    nq = seq // tile
    m = n_batch * seq
    hps = FOX_BLOCKS_PER_STEP * HEAD_BLOCK
    nhb = N_HEADS // hps
    d4 = d_cum[:m].reshape(n_batch, seq, nhb, hps)
    d_col = d4.transpose(0, 2, 1, 3)
    d_row = d4.transpose(0, 2, 3, 1)
    cols = hps * HEAD_DIM
    return pl.pallas_call(
        functools.partial(_fox_attn_t_kernel, tile=tile),
        grid=(n_batch, nhb, nq),
        in_specs=[
            pl.BlockSpec((cols, tile), lambda b, g, i: (g, b * nq + i)),
            pl.BlockSpec((seq, cols), lambda b, g, i: (b, g)),
            pl.BlockSpec((cols, seq), lambda b, g, i: (g, b)),
            pl.BlockSpec((None, None, hps, tile), lambda b, g, i: (b, g, 0, i)),
            pl.BlockSpec((None, None, seq, hps), lambda b, g, i: (b, g, 0, 0)),
        ],
        out_specs=pl.BlockSpec((tile, cols), lambda b, g, i: (b * nq + i, g)),
        out_shape=jax.ShapeDtypeStruct((m, D_MODEL), F32),
        scratch_shapes=[
            pltpu.VMEM((FOX_BLOCKS_PER_STEP, HEAD_BLOCK * HEAD_DIM, HEAD_BLOCK * tile), BF16),
            pltpu.VMEM((hps, 1, tile), F32),
            pltpu.VMEM((hps, 1, tile), F32),
            pltpu.VMEM((hps, HEAD_DIM, tile), F32),
            pltpu.VMEM((FOX_BLOCKS_PER_STEP, tile, HEAD_BLOCK * tile), BF16),
        ],
        compiler_params=_cparams("parallel", "parallel", "arbitrary"),
        name="fox_attn",
    )(q_t, k_bf, v_t, d_row, d_col)


HT_LANES = V7X_LANES


def _lane_head_onehot(n_heads, per_head):
    r = lax.broadcasted_iota(jnp.int32, (n_heads, n_heads * per_head), 0)
    c = lax.broadcasted_iota(jnp.int32, (n_heads, n_heads * per_head), 1)
    return jnp.where(_div_pow2(c, per_head) == r, 1.0, 0.0).astype(BF16)


def _expand_heads(x, onehot):
    return sum(_dot(part, onehot) for part in _split_bf16(x, 3))


def _diag_blocks_to_rows(acc_t, n_heads, feat, per_head):
    rows, lanes = acc_t.shape
    r = lax.broadcasted_iota(jnp.int32, (rows, lanes), 0)
    c = lax.broadcasted_iota(jnp.int32, (rows, lanes), 1)
    z = jnp.where(_div_pow2(r, feat) == _div_pow2(c, per_head), acc_t, 0.0)
    cc = lax.broadcasted_iota(jnp.int32, (lanes, lanes), 0)
    tt = lax.broadcasted_iota(jnp.int32, (lanes, lanes), 1)
    fold = jnp.where(_mod_pow2(cc, per_head) == tt, 1.0, 0.0).astype(BF16)
    out_t = sum(_dot(part, fold) for part in _split_bf16(z, 3))
    return out_t.T


def block_diag_queries(q_rows, n_seq, per_seq, n_heads):
    q4 = q_rows.reshape(n_seq, per_seq, n_heads, HEAD_DIM).transpose(0, 2, 3, 1)
    eye = jnp.eye(n_heads, dtype=q_rows.dtype)
    q_bd = q4[:, :, :, None, :] * eye[None, :, None, :, None]
    return q_bd.reshape(n_seq, n_heads * HEAD_DIM, n_heads * per_seq).astype(BF16)


PAGES_PER_STEP = 8


def feature_major_pages(pool):
    nd = pool.ndim
    return pool.transpose(0, 1, *range(3, nd), 2).reshape(pool.shape[0], pool.shape[1], -1, pool.shape[2])


def _joint_update(s_list, mk_list, vt_list, m_ref, l_ref, acc_ref):
    s_list = [s if mk is None else jnp.where(mk, s, NEG) for s, mk in zip(s_list, mk_list)]
    m_old = m_ref[0]
    m_new = m_old
    for s in s_list:
        m_new = jnp.maximum(m_new, jnp.max(s, axis=0, keepdims=True))
    alpha = jnp.exp(m_old - m_new)
    l_new = alpha * l_ref[0]
    pv = None
    for s, mk, v_t in zip(s_list, mk_list, vt_list):
        pr = jnp.exp(s - m_new)
        if mk is not None:
            pr = jnp.where(mk, pr, 0.0)
        l_new = l_new + jnp.sum(pr, axis=0, keepdims=True)
        term = _dot(v_t.astype(BF16), pr.astype(BF16))
        pv = term if pv is None else pv + term
    l_ref[0] = l_new
    acc_ref[0] = alpha * acc_ref[0] + pv
    m_ref[0] = m_new


def _fox_decode_fm_kernel(pt_ref, qbd_ref, qbdt_ref, *refs, n_steps, per_seq, pps):
    k_refs = refs[:pps]
    v_refs = refs[pps:2 * pps]
    lf_refs = refs[2 * pps:3 * pps]
    kn_ref, vn_ref, fn_ref, bf_ref, o_ref, lfn_ref, qz_ref, qzt_ref, m_ref, l_ref, acc_ref, carry_ref = refs[3 * pps:]
    p = pl.program_id(1)

    @pl.when(p == 0)
    def _():
        qz_ref[...] = (qbd_ref[...].astype(F32) * HEAD_DIM ** -0.5).astype(BF16)
        qzt_ref[...] = (qbdt_ref[...].astype(F32) * HEAD_DIM ** -0.5).astype(BF16)
        _reset_softmax(m_ref, l_ref, acc_ref)
        carry_ref[...] = jnp.zeros(carry_ref.shape, F32)

    tri = _tril_bf16(PAGE_SIZE)
    onehot = _lane_head_onehot(N_HEADS, per_seq)

    def update(s_t, mk, pv_of):
        if mk is not None:
            s_t = jnp.where(mk, s_t, NEG)
        m_old = m_ref[0]
        m_new = jnp.maximum(m_old, jnp.max(s_t, axis=0, keepdims=True))
        alpha = jnp.exp(m_old - m_new)
        pr = jnp.exp(s_t - m_new)
        if mk is not None:
            pr = jnp.where(mk, pr, 0.0)
        l_ref[0] = alpha * l_ref[0] + jnp.sum(pr, axis=0, keepdims=True)
        acc_ref[0] = alpha * acc_ref[0] + pv_of(pr.astype(BF16))
        m_ref[0] = m_new

    @pl.when(p < n_steps)
    def _():
        local, s_raw = [], []
        for i in range(pps):
            lf_t = lf_refs[i][...]
            lf_exp = sum(lax.dot_general(part, onehot, (((0,), (0,)), ((), ())), preferred_element_type=F32)
                         for part in _split_bf16(lf_t, 3))
            local.append(_cumsum_rows(lf_exp, tri))
            s_raw.append(_dot(qzt_ref[...], k_refs[i][...].astype(BF16)).T)
        carry = carry_ref[...]
        s_list = []
        for i in range(pps):
            d = carry + local[i]
            s_list.append(s_raw[i] - d)
            carry = d[PAGE_SIZE - 1:PAGE_SIZE, :]
        carry_ref[...] = carry
        _joint_update(s_list, [None] * pps, [r[...] for r in v_refs], m_ref, l_ref, acc_ref)

    @pl.when(p == n_steps)
    def _():
        lf_new = _log_sigmoid(fn_ref[...] + bf_ref[...])
        lfn_ref[...] = lf_new
        pad = PAGE_SIZE - per_seq
        k_rows = jnp.concatenate([kn_ref[...], jnp.zeros((pad, D_MODEL), F32)], axis=0).astype(BF16)
        v_rows = jnp.concatenate([vn_ref[...], jnp.zeros((pad, D_MODEL), F32)], axis=0).astype(BF16)
        lf_rows = jnp.concatenate([lf_new, jnp.zeros((pad, N_HEADS), F32)], axis=0)
        d = carry_ref[...] + _cumsum_rows(_expand_heads(lf_rows, onehot), tri)
        srow = lax.broadcasted_iota(jnp.int32, (PAGE_SIZE, HT_LANES), 0)
        tq = _mod_pow2(lax.broadcasted_iota(jnp.int32, (PAGE_SIZE, HT_LANES), 1), per_seq)
        s_t = _dot(k_rows, qz_ref[...]) - d
        update(s_t, srow <= tq, lambda pr: lax.dot_general(v_rows, pr, (((0,), (0,)), ((), ())),
                                                           preferred_element_type=F32))
        o_t = acc_ref[0] / jnp.maximum(l_ref[0], 1e-30)
        o_ref[...] = _diag_blocks_to_rows(o_t, N_HEADS, HEAD_DIM, per_seq)[:per_seq, :]


def fox_decode_fm(page_table, q_bd, kv_pool_t, lf_pool_t, layer, kv_new, f_new, b_f, row_block0, n_seq, per_seq):
    n_pages = page_table.shape[1]
    pps = PAGES_PER_STEP
    assert N_HEADS * per_seq == HT_LANES and n_pages % pps == 0
    n_steps = n_pages // pps
    kern = functools.partial(_fox_decode_fm_kernel, n_steps=n_steps, per_seq=per_seq, pps=pps)

    def page(i):
        return lambda b, p, pt: pt[b, jnp.minimum(p, n_steps - 1) * pps + i]

    def page_spec(rows, blk, i):
        return pl.BlockSpec((None, None, rows, PAGE_SIZE), lambda b, p, pt, i=i: (layer, page(i)(b, p, pt), blk, 0))

    grid_spec = pltpu.PrefetchScalarGridSpec(
        num_scalar_prefetch=1,
        grid=(n_seq, n_steps + 1),
        in_specs=[pl.BlockSpec((None, D_MODEL, HT_LANES), lambda b, p, pt: (b, 0, 0)),
                  pl.BlockSpec((None, HT_LANES, D_MODEL), lambda b, p, pt: (b, 0, 0))]
        + [page_spec(D_MODEL, 0, i) for i in range(pps)]
        + [page_spec(D_MODEL, 1, i) for i in range(pps)]
        + [page_spec(N_HEADS, 0, i) for i in range(pps)]
        + [
            pl.BlockSpec((per_seq, D_MODEL), lambda b, p, pt: (row_block0 + b, 0)),
            pl.BlockSpec((per_seq, D_MODEL), lambda b, p, pt: (row_block0 + b, 1)),
            pl.BlockSpec((per_seq, N_HEADS), lambda b, p, pt: (row_block0 + b, 0)),
            pl.BlockSpec((1, N_HEADS), lambda b, p, pt: (0, 0)),
        ],
        out_specs=[
            pl.BlockSpec((per_seq, D_MODEL), lambda b, p, pt: (b, 0)),
            pl.BlockSpec((per_seq, N_HEADS), lambda b, p, pt: (b, 0)),
        ],
        scratch_shapes=[
            pltpu.VMEM((D_MODEL, HT_LANES), BF16),
            pltpu.VMEM((HT_LANES, D_MODEL), BF16),
            pltpu.VMEM((1, 1, HT_LANES), F32),
            pltpu.VMEM((1, 1, HT_LANES), F32),
            pltpu.VMEM((1, D_MODEL, HT_LANES), F32),
            pltpu.VMEM((1, HT_LANES), F32),
        ],
    )
    return pl.pallas_call(
        kern,
        grid_spec=grid_spec,
        out_shape=[jax.ShapeDtypeStruct((n_seq * per_seq, D_MODEL), F32),
                   jax.ShapeDtypeStruct((n_seq * per_seq, N_HEADS), F32)],
        compiler_params=_cparams("parallel", "arbitrary"),
        name="fox_decode",
    )(page_table, q_bd, q_bd.transpose(0, 2, 1), *([kv_pool_t] * (2 * pps)), *([lf_pool_t] * pps),
      kv_new, kv_new, f_new, b_f.reshape(1, N_HEADS))


DEC_ROW_TILE = 512


def _group_rows_to_heads(acc_t):
    return jnp.concatenate([acc_t[(h // NSA_HPG) * HEAD_DIM:(h // NSA_HPG + 1) * HEAD_DIM] for h in range(N_HEADS)],
                           axis=0)


def group_block_diag_queries(q_rows, n_seq, per_seq):
    q5 = q_rows.reshape(n_seq, per_seq, NSA_KV_GROUPS, NSA_HPG, HEAD_DIM).transpose(0, 2, 4, 3, 1)
    eye = jnp.eye(NSA_KV_GROUPS, dtype=q_rows.dtype)
    q_bd = q5[:, :, :, None, :, :] * eye[None, :, None, :, None, None]
    return q_bd.reshape(n_seq, GROUP_COLS, N_HEADS * per_seq).astype(BF16)


def _nsa_dec_bias_kernel(tbl_ref, o_ref, *, pos0, per_seq):
    onehot = _lane_head_onehot(N_HEADS, per_seq)
    tab = _expand_heads(tbl_ref[...], onehot)
    row = lax.broadcasted_iota(jnp.int32, (PAGE_SIZE, HT_LANES), 0)
    t = _mod_pow2(lax.broadcasted_iota(jnp.int32, (PAGE_SIZE, HT_LANES), 1), per_seq)
    dists = (PAGE_SIZE + t - row, t - row, pos0 + t - (row * CMP_STRIDE + CMP_LEN - 1))
    for n, dist in enumerate(dists):
        bucket = _t5_bucket(dist)
        val = jnp.broadcast_to(tab[0:1, :], (PAGE_SIZE, HT_LANES))
        for k in range(1, N_BUCKETS):
            val = jnp.where(bucket == k, tab[k:k + 1, :], val)
        o_ref[n] = val
    o_ref[3] = jnp.broadcast_to(tab[N_BUCKETS - 1:N_BUCKETS, :], (PAGE_SIZE, HT_LANES))


def nsa_dec_bias(table, pos0, per_seq):
    assert PAGE_SIZE >= T5_SATURATION_DIST
    return pl.pallas_call(
        functools.partial(_nsa_dec_bias_kernel, pos0=pos0, per_seq=per_seq),
        out_shape=jax.ShapeDtypeStruct((4, PAGE_SIZE, HT_LANES), F32),
        name="nsa_dec_bias",
    )(table)


def nsa_dec_compress(xd, pe_t, w1_bd, w2_bd):
    r_total = xd.shape[1]
    rows = min(DEC_ROW_TILE, r_total)
    return pl.pallas_call(
        functools.partial(_compress_kernel, rows=rows, l_major=True),
        grid=(r_total // rows, 2),
        in_specs=[
            pl.BlockSpec((CMP_STRIDE, rows, GROUP_COLS), lambda i, j: (0, i, j)),
            pl.BlockSpec((None, CMP_LEN, 1, GROUP_COLS), lambda i, j: (j, 0, 0, 0)),
            pl.BlockSpec((None, CMP_LEN, GROUP_COLS, NSA_KV_GROUPS * CMP_HIDDEN), lambda i, j: (j, 0, 0, 0)),
            pl.BlockSpec((None, NSA_KV_GROUPS * CMP_HIDDEN, GROUP_COLS), lambda i, j: (j, 0, 0)),
        ],
        out_specs=pl.BlockSpec((None, rows, GROUP_COLS), lambda i, j: (j, i, 0)),
        out_shape=jax.ShapeDtypeStruct((2, r_total, GROUP_COLS), F32),
        compiler_params=_cparams("parallel", "arbitrary"),
        name="nsa_dec_compress",
    )(xd, pe_t, w1_bd, w2_bd)


def _dec_finish(m_ref, l_ref, acc_ref, gate_row, per_seq):
    o_t = acc_ref[0] / jnp.maximum(l_ref[0], 1e-30) * _sigmoid(gate_row)
    return _diag_blocks_to_rows(_group_rows_to_heads(o_t), N_HEADS, HEAD_DIM, per_seq)[:per_seq, :]


def _nsa_dec_cmp_kernel(qbd_ref, kc_ref, vc_ref, g_ref, bias_ref, o_ref, sel_ref, score_ref, *,
                        n_cmp, n_sel, pos0, per_seq):
    n_c = kc_ref.shape[0]
    c_idx = lax.broadcasted_iota(jnp.int32, (n_c, HT_LANES), 0)
    lane = lax.broadcasted_iota(jnp.int32, (n_c, HT_LANES), 1)
    t = _mod_pow2(lane, per_seq)
    dist = pos0 + t - (c_idx * CMP_STRIDE + CMP_LEN - 1)
    valid = (dist >= 0) & (c_idx < n_cmp)
    qz = (qbd_ref[...].astype(F32) * HEAD_DIM ** -0.5).astype(BF16)
    s = _dot(kc_ref[...], qz) + bias_ref[2]
    s = jnp.where(valid, s, NEG)
    m = jnp.max(s, axis=0, keepdims=True)
    e = jnp.where(valid, jnp.exp(s - m), 0.0)
    p = e / jnp.maximum(jnp.sum(e, axis=0, keepdims=True), 1e-30)
    o_t = lax.dot_general(vc_ref[...], p.astype(BF16), (((0,), (0,)), ((), ())), preferred_element_type=F32)
    o_t = o_t * _sigmoid(g_ref[0])
    o_ref[...] = _diag_blocks_to_rows(_group_rows_to_heads(o_t), N_HEADS, HEAD_DIM, per_seq)[:per_seq, :]
    src = lax.broadcasted_iota(jnp.int32, (HT_LANES, HT_LANES), 0)
    dst = lax.broadcasted_iota(jnp.int32, (HT_LANES, HT_LANES), 1)
    same_t = _mod_pow2(src, per_seq) == _mod_pow2(dst, per_seq)
    g_of_src = _div_pow2(src, per_seq * NSA_HPG)
    n_gt = NSA_KV_GROUPS * per_seq
    head_sum = jnp.where(same_t & (g_of_src == _div_pow2(dst, per_seq)) & (dst < n_gt), 1.0, 0.0).astype(BF16)
    psum = sum(_dot(part, head_sum) for part in _split_bf16(p, 2))
    j_i = lax.broadcasted_iota(jnp.int32, (SEL_PAD, n_c), 0) * SEL_LEN
    c_i = lax.broadcasted_iota(jnp.int32, (SEL_PAD, n_c), 1) * CMP_STRIDE
    inc_t = jnp.where((c_i < j_i + SEL_LEN) & (c_i + CMP_LEN > j_i) & (j_i < n_sel * SEL_LEN)
                      & (c_i < n_cmp * CMP_STRIDE), 1.0, 0.0).astype(BF16)
    imp_t = sum(_dot(inc_t, part) for part in _split_bf16(psum, 2))
    j_t = lax.broadcasted_iota(jnp.int32, (SEL_PAD, HT_LANES), 0)
    q_t = pos0 + _mod_pow2(lax.broadcasted_iota(jnp.int32, (SEL_PAD, HT_LANES), 1), per_seq)
    cur = jnp.right_shift(q_t, 6)
    forced = (j_t == 0) | (j_t == cur) | (j_t == cur - 1)
    avail = j_t <= cur
    score_t = jnp.where(forced, BIG, jnp.where(avail, imp_t, -BIG))
    score_ref[...] = score_t
    n_rank = -(-n_sel // 8) * 8
    sc = score_t[:n_rank]
    j_s = lax.broadcasted_iota(jnp.int32, (n_rank, HT_LANES), 0)

    def rank_step(r, rank):
        row = jnp.broadcast_to(score_ref[pl.ds(r, 1), :], (n_rank, HT_LANES))
        ahead = jnp.where(row > sc, 1.0, jnp.where((row == sc) & (j_s > r), 1.0, 0.0))
        return rank + ahead

    rank = lax.fori_loop(0, n_sel, rank_step, jnp.zeros((n_rank, HT_LANES), F32))
    chosen = jnp.where((rank < float(N_SELECT)) & (j_s < n_sel), 1.0, 0.0)
    chosen = jnp.concatenate([chosen, jnp.zeros((SEL_PAD - n_rank, HT_LANES), F32)], axis=0).astype(BF16)
    spread = jnp.where(same_t & (src == _div_pow2(dst, per_seq * NSA_HPG) * per_seq + _mod_pow2(dst, per_seq)),
                       1.0, 0.0).astype(BF16)
    sel_ref[...] = _dot(chosen, spread).astype(BF16)


def nsa_dec_cmp(q_bd, k_cmp, v_cmp, gate_rows, bias, n_cmp, n_sel, pos0, per_seq):
    n_seq, n_c = k_cmp.shape[:2]
    kern = functools.partial(_nsa_dec_cmp_kernel, n_cmp=n_cmp, n_sel=n_sel, pos0=pos0, per_seq=per_seq)
    return pl.pallas_call(
        kern,
        grid=(n_seq,),
        in_specs=[
            pl.BlockSpec((None, GROUP_COLS, HT_LANES), lambda b: (b, 0, 0)),
            pl.BlockSpec((None, n_c, GROUP_COLS), lambda b: (b, 0, 0)),
            pl.BlockSpec((None, n_c, GROUP_COLS), lambda b: (b, 0, 0)),
            pl.BlockSpec((None, 3, 1, HT_LANES), lambda b: (b, 0, 0, 0)),
            pl.BlockSpec((4, PAGE_SIZE, HT_LANES), lambda b: (0, 0, 0)),
        ],
        out_specs=[
            pl.BlockSpec((per_seq, D_MODEL), lambda b: (b, 0)),
            pl.BlockSpec((None, SEL_PAD, HT_LANES), lambda b: (b, 0, 0)),
        ],
        out_shape=[
            jax.ShapeDtypeStruct((n_seq * per_seq, D_MODEL), F32),
            jax.ShapeDtypeStruct((n_seq, SEL_PAD, HT_LANES), BF16),
        ],
        scratch_shapes=[pltpu.VMEM((SEL_PAD, HT_LANES), F32)],
        compiler_params=_cparams("parallel"),
        name="nsa_dec_cmp",
    )(q_bd, k_cmp, v_cmp, gate_rows, bias)


def _dec_gather_fm_kernel(pt_ref, *refs, pps):
    o_ref = refs[pps]
    rpp = PAGE_SIZE // CMP_STRIDE
    for i in range(pps):
        x3 = refs[i][...].T.reshape(rpp, CMP_STRIDE, 2 * GROUP_COLS)
        for l in range(CMP_STRIDE):
            o_ref[l, i * rpp:(i + 1) * rpp, :] = x3[:, l, :]


def nsa_dec_gather_fm(page_table, pool_t, layer):
    n_seq, n_pages = page_table.shape
    pps = PAGES_PER_STEP
    assert n_pages % pps == 0
    rpp = PAGE_SIZE // CMP_STRIDE
    grid_spec = pltpu.PrefetchScalarGridSpec(
        num_scalar_prefetch=1,
        grid=(n_seq, n_pages // pps),
        in_specs=[pl.BlockSpec((None, None, 2 * GROUP_COLS, PAGE_SIZE),
                               lambda b, p, pt, i=i: (layer, pt[b, p * pps + i], 0, 0)) for i in range(pps)],
        out_specs=pl.BlockSpec((CMP_STRIDE, rpp * pps, 2 * GROUP_COLS),
                               lambda b, p, pt: (0, b * (n_pages // pps) + p, 0)),
    )
    return pl.pallas_call(
        functools.partial(_dec_gather_fm_kernel, pps=pps),
        grid_spec=grid_spec,
        out_shape=jax.ShapeDtypeStruct((CMP_STRIDE, n_seq * n_pages * rpp, 2 * GROUP_COLS), F32),
        compiler_params=_cparams("parallel", "arbitrary"),
        name="nsa_dec_gather",
    )(page_table, *([pool_t] * pps))


def _dec_update(s_t, mk, pv_of, m_ref, l_ref, acc_ref):
    if mk is not None:
        s_t = jnp.where(mk, s_t, NEG)
    m_old = m_ref[0]
    m_new = jnp.maximum(m_old, jnp.max(s_t, axis=0, keepdims=True))
    alpha = jnp.exp(m_old - m_new)
    pr = jnp.exp(s_t - m_new)
    if mk is not None:
        pr = jnp.where(mk, pr, 0.0)
    l_ref[0] = alpha * l_ref[0] + jnp.sum(pr, axis=0, keepdims=True)
    acc_ref[0] = alpha * acc_ref[0] + pv_of(pr.astype(BF16))
    m_ref[0] = m_new


def _pv_key_major(v_rows):
    return lambda pr: lax.dot_general(v_rows, pr, (((0,), (0,)), ((), ())), preferred_element_type=F32)


def _nsa_dec_sel_fm_kernel(pt_ref, qbd_ref, qbdt_ref, *refs, n_steps, per_seq, pps):
    k_refs = refs[:pps]
    v_refs = refs[pps:2 * pps]
    kn_ref, vn_ref, sel_ref, g_ref, bias_ref, o_ref, qz_ref, qzt_ref, m_ref, l_ref, acc_ref = refs[2 * pps:]
    p = pl.program_id(1)

    @pl.when(p == 0)
    def _():
        qz_ref[...] = (qbd_ref[...].astype(F32) * HEAD_DIM ** -0.5).astype(BF16)
        qzt_ref[...] = (qbdt_ref[...].astype(F32) * HEAD_DIM ** -0.5).astype(BF16)
        _reset_softmax(m_ref, l_ref, acc_ref)

    srow = lax.broadcasted_iota(jnp.int32, (PAGE_SIZE, SEL_PAD), 0)
    jcol = lax.broadcasted_iota(jnp.int32, (PAGE_SIZE, SEL_PAD), 1)
    blocks_per_page = PAGE_SIZE // SEL_LEN

    def member(first_block):
        expand = jnp.where(jcol == first_block + jnp.right_shift(srow, 6), 1.0, 0.0).astype(BF16)
        return _dot(expand, sel_ref[...]) > 0.5

    @pl.when(p < n_steps)
    def _():
        s_list, mk_list = [], []
        for i in range(pps):
            page_no = p * pps + i
            bias = jnp.where(page_no == n_steps * pps - 1, bias_ref[0], bias_ref[3])
            s_list.append(_dot(qzt_ref[...], k_refs[i][...].astype(BF16)).T + bias)
            mk_list.append(member(page_no * blocks_per_page))
        _joint_update(s_list, mk_list, [r[...] for r in v_refs], m_ref, l_ref, acc_ref)

    @pl.when(p == n_steps)
    def _():
        pad = PAGE_SIZE - per_seq
        k_rows = jnp.concatenate([kn_ref[...], jnp.zeros((pad, GROUP_COLS), F32)], axis=0).astype(BF16)
        v_rows = jnp.concatenate([vn_ref[...], jnp.zeros((pad, GROUP_COLS), F32)], axis=0).astype(BF16)
        krow = lax.broadcasted_iota(jnp.int32, (PAGE_SIZE, HT_LANES), 0)
        tq = _mod_pow2(lax.broadcasted_iota(jnp.int32, (PAGE_SIZE, HT_LANES), 1), per_seq)
        mk = member(n_steps * pps * blocks_per_page) & (krow <= tq)
        s_t = _dot(k_rows, qz_ref[...]) + bias_ref[1]
        _dec_update(s_t, mk, _pv_key_major(v_rows), m_ref, l_ref, acc_ref)
        o_ref[...] = _dec_finish(m_ref, l_ref, acc_ref, g_ref[1], per_seq)


def nsa_dec_sel_fm(page_table, q_bd, pool_t, layer, kv_new, sel, gate_rows, bias, row_block0, per_seq):
    n_seq, n_pages = page_table.shape
    pps = PAGES_PER_STEP
    assert n_pages % pps == 0
    n_steps = n_pages // pps
    kern = functools.partial(_nsa_dec_sel_fm_kernel, n_steps=n_steps, per_seq=per_seq, pps=pps)

    def page_spec(blk, i):
        return pl.BlockSpec((None, None, GROUP_COLS, PAGE_SIZE),
                            lambda b, p, pt, i=i: (layer, pt[b, jnp.minimum(p, n_steps - 1) * pps + i], blk, 0))

    grid_spec = pltpu.PrefetchScalarGridSpec(
        num_scalar_prefetch=1,
        grid=(n_seq, n_steps + 1),
        in_specs=[pl.BlockSpec((None, GROUP_COLS, HT_LANES), lambda b, p, pt: (b, 0, 0)),
                  pl.BlockSpec((None, HT_LANES, GROUP_COLS), lambda b, p, pt: (b, 0, 0))]
        + [page_spec(2, i) for i in range(pps)] + [page_spec(3, i) for i in range(pps)]
        + [
            pl.BlockSpec((per_seq, GROUP_COLS), lambda b, p, pt: (row_block0 + b, 2)),
            pl.BlockSpec((per_seq, GROUP_COLS), lambda b, p, pt: (row_block0 + b, 3)),
            pl.BlockSpec((None, SEL_PAD, HT_LANES), lambda b, p, pt: (b, 0, 0)),
            pl.BlockSpec((None, 3, 1, HT_LANES), lambda b, p, pt: (b, 0, 0, 0)),
            pl.BlockSpec((4, PAGE_SIZE, HT_LANES), lambda b, p, pt: (0, 0, 0)),
        ],
        out_specs=pl.BlockSpec((per_seq, D_MODEL), lambda b, p, pt: (b, 0)),
        scratch_shapes=[
            pltpu.VMEM((GROUP_COLS, HT_LANES), BF16),
            pltpu.VMEM((HT_LANES, GROUP_COLS), BF16),
            pltpu.VMEM((1, 1, HT_LANES), F32),
            pltpu.VMEM((1, 1, HT_LANES), F32),
            pltpu.VMEM((1, GROUP_COLS, HT_LANES), F32),
        ],
    )
    return pl.pallas_call(
        kern,
        grid_spec=grid_spec,
        out_shape=jax.ShapeDtypeStruct((n_seq * per_seq, D_MODEL), F32),
        compiler_params=_cparams("parallel", "arbitrary"),
        name="nsa_dec_sel",
    )(page_table, q_bd, q_bd.transpose(0, 2, 1), *([pool_t] * (2 * pps)), kv_new, kv_new, sel, gate_rows, bias)


def _nsa_dec_win_fm_kernel(qbd_ref, qbdt_ref, wt_ref, wn_ref, g_ref, bias_ref, o_ref, ws_ref, m_ref, l_ref, acc_ref, *,
                           per_seq):
    n_past = wt_ref.shape[1]
    n_tiles = n_past // PAGE_SIZE
    qz = (qbd_ref[...].astype(F32) * HEAD_DIM ** -0.5).astype(BF16)
    qzt = (qbdt_ref[...].astype(F32) * HEAD_DIM ** -0.5).astype(BF16)
    _reset_softmax(m_ref, l_ref, acc_ref)
    krow = lax.broadcasted_iota(jnp.int32, (PAGE_SIZE, HT_LANES), 0)
    tq = _mod_pow2(lax.broadcasted_iota(jnp.int32, (PAGE_SIZE, HT_LANES), 1), per_seq)
    s_qm = _dot(qzt, wt_ref[:GROUP_COLS, :].astype(BF16))
    for w in range(n_tiles):
        keys = slice(w * PAGE_SIZE, (w + 1) * PAGE_SIZE)
        mk = (krow > tq) if w == 0 else None
        bias = bias_ref[0] if w == n_tiles - 1 else bias_ref[3]
        v_t = wt_ref[GROUP_COLS:, keys].astype(BF16)
        _dec_update(s_qm[:, keys].T + bias, mk, lambda pr, v_t=v_t: _dot(v_t, pr), m_ref, l_ref, acc_ref)
    pad = PAGE_SIZE - per_seq
    new = jnp.concatenate([wn_ref[...], jnp.zeros((pad, 2 * GROUP_COLS), F32)], axis=0)
    s_t = _dot(new[:, :GROUP_COLS].astype(BF16), qz) + bias_ref[1]
    _dec_update(s_t, krow <= tq, _pv_key_major(new[:, GROUP_COLS:].astype(BF16)), m_ref, l_ref, acc_ref)
    o_ref[...] = _dec_finish(m_ref, l_ref, acc_ref, g_ref[2], per_seq)
    shifted = pltpu.roll(wt_ref[...], n_past - per_seq, 1)
    ws_ref[...] = shifted
    lane = lax.broadcasted_iota(jnp.int32, (2 * GROUP_COLS, PAGE_SIZE), 1)
    new_t = pltpu.roll(new.T, PAGE_SIZE - per_seq, 1)
    ws_ref[:, n_past - PAGE_SIZE:] = jnp.where(lane >= PAGE_SIZE - per_seq, new_t, shifted[:, n_past - PAGE_SIZE:])


def nsa_dec_win_fm(q_bd, win_t, layer, win_new, gate_rows, bias, row_block0, per_seq):
    _, n_seq, n_feat, n_past = win_t.shape
    assert n_past == WINDOW and n_past % PAGE_SIZE == 0 and PAGE_SIZE + 1 >= T5_SATURATION_DIST
    kern = functools.partial(_nsa_dec_win_fm_kernel, per_seq=per_seq)
    return pl.pallas_call(
        kern,
        grid=(n_seq,),
        in_specs=[
            pl.BlockSpec((None, GROUP_COLS, HT_LANES), lambda b: (b, 0, 0)),
            pl.BlockSpec((None, HT_LANES, GROUP_COLS), lambda b: (b, 0, 0)),
            pl.BlockSpec((None, None, n_feat, n_past), lambda b: (layer, b, 0, 0)),
            pl.BlockSpec((per_seq, n_feat), lambda b: (row_block0 + b, 0)),
            pl.BlockSpec((None, 3, 1, HT_LANES), lambda b: (b, 0, 0, 0)),
            pl.BlockSpec((4, PAGE_SIZE, HT_LANES), lambda b: (0, 0, 0)),
        ],
        out_specs=[
            pl.BlockSpec((per_seq, D_MODEL), lambda b: (b, 0)),
            pl.BlockSpec((None, n_feat, n_past), lambda b: (b, 0, 0)),
        ],
        out_shape=[
            jax.ShapeDtypeStruct((n_seq * per_seq, D_MODEL), F32),
            jax.ShapeDtypeStruct((n_seq, n_feat, n_past), F32),
        ],
        scratch_shapes=[
            pltpu.VMEM((1, 1, HT_LANES), F32),
            pltpu.VMEM((1, 1, HT_LANES), F32),
            pltpu.VMEM((1, GROUP_COLS, HT_LANES), F32),
        ],
        compiler_params=_cparams("parallel"),
        name="nsa_dec_win",
    )(q_bd, q_bd.transpose(0, 2, 1), win_t, win_new, gate_rows, bias)


def nsa_decode_fm(table, page_table, pool_t, win_t, layer, q_rows, kv_new, win_new, gates_rows, cw, row_block0, pos0,
                  per_seq):
    n_seq, n_pages = page_table.shape
    pe_t, w1_bd, w2_bd = cw
    past = n_pages * PAGE_SIZE
    s_total = past + per_seq
    n_cmp = (s_total - CMP_LEN) // CMP_STRIDE + 1
    n_sel = -(-s_total // SEL_LEN)
    assert n_cmp == past // CMP_STRIDE - 1 and n_sel == past // SEL_LEN + 1 and per_seq <= CMP_STRIDE
    assert pos0 == past and pos0 % SEL_LEN == 0 and pos0 >= WINDOW and past // CMP_STRIDE == PAGE_SIZE
    bias = nsa_dec_bias(table, pos0, per_seq)
    ctok = nsa_dec_compress(nsa_dec_gather_fm(page_table, pool_t, layer), pe_t, w1_bd, w2_bd).astype(BF16)
    rows_per_seq = past // CMP_STRIDE
    k_cmp = ctok[0].reshape(n_seq, rows_per_seq, GROUP_COLS)
    v_cmp = ctok[1].reshape(n_seq, rows_per_seq, GROUP_COLS)
    q_bd = group_block_diag_queries(q_rows, n_seq, per_seq)
    gate_rows = gates_rows.reshape(n_seq, per_seq, 3, N_HEADS).transpose(0, 2, 3, 1).reshape(n_seq, 3, 1, HT_LANES)
    o_cmp, sel = nsa_dec_cmp(q_bd, k_cmp, v_cmp, gate_rows, bias, n_cmp, n_sel, pos0, per_seq)
    o_slc = nsa_dec_sel_fm(page_table, q_bd, pool_t, layer, kv_new, sel, gate_rows, bias, row_block0, per_seq)
    o_win, win_state_t = nsa_dec_win_fm(q_bd, win_t, layer, win_new, gate_rows, bias, row_block0, per_seq)
    return o_cmp, o_slc, o_win, win_state_t


GLA_SUB = 16
GLA_BLOCK = 512


def _gla_kernel(q_ref, k_ref, v_ref, r_ref, a_ref, wa_ref, ba_ref, ng_ref, s0_ref, o_ref, sfin_ref, s_ref, *,
                n_chunks, chunk):
    t_blk = pl.program_id(2)

    @pl.when(t_blk == 0)
    def _():
        s_ref[...] = s0_ref[...]

    tri = _tril_bf16(chunk)
    it = lax.broadcasted_iota(jnp.int32, (chunk, chunk), 0)
    js = lax.broadcasted_iota(jnp.int32, (chunk, chunk), 1)
    same_sub = jnp.right_shift(it, 4) == jnp.right_shift(js, 4)
    n_sub = chunk // GLA_SUB

    def chunk_step(c, carry):
        r0 = pl.multiple_of(c * chunk, chunk)
        rows = pl.ds(r0, chunk)
        q = q_ref[rows, :] * GLA_DK ** -0.5
        k = k_ref[rows, :]
        v = v_ref[rows, :].astype(BF16)
        g = _log_sigmoid(_dot(a_ref[rows, :].astype(BF16), wa_ref[...]) + ba_ref[...]) / GLA_TAU
        b = _cumsum_rows(g, tri)
        s_old = s_ref[...]
        o = _dot((q * jnp.exp(b)).astype(BF16), s_old.astype(BF16))
        a_mat = jnp.zeros((chunk, chunk), F32)
        for d in range(GLA_SUB):
            if d == 0:
                val = jnp.sum(q * k, axis=-1, keepdims=True)
            else:
                k_sh = pltpu.roll(k, d, 0)
                b_sh = pltpu.roll(b, d, 0)
                val = jnp.sum(q * k_sh * jnp.exp(jnp.minimum(b - b_sh, 0.0)), axis=-1, keepdims=True)
            a_mat = a_mat + jnp.where((it - js == d) & same_sub, val, 0.0)
        if n_sub > 1:
            blocks = [jnp.zeros((GLA_SUB, chunk), F32)]
            for a in range(1, n_sub):
                ref_b = b[a * GLA_SUB - 1:a * GLA_SUB, :]
                qa = q[a * GLA_SUB:(a + 1) * GLA_SUB] * jnp.exp(b[a * GLA_SUB:(a + 1) * GLA_SUB] - ref_b)
                kk = k * jnp.exp(jnp.minimum(ref_b - b, 0.0))
                blocks.append(_dot_nt(qa.astype(BF16), kk.astype(BF16)))
            off = jnp.concatenate(blocks, axis=0)
            a_mat = a_mat + jnp.where(jnp.right_shift(js, 4) < jnp.right_shift(it, 4), off, 0.0)
        o = o + _dot(a_mat.astype(BF16), v)
        b_last = b[chunk - 1:chunk, :]
        kd = k * jnp.exp(b_last - b)
        pad_k = jnp.zeros((GLA_DK - chunk, GLA_DK), F32)
        pad_v = jnp.zeros((GLA_DK - chunk, GLA_DV), BF16)
        kd_t = jnp.concatenate([kd, pad_k], axis=0).T
        dec = jnp.broadcast_to(jnp.exp(b_last), (GLA_DK, GLA_DK)).T
        s_ref[...] = (s_old * jnp.concatenate([dec] * (GLA_DV // GLA_DK), axis=1)
                      + _dot(kd_t.astype(BF16), jnp.concatenate([v, pad_v], axis=0)))
        y = o * lax.rsqrt(jnp.mean(o * o, axis=-1, keepdims=True) + EPS) * ng_ref[...]
        rr = r_ref[rows, :]
        o_ref[rows, :] = y * (rr * _sigmoid(rr))
        return carry

    lax.fori_loop(0, n_chunks, chunk_step, 0)

    @pl.when(t_blk == pl.num_programs(2) - 1)
    def _():
        sfin_ref[...] = s_ref[...]


def gla_prompt(q, k, v, r, a_lr, w_a2, b_a, norm_g, s0, n_batch, seq, row_block0=0):
    chunk = GLA_CHUNK if seq % GLA_CHUNK == 0 else seq
    blk = min(GLA_BLOCK, seq)
    nt = seq // blk
    m = n_batch * seq
    kern = functools.partial(_gla_kernel, n_chunks=blk // chunk, chunk=chunk)
    row = lambda b, h, t: row_block0 + b * nt + t
    return pl.pallas_call(
        kern,
        grid=(n_batch, GLA_HEADS, nt),
        in_specs=[
            pl.BlockSpec((blk, GLA_DK), lambda b, h, t: (row(b, h, t), h)),
            pl.BlockSpec((blk, GLA_DK), lambda b, h, t: (row(b, h, t), h)),
            pl.BlockSpec((blk, GLA_DV), lambda b, h, t: (row(b, h, t), h)),
            pl.BlockSpec((blk, GLA_DV), lambda b, h, t: (row(b, h, t), h)),
            pl.BlockSpec((blk, GLA_RANK), lambda b, h, t: (row(b, h, t), 0)),
            pl.BlockSpec((GLA_RANK, GLA_DK), lambda b, h, t: (0, h)),
            pl.BlockSpec((1, GLA_DK), lambda b, h, t: (0, h)),
            pl.BlockSpec((1, GLA_DV), lambda b, h, t: (0, 0)),
            pl.BlockSpec((None, None, GLA_DK, GLA_DV), lambda b, h, t: (b, h, 0, 0)),
        ],
        out_specs=[
            pl.BlockSpec((blk, GLA_DV), lambda b, h, t: (b * nt + t, h)),
            pl.BlockSpec((None, None, GLA_DK, GLA_DV), lambda b, h, t: (b, h, 0, 0)),
        ],
        out_shape=[
            jax.ShapeDtypeStruct((m, D_MODEL), F32),
            jax.ShapeDtypeStruct((n_batch, GLA_HEADS, GLA_DK, GLA_DV), F32),
        ],
        scratch_shapes=[pltpu.VMEM((GLA_DK, GLA_DV), F32)],
        compiler_params=_cparams("parallel", "parallel", "arbitrary"),
        name="gla",
    )(q, k, v, r, a_lr, w_a2.astype(BF16), b_a.reshape(1, GLA_QK), norm_g.reshape(1, GLA_DV), s0)


def _gla_decode_kernel(q_ref, k_ref, v_ref, r_ref, a_ref, wa_ref, ba_ref, ng_ref, s0_ref, o_ref, s_ref, *, n_rows):
    pad_rows = 2 * n_rows

    def pad_to(x, rows):
        return jnp.concatenate([x, jnp.zeros((rows - x.shape[0], x.shape[1]), x.dtype)], axis=0)

    row = lax.broadcasted_iota(jnp.int32, (n_rows, GLA_DK), 0)
    row_v = lax.broadcasted_iota(jnp.int32, (n_rows, GLA_DV), 0)
    q = q_ref[...] * GLA_DK ** -0.5
    k = k_ref[...]
    v = v_ref[...]
    g = _dot(pad_to(a_ref[...], pad_rows).astype(BF16), wa_ref[...])[:n_rows]
    g = _log_sigmoid(g + ba_ref[...]) / GLA_TAU
    b = g
    shift = 1
    while shift < n_rows:
        b = b + jnp.where(row >= shift, pltpu.roll(b, shift, 0), 0.0)
        shift *= 2
    s_old = s0_ref[...]
    o = _dot(pad_to(q * jnp.exp(b), pad_rows).astype(BF16), s_old.astype(BF16))[:n_rows]
    for d in range(n_rows):
        if d == 0:
            val = jnp.sum(q * k, axis=-1, keepdims=True)
            o = o + val * v
        else:
            k_sh = pltpu.roll(k, d, 0)
            b_sh = pltpu.roll(b, d, 0)
            val = jnp.sum(q * k_sh * jnp.exp(jnp.minimum(b - b_sh, 0.0)), axis=-1, keepdims=True)
            o = o + jnp.where(row_v >= d, val * pltpu.roll(v, d, 0), 0.0)
    b_last = b[n_rows - 1:n_rows, :]
    kd_t = pad_to(k * jnp.exp(b_last - b), GLA_DK).T
    dec = jnp.broadcast_to(jnp.exp(b_last), (GLA_DK, GLA_DK)).T
    s_ref[...] = (s_old * jnp.concatenate([dec] * (GLA_DV // GLA_DK), axis=1)
                  + _dot(kd_t.astype(BF16), pad_to(v, GLA_DK).astype(BF16)))
    y = o * lax.rsqrt(jnp.mean(o * o, axis=-1, keepdims=True) + EPS) * ng_ref[...]
    rr = r_ref[...]
    o_ref[...] = y * (rr * _sigmoid(rr))


def gla_decode(q, k, v, r, a_lr, w_a2, b_a, norm_g, s0, n_seq, per_seq, row_block0):
    assert per_seq % GLA_CHUNK != 0 and per_seq % 8 == 0
    kern = functools.partial(_gla_decode_kernel, n_rows=per_seq)
    row = lambda b, h: (row_block0 + b, h)
    return pl.pallas_call(
        kern,
        grid=(n_seq, GLA_HEADS),
        in_specs=[
            pl.BlockSpec((per_seq, GLA_DK), row),
            pl.BlockSpec((per_seq, GLA_DK), row),
            pl.BlockSpec((per_seq, GLA_DV), row),
            pl.BlockSpec((per_seq, GLA_DV), row),
            pl.BlockSpec((per_seq, GLA_RANK), lambda b, h: (row_block0 + b, 0)),
            pl.BlockSpec((GLA_RANK, GLA_DK), lambda b, h: (0, h)),
            pl.BlockSpec((1, GLA_DK), lambda b, h: (0, h)),
            pl.BlockSpec((1, GLA_DV), lambda b, h: (0, 0)),
            pl.BlockSpec((None, None, GLA_DK, GLA_DV), lambda b, h: (b, h, 0, 0)),
        ],
        out_specs=[
            pl.BlockSpec((per_seq, GLA_DV), lambda b, h: (b, h)),
            pl.BlockSpec((None, None, GLA_DK, GLA_DV), lambda b, h: (b, h, 0, 0)),
        ],
        out_shape=[
            jax.ShapeDtypeStruct((n_seq * per_seq, D_MODEL), F32),
            jax.ShapeDtypeStruct((n_seq, GLA_HEADS, GLA_DK, GLA_DV), F32),
        ],
        compiler_params=_cparams("parallel", "parallel"),
        name="gla_decode",
    )(q, k, v, r, a_lr, w_a2.astype(BF16), b_a.reshape(1, GLA_QK), norm_g.reshape(1, GLA_DV), s0)


_C1 = D_MODEL + NSA_KV_COLS
_C2 = _C1 + NSA_WIN_COLS
NSA_SPLITS = ((D_MODEL, _C1, F32), (D_MODEL, _C1, BF16), (_C1, _C2, F32), (_C1, _C2, BF16))
NSA_SPLITS_T = ((0, D_MODEL, BF16), (_C1 - GROUP_COLS, _C1, BF16), (_C2 - GROUP_COLS, _C2, BF16),
                (_C2, _C2 + 3 * N_HEADS, F32))
FOX_SPLITS = ((D_MODEL, 3 * D_MODEL, F32), (D_MODEL, 2 * D_MODEL, BF16), (3 * D_MODEL, 3 * D_MODEL + N_HEADS, F32))
FOX_SPLITS_T = ((0, D_MODEL, BF16), (2 * D_MODEL, 3 * D_MODEL, BF16))
GLA_SPLITS = ((0, GLA_QK, F32), (GLA_QK, 2 * GLA_QK, F32), (2 * GLA_QK, 2 * GLA_QK + D_MODEL, F32),
              (2 * GLA_QK + D_MODEL, 2 * GLA_QK + 2 * D_MODEL, F32),
              (2 * GLA_QK + 2 * D_MODEL, 2 * GLA_QK + 2 * D_MODEL + GLA_RANK, F32))


def kernel(x_prompt, x_sample, cache_nsa_kv, state_nsa_win, cache_fox_kv, cache_fox_logf, state_gla, page_table, rel_bias, norm_g, final_g, nsa_w_in, nsa_phi_pe, nsa_phi_w1, nsa_phi_w2, nsa_w_out, fox_w_in, fox_b_f, fox_w_out, gla_w_in, gla_w_a2, gla_b_a, gla_norm_g, gla_w_out, mlp_w1, mlp_w2):
    bp, tp, _ = x_prompt.shape
    bs, ts, _ = x_sample.shape
    mp, ms = bp * tp, bs * ts
    pos_s = page_table.shape[1] * PAGE_SIZE
    x = jnp.concatenate([x_prompt.reshape(mp, D_MODEL), x_sample.reshape(ms, D_MODEL)], axis=0)
    toep = t5_tiles(rel_bias, ATT_TILE)
    nsa_pool_t = feature_major_pages(cache_nsa_kv)
    nsa_win_t = feature_major_pages(state_nsa_win)
    fox_pool_t = feature_major_pages(cache_fox_kv)
    fox_lf_t = feature_major_pages(cache_fox_logf)

    def both(a, shape_p, shape_s):
        return a[:mp].reshape(shape_p), a[mp:].reshape(shape_s)

    nsa_kv_p, nsa_kv_s, nsa_win_p, nsa_win_s = [], [], [], []
    fox_kv_p, fox_kv_s, fox_lf_p, fox_lf_s, gla_p, gla_s = [], [], [], [], [], []
    G = NSA_KV_GROUPS
    for i in range(DEPTH):
        kind, li = i % N_MIXERS, i // N_MIXERS
        if kind == 0:
            kvn, kv_bf, winn, win_bf, q_t, vslc_t, vwin_t, gates_t = norm_proj(
                x, norm_g[i, 0], nsa_w_in[li], NSA_SPLITS, NSA_SPLITS_T)
            cw = compress_weights(nsa_phi_pe[li], nsa_phi_w1[li], nsa_phi_w2[li])
            o_parts_p = nsa_prompt_t(rel_bias, toep, q_t, kvn, kv_bf, win_bf, vslc_t, vwin_t, gates_t, cw, bp, tp)
            kv_p, kv_s = both(kvn, (bp, tp, 4, G, HEAD_DIM), (bs, ts, 4, G, HEAD_DIM))
            w_p, w_s = both(winn, (bp, tp, 2, G, HEAD_DIM), (bs, ts, 2, G, HEAD_DIM))
            *o_parts_s, ws_t = nsa_decode_fm(rel_bias, page_table, nsa_pool_t, nsa_win_t, li, q_t[:, mp:].T, kvn, winn,
                                             gates_t[:, mp:].T, cw, mp // ts, pos_s, ts)
            ws_p = w_p[:, tp - min(WINDOW, tp):]
            ws_s = ws_t.reshape(bs, 2, G, HEAD_DIM, -1).transpose(0, 4, 1, 2, 3)
            nsa_kv_p.append(kv_p); nsa_kv_s.append(kv_s); nsa_win_p.append(ws_p); nsa_win_s.append(ws_s)
            o_parts_p = list(o_parts_p)
            w_out = nsa_w_out[li]
        elif kind == 1:
            kvn, k_bf, fr, q_t, v_t = norm_proj(x, norm_g[i, 0], fox_w_in[li], FOX_SPLITS, FOX_SPLITS_T)
            kv_p, kv_s = both(kvn, (bp, tp, 2, N_HEADS, HEAD_DIM), (bs, ts, 2, N_HEADS, HEAD_DIM))
            lf_all, d_all = fox_prep_prompt(fr, fox_b_f[li], bp, tp)
            o_p = fox_attn_t(q_t, k_bf, v_t, d_all, bp, tp)
            lf_p = lf_all.reshape(bp, tp, N_HEADS)
            q_bd = block_diag_queries(q_t[:, mp:].T, bs, ts, N_HEADS)
            o_s, lf_s = fox_decode_fm(page_table, q_bd, fox_pool_t, fox_lf_t, li, kvn, fr, fox_b_f[li], mp // ts,
                                      bs, ts)
            lf_s = lf_s.reshape(bs, ts, N_HEADS)
            fox_kv_p.append(kv_p); fox_kv_s.append(kv_s); fox_lf_p.append(lf_p); fox_lf_s.append(lf_s)
            o_parts_p, o_parts_s = [o_p], [o_s]
            w_out = fox_w_out[li]
        else:
            q, k, v, r, a = norm_proj(x, norm_g[i, 0], gla_w_in[li], GLA_SPLITS)
            s_zero = jnp.zeros((bp, GLA_HEADS, GLA_DK, GLA_DV), F32)
            o_p, s_p = gla_prompt(q, k, v, r, a, gla_w_a2[li], gla_b_a[li], gla_norm_g[li], s_zero, bp, tp)
            o_s, s_s = gla_decode(q, k, v, r, a, gla_w_a2[li], gla_b_a[li], gla_norm_g[li], state_gla[li], bs, ts,
                                  mp // ts)
            gla_p.append(s_p); gla_s.append(s_s)
            o_parts_p, o_parts_s = [o_p], [o_s]
            w_out = gla_w_out[li]
        x = out_mlp(x, o_parts_p, o_parts_s, w_out.astype(BF16), norm_g[i, 1], mlp_w1[i].astype(BF16),
                    mlp_w2[i].astype(BF16))
    y = final_norm(x, final_g)
    y_p, y_s = both(y, (bp, tp, D_MODEL), (bs, ts, D_MODEL))
    return (y_p, y_s, jnp.stack(nsa_kv_p), jnp.stack(nsa_kv_s), jnp.stack(nsa_win_p), jnp.stack(nsa_win_s),
            jnp.stack(fox_kv_p), jnp.stack(fox_kv_s), jnp.stack(fox_lf_p), jnp.stack(fox_lf_s),
            jnp.stack(gla_p), jnp.stack(gla_s))
```
